```python
import math
import jax, jax.numpy as jnp
from jax import lax
import numpy as np

D_MODEL = 1024
BATCH = 16
SEQ = 4096
DEPTH = 1

HEAD_DIM = 64
NSA_HEADS = 8
NSA_KV_HEADS = 2
NSA_GROUP = NSA_HEADS // NSA_KV_HEADS
SB_HEADS = 8
CMP_BLOCK = 32
CMP_STRIDE = 16
CMP_HIDDEN = HEAD_DIM
SEL_BLOCK = 64
SEL_TOPK = 16
WINDOW = 512
NSA_QBLOCK = 64
SB_QBLOCK = 128
N_BUCKETS = 32
MAX_DISTANCE = 128
D_FF = 4 * D_MODEL
EPS = 1e-6
FORCED_BONUS = 1e4
NEG_BLOCK = -1e9
NEG_LOGIT = -1e30

Q_A_W = NSA_HEADS * HEAD_DIM
KV_A_W = NSA_KV_HEADS * HEAD_DIM
GATE_A_W = NSA_HEADS * 3
SB_W = SB_HEADS * HEAD_DIM
IN_SPLITS = (Q_A_W, KV_A_W, KV_A_W, KV_A_W, KV_A_W, KV_A_W, KV_A_W, GATE_A_W, SB_W, SB_W, SB_W, D_MODEL, D_MODEL)
IN_WIDTH = Q_A_W + 6 * KV_A_W + GATE_A_W + 3 * SB_W + 2 * D_MODEL

kernel_name = 'hybrid_nsa_stickbreaking_adaln_block'


def rms_norm(x, g):
    x32 = x.astype(jnp.float32)
    y = x32 * lax.rsqrt(jnp.mean(x32 * x32, axis=-1, keepdims=True) + EPS)
    return (y * g.astype(jnp.float32)).astype(x.dtype)


def rel_bucket(dist):
    n = jnp.maximum(dist, 0)
    max_exact = N_BUCKETS // 2
    nf = jnp.maximum(n, 1).astype(jnp.float32)
    large = max_exact + (jnp.log(nf / max_exact) / math.log(MAX_DISTANCE / max_exact) * (N_BUCKETS - max_exact)).astype(jnp.int32)
    large = jnp.minimum(large, N_BUCKETS - 1)
    return jnp.where(n < max_exact, n, large)


def masked_softmax(logits, mask):
    masked = jnp.where(mask, logits, NEG_LOGIT)
    m = jnp.max(masked, axis=-1, keepdims=True)
    e = jnp.where(mask, jnp.exp(masked - m), 0.0)
    return e / jnp.maximum(jnp.sum(e, axis=-1, keepdims=True), 1e-30)


def compress_blocks(src, pos, w1, w2):
    B, S = src.shape[0], src.shape[1]
    nc = (S - CMP_BLOCK) // CMP_STRIDE + 1
    idx = jnp.arange(nc)[:, None] * CMP_STRIDE + jnp.arange(CMP_BLOCK)[None, :]
    blk = src[:, idx] + pos[None, None, :, None, :]
    blk = blk.transpose(0, 1, 3, 2, 4).reshape(B, nc, NSA_KV_HEADS, CMP_BLOCK * HEAD_DIM)
    return jax.nn.silu(blk @ w1) @ w2


def cmp_to_sel_overlap(nc, nsel):
    c_start = jnp.arange(nc) * CMP_STRIDE
    s_start = jnp.arange(nsel) * SEL_BLOCK
    ov = jnp.minimum(c_start[:, None] + CMP_BLOCK, s_start[None, :] + SEL_BLOCK) - jnp.maximum(c_start[:, None], s_start[None, :])
    return jnp.clip(ov, 0, CMP_BLOCK).astype(jnp.float32) / CMP_BLOCK


def nsa_attention(q, k_cmp, v_cmp, k_slc, v_slc, k_win, v_win, gates, rel_bias):
    B, S = q.shape[0], q.shape[1]
    nc = k_cmp.shape[1]
    nsel = S // SEL_BLOCK
    n_top = min(SEL_TOPK, nsel)
    scale = HEAD_DIM ** -0.5
    cmp_last = jnp.arange(nc) * CMP_STRIDE + CMP_BLOCK - 1
    overlap = cmp_to_sel_overlap(nc, nsel)
    tbl = rel_bias.astype(jnp.float32)
    tbl_g = tbl.reshape(N_BUCKETS, NSA_KV_HEADS, NSA_GROUP)
    ksel = k_slc.reshape(B, nsel, SEL_BLOCK, NSA_KV_HEADS, HEAD_DIM).transpose(0, 3, 1, 2, 4)
    vsel = v_slc.reshape(B, nsel, SEL_BLOCK, NSA_KV_HEADS, HEAD_DIM).transpose(0, 3, 1, 2, 4)
    kw_pad = jnp.pad(k_win, ((0, 0), (WINDOW, 0), (0, 0), (0, 0)))
    vw_pad = jnp.pad(v_win, ((0, 0), (WINDOW, 0), (0, 0), (0, 0)))
    bidx = jnp.arange(B)[:, None, None, None]
    hidx = jnp.arange(NSA_KV_HEADS)[None, :, None, None]
    blk = jnp.arange(nsel)
    tok_in_blk = jnp.arange(SEL_BLOCK)
    win_off = jnp.arange(WINDOW + NSA_QBLOCK)

    def head_bias(dist):
        b = tbl[rel_bucket(dist)]
        return jnp.moveaxis(b, -1, 0).reshape(NSA_KV_HEADS, NSA_GROUP, dist.shape[0], dist.shape[1])

    def block(i):
        start = i * NSA_QBLOCK
        t = start + jnp.arange(NSA_QBLOCK)
        qb = lax.dynamic_slice_in_dim(q, start, NSA_QBLOCK, axis=1)
        gb = lax.dynamic_slice_in_dim(gates, start, NSA_QBLOCK, axis=1)
        dist_c = t[:, None] - cmp_last[None, :]
        s_c = jnp.einsum('bqhgd,bchd->bhgqc', qb, k_cmp).astype(jnp.float32) * scale + head_bias(dist_c)
        p_c = masked_softmax(s_c, dist_c >= 0)
        o_c = jnp.einsum('bhgqc,bchd->bqhgd', p_c.astype(v_cmp.dtype), v_cmp)
        imp = jnp.einsum('bhgqc,cn->bhqn', p_c, overlap)
        cur = (t // SEL_BLOCK)[:, None]
        forced = (blk == 0) | (blk == cur) | (blk == cur - 1)
        imp = jnp.where(blk > cur, NEG_BLOCK, imp + jnp.where(forced, FORCED_BONUS, 0.0))
        _, top = lax.top_k(imp, n_top)
        kg = ksel[bidx, hidx, top]
        vg = vsel[bidx, hidx, top]
        tok = top[..., None] * SEL_BLOCK + tok_in_blk
        dist_s = t[:, None, None] - tok
        bias_s = jnp.moveaxis(tbl_g[rel_bucket(dist_s), hidx[..., None]], -1, 2)
        s_s = jnp.einsum('bqhgd,bhqnkd->bhgqnk', qb, kg).astype(jnp.float32) * scale + bias_s
        shp = s_s.shape
        m_s = (dist_s >= 0)[:, :, None].reshape(B, NSA_KV_HEADS, 1, NSA_QBLOCK, -1)
        p_s = masked_softmax(s_s.reshape(shp[0], shp[1], shp[2], shp[3], -1), m_s).reshape(shp)
        o_s = jnp.einsum('bhgqnk,bhqnkd->bqhgd', p_s.astype(vg.dtype), vg)
        kwb = lax.dynamic_slice_in_dim(kw_pad, start, WINDOW + NSA_QBLOCK, axis=1)
        vwb = lax.dynamic_slice_in_dim(vw_pad, start, WINDOW + NSA_QBLOCK, axis=1)
        s_pos = start - WINDOW + win_off
        dist_w = t[:, None] - s_pos[None, :]
        m_w = (dist_w >= 0) & (dist_w < WINDOW) & (s_pos >= 0)[None, :]
        s_w = jnp.einsum('bqhgd,bkhd->bhgqk', qb, kwb).astype(jnp.float32) * scale + head_bias(dist_w)
        p_w = masked_softmax(s_w, m_w)
        o_w = jnp.einsum('bhgqk,bkhd->bqhgd', p_w.astype(vwb.dtype), vwb)
        return gb[..., 0:1] * o_c + gb[..., 1:2] * o_s + gb[..., 2:3] * o_w

    out = lax.map(block, jnp.arange(S // NSA_QBLOCK))
    return jnp.swapaxes(out, 0, 1).reshape(B, S, NSA_HEADS * HEAD_DIM)


def stick_breaking_attention(q, k, v):
    B, S = q.shape[0], q.shape[1]
    scale = HEAD_DIM ** -0.5
    key_pos = jnp.arange(S)

    def block(i):
        start = i * SB_QBLOCK
        t = start + jnp.arange(SB_QBLOCK)
        qb = lax.dynamic_slice_in_dim(q, start, SB_QBLOCK, axis=1)
        z = jnp.einsum('bqhd,bshd->bhqs', qb, k).astype(jnp.float32) * scale
        mask = key_pos[None, :] < t[:, None]
        log_keep = jnp.where(mask, jax.nn.log_sigmoid(-z), 0.0)
        suffix = lax.cumsum(log_keep, axis=3, reverse=True) - log_keep
        a = jnp.where(mask, jnp.exp(jax.nn.log_sigmoid(z) + suffix), 0.0)
        return jnp.einsum('bhqs,bshd->bqhd', a.astype(v.dtype), v)

    out = lax.map(block, jnp.arange(S // SB_QBLOCK))
    return jnp.swapaxes(out, 0, 1).reshape(B, S, SB_HEADS * HEAD_DIM)


def setup_inputs(seed: int = 0) -> dict:
    key = jax.random.key(seed)
    ks = jax.random.split(key, 20)
    f32 = jnp.float32
    nrm = lambda k, shape, s: jax.random.normal(k, shape, f32) * s
    gain = lambda k, shape: 1.0 + 0.02 * jax.random.normal(k, shape, f32)
    return {
        'x': nrm(ks[0], (BATCH, SEQ, D_MODEL), 1.0),
        'c': nrm(ks[1], (BATCH, D_MODEL), 1.0),
        'rel_bias': nrm(ks[2], (N_BUCKETS, NSA_HEADS), 0.5),
        'ada_w': nrm(ks[3], (DEPTH, D_MODEL, 6 * D_MODEL), 0.5 * D_MODEL ** -0.5),
        'ada_b': nrm(ks[4], (DEPTH, 6 * D_MODEL), 0.02),
        'norm1_g': gain(ks[5], (DEPTH, D_MODEL)),
        'norm2_g': gain(ks[6], (DEPTH, D_MODEL)),
        'w_in': nrm(ks[7], (DEPTH, D_MODEL, IN_WIDTH), D_MODEL ** -0.5),
        'cmp_pos': nrm(ks[8], (DEPTH, CMP_BLOCK, HEAD_DIM), 0.5),
        'cmp_k_w1': nrm(ks[9], (DEPTH, CMP_BLOCK * HEAD_DIM, CMP_HIDDEN), (CMP_BLOCK * HEAD_DIM) ** -0.5),
        'cmp_k_w2': nrm(ks[10], (DEPTH, CMP_HIDDEN, HEAD_DIM), CMP_HIDDEN ** -0.5),
        'cmp_v_w1': nrm(ks[11], (DEPTH, CMP_BLOCK * HEAD_DIM, CMP_HIDDEN), (CMP_BLOCK * HEAD_DIM) ** -0.5),
        'cmp_v_w2': nrm(ks[12], (DEPTH, CMP_HIDDEN, HEAD_DIM), CMP_HIDDEN ** -0.5),
        'q_norm_g': gain(ks[13], (DEPTH, HEAD_DIM)),
        'k_norm_g': gain(ks[14], (DEPTH, 3, HEAD_DIM)),
        'w_up_nsa': nrm(ks[15], (DEPTH, Q_A_W, D_MODEL), Q_A_W ** -0.5),
        'w_up_sb': nrm(ks[16], (DEPTH, SB_W, D_MODEL), SB_W ** -0.5),
        'w_out': nrm(ks[17], (DEPTH, D_MODEL, D_MODEL), D_MODEL ** -0.5),
        'mlp_w1': nrm(ks[18], (DEPTH, D_MODEL, D_FF), D_MODEL ** -0.5),
        'mlp_w2': nrm(ks[19], (DEPTH, D_FF, D_MODEL), D_FF ** -0.5),
    }


def reference(x, c, rel_bias, ada_w, ada_b, norm1_g, norm2_g, w_in, cmp_pos, cmp_k_w1, cmp_k_w2, cmp_v_w1, cmp_v_w2, q_norm_g, k_norm_g, w_up_nsa, w_up_sb, w_out, mlp_w1, mlp_w2):
    B, S, _ = x.shape
    split_at = np.cumsum(IN_SPLITS)[:-1].tolist()
    kv_shape = (B, S, NSA_KV_HEADS, HEAD_DIM)
    sb_shape = (B, S, SB_HEADS, HEAD_DIM)
    h = x
    for layer in range(DEPTH):
        mod = jax.nn.silu(c) @ ada_w[layer] + ada_b[layer]
        shift1, scale1, gate1, shift2, scale2, gate2 = [m[:, None, :] for m in jnp.split(mod, 6, axis=-1)]
        u = rms_norm(h, norm1_g[layer]) * (1 + scale1) + shift1
        z = u @ w_in[layer]
        q_a, kc, vc, ksl, vsl, kwn, vwn, g_a, q_b, k_b, v_b, m_a, m_b = jnp.split(z, split_at, axis=-1)
        q_a = rms_norm(q_a.reshape(B, S, NSA_HEADS, HEAD_DIM), q_norm_g[layer]).reshape(B, S, NSA_KV_HEADS, NSA_GROUP, HEAD_DIM)
        k_cmp = rms_norm(compress_blocks(kc.reshape(kv_shape), cmp_pos[layer], cmp_k_w1[layer], cmp_k_w2[layer]), k_norm_g[layer, 0])
        v_cmp = compress_blocks(vc.reshape(kv_shape), cmp_pos[layer], cmp_v_w1[layer], cmp_v_w2[layer])
        k_slc = rms_norm(ksl.reshape(kv_shape), k_norm_g[layer, 1])
        k_win = rms_norm(kwn.reshape(kv_shape), k_norm_g[layer, 2])
        g_nsa = jax.nn.sigmoid(g_a).reshape(B, S, NSA_KV_HEADS, NSA_GROUP, 3)
        y_a = nsa_attention(q_a, k_cmp, v_cmp, k_slc, vsl.reshape(kv_shape), k_win, vwn.reshape(kv_shape), g_nsa, rel_bias) @ w_up_nsa[layer]
        y_b = stick_breaking_attention(q_b.reshape(sb_shape), k_b.reshape(sb_shape), v_b.reshape(sb_shape)) @ w_up_sb[layer]
        mixed = (jax.nn.sigmoid(m_a) * y_a + jax.nn.sigmoid(m_b) * y_b) @ w_out[layer]
        h = h + gate1 * mixed
        u2 = rms_norm(h, norm2_g[layer]) * (1 + scale2) + shift2
        ff = jnp.square(jax.nn.relu(u2 @ mlp_w1[layer])) @ mlp_w2[layer]
        h = h + gate2 * ff
    return h
```

```python
import functools
import math

import numpy as np
import jax
import jax.numpy as jnp
from jax import lax
from jax.experimental import pallas as pl
from jax.experimental.pallas import tpu as pltpu

f32 = jnp.float32
bf16 = jnp.bfloat16

HEAD_DIM = 64
NSA_HEADS = 8
NSA_KV_HEADS = 2
NSA_GROUP = NSA_HEADS // NSA_KV_HEADS
SB_HEADS = 8
CMP_BLOCK = 32
CMP_STRIDE = 16
SEL_BLOCK = 64
SEL_TOPK = 16
WINDOW = 512
N_BUCKETS = 32
MAX_DISTANCE = 128
EPS = 1e-6
FORCED_BONUS = 1e4
NEG_BLOCK = -1e9

LANES = 128
MASKED = -1e30
UNSELECTED = -1e9
VMEM_LIMIT = 56 * 1024 * 1024

TQ = 128
TM = 512


def _bucket_thresholds():
    n = np.arange(0, 4 * MAX_DISTANCE)
    max_exact = N_BUCKETS // 2
    nf = np.maximum(n, 1).astype(np.float32)
    large = max_exact + (np.log(nf / max_exact) / math.log(MAX_DISTANCE / max_exact)
                         * (N_BUCKETS - max_exact)).astype(np.int32)
    large = np.minimum(large, N_BUCKETS - 1)
    b = np.where(n < max_exact, n, large)
    assert np.all(np.diff(b) >= 0) and b[-1] == N_BUCKETS - 1
    return [int(np.argmax(b >= k)) for k in range(N_BUCKETS)]


BUCKET_START = _bucket_thresholds()
assert BUCKET_START[-1] <= LANES


def _dot(a, b):
    return jnp.dot(a, b, preferred_element_type=f32)


def _dot_nt(a, b):
    return lax.dot_general(a, b, (((1,), (1,)), ((), ())), preferred_element_type=f32)


def _split(a):
    hi = a.astype(bf16)
    lo = (a - hi.astype(f32)).astype(bf16)
    return hi, lo


def _params(n_grid):
    return pltpu.CompilerParams(dimension_semantics=("arbitrary",) * n_grid,
                                vmem_limit_bytes=VMEM_LIMIT)


def _adaln_kernel(c_ref, w_ref, b_ref, o_ref):
    c = c_ref[...]
    a = c * jax.nn.sigmoid(c)
    ah, al = _split(a)
    wh, wl = _split(w_ref[...])
    o_ref[...] = _dot(ah, wh) + _dot(ah, wl) + _dot(al, wh) + b_ref[...]


def _adaln(c, w, b):
    bsz, d = c.shape
    n = w.shape[1]
    return pl.pallas_call(
        _adaln_kernel,
        grid=(n // d,),
        in_specs=[pl.BlockSpec((bsz, d), lambda j: (0, 0)),
                  pl.BlockSpec((d, d), lambda j: (0, j)),
                  pl.BlockSpec((1, d), lambda j: (0, j))],
        out_specs=pl.BlockSpec((bsz, d), lambda j: (0, j)),
        out_shape=jax.ShapeDtypeStruct((bsz, n), f32),
        compiler_params=_params(1),
        name="adaln",
    )(c, w, b.reshape(1, n))


def _bias_of_dist(dist, tbl_ref, h):
    out = jnp.full(dist.shape, tbl_ref[0, h], f32)
    for k in range(1, N_BUCKETS):
        out = jnp.where(dist >= BUCKET_START[k], tbl_ref[k, h], out)
    return jnp.where(dist >= 0, out, MASKED)


def _cmp_bias_kernel(tbl_ref, o_ref):
    h = pl.program_id(0)
    i = pl.program_id(1)
    rows, nc = o_ref.shape[1], o_ref.shape[2]
    t = i * rows + lax.broadcasted_iota(jnp.int32, (rows, nc), 0)
    j = lax.broadcasted_iota(jnp.int32, (rows, nc), 1)
    o_ref[0] = _bias_of_dist(t - (j * CMP_STRIDE + CMP_BLOCK - 1), tbl_ref, h)


def _near_bias_kernel(tbl_ref, o_ref):
    h = pl.program_id(0)
    r = lax.broadcasted_iota(jnp.int32, (TQ, TQ), 0)
    c = lax.broadcasted_iota(jnp.int32, (TQ, TQ), 1)
    o_ref[0, 0] = _bias_of_dist(r - c + TQ, tbl_ref, h)
    o_ref[0, 1] = _bias_of_dist(r - c, tbl_ref, h)


def _bias_tiles(rel_bias, seq, nc_pad):
    tbl = rel_bias.astype(f32)
    smem = pl.BlockSpec(memory_space=pltpu.SMEM)
    rows = 512
    cmp_bias = pl.pallas_call(
        _cmp_bias_kernel,
        grid=(NSA_HEADS, seq // rows),
        in_specs=[smem],
        out_specs=pl.BlockSpec((1, rows, nc_pad), lambda h, i: (h, i, 0)),
        out_shape=jax.ShapeDtypeStruct((NSA_HEADS, seq, nc_pad), f32),
        compiler_params=_params(2),
        name="cmp_bias",
    )(tbl)
    near_bias = pl.pallas_call(
        _near_bias_kernel,
        grid=(NSA_HEADS,),
        in_specs=[smem],
        out_specs=pl.BlockSpec((1, 2, TQ, TQ), lambda h: (h, 0, 0, 0)),
        out_shape=jax.ShapeDtypeStruct((NSA_HEADS, 2, TQ, TQ), f32),
        compiler_params=_params(1),
        name="near_bias",
    )(tbl)
    return cmp_bias, near_bias


_QA_W = NSA_HEADS * LANES
_KV_W = NSA_KV_HEADS * LANES
_CMP_W = NSA_KV_HEADS * HEAD_DIM
_SB_W = SB_HEADS * HEAD_DIM


def _layout(d_model):
    names = ["qa", "kc", "vc", "ksl", "vsl", "kwn", "vwn", "g", "qb", "kb", "vb", "ma", "mb"]
    widths = [_QA_W, _CMP_W, _CMP_W, _KV_W, _KV_W, _KV_W, _KV_W, _KV_W, _SB_W, _SB_W, _SB_W,
              d_model, d_model]
    offs = np.concatenate([[0], np.cumsum(widths)])
    return {n: (int(offs[i]), int(offs[i + 1])) for i, n in enumerate(names)}, int(offs[-1])


def _pack_w_in(w_in, d_model):
    q_w = NSA_HEADS * HEAD_DIM
    kv_w = NSA_KV_HEADS * HEAD_DIM
    g_w = NSA_HEADS * 3
    sizes = [q_w] + [kv_w] * 6 + [g_w] + [_SB_W] * 3 + [d_model, d_model]
    offs = np.concatenate([[0], np.cumsum(sizes)])
    parts = [w_in[:, int(offs[i]):int(offs[i + 1])] for i in range(len(sizes))]
    d = w_in.shape[0]

    def pad_heads(p, n_heads, width):
        p = p.reshape(d, n_heads, width)
        p = jnp.pad(p, ((0, 0), (0, 0), (0, LANES - width)))
        return p.reshape(d, n_heads * LANES)

    packed = [pad_heads(parts[0], NSA_HEADS, HEAD_DIM), parts[1], parts[2]]
    packed += [pad_heads(parts[k], NSA_KV_HEADS, HEAD_DIM) for k in (3, 4, 5, 6)]
    packed += [pad_heads(parts[7], NSA_KV_HEADS, NSA_GROUP * 3)]
    packed += parts[8:]
    return jnp.concatenate(packed, axis=1).astype(bf16)


def _inproj_kernel(lay, x_ref, g1_ref, sc_ref, sh_ref, w_ref, gsum_ref, gq_ref, gk_ref,
                   qa_ref, kc_ref, vc_ref, ks_ref, vs_ref, kw_ref, vw_ref, g_ref,
                   qb_ref, kb_ref, vb_ref, ma_ref, mb_ref):
    i = pl.program_id(1)
    x = x_ref[0]
    ms = jnp.mean(x * x, axis=-1, keepdims=True)
    u = (x * lax.rsqrt(ms + EPS) * g1_ref[...]) * (1.0 + sc_ref[0]) + sh_ref[0]
    ub = u.astype(bf16)

    def proj(name):
        lo, hi = lay[name]
        return _dot(ub, w_ref[:, lo:hi])

    def head_norm(z, gain):
        n = z.shape[1]
        hi, lo = _split(z * z)
        g = gsum_ref[:n, :n]
        ss = _dot(hi, g) + _dot(lo, g)
        return z * lax.rsqrt(ss * (1.0 / HEAD_DIM) + EPS) * gain

    scale = HEAD_DIM ** -0.5
    qa_ref[0] = (head_norm(proj("qa"), gq_ref[...]) * scale).astype(bf16)
    kc_ref[0] = proj("kc")
    vc_ref[0] = proj("vc")

    rows = x.shape[0]
    lane = lax.broadcasted_iota(jnp.int32, (rows, _KV_W), 1)
    tok_blk = (i * rows + lax.broadcasted_iota(jnp.int32, (rows, _KV_W), 0)) // SEL_BLOCK
    in_pad = (lane & HEAD_DIM) != 0
    onehot = in_pad & ((lane & (HEAD_DIM - 1)) == tok_blk)
    ones_col = (lane & (LANES - 1)) == HEAD_DIM

    ks = head_norm(proj("ksl"), gk_ref[1:2, :])
    ks_ref[0] = jnp.where(onehot, 1.0, ks).astype(bf16)
    vs_ref[0] = jnp.where(ones_col, 1.0, proj("vsl")).astype(bf16)
    kw_ref[0] = head_norm(proj("kwn"), gk_ref[2:3, :]).astype(bf16)
    vw_ref[0] = jnp.where(ones_col, 1.0, proj("vwn")).astype(bf16)
    g_ref[0] = jax.nn.sigmoid(proj("g"))
    qb_ref[0] = (proj("qb") * scale).astype(bf16)
    kb_ref[0] = proj("kb").astype(bf16)
    vb_ref[0] = proj("vb").astype(bf16)
    ma_ref[0] = jax.nn.sigmoid(proj("ma")).astype(bf16)
    mb_ref[0] = jax.nn.sigmoid(proj("mb")).astype(bf16)


def _inproj(x, g1, scale1, shift1, w_packed, gsum, gq, gk):
    bsz, seq, d = x.shape
    lay, width = _layout(d)
    assert w_packed.shape == (d, width)
    tok = lambda w: pl.BlockSpec((1, TM, w), lambda b, i: (b, i, 0))
    full = lambda a: pl.BlockSpec(a.shape, lambda b, i: (0,) * a.ndim,
                                  pipeline_mode=pl.Buffered(1))
    mod = pl.BlockSpec((1, 1, d), lambda b, i: (b, 0, 0))
    out_w =[(_QA_W, bf16), (_CMP_W, f32), (_CMP_W, f32), (_KV_W, bf16), (_KV_W, bf16),
             (_KV_W, bf16), (_KV_W, bf16), (_KV_W, f32), (_SB_W, bf16), (_SB_W, bf16),
             (_SB_W, bf16), (d, bf16), (d, bf16)]
    return pl.pallas_call(
        functools.partial(_inproj_kernel, lay),
        grid=(bsz, seq // TM),
        in_specs=[tok(d), full(g1), mod, mod, full(w_packed), full(gsum), full(gq), full(gk)],
        out_specs=[tok(w) for w, _ in out_w],
        out_shape=[jax.ShapeDtypeStruct((bsz, seq, w), dt) for w, dt in out_w],
        compiler_params=_params(2),
        name="inproj",
    )(x, g1, scale1, shift1, w_packed, gsum, gq, gk)


def _compress_kernel(xk_ref, xv_ref, pos_ref, w1k_ref, w2k_ref, w1v_ref, w2v_ref, gk_ref,
                     ko_ref, vo_ref):
    half = pos_ref.shape[1]

    def mlp(x, w1_ref, w2_ref):
        xa = (x + pos_ref[0:1, :]).astype(bf16)
        xb = (x + pos_ref[1:2, :]).astype(bf16)
        first = _dot(xa, w1_ref[:half, :])
        second = _dot(xb, w1_ref[half:, :])
        n = x.shape[0]
        pre = first + pltpu.roll(second, n - 1, 0)
        hid = pre * jax.nn.sigmoid(pre)
        return _dot(hid.astype(bf16), w2_ref[...])

    for h in range(NSA_KV_HEADS):
        k = mlp(xk_ref[0, h], w1k_ref, w2k_ref)
        ms = jnp.mean(k * k, axis=-1, keepdims=True)
        ko_ref[0, h] = (k * lax.rsqrt(ms + EPS) * gk_ref[...]).astype(bf16)
        vo_ref[0, h] = mlp(xv_ref[0, h], w1v_ref, w2v_ref).astype(bf16)


def _compress(xk, xv, pos2, w1k, w2k, w1v, w2v, gk0):
    bsz, hk, nch, width = xk.shape
    blk = pl.BlockSpec((1, hk, nch, width), lambda b: (b, 0, 0, 0))
    full = lambda a: pl.BlockSpec(a.shape, lambda b: (0,) * a.ndim)
    out = pl.BlockSpec((1, hk, nch, HEAD_DIM), lambda b: (b, 0, 0, 0))
    shape = jax.ShapeDtypeStruct((bsz, hk, nch, HEAD_DIM), bf16)
    return pl.pallas_call(
        _compress_kernel,
        grid=(bsz,),
        in_specs=[blk, blk, full(pos2), full(w1k), full(w2k), full(w1v), full(w2v), full(gk0)],
        out_specs=[out, out],
        out_shape=[shape, shape],
        compiler_params=_params(1),
        name="compress",
    )(xk, xv, pos2, w1k, w2k, w1v, w2v, gk0)


def _flash_step(s, v, m_ref, acc_ref):
    m_prev = m_ref[...]
    m_new = jnp.maximum(m_prev, jnp.max(s, axis=1, keepdims=True))
    alpha = jnp.exp(m_prev - m_new)
    p = jnp.exp(s - m_new)
    acc_ref[...] = alpha * acc_ref[...] + _dot(p.astype(bf16), v)
    m_ref[...] = m_new


def _flash_init(m_ref, acc_ref):
    m_ref[...] = jnp.full(m_ref.shape, MASKED, f32)
    acc_ref[...] = jnp.zeros(acc_ref.shape, f32)


def _flash_out(acc_ref):
    acc = acc_ref[...]
    return acc[:, :HEAD_DIM] / acc[:, HEAD_DIM:HEAD_DIM + 1]


def _nsa_kernel(n_top, tbl_ref, q_ref, g_ref, kc_ref, vc_ref, ks_ref, vs_ref, kw_ref, vw_ref,
                bc_ref, bn_ref, ovt_ref, o_ref, qs_ref, b31_ref, m_ref, acc_ref):
    h = pl.program_id(1)
    i = pl.program_id(2)
    rows = NSA_GROUP * TQ
    start = i * TQ

    qpad = jnp.concatenate([q_ref[0, :, g * LANES:(g + 1) * LANES] for g in range(NSA_GROUP)],
                           axis=0)
    b31_ref[...] = jnp.concatenate(
        [jnp.full((TQ, LANES), tbl_ref[N_BUCKETS - 1, h * NSA_GROUP + g], f32)
         for g in range(NSA_GROUP)], axis=0)

    bc = bc_ref[...].reshape(rows, bc_ref.shape[2])
    s_c = _dot_nt(qpad[:, :HEAD_DIM], kc_ref[0, 0]) + bc
    visible = bc > 0.5 * MASKED
    m_c = jnp.max(s_c, axis=1, keepdims=True)
    e_c = jnp.where(visible, jnp.exp(s_c - m_c), 0.0)
    p_c = e_c / jnp.maximum(jnp.sum(e_c, axis=1, keepdims=True), 1e-30)
    o_c = _dot(p_c.astype(bf16), vc_ref[0, 0])

    p_sum = p_c[0:TQ]
    for g in range(1, NSA_GROUP):
        p_sum = p_sum + p_c[g * TQ:(g + 1) * TQ]
    p_hi, p_lo = _split(p_sum)
    imp = _dot_nt(ovt_ref[...], p_hi) + _dot_nt(ovt_ref[...], p_lo)
    nblk = imp.shape[0]
    blk = lax.broadcasted_iota(jnp.int32, (nblk, TQ), 0)
    cur = (start + lax.broadcasted_iota(jnp.int32, (nblk, TQ), 1)) // SEL_BLOCK
    forced = (blk == 0) | (blk == cur) | (blk == cur - 1)
    imp = jnp.where(blk > cur, NEG_BLOCK, imp + jnp.where(forced, FORCED_BONUS, 0.0))
    rank = jnp.zeros((nblk, TQ), f32)
    for b2 in range(nblk):
        other = imp[b2:b2 + 1, :]
        rank = rank + jnp.where(blk > b2, jnp.where(other >= imp, 1.0, 0.0),
                                jnp.where(other > imp, 1.0, 0.0))
    usable = (rank < n_top) & (blk <= cur)
    sel_t = jnp.where(usable, 0.0, UNSELECTED)
    sel_pad = jnp.concatenate([jnp.zeros((LANES - nblk, TQ), f32), sel_t], axis=0).T
    sel_rows = jnp.concatenate([sel_pad.astype(bf16)] * NSA_GROUP, axis=0)
    qs_ref[...] = qpad + sel_rows

    _flash_init(m_ref, acc_ref)

    def far_chunk(c, carry):
        off = pl.multiple_of(c * TQ, TQ)
        s = _dot_nt(qs_ref[...], ks_ref[0, pl.ds(off, TQ), :]) + b31_ref[...]
        _flash_step(s, vs_ref[0, pl.ds(off, TQ), :], m_ref, acc_ref)
        return carry

    lax.fori_loop(0, jnp.maximum(i - 1, 0), far_chunk, 0)

    def near_bias(j):
        return bn_ref[:, j].reshape(rows, TQ)

    @pl.when(i >= 1)
    def _():
        off = pl.multiple_of(start - TQ, TQ)
        s = _dot_nt(qs_ref[...], ks_ref[0, pl.ds(off, TQ), :]) + near_bias(0)
        _flash_step(s, vs_ref[0, pl.ds(off, TQ), :], m_ref, acc_ref)

    off_d = pl.multiple_of(start, TQ)
    s = _dot_nt(qs_ref[...], ks_ref[0, pl.ds(off_d, TQ), :]) + near_bias(1)
    _flash_step(s, vs_ref[0, pl.ds(off_d, TQ), :], m_ref, acc_ref)
    o_s = _flash_out(acc_ref)

    _flash_init(m_ref, acc_ref)
    n_chunks = WINDOW // TQ + 1
    for j in range(n_chunks):
        back = n_chunks - 1 - j

        @pl.when(i >= back)
        def _(j=j, back=back):
            off = pl.multiple_of(start - back * TQ, TQ)
            s = _dot_nt(qpad, kw_ref[0, pl.ds(off, TQ), :])
            if back == n_chunks - 1:
                r = lax.broadcasted_iota(jnp.int32, (rows, TQ), 0) & (TQ - 1)
                c = lax.broadcasted_iota(jnp.int32, (rows, TQ), 1)
                s = s + b31_ref[...] + jnp.where(c > r, 0.0, MASKED)
            elif back >= 2:
                s = s + b31_ref[...]
            else:
                s = s + near_bias(1 - back)
            _flash_step(s, vw_ref[0, pl.ds(off, TQ), :], m_ref, acc_ref)

    o_w = _flash_out(acc_ref)

    gates = g_ref[0]
    outs = []
    for g in range(NSA_GROUP):
        sl = slice(g * TQ, (g + 1) * TQ)
        outs.append(gates[:, 3 * g:3 * g + 1] * o_c[sl]
                    + gates[:, 3 * g + 1:3 * g + 2] * o_s[sl]
                    + gates[:, 3 * g + 2:3 * g + 3] * o_w[sl])
    o_ref[0] = jnp.concatenate(outs, axis=1).astype(bf16)


def _nsa(tbl, qa, gates, kcmp, vcmp, ks, vs, kw, vw, cmp_bias, near_bias, ovt):
    bsz, seq, _ = qa.shape
    nc = kcmp.shape[2]
    n_top = min(SEL_TOPK, seq // SEL_BLOCK)
    grp_w = NSA_GROUP * LANES
    rows = NSA_GROUP * TQ
    kv = pl.BlockSpec((1, seq, LANES), lambda b, h, i: (b, 0, h))
    cmp = pl.BlockSpec((1, 1, nc, HEAD_DIM), lambda b, h, i: (b, h, 0, 0))
    return pl.pallas_call(
        functools.partial(_nsa_kernel, n_top),
        grid=(bsz, NSA_KV_HEADS, seq // TQ),
        in_specs=[pl.BlockSpec(memory_space=pltpu.SMEM),
                  pl.BlockSpec((1, TQ, grp_w), lambda b, h, i: (b, i, h)),
                  pl.BlockSpec((1, TQ, LANES), lambda b, h, i: (b, i, h)),
                  cmp, cmp, kv, kv, kv, kv,
                  pl.BlockSpec((NSA_GROUP, TQ, nc), lambda b, h, i: (h, i, 0)),
                  pl.BlockSpec((NSA_GROUP, 2, TQ, TQ), lambda b, h, i: (h, 0, 0, 0)),
                  pl.BlockSpec(ovt.shape, lambda b, h, i: (0, 0))],
        out_specs=pl.BlockSpec((1, TQ, NSA_GROUP * HEAD_DIM), lambda b, h, i: (b, i, h)),
        out_shape=jax.ShapeDtypeStruct((bsz, seq, NSA_HEADS * HEAD_DIM), bf16),
        scratch_shapes=[pltpu.VMEM((rows, LANES), bf16),
                        pltpu.VMEM((rows, LANES), f32),
                        pltpu.VMEM((rows, LANES), f32),
                        pltpu.VMEM((rows, LANES), f32)],
        compiler_params=_params(3),
        name="nsa",
    )(tbl, qa, gates, kcmp, vcmp, ks, vs, kw, vw, cmp_bias, near_bias, ovt)


def _sb_kernel(q_ref, k_ref, v_ref, tri_ref, o_ref, carry_ref, acc_ref):
    i = pl.program_id(2)
    q = q_ref[0]
    lane = lax.broadcasted_iota(jnp.int32, q.shape, 1)
    zero = jnp.zeros_like(q)
    q_heads = (jnp.where(lane < HEAD_DIM, q, zero), jnp.where(lane >= HEAD_DIM, q, zero))
    carry_ref[...] = jnp.zeros(carry_ref.shape, f32)
    acc_ref[...] = jnp.zeros(acc_ref.shape, f32)
    r = lax.broadcasted_iota(jnp.int32, (TQ, TQ), 0)
    c = lax.broadcasted_iota(jnp.int32, (TQ, TQ), 1)
    before = c < r

    def chunk(off, diagonal):
        k = k_ref[0, pl.ds(off, TQ), :]
        v = v_ref[0, pl.ds(off, TQ), :]
        for hh in range(2):
            z = _dot_nt(q_heads[hh], k)
            sp = jnp.maximum(z, 0.0) + jnp.log(1.0 + jnp.exp(-jnp.abs(z)))
            log_keep = -sp
            if diagonal:
                log_keep = jnp.where(before, log_keep, 0.0)
            hi, lo = _split(log_keep)
            sums = _dot(hi, tri_ref[...]) + _dot(lo, tri_ref[...])
            logit = (z - sp) + sums[:, :TQ] + carry_ref[hh]
            a = jnp.exp(logit)
            if diagonal:
                a = jnp.where(before, a, 0.0)
            acc_ref[hh] = acc_ref[hh] + _dot(a.astype(bf16), v)
            carry_ref[hh] = carry_ref[hh] + sums[:, TQ:]

    chunk(pl.multiple_of(i * TQ, TQ), True)

    def body(n, carry):
        chunk(pl.multiple_of((i - 1 - n) * TQ, TQ), False)
        return carry

    lax.fori_loop(0, i, body, 0)
    o_ref[0] = jnp.where(lane < HEAD_DIM, acc_ref[0], acc_ref[1]).astype(bf16)


def _sb(qb, kb, vb, tri):
    bsz, seq, width = qb.shape
    q_spec = pl.BlockSpec((1, TQ, LANES), lambda b, p, i: (b, i, p))
    kv_spec = pl.BlockSpec((1, seq, LANES), lambda b, p, i: (b, 0, p))
    return pl.pallas_call(
        _sb_kernel,
        grid=(bsz, width // LANES, seq // TQ),
        in_specs=[q_spec, kv_spec, kv_spec, pl.BlockSpec(tri.shape, lambda b, p, i: (0, 0))],
        out_specs=q_spec,
        out_shape=jax.ShapeDtypeStruct((bsz, seq, width), bf16),
        scratch_shapes=[pltpu.VMEM((2, TQ, LANES), f32), pltpu.VMEM((2, TQ, LANES), f32)],
        compiler_params=_params(3),
        name="sb",
    )(qb, kb, vb, tri)


def _merge_kernel(ya_ref, yb_ref, ma_ref, mb_ref, x_ref, gate_ref, wa_ref, wb_ref, wo_ref, o_ref):
    y_a = _dot(ya_ref[0], wa_ref[...])
    y_b = _dot(yb_ref[0], wb_ref[...])
    mixed = ma_ref[0].astype(f32) * y_a + mb_ref[0].astype(f32) * y_b
    o_ref[0] = x_ref[0] + gate_ref[0] * _dot(mixed.astype(bf16), wo_ref[...])


def _merge(ya, yb, ma, mb, x, gate1, wa, wb, wo):
    bsz, seq, d = x.shape
    tok = lambda w: pl.BlockSpec((1, TM, w), lambda b, i: (b, i, 0))
    full = lambda a: pl.BlockSpec(a.shape, lambda b, i: (0,) * a.ndim)
    return pl.pallas_call(
        _merge_kernel,
        grid=(bsz, seq // TM),
        in_specs=[tok(ya.shape[2]), tok(yb.shape[2]), tok(d), tok(d), tok(d),
                  pl.BlockSpec((1, 1, d), lambda b, i: (b, 0, 0)), full(wa), full(wb), full(wo)],
        out_specs=tok(d),
        out_shape=jax.ShapeDtypeStruct((bsz, seq, d), f32),
        compiler_params=_params(2),
        name="merge",
    )(ya, yb, ma, mb, x, gate1, wa, wb, wo)


def _mlp_kernel(h_ref, g2_ref, sc_ref, sh_ref, gate_ref, w1_ref, w2_ref, o_ref):
    hres = h_ref[0]
    d = hres.shape[1]
    ms = jnp.mean(hres * hres, axis=-1, keepdims=True)
    u = (hres * lax.rsqrt(ms + EPS) * g2_ref[...]) * (1.0 + sc_ref[0]) + sh_ref[0]
    ub = u.astype(bf16)
    ff = jnp.zeros(hres.shape, f32)
    for c in range(w1_ref.shape[1] // d):
        hid = jnp.maximum(_dot(ub, w1_ref[:, c * d:(c + 1) * d]), 0.0)
        ff = ff + _dot((hid * hid).astype(bf16), w2_ref[c * d:(c + 1) * d, :])
    o_ref[0] = hres + gate_ref[0] * ff


def _mlp(hres, g2, scale2, shift2, gate2, w1, w2):
    bsz, seq, d = hres.shape
    tok = pl.BlockSpec((1, TM, d), lambda b, i: (b, i, 0))
    mod = pl.BlockSpec((1, 1, d), lambda b, i: (b, 0, 0))
    const = lambda a: pl.BlockSpec(a.shape, lambda b, i: (0,) * a.ndim,
                                   pipeline_mode=pl.Buffered(1))
    return pl.pallas_call(
        _mlp_kernel,
        grid=(bsz, seq // TM),
        in_specs=[tok, pl.BlockSpec(g2.shape, lambda b, i: (0, 0)), mod, mod, mod,
                  const(w1), const(w2)],
        out_specs=tok,
        out_shape=jax.ShapeDtypeStruct((bsz, seq, d), f32),
        compiler_params=_params(2),
        name="mlp",
    )(hres, g2, scale2, shift2, gate2, w1, w2)


def _overlap_t(nc_pad, nsel_pad, nc, nsel):
    c_start = np.arange(nc_pad) * CMP_STRIDE
    s_start = np.arange(nsel_pad) * SEL_BLOCK
    ov = (np.minimum(c_start[None, :] + CMP_BLOCK, s_start[:, None] + SEL_BLOCK)
          - np.maximum(c_start[None, :], s_start[:, None]))
    ov = np.clip(ov, 0, CMP_BLOCK).astype(np.float32) / CMP_BLOCK
    ov[nsel:, :] = 0.0
    ov[:, nc:] = 0.0
    return ov


def _layer(h, mod, rel_tiles, tbl, p):
    bsz, seq, d = h.shape
    shift1, scale1, gate1, shift2, scale2, gate2 = [
        mod[:, k * d:(k + 1) * d].reshape(bsz, 1, d) for k in range(6)]
    cmp_bias, near_bias = rel_tiles

    group_ones = np.kron(np.eye(_QA_W // HEAD_DIM), np.ones((HEAD_DIM, HEAD_DIM)))
    group_ones = group_ones * (np.arange(_QA_W)[None, :] % LANES < HEAD_DIM)
    gsum = jnp.asarray(group_ones, bf16)
    pad_gain = lambda g, n: jnp.tile(jnp.pad(g, (0, LANES - HEAD_DIM)), n).reshape(1, n * LANES)
    gq = pad_gain(p["q_norm_g"], NSA_HEADS)
    gk = jnp.concatenate([pad_gain(p["k_norm_g"][k], NSA_KV_HEADS) for k in range(3)], axis=0)

    (qa, kc, vc, ks, vs, kw, vw, gates, qb, kb, vb, ma, mb) = _inproj(
        h, p["norm1_g"].reshape(1, d), scale1, shift1, _pack_w_in(p["w_in"], d), gsum, gq, gk)

    nch = seq // CMP_STRIDE
    chunks = lambda a: a.reshape(bsz, nch, CMP_STRIDE, NSA_KV_HEADS, HEAD_DIM).transpose(
        0, 3, 1, 2, 4).reshape(bsz, NSA_KV_HEADS, nch, CMP_STRIDE * HEAD_DIM)
    pos2 = p["cmp_pos"].reshape(2, CMP_STRIDE * HEAD_DIM)
    kcmp, vcmp = _compress(chunks(kc), chunks(vc), pos2,
                           p["cmp_k_w1"].astype(bf16), p["cmp_k_w2"].astype(bf16),
                           p["cmp_v_w1"].astype(bf16), p["cmp_v_w2"].astype(bf16),
                           p["k_norm_g"][0].reshape(1, HEAD_DIM))

    nc = (seq - CMP_BLOCK) // CMP_STRIDE + 1
    ovt = jnp.asarray(_overlap_t(nch, HEAD_DIM, nc, seq // SEL_BLOCK), bf16)
    y_nsa = _nsa(tbl, qa, gates, kcmp, vcmp, ks, vs, kw, vw, cmp_bias, near_bias, ovt)

    tri = np.concatenate([np.tril(np.ones((TQ, TQ)), -1), np.ones((TQ, TQ))], axis=1)
    y_sb = _sb(qb, kb, vb, jnp.asarray(tri, bf16))

    h1 = _merge(y_nsa, y_sb, ma, mb, h, gate1, p["w_up_nsa"].astype(bf16),
                p["w_up_sb"].astype(bf16), p["w_out"].astype(bf16))
    return _mlp(h1, p["norm2_g"].reshape(1, d), scale2, shift2, gate2,
                p["mlp_w1"].astype(bf16), p["mlp_w2"].astype(bf16))


def kernel(x, c, rel_bias, ada_w, ada_b, norm1_g, norm2_g, w_in, cmp_pos, cmp_k_w1, cmp_k_w2,
           cmp_v_w1, cmp_v_w2, q_norm_g, k_norm_g, w_up_nsa, w_up_sb, w_out, mlp_w1, mlp_w2):
    bsz, seq, d = x.shape
    assert seq % TM == 0 and seq // SEL_BLOCK <= HEAD_DIM and seq >= WINDOW + TQ
    assert CMP_BLOCK == 2 * CMP_STRIDE and TQ == 2 * SEL_BLOCK
    tbl = rel_bias.astype(f32)
    rel_tiles = _bias_tiles(tbl, seq, seq // CMP_STRIDE)
    stacked = dict(norm1_g=norm1_g, norm2_g=norm2_g, w_in=w_in, cmp_pos=cmp_pos,
                   cmp_k_w1=cmp_k_w1, cmp_k_w2=cmp_k_w2, cmp_v_w1=cmp_v_w1, cmp_v_w2=cmp_v_w2,
                   q_norm_g=q_norm_g, k_norm_g=k_norm_g, w_up_nsa=w_up_nsa, w_up_sb=w_up_sb,
                   w_out=w_out, mlp_w1=mlp_w1, mlp_w2=mlp_w2)
    h = x
    for layer in range(ada_w.shape[0]):
        mod = _adaln(c, ada_w[layer], ada_b[layer])
        h = _layer(h, mod, rel_tiles, tbl, {k: v[layer] for k, v in stacked.items()})
    return h
```

```python
import functools
import math

import numpy as np
import jax
import jax.numpy as jnp
from jax import lax
from jax.experimental import pallas as pl
from jax.experimental.pallas import tpu as pltpu

f32 = jnp.float32
bf16 = jnp.bfloat16

HEAD_DIM = 64
NSA_HEADS = 8
NSA_KV_HEADS = 2
NSA_GROUP = NSA_HEADS // NSA_KV_HEADS
SB_HEADS = 8
CMP_BLOCK = 32
CMP_STRIDE = 16
SEL_BLOCK = 64
SEL_TOPK = 16
WINDOW = 512
N_BUCKETS = 32
MAX_DISTANCE = 128
EPS = 1e-6
FORCED_BONUS = 1e4
NEG_BLOCK = -1e9

LANES = 128
MASKED = -1e30
UNSELECTED = -1e9
VMEM_LIMIT = 56 * 1024 * 1024

TQ = 128
TS = 256
TM = 512


def _bucket_thresholds():
    n = np.arange(0, 4 * MAX_DISTANCE)
    max_exact = N_BUCKETS // 2
    nf = np.maximum(n, 1).astype(np.float32)
    large = max_exact + (np.log(nf / max_exact) / math.log(MAX_DISTANCE / max_exact)
                         * (N_BUCKETS - max_exact)).astype(np.int32)
    large = np.minimum(large, N_BUCKETS - 1)
    b = np.where(n < max_exact, n, large)
    assert np.all(np.diff(b) >= 0) and b[-1] == N_BUCKETS - 1
    return [int(np.argmax(b >= k)) for k in range(N_BUCKETS)]


BUCKET_START = _bucket_thresholds()
assert BUCKET_START[-1] <= LANES


def _dot(a, b):
    return jnp.dot(a, b, preferred_element_type=f32)


def _dot_nt(a, b):
    return lax.dot_general(a, b, (((1,), (1,)), ((), ())), preferred_element_type=f32)


def _split(a):
    hi = a.astype(bf16)
    lo = (a - hi.astype(f32)).astype(bf16)
    return hi, lo


def _params(n_grid):
    return pltpu.CompilerParams(dimension_semantics=("arbitrary",) * n_grid,
                                vmem_limit_bytes=VMEM_LIMIT)


def _adaln_kernel(c_ref, w_ref, b_ref, o_ref):
    c = c_ref[...]
    a = c * jax.nn.sigmoid(c)
    ah, al = _split(a)
    wh, wl = _split(w_ref[...])
    o_ref[...] = _dot(ah, wh) + _dot(ah, wl) + _dot(al, wh) + b_ref[...]


def _adaln(c, w, b):
    bsz, d = c.shape
    n = w.shape[1]
    return pl.pallas_call(
        _adaln_kernel,
        grid=(n // d,),
        in_specs=[pl.BlockSpec((bsz, d), lambda j: (0, 0)),
                  pl.BlockSpec((d, d), lambda j: (0, j)),
                  pl.BlockSpec((1, d), lambda j: (0, j))],
        out_specs=pl.BlockSpec((bsz, d), lambda j: (0, j)),
        out_shape=jax.ShapeDtypeStruct((bsz, n), f32),
        compiler_params=_params(1),
        name="adaln",
    )(c, w, b.reshape(1, n))


def _bias_of_dist(dist, tbl_ref, h):
    out = jnp.full(dist.shape, tbl_ref[0, h], f32)
    for k in range(1, N_BUCKETS):
        out = jnp.where(dist >= BUCKET_START[k], tbl_ref[k, h], out)
    return jnp.where(dist >= 0, out, MASKED)


def _cmp_bias_kernel(tbl_ref, o_ref):
    h = pl.program_id(0)
    i = pl.program_id(1)
    rows, nc = o_ref.shape[1], o_ref.shape[2]
    t = i * rows + lax.broadcasted_iota(jnp.int32, (rows, nc), 0)
    j = lax.broadcasted_iota(jnp.int32, (rows, nc), 1)
    o_ref[0] = _bias_of_dist(t - (j * CMP_STRIDE + CMP_BLOCK - 1), tbl_ref, h)


def _near_bias_kernel(tbl_ref, o_ref):
    h = pl.program_id(0)
    r = lax.broadcasted_iota(jnp.int32, (TQ, 2 * TQ), 0)
    c = lax.broadcasted_iota(jnp.int32, (TQ, 2 * TQ), 1)
    o_ref[0] = _bias_of_dist(r - c + TQ, tbl_ref, h)


def _bias_tiles(rel_bias, seq, nc_pad):
    tbl = rel_bias.astype(f32)
    smem = pl.BlockSpec(memory_space=pltpu.SMEM)
    rows = 512
    cmp_bias = pl.pallas_call(
        _cmp_bias_kernel,
        grid=(NSA_HEADS, seq // rows),
        in_specs=[smem],
        out_specs=pl.BlockSpec((1, rows, nc_pad), lambda h, i: (h, i, 0)),
        out_shape=jax.ShapeDtypeStruct((NSA_HEADS, seq, nc_pad), f32),
        compiler_params=_params(2),
        name="cmp_bias",
    )(tbl)
    near_bias = pl.pallas_call(
        _near_bias_kernel,
        grid=(NSA_HEADS,),
        in_specs=[smem],
        out_specs=pl.BlockSpec((1, TQ, 2 * TQ), lambda h: (h, 0, 0)),
        out_shape=jax.ShapeDtypeStruct((NSA_HEADS, TQ, 2 * TQ), f32),
        compiler_params=_params(1),
        name="near_bias",
    )(tbl)
    return cmp_bias, near_bias


_QA_W = NSA_HEADS * LANES
_KV_W = NSA_KV_HEADS * LANES
_CMP_W = NSA_KV_HEADS * HEAD_DIM
_SB_W = SB_HEADS * HEAD_DIM


def _layout(d_model):
    names = ["qa", "kc", "vc", "ksl", "vsl", "kwn", "vwn", "g", "qb", "kb", "vb", "ma", "mb"]
    widths = [_QA_W, _CMP_W, _CMP_W, _KV_W, _KV_W, _KV_W, _KV_W, _KV_W, _SB_W, _SB_W, _SB_W,
              d_model, d_model]
    offs = np.concatenate([[0], np.cumsum(widths)])
    return {n: (int(offs[i]), int(offs[i + 1])) for i, n in enumerate(names)}, int(offs[-1])


def _pack_w_in(w_in, d_model):
    q_w = NSA_HEADS * HEAD_DIM
    kv_w = NSA_KV_HEADS * HEAD_DIM
    g_w = NSA_HEADS * 3
    sizes = [q_w] + [kv_w] * 6 + [g_w] + [_SB_W] * 3 + [d_model, d_model]
    offs = np.concatenate([[0], np.cumsum(sizes)])
    parts = [w_in[:, int(offs[i]):int(offs[i + 1])] for i in range(len(sizes))]
    d = w_in.shape[0]

    def pad_heads(p, n_heads, width):
        p = p.reshape(d, n_heads, width)
        p = jnp.pad(p, ((0, 0), (0, 0), (0, LANES - width)))
        return p.reshape(d, n_heads * LANES)

    packed = [pad_heads(parts[0], NSA_HEADS, HEAD_DIM), parts[1], parts[2]]
    packed += [pad_heads(parts[k], NSA_KV_HEADS, HEAD_DIM) for k in (3, 4, 5, 6)]
    packed += [pad_heads(parts[7], NSA_KV_HEADS, NSA_GROUP * 3)]
    packed += parts[8:]
    return jnp.concatenate(packed, axis=1).astype(bf16)


def _inproj_kernel(lay, x_ref, g1_ref, sc_ref, sh_ref, w_ref, gsum_ref, gq_ref, gk_ref,
                   qa_ref, kc_ref, vc_ref, ks_ref, vs_ref, kw_ref, vw_ref, g_ref,
                   qb_ref, kb_ref, vb_ref, ma_ref, mb_ref):
    i = pl.program_id(1)
    x = x_ref[0]
    ms = jnp.mean(x * x, axis=-1, keepdims=True)
    u = (x * lax.rsqrt(ms + EPS) * g1_ref[...]) * (1.0 + sc_ref[0]) + sh_ref[0]
    ub = u.astype(bf16)

    def proj(name):
        lo, hi = lay[name]
        return _dot(ub, w_ref[:, lo:hi])

    def head_norm(z, gain):
        n = z.shape[1]
        hi, lo = _split(z * z)
        g = gsum_ref[:n, :n]
        ss = _dot(hi, g) + _dot(lo, g)
        return z * lax.rsqrt(ss * (1.0 / HEAD_DIM) + EPS) * gain

    scale = HEAD_DIM ** -0.5
    qa_ref[0] = (head_norm(proj("qa"), gq_ref[...]) * scale).astype(bf16)
    kc_ref[0] = proj("kc")
    vc_ref[0] = proj("vc")

    rows = x.shape[0]
    lane = lax.broadcasted_iota(jnp.int32, (rows, _KV_W), 1)
    tok_blk = (i * rows + lax.broadcasted_iota(jnp.int32, (rows, _KV_W), 0)) // SEL_BLOCK
    in_pad = (lane & HEAD_DIM) != 0
    onehot = in_pad & ((lane & (HEAD_DIM - 1)) == tok_blk)
    ones_col = (lane & (LANES - 1)) == HEAD_DIM

    ks = head_norm(proj("ksl"), gk_ref[1:2, :])
    ks_ref[0] = jnp.where(onehot, 1.0, ks).astype(bf16)
    vs_ref[0] = jnp.where(ones_col, 1.0, proj("vsl")).astype(bf16)
    kw_ref[0] = head_norm(proj("kwn"), gk_ref[2:3, :]).astype(bf16)
    vw_ref[0] = jnp.where(ones_col, 1.0, proj("vwn")).astype(bf16)
    g_ref[0] = jax.nn.sigmoid(proj("g"))
    qb_ref[0] = (proj("qb") * scale).astype(bf16)
    kb_ref[0] = proj("kb").astype(bf16)
    vb_ref[0] = proj("vb").astype(bf16)
    ma_ref[0] = jax.nn.sigmoid(proj("ma")).astype(bf16)
    mb_ref[0] = jax.nn.sigmoid(proj("mb")).astype(bf16)


def _inproj(x, g1, scale1, shift1, w_packed, gsum, gq, gk):
    bsz, seq, d = x.shape
    lay, width = _layout(d)
    assert w_packed.shape == (d, width)
    tok = lambda w: pl.BlockSpec((1, TM, w), lambda b, i: (b, i, 0))
    full = lambda a: pl.BlockSpec(a.shape, lambda b, i: (0,) * a.ndim,
                                  pipeline_mode=pl.Buffered(1))
    mod = pl.BlockSpec((1, 1, d), lambda b, i: (b, 0, 0))
    out_w =[(_QA_W, bf16), (_CMP_W, f32), (_CMP_W, f32), (_KV_W, bf16), (_KV_W, bf16),
             (_KV_W, bf16), (_KV_W, bf16), (_KV_W, f32), (_SB_W, bf16), (_SB_W, bf16),
             (_SB_W, bf16), (d, bf16), (d, bf16)]
    return pl.pallas_call(
        functools.partial(_inproj_kernel, lay),
        grid=(bsz, seq // TM),
        in_specs=[tok(d), full(g1), mod, mod, full(w_packed), full(gsum), full(gq), full(gk)],
        out_specs=[tok(w) for w, _ in out_w],
        out_shape=[jax.ShapeDtypeStruct((bsz, seq, w), dt) for w, dt in out_w],
        compiler_params=_params(2),
        name="inproj",
    )(x, g1, scale1, shift1, w_packed, gsum, gq, gk)


def _compress_kernel(xk_ref, xv_ref, pos_ref, w1k_ref, w2k_ref, w1v_ref, w2v_ref, gk_ref,
                     ko_ref, vo_ref):
    half = pos_ref.shape[1]

    def mlp(x, w1_ref, w2_ref):
        xa = (x + pos_ref[0:1, :]).astype(bf16)
        xb = (x + pos_ref[1:2, :]).astype(bf16)
        first = _dot(xa, w1_ref[:half, :])
        second = _dot(xb, w1_ref[half:, :])
        n = x.shape[0]
        pre = first + pltpu.roll(second, n - 1, 0)
        hid = pre * jax.nn.sigmoid(pre)
        return _dot(hid.astype(bf16), w2_ref[...])

    for h in range(NSA_KV_HEADS):
        k = mlp(xk_ref[0, h], w1k_ref, w2k_ref)
        ms = jnp.mean(k * k, axis=-1, keepdims=True)
        ko_ref[0, h] = (k * lax.rsqrt(ms + EPS) * gk_ref[...]).astype(bf16)
        vo_ref[0, h] = mlp(xv_ref[0, h], w1v_ref, w2v_ref).astype(bf16)


def _compress(xk, xv, pos2, w1k, w2k, w1v, w2v, gk0):
    bsz, hk, nch, width = xk.shape
    blk = pl.BlockSpec((1, hk, nch, width), lambda b: (b, 0, 0, 0))
    full = lambda a: pl.BlockSpec(a.shape, lambda b: (0,) * a.ndim)
    out = pl.BlockSpec((1, hk, nch, HEAD_DIM), lambda b: (b, 0, 0, 0))
    shape = jax.ShapeDtypeStruct((bsz, hk, nch, HEAD_DIM), bf16)
    return pl.pallas_call(
        _compress_kernel,
        grid=(bsz,),
        in_specs=[blk, blk, full(pos2), full(w1k), full(w2k), full(w1v), full(w2v), full(gk0)],
        out_specs=[out, out],
        out_shape=[shape, shape],
        compiler_params=_params(1),
        name="compress",
    )(xk, xv, pos2, w1k, w2k, w1v, w2v, gk0)


def _flash_step(s, v, m_ref, acc_ref):
    m_prev = m_ref[...]
    m_new = jnp.maximum(m_prev, jnp.max(s, axis=1, keepdims=True))
    alpha = jnp.exp(m_prev - m_new)
    p = jnp.exp(s - jnp.concatenate([m_new] * (s.shape[1] // LANES), axis=1))
    acc_ref[...] = alpha * acc_ref[...] + _dot(p.astype(bf16), v)
    m_ref[...] = m_new


def _flash_init(m_ref, acc_ref):
    m_ref[...] = jnp.full(m_ref.shape, MASKED, f32)
    acc_ref[...] = jnp.zeros(acc_ref.shape, f32)


def _flash_out(acc_ref):
    acc = acc_ref[...]
    return acc[:, :HEAD_DIM] / acc[:, HEAD_DIM:HEAD_DIM + 1]


def _nsa_kernel(n_top, tbl_ref, q_ref, g_ref, kc_ref, vc_ref, ks_ref, vs_ref, kw_ref, vw_ref,
                bc_ref, bn_ref, ovt_ref, o_ref, qs_ref, qw_ref, b31_ref, m_ref, acc_ref):
    h = pl.program_id(1)
    i = pl.program_id(2)
    rows = NSA_GROUP * TQ
    start = i * TQ

    qpad = jnp.concatenate([q_ref[0, :, g * LANES:(g + 1) * LANES] for g in range(NSA_GROUP)],
                           axis=0)
    b31_ref[...] = jnp.concatenate(
        [jnp.full((TQ, LANES), tbl_ref[N_BUCKETS - 1, h * NSA_GROUP + g], f32)
         for g in range(NSA_GROUP)], axis=0)

    bc = bc_ref[...].reshape(rows, bc_ref.shape[2])
    s_c = _dot_nt(qpad[:, :HEAD_DIM], kc_ref[0, 0]) + bc
    visible = bc > 0.5 * MASKED
    m_c = jnp.max(s_c, axis=1, keepdims=True)
    e_c = jnp.where(visible, jnp.exp(s_c - m_c), 0.0)
    p_c = e_c / jnp.maximum(jnp.sum(e_c, axis=1, keepdims=True), 1e-30)
    o_c = _dot(p_c.astype(bf16), vc_ref[0, 0])

    p_sum = p_c[0:TQ]
    for g in range(1, NSA_GROUP):
        p_sum = p_sum + p_c[g * TQ:(g + 1) * TQ]
    p_hi, p_lo = _split(p_sum)
    imp = _dot_nt(ovt_ref[...], p_hi) + _dot_nt(ovt_ref[...], p_lo)
    nblk = imp.shape[0]
    blk = lax.broadcasted_iota(jnp.int32, (nblk, TQ), 0)
    cur = (start + lax.broadcasted_iota(jnp.int32, (nblk, TQ), 1)) // SEL_BLOCK
    forced = (blk == 0) | (blk == cur) | (blk == cur - 1)
    imp = jnp.where(blk > cur, NEG_BLOCK, imp + jnp.where(forced, FORCED_BONUS, 0.0))
    rank = jnp.zeros((nblk, TQ), f32)
    for b2 in range(nblk):
        other = imp[b2:b2 + 1, :]
        rank = rank + jnp.where(blk > b2, jnp.where(other >= imp, 1.0, 0.0),
                                jnp.where(other > imp, 1.0, 0.0))
    usable = (rank < n_top) & (blk <= cur)
    sel_t = jnp.where(usable, 0.0, UNSELECTED)
    sel_pad = jnp.concatenate([jnp.zeros((LANES - nblk, TQ), f32), sel_t], axis=0).T
    sel_rows = jnp.concatenate([sel_pad.astype(bf16)] * NSA_GROUP, axis=0)
    qs_ref[...] = qpad + sel_rows

    qw_ref[...] = qpad

    def last_bucket(width):
        return jnp.concatenate([b31_ref[...]] * (width // LANES), axis=1)

    def attend(q_rows_ref, k_ref, v_ref, off, width, bias):
        off = pl.multiple_of(off, TQ)
        s = _dot_nt(q_rows_ref[...], k_ref[0, pl.ds(off, width), :]) + bias
        _flash_step(s, v_ref[0, pl.ds(off, width), :], m_ref, acc_ref)

    def near_steps(q_rows_ref, k_ref, v_ref):
        @pl.when(i >= 1)
        def _():
            attend(q_rows_ref, k_ref, v_ref, start - TQ, 2 * TQ,
                   bn_ref[...].reshape(rows, 2 * TQ))

        @pl.when(i == 0)
        def _():
            attend(q_rows_ref, k_ref, v_ref, start, TQ, bn_ref[:, :, TQ:].reshape(rows, TQ))

    _flash_init(m_ref, acc_ref)
    n_far = jnp.maximum(i - 1, 0)

    def far_quad(c, carry):
        off = c * (4 * TQ)
        attend(qs_ref, ks_ref, vs_ref, off, 2 * TQ, last_bucket(2 * TQ))
        attend(qs_ref, ks_ref, vs_ref, off + 2 * TQ, 2 * TQ, last_bucket(2 * TQ))
        return carry

    lax.fori_loop(0, n_far // 4, far_quad, 0)
    rem_off = (n_far // 4) * (4 * TQ)

    @pl.when((n_far & 2) != 0)
    def _():
        attend(qs_ref, ks_ref, vs_ref, rem_off, 2 * TQ, last_bucket(2 * TQ))

    @pl.when((n_far & 1) != 0)
    def _():
        attend(qs_ref, ks_ref, vs_ref, rem_off + (n_far & 2) * TQ, TQ, last_bucket(TQ))

    near_steps(qs_ref, ks_ref, vs_ref)
    o_s = _flash_out(acc_ref)

    _flash_init(m_ref, acc_ref)
    assert WINDOW == 4 * TQ

    @pl.when(i >= 4)
    def _():
        r = lax.broadcasted_iota(jnp.int32, (rows, TQ), 0) & (TQ - 1)
        c = lax.broadcasted_iota(jnp.int32, (rows, TQ), 1)
        attend(qw_ref, kw_ref, vw_ref, start - 4 * TQ, TQ,
               last_bucket(TQ) + jnp.where(c > r, 0.0, MASKED))

    @pl.when(i >= 3)
    def _():
        attend(qw_ref, kw_ref, vw_ref, start - 3 * TQ, 2 * TQ, last_bucket(2 * TQ))

    @pl.when(i == 2)
    def _():
        attend(qw_ref, kw_ref, vw_ref, start - 2 * TQ, TQ, last_bucket(TQ))

    near_steps(qw_ref, kw_ref, vw_ref)
    o_w = _flash_out(acc_ref)

    gates = g_ref[0]
    outs = []
    for g in range(NSA_GROUP):
        sl = slice(g * TQ, (g + 1) * TQ)
        outs.append(gates[:, 3 * g:3 * g + 1] * o_c[sl]
                    + gates[:, 3 * g + 1:3 * g + 2] * o_s[sl]
                    + gates[:, 3 * g + 2:3 * g + 3] * o_w[sl])
    o_ref[0] = jnp.concatenate(outs, axis=1).astype(bf16)


def _nsa(tbl, qa, gates, kcmp, vcmp, ks, vs, kw, vw, cmp_bias, near_bias, ovt):
    bsz, seq, _ = qa.shape
    nc = kcmp.shape[2]
    n_top = min(SEL_TOPK, seq // SEL_BLOCK)
    grp_w = NSA_GROUP * LANES
    rows = NSA_GROUP * TQ
    kv = pl.BlockSpec((1, seq, LANES), lambda b, h, i: (b, 0, h))
    cmp = pl.BlockSpec((1, 1, nc, HEAD_DIM), lambda b, h, i: (b, h, 0, 0))
    return pl.pallas_call(
        functools.partial(_nsa_kernel, n_top),
        grid=(bsz, NSA_KV_HEADS, seq // TQ),
        in_specs=[pl.BlockSpec(memory_space=pltpu.SMEM),
                  pl.BlockSpec((1, TQ, grp_w), lambda b, h, i: (b, i, h)),
                  pl.BlockSpec((1, TQ, LANES), lambda b, h, i: (b, i, h)),
                  cmp, cmp, kv, kv, kv, kv,
                  pl.BlockSpec((NSA_GROUP, TQ, nc), lambda b, h, i: (h, i, 0)),
                  pl.BlockSpec((NSA_GROUP, TQ, 2 * TQ), lambda b, h, i: (h, 0, 0)),
                  pl.BlockSpec(ovt.shape, lambda b, h, i: (0, 0))],
        out_specs=pl.BlockSpec((1, TQ, NSA_GROUP * HEAD_DIM), lambda b, h, i: (b, i, h)),
        out_shape=jax.ShapeDtypeStruct((bsz, seq, NSA_HEADS * HEAD_DIM), bf16),
        scratch_shapes=[pltpu.VMEM((rows, LANES), bf16),
                        pltpu.VMEM((rows, LANES), bf16),
                        pltpu.VMEM((rows, LANES), f32),
                        pltpu.VMEM((rows, LANES), f32),
                        pltpu.VMEM((rows, LANES), f32)],
        compiler_params=_params(3),
        name="nsa",
    )(tbl, qa, gates, kcmp, vcmp, ks, vs, kw, vw, cmp_bias, near_bias, ovt)


def _sb_kernel(q_ref, k_ref, v_ref, tri_ref, o_ref, carry_ref, acc_ref):
    i = pl.program_id(2)
    q = q_ref[0]
    lane = lax.broadcasted_iota(jnp.int32, q.shape, 1)
    zero = jnp.zeros_like(q)
    q_heads = (jnp.where(lane < HEAD_DIM, q, zero), jnp.where(lane >= HEAD_DIM, q, zero))
    carry_ref[...] = jnp.zeros(carry_ref.shape, f32)
    acc_ref[...] = jnp.zeros(acc_ref.shape, f32)
    r = lax.broadcasted_iota(jnp.int32, (TS, TS), 0)
    c = lax.broadcasted_iota(jnp.int32, (TS, TS), 1)
    before = c < r

    def chunk(off, diagonal):
        k = k_ref[0, pl.ds(off, TS), :]
        v = v_ref[0, pl.ds(off, TS), :]
        for hh in range(2):
            z = _dot_nt(q_heads[hh], k)
            soft = jnp.log(1.0 + jnp.exp(-jnp.abs(z)))
            log_keep = jnp.minimum(-z, 0.0) - soft
            log_sig = jnp.minimum(z, 0.0) - soft
            if diagonal:
                log_keep = jnp.where(before, log_keep, 0.0)
            hi, lo = _split(log_keep)
            later = _dot(hi, tri_ref[...]) + _dot(lo, tri_ref[...])
            carry = carry_ref[hh]
            logit = log_sig + later + jnp.concatenate([carry] * (TS // LANES), axis=1)
            a = jnp.exp(logit)
            if diagonal:
                a = jnp.where(before, a, 0.0)
            acc_ref[hh] = acc_ref[hh] + _dot(a.astype(bf16), v)
            carry_ref[hh] = carry + jnp.sum(log_keep, axis=1, keepdims=True)

    chunk(pl.multiple_of(i * TS, TS), True)

    def body(n, carry):
        chunk(pl.multiple_of((i - 1 - n) * TS, TS), False)
        return carry

    lax.fori_loop(0, i, body, 0)
    o_ref[0] = jnp.where(lane < HEAD_DIM, acc_ref[0], acc_ref[1]).astype(bf16)


def _sb(qb, kb, vb, tri):
    bsz, seq, width = qb.shape
    q_spec = pl.BlockSpec((1, TS, LANES), lambda b, p, i: (b, i, p))
    kv_spec = pl.BlockSpec((1, seq, LANES), lambda b, p, i: (b, 0, p))
    return pl.pallas_call(
        _sb_kernel,
        grid=(bsz, width // LANES, seq // TS),
        in_specs=[q_spec, kv_spec, kv_spec, pl.BlockSpec(tri.shape, lambda b, p, i: (0, 0))],
        out_specs=q_spec,
        out_shape=jax.ShapeDtypeStruct((bsz, seq, width), bf16),
        scratch_shapes=[pltpu.VMEM((2, TS, LANES), f32), pltpu.VMEM((2, TS, LANES), f32)],
        compiler_params=_params(3),
        name="sb",
    )(qb, kb, vb, tri)


def _merge_kernel(ya_ref, yb_ref, ma_ref, mb_ref, x_ref, gate_ref, wa_ref, wb_ref, wo_ref, o_ref):
    y_a = _dot(ya_ref[0], wa_ref[...])
    y_b = _dot(yb_ref[0], wb_ref[...])
    mixed = ma_ref[0].astype(f32) * y_a + mb_ref[0].astype(f32) * y_b
    o_ref[0] = x_ref[0] + gate_ref[0] * _dot(mixed.astype(bf16), wo_ref[...])


def _merge(ya, yb, ma, mb, x, gate1, wa, wb, wo):
    bsz, seq, d = x.shape
    tok = lambda w: pl.BlockSpec((1, TM, w), lambda b, i: (b, i, 0))
    full = lambda a: pl.BlockSpec(a.shape, lambda b, i: (0,) * a.ndim)
    return pl.pallas_call(
        _merge_kernel,
        grid=(bsz, seq // TM),
        in_specs=[tok(ya.shape[2]), tok(yb.shape[2]), tok(d), tok(d), tok(d),
                  pl.BlockSpec((1, 1, d), lambda b, i: (b, 0, 0)), full(wa), full(wb), full(wo)],
        out_specs=tok(d),
        out_shape=jax.ShapeDtypeStruct((bsz, seq, d), f32),
        compiler_params=_params(2),
        name="merge",
    )(ya, yb, ma, mb, x, gate1, wa, wb, wo)


def _mlp_kernel(h_ref, g2_ref, sc_ref, sh_ref, gate_ref, w1_ref, w2_ref, o_ref):
    hres = h_ref[0]
    d = hres.shape[1]
    ms = jnp.mean(hres * hres, axis=-1, keepdims=True)
    u = (hres * lax.rsqrt(ms + EPS) * g2_ref[...]) * (1.0 + sc_ref[0]) + sh_ref[0]
    ub = u.astype(bf16)
    ff = jnp.zeros(hres.shape, f32)
    for c in range(w1_ref.shape[1] // d):
        hid = jnp.maximum(_dot(ub, w1_ref[:, c * d:(c + 1) * d]), 0.0)
        ff = ff + _dot((hid * hid).astype(bf16), w2_ref[c * d:(c + 1) * d, :])
    o_ref[0] = hres + gate_ref[0] * ff


def _mlp(hres, g2, scale2, shift2, gate2, w1, w2):
    bsz, seq, d = hres.shape
    tok = pl.BlockSpec((1, TM, d), lambda b, i: (b, i, 0))
    mod = pl.BlockSpec((1, 1, d), lambda b, i: (b, 0, 0))
    const = lambda a: pl.BlockSpec(a.shape, lambda b, i: (0,) * a.ndim,
                                   pipeline_mode=pl.Buffered(1))
    return pl.pallas_call(
        _mlp_kernel,
        grid=(bsz, seq // TM),
        in_specs=[tok, pl.BlockSpec(g2.shape, lambda b, i: (0, 0)), mod, mod, mod,
                  const(w1), const(w2)],
        out_specs=tok,
        out_shape=jax.ShapeDtypeStruct((bsz, seq, d), f32),
        compiler_params=_params(2),
        name="mlp",
    )(hres, g2, scale2, shift2, gate2, w1, w2)


def _overlap_t(nc_pad, nsel_pad, nc, nsel):
    c_start = np.arange(nc_pad) * CMP_STRIDE
    s_start = np.arange(nsel_pad) * SEL_BLOCK
    ov = (np.minimum(c_start[None, :] + CMP_BLOCK, s_start[:, None] + SEL_BLOCK)
          - np.maximum(c_start[None, :], s_start[:, None]))
    ov = np.clip(ov, 0, CMP_BLOCK).astype(np.float32) / CMP_BLOCK
    ov[nsel:, :] = 0.0
    ov[:, nc:] = 0.0
    return ov


def _layer(h, mod, rel_tiles, tbl, p):
    bsz, seq, d = h.shape
    shift1, scale1, gate1, shift2, scale2, gate2 = [
        mod[:, k * d:(k + 1) * d].reshape(bsz, 1, d) for k in range(6)]
    cmp_bias, near_bias = rel_tiles

    group_ones = np.kron(np.eye(_QA_W // HEAD_DIM), np.ones((HEAD_DIM, HEAD_DIM)))
    group_ones = group_ones * (np.arange(_QA_W)[None, :] % LANES < HEAD_DIM)
    gsum = jnp.asarray(group_ones, bf16)
    pad_gain = lambda g, n: jnp.tile(jnp.pad(g, (0, LANES - HEAD_DIM)), n).reshape(1, n * LANES)
    gq = pad_gain(p["q_norm_g"], NSA_HEADS)
    gk = jnp.concatenate([pad_gain(p["k_norm_g"][k], NSA_KV_HEADS) for k in range(3)], axis=0)

    (qa, kc, vc, ks, vs, kw, vw, gates, qb, kb, vb, ma, mb) = _inproj(
        h, p["norm1_g"].reshape(1, d), scale1, shift1, _pack_w_in(p["w_in"], d), gsum, gq, gk)

    nch = seq // CMP_STRIDE
    chunks = lambda a: a.reshape(bsz, nch, CMP_STRIDE, NSA_KV_HEADS, HEAD_DIM).transpose(
        0, 3, 1, 2, 4).reshape(bsz, NSA_KV_HEADS, nch, CMP_STRIDE * HEAD_DIM)
    pos2 = p["cmp_pos"].reshape(2, CMP_STRIDE * HEAD_DIM)
    kcmp, vcmp = _compress(chunks(kc), chunks(vc), pos2,
                           p["cmp_k_w1"].astype(bf16), p["cmp_k_w2"].astype(bf16),
                           p["cmp_v_w1"].astype(bf16), p["cmp_v_w2"].astype(bf16),
                           p["k_norm_g"][0].reshape(1, HEAD_DIM))

    nc = (seq - CMP_BLOCK) // CMP_STRIDE + 1
    ovt = jnp.asarray(_overlap_t(nch, HEAD_DIM, nc, seq // SEL_BLOCK), bf16)
    y_nsa = _nsa(tbl, qa, gates, kcmp, vcmp, ks, vs, kw, vw, cmp_bias, near_bias, ovt)

    y_sb = _sb(qb, kb, vb, jnp.asarray(np.tril(np.ones((TS, TS)), -1), bf16))

    h1 = _merge(y_nsa, y_sb, ma, mb, h, gate1, p["w_up_nsa"].astype(bf16),
                p["w_up_sb"].astype(bf16), p["w_out"].astype(bf16))
    return _mlp(h1, p["norm2_g"].reshape(1, d), scale2, shift2, gate2,
                p["mlp_w1"].astype(bf16), p["mlp_w2"].astype(bf16))


def kernel(x, c, rel_bias, ada_w, ada_b, norm1_g, norm2_g, w_in, cmp_pos, cmp_k_w1, cmp_k_w2,
           cmp_v_w1, cmp_v_w2, q_norm_g, k_norm_g, w_up_nsa, w_up_sb, w_out, mlp_w1, mlp_w2):
    bsz, seq, d = x.shape
    assert seq % TM == 0 and seq // SEL_BLOCK <= HEAD_DIM and seq >= WINDOW + TQ
    assert CMP_BLOCK == 2 * CMP_STRIDE and TQ == 2 * SEL_BLOCK
    tbl = rel_bias.astype(f32)
    rel_tiles = _bias_tiles(tbl, seq, seq // CMP_STRIDE)
    stacked = dict(norm1_g=norm1_g, norm2_g=norm2_g, w_in=w_in, cmp_pos=cmp_pos,
                   cmp_k_w1=cmp_k_w1, cmp_k_w2=cmp_k_w2, cmp_v_w1=cmp_v_w1, cmp_v_w2=cmp_v_w2,
                   q_norm_g=q_norm_g, k_norm_g=k_norm_g, w_up_nsa=w_up_nsa, w_up_sb=w_up_sb,
                   w_out=w_out, mlp_w1=mlp_w1, mlp_w2=mlp_w2)
    h = x
    for layer in range(ada_w.shape[0]):
        mod = _adaln(c, ada_w[layer], ada_b[layer])
        h = _layer(h, mod, rel_tiles, tbl, {k: v[layer] for k, v in stacked.items()})
    return h
```

```python
import functools
import math

import numpy as np
import jax
import jax.numpy as jnp
from jax import lax
from jax.experimental import pallas as pl
from jax.experimental.pallas import tpu as pltpu

f32 = jnp.float32
bf16 = jnp.bfloat16

HEAD_DIM = 64
NSA_HEADS = 8
NSA_KV_HEADS = 2
NSA_GROUP = NSA_HEADS // NSA_KV_HEADS
SB_HEADS = 8
CMP_BLOCK = 32
CMP_STRIDE = 16
SEL_BLOCK = 64
SEL_TOPK = 16
WINDOW = 512
N_BUCKETS = 32
MAX_DISTANCE = 128
EPS = 1e-6
FORCED_BONUS = 1e4
NEG_BLOCK = -1e9

LANES = 128
MASKED = -1e30
UNSELECTED = -1e9
EXP_UNDERFLOW = -104.0
VMEM_LIMIT = 56 * 1024 * 1024

TQ = 128
TS = 256
TM = 512


def _bucket_thresholds():
    n = np.arange(0, 4 * MAX_DISTANCE)
    max_exact = N_BUCKETS // 2
    nf = np.maximum(n, 1).astype(np.float32)
    large = max_exact + (np.log(nf / max_exact) / math.log(MAX_DISTANCE / max_exact)
                         * (N_BUCKETS - max_exact)).astype(np.int32)
    large = np.minimum(large, N_BUCKETS - 1)
    b = np.where(n < max_exact, n, large)
    assert np.all(np.diff(b) >= 0) and b[-1] == N_BUCKETS - 1
    return [int(np.argmax(b >= k)) for k in range(N_BUCKETS)]


BUCKET_START = _bucket_thresholds()
assert BUCKET_START[-1] <= LANES


def _dot(a, b):
    return jnp.dot(a, b, preferred_element_type=f32)


def _dot_nt(a, b):
    return lax.dot_general(a, b, (((1,), (1,)), ((), ())), preferred_element_type=f32)


def _split(a):
    hi = a.astype(bf16)
    lo = (a - hi.astype(f32)).astype(bf16)
    return hi, lo


def _params(n_grid):
    return pltpu.CompilerParams(dimension_semantics=("arbitrary",) * n_grid,
                                vmem_limit_bytes=VMEM_LIMIT)


def _adaln_kernel(c_ref, w_ref, b_ref, o_ref):
    c = c_ref[...]
    a = c * jax.nn.sigmoid(c)
    ah, al = _split(a)
    wh, wl = _split(w_ref[...])
    o_ref[...] = _dot(ah, wh) + _dot(ah, wl) + _dot(al, wh) + b_ref[...]


def _adaln(c, w, b):
    bsz, d = c.shape
    n = w.shape[1]
    return pl.pallas_call(
        _adaln_kernel,
        grid=(n // d,),
        in_specs=[pl.BlockSpec((bsz, d), lambda j: (0, 0)),
                  pl.BlockSpec((d, d), lambda j: (0, j)),
                  pl.BlockSpec((1, d), lambda j: (0, j))],
        out_specs=pl.BlockSpec((bsz, d), lambda j: (0, j)),
        out_shape=jax.ShapeDtypeStruct((bsz, n), f32),
        compiler_params=_params(1),
        name="adaln",
    )(c, w, b.reshape(1, n))


def _bias_of_dist(dist, tbl_ref, h):
    out = jnp.full(dist.shape, tbl_ref[0, h], f32)
    for k in range(1, N_BUCKETS):
        out = jnp.where(dist >= BUCKET_START[k], tbl_ref[k, h], out)
    return jnp.where(dist >= 0, out, MASKED)


def _cmp_bias_kernel(tbl_ref, o_ref):
    h = pl.program_id(0)
    i = pl.program_id(1)
    rows, nc = o_ref.shape[1], o_ref.shape[2]
    t = i * rows + lax.broadcasted_iota(jnp.int32, (rows, nc), 0)
    j = lax.broadcasted_iota(jnp.int32, (rows, nc), 1)
    o_ref[0] = _bias_of_dist(t - (j * CMP_STRIDE + CMP_BLOCK - 1), tbl_ref, h)


def _near_bias_kernel(tbl_ref, o_ref):
    h = pl.program_id(0)
    r = lax.broadcasted_iota(jnp.int32, (TQ, 2 * TQ), 0)
    c = lax.broadcasted_iota(jnp.int32, (TQ, 2 * TQ), 1)
    o_ref[0] = _bias_of_dist(r - c + TQ, tbl_ref, h)


def _bias_tiles(rel_bias, seq, nc_pad):
    tbl = rel_bias.astype(f32)
    smem = pl.BlockSpec(memory_space=pltpu.SMEM)
    rows = 512
    cmp_bias = pl.pallas_call(
        _cmp_bias_kernel,
        grid=(NSA_HEADS, seq // rows),
        in_specs=[smem],
        out_specs=pl.BlockSpec((1, rows, nc_pad), lambda h, i: (h, i, 0)),
        out_shape=jax.ShapeDtypeStruct((NSA_HEADS, seq, nc_pad), f32),
        compiler_params=_params(2),
        name="cmp_bias",
    )(tbl)
    near_bias = pl.pallas_call(
        _near_bias_kernel,
        grid=(NSA_HEADS,),
        in_specs=[smem],
        out_specs=pl.BlockSpec((1, TQ, 2 * TQ), lambda h: (h, 0, 0)),
        out_shape=jax.ShapeDtypeStruct((NSA_HEADS, TQ, 2 * TQ), f32),
        compiler_params=_params(1),
        name="near_bias",
    )(tbl)
    return cmp_bias, near_bias


_QA_W = NSA_HEADS * LANES
_KV_W = NSA_KV_HEADS * LANES
_CMP_W = NSA_KV_HEADS * HEAD_DIM
_SB_W = SB_HEADS * HEAD_DIM


def _layout(d_model):
    names = ["qa", "kc", "vc", "ksl", "vsl", "kwn", "vwn", "g", "qb", "kb", "vb", "ma", "mb"]
    widths = [_QA_W, _CMP_W, _CMP_W, _KV_W, _KV_W, _KV_W, _KV_W, _KV_W, _SB_W, _SB_W, _SB_W,
              d_model, d_model]
    offs = np.concatenate([[0], np.cumsum(widths)])
    return {n: (int(offs[i]), int(offs[i + 1])) for i, n in enumerate(names)}, int(offs[-1])


def _pack_w_in(w_in, d_model):
    q_w = NSA_HEADS * HEAD_DIM
    kv_w = NSA_KV_HEADS * HEAD_DIM
    g_w = NSA_HEADS * 3
    sizes = [q_w] + [kv_w] * 6 + [g_w] + [_SB_W] * 3 + [d_model, d_model]
    offs = np.concatenate([[0], np.cumsum(sizes)])
    parts = [w_in[:, int(offs[i]):int(offs[i + 1])] for i in range(len(sizes))]
    d = w_in.shape[0]

    def pad_heads(p, n_heads, width):
        p = p.reshape(d, n_heads, width)
        p = jnp.pad(p, ((0, 0), (0, 0), (0, LANES - width)))
        return p.reshape(d, n_heads * LANES)

    packed = [pad_heads(parts[0], NSA_HEADS, HEAD_DIM), parts[1], parts[2]]
    packed += [pad_heads(parts[k], NSA_KV_HEADS, HEAD_DIM) for k in (3, 4, 5, 6)]
    packed += [pad_heads(parts[7], NSA_KV_HEADS, NSA_GROUP * 3)]
    packed += parts[8:]
    return jnp.concatenate(packed, axis=1).astype(bf16)


def _inproj_kernel(lay, x_ref, g1_ref, sc_ref, sh_ref, w_ref, gsum_ref, gq_ref, gk_ref,
                   qa_ref, kc_ref, vc_ref, ks_ref, vs_ref, kw_ref, vw_ref, g_ref,
                   qb_ref, kb_ref, vb_ref, ma_ref, mb_ref):
    i = pl.program_id(1)
    x = x_ref[0]
    ms = jnp.mean(x * x, axis=-1, keepdims=True)
    u = (x * lax.rsqrt(ms + EPS) * g1_ref[...]) * (1.0 + sc_ref[0]) + sh_ref[0]
    ub = u.astype(bf16)

    def proj(name):
        lo, hi = lay[name]
        return _dot(ub, w_ref[:, lo:hi])

    def head_norm(z, gain):
        n = z.shape[1]
        hi, lo = _split(z * z)
        g = gsum_ref[:n, :n]
        ss = _dot(hi, g) + _dot(lo, g)
        return z * lax.rsqrt(ss * (1.0 / HEAD_DIM) + EPS) * gain

    scale = HEAD_DIM ** -0.5
    qa_ref[0] = (head_norm(proj("qa"), gq_ref[...]) * scale).astype(bf16)
    kc_ref[0] = proj("kc")
    vc_ref[0] = proj("vc")

    rows = x.shape[0]
    lane = lax.broadcasted_iota(jnp.int32, (rows, _KV_W), 1)
    tok_blk = (i * rows + lax.broadcasted_iota(jnp.int32, (rows, _KV_W), 0)) // SEL_BLOCK
    in_pad = (lane & HEAD_DIM) != 0
    onehot = in_pad & ((lane & (HEAD_DIM - 1)) == tok_blk)
    ones_col = (lane & (LANES - 1)) == HEAD_DIM

    ks = head_norm(proj("ksl"), gk_ref[1:2, :])
    ks_ref[0] = jnp.where(onehot, 1.0, ks).astype(bf16)
    vs_ref[0] = jnp.where(ones_col, 1.0, proj("vsl")).astype(bf16)
    kw_ref[0] = head_norm(proj("kwn"), gk_ref[2:3, :]).astype(bf16)
    vw_ref[0] = jnp.where(ones_col, 1.0, proj("vwn")).astype(bf16)
    g_ref[0] = jax.nn.sigmoid(proj("g"))
    qb_ref[0] = (proj("qb") * scale).astype(bf16)
    kb_ref[0] = proj("kb").astype(bf16)
    vb_ref[0] = proj("vb").astype(bf16)
    ma_ref[0] = jax.nn.sigmoid(proj("ma")).astype(bf16)
    mb_ref[0] = jax.nn.sigmoid(proj("mb")).astype(bf16)


def _inproj(x, g1, scale1, shift1, w_packed, gsum, gq, gk):
    bsz, seq, d = x.shape
    lay, width = _layout(d)
    assert w_packed.shape == (d, width)
    tok = lambda w: pl.BlockSpec((1, TM, w), lambda b, i: (b, i, 0))
    full = lambda a: pl.BlockSpec(a.shape, lambda b, i: (0,) * a.ndim,
                                  pipeline_mode=pl.Buffered(1))
    mod = pl.BlockSpec((1, 1, d), lambda b, i: (b, 0, 0))
    out_w =[(_QA_W, bf16), (_CMP_W, f32), (_CMP_W, f32), (_KV_W, bf16), (_KV_W, bf16),
             (_KV_W, bf16), (_KV_W, bf16), (_KV_W, f32), (_SB_W, bf16), (_SB_W, bf16),
             (_SB_W, bf16), (d, bf16), (d, bf16)]
    return pl.pallas_call(
        functools.partial(_inproj_kernel, lay),
        grid=(bsz, seq // TM),
        in_specs=[tok(d), full(g1), mod, mod, full(w_packed), full(gsum), full(gq), full(gk)],
        out_specs=[tok(w) for w, _ in out_w],
        out_shape=[jax.ShapeDtypeStruct((bsz, seq, w), dt) for w, dt in out_w],
        compiler_params=_params(2),
        name="inproj",
    )(x, g1, scale1, shift1, w_packed, gsum, gq, gk)


def _compress_kernel(xk_ref, xv_ref, pos_ref, w1k_ref, w2k_ref, w1v_ref, w2v_ref, gk_ref,
                     ko_ref, vo_ref):
    half = pos_ref.shape[1]

    def mlp(x, w1_ref, w2_ref):
        xa = (x + pos_ref[0:1, :]).astype(bf16)
        xb = (x + pos_ref[1:2, :]).astype(bf16)
        first = _dot(xa, w1_ref[:half, :])
        second = _dot(xb, w1_ref[half:, :])
        n = x.shape[0]
        pre = first + pltpu.roll(second, n - 1, 0)
        hid = pre * jax.nn.sigmoid(pre)
        return _dot(hid.astype(bf16), w2_ref[...])

    for h in range(NSA_KV_HEADS):
        k = mlp(xk_ref[0, h], w1k_ref, w2k_ref)
        ms = jnp.mean(k * k, axis=-1, keepdims=True)
        ko_ref[0, h] = (k * lax.rsqrt(ms + EPS) * gk_ref[...]).astype(bf16)
        vo_ref[0, h] = mlp(xv_ref[0, h], w1v_ref, w2v_ref).astype(bf16)


def _compress(xk, xv, pos2, w1k, w2k, w1v, w2v, gk0):
    bsz, hk, nch, width = xk.shape
    blk = pl.BlockSpec((1, hk, nch, width), lambda b: (b, 0, 0, 0))
    full = lambda a: pl.BlockSpec(a.shape, lambda b: (0,) * a.ndim)
    out = pl.BlockSpec((1, hk, nch, HEAD_DIM), lambda b: (b, 0, 0, 0))
    shape = jax.ShapeDtypeStruct((bsz, hk, nch, HEAD_DIM), bf16)
    return pl.pallas_call(
        _compress_kernel,
        grid=(bsz,),
        in_specs=[blk, blk, full(pos2), full(w1k), full(w2k), full(w1v), full(w2v), full(gk0)],
        out_specs=[out, out],
        out_shape=[shape, shape],
        compiler_params=_params(1),
        name="compress",
    )(xk, xv, pos2, w1k, w2k, w1v, w2v, gk0)


def _flash_step(s, v, m_ref, acc_ref):
    m_prev = m_ref[...]
    m_new = jnp.maximum(m_prev, jnp.max(s, axis=1, keepdims=True))
    alpha = jnp.exp(m_prev - m_new)
    p = jnp.exp(s - jnp.concatenate([m_new] * (s.shape[1] // LANES), axis=1))
    acc_ref[...] = alpha * acc_ref[...] + _dot(p.astype(bf16), v)
    m_ref[...] = m_new


def _flash_init(m_ref, acc_ref):
    m_ref[...] = jnp.full(m_ref.shape, MASKED, f32)
    acc_ref[...] = jnp.zeros(acc_ref.shape, f32)


def _flash_out(acc_ref):
    acc = acc_ref[...]
    return acc[:, :HEAD_DIM] / acc[:, HEAD_DIM:HEAD_DIM + 1]


def _nsa_kernel(n_top, tbl_ref, q_ref, g_ref, kc_ref, vc_ref, ks_ref, vs_ref, kw_ref, vw_ref,
                bc_ref, bn_ref, ovt_ref, o_ref, qs_ref, qw_ref, b31_ref, oc_ref,
                m_ref, acc_ref, mw_ref, accw_ref):
    h = pl.program_id(1)
    i = pl.program_id(2)
    rows = NSA_GROUP * TQ
    start = i * TQ
    assert WINDOW == 4 * TQ

    def last_bucket(width):
        return jnp.concatenate([b31_ref[...]] * (width // LANES), axis=1)

    def attend(q_rows_ref, k_ref, v_ref, off, width, bias, state):
        off = pl.multiple_of(off, TQ)
        s = _dot_nt(q_rows_ref[...], k_ref[0, pl.ds(off, width), :]) + bias
        _flash_step(s, v_ref[0, pl.ds(off, width), :], *state)

    def when(cond, guarded):
        return pl.when(cond) if guarded else (lambda fn: fn())

    def near_steps(q_rows_ref, k_ref, v_ref, state, guarded):
        @when(i >= 1, guarded)
        def _():
            attend(q_rows_ref, k_ref, v_ref, start - TQ, 2 * TQ,
                   bn_ref[...].reshape(rows, 2 * TQ), state)

        if guarded:
            @pl.when(i == 0)
            def _():
                attend(q_rows_ref, k_ref, v_ref, start, TQ,
                       bn_ref[:, :, TQ:].reshape(rows, TQ), state)

    def head(guarded):
        qpad = jnp.concatenate(
            [q_ref[0, :, g * LANES:(g + 1) * LANES] for g in range(NSA_GROUP)], axis=0)
        qw_ref[...] = qpad
        b31_ref[...] = jnp.concatenate(
            [jnp.full((TQ, LANES), tbl_ref[N_BUCKETS - 1, h * NSA_GROUP + g], f32)
             for g in range(NSA_GROUP)], axis=0)

        bc = bc_ref[...].reshape(rows, bc_ref.shape[2])
        s_c = _dot_nt(qpad[:, :HEAD_DIM], kc_ref[0, 0]) + bc
        visible = bc > 0.5 * MASKED
        m_c = jnp.max(s_c, axis=1, keepdims=True)
        e_c = jnp.where(visible, jnp.exp(s_c - m_c), 0.0)
        p_c = e_c / jnp.maximum(jnp.sum(e_c, axis=1, keepdims=True), 1e-30)
        oc_ref[...] = _dot(p_c.astype(bf16), vc_ref[0, 0])

        p_sum = p_c[0:TQ]
        for g in range(1, NSA_GROUP):
            p_sum = p_sum + p_c[g * TQ:(g + 1) * TQ]
        p_hi, p_lo = _split(p_sum)
        imp = _dot_nt(ovt_ref[...], p_hi) + _dot_nt(ovt_ref[...], p_lo)
        nblk = imp.shape[0]
        blk = lax.broadcasted_iota(jnp.int32, (nblk, TQ), 0)
        cur = (start + lax.broadcasted_iota(jnp.int32, (nblk, TQ), 1)) // SEL_BLOCK
        forced = (blk == 0) | (blk == cur) | (blk == cur - 1)
        imp = jnp.where(blk > cur, NEG_BLOCK, imp + jnp.where(forced, FORCED_BONUS, 0.0))
        rank = jnp.zeros((nblk, TQ), f32)
        for b2 in range(nblk):
            other = imp[b2:b2 + 1, :]
            rank = rank + jnp.where(blk > b2, jnp.where(other >= imp, 1.0, 0.0),
                                    jnp.where(other > imp, 1.0, 0.0))
        usable = (rank < n_top) & (blk <= cur)
        sel_t = jnp.where(usable, 0.0, UNSELECTED)
        sel_pad = jnp.concatenate([jnp.zeros((LANES - nblk, TQ), f32), sel_t], axis=0).T
        sel_rows = jnp.concatenate([sel_pad.astype(bf16)] * NSA_GROUP, axis=0)
        qs_ref[...] = qpad + sel_rows

        win = (mw_ref, accw_ref)
        _flash_init(*win)

        @when(i >= 4, guarded)
        def _():
            r = lax.broadcasted_iota(jnp.int32, (rows, TQ), 0) & (TQ - 1)
            c = lax.broadcasted_iota(jnp.int32, (rows, TQ), 1)
            attend(qw_ref, kw_ref, vw_ref, start - 4 * TQ, TQ,
                   last_bucket(TQ) + jnp.where(c > r, 0.0, MASKED), win)

        @when(i >= 3, guarded)
        def _():
            attend(qw_ref, kw_ref, vw_ref, start - 3 * TQ, 2 * TQ, last_bucket(2 * TQ), win)

        if guarded:
            @pl.when(i == 2)
            def _():
                attend(qw_ref, kw_ref, vw_ref, start - 2 * TQ, TQ, last_bucket(TQ), win)

        near_steps(qw_ref, kw_ref, vw_ref, win, guarded)
        _flash_init(m_ref, acc_ref)

    def tail(guarded):
        near_steps(qs_ref, ks_ref, vs_ref, (m_ref, acc_ref), guarded)
        o_c = oc_ref[...]
        o_s = _flash_out(acc_ref)
        o_w = _flash_out(accw_ref)
        gates = g_ref[0]
        outs = []
        for g in range(NSA_GROUP):
            sl = slice(g * TQ, (g + 1) * TQ)
            outs.append(gates[:, 3 * g:3 * g + 1] * o_c[sl]
                        + gates[:, 3 * g + 1:3 * g + 2] * o_s[sl]
                        + gates[:, 3 * g + 2:3 * g + 3] * o_w[sl])
        o_ref[0] = jnp.concatenate(outs, axis=1).astype(bf16)

    interior = i >= 4
    pl.when(interior)(lambda: head(False))
    pl.when(jnp.logical_not(interior))(lambda: head(True))

    sel = (m_ref, acc_ref)
    n_far = jnp.maximum(i - 1, 0)

    def far_quad(c, carry):
        off = c * (4 * TQ)
        attend(qs_ref, ks_ref, vs_ref, off, 2 * TQ, last_bucket(2 * TQ), sel)
        attend(qs_ref, ks_ref, vs_ref, off + 2 * TQ, 2 * TQ, last_bucket(2 * TQ), sel)
        return carry

    lax.fori_loop(0, n_far // 4, far_quad, 0)
    rem_off = (n_far // 4) * (4 * TQ)

    @pl.when((n_far & 2) != 0)
    def _():
        attend(qs_ref, ks_ref, vs_ref, rem_off, 2 * TQ, last_bucket(2 * TQ), sel)

    @pl.when((n_far & 1) != 0)
    def _():
        attend(qs_ref, ks_ref, vs_ref, rem_off + (n_far & 2) * TQ, TQ, last_bucket(TQ), sel)

    pl.when(interior)(lambda: tail(False))
    pl.when(jnp.logical_not(interior))(lambda: tail(True))


def _nsa(tbl, qa, gates, kcmp, vcmp, ks, vs, kw, vw, cmp_bias, near_bias, ovt):
    bsz, seq, _ = qa.shape
    nc = kcmp.shape[2]
    n_top = min(SEL_TOPK, seq // SEL_BLOCK)
    grp_w = NSA_GROUP * LANES
    rows = NSA_GROUP * TQ
    kv = pl.BlockSpec((1, seq, LANES), lambda b, h, i: (b, 0, h))
    cmp = pl.BlockSpec((1, 1, nc, HEAD_DIM), lambda b, h, i: (b, h, 0, 0))
    return pl.pallas_call(
        functools.partial(_nsa_kernel, n_top),
        grid=(bsz, NSA_KV_HEADS, seq // TQ),
        in_specs=[pl.BlockSpec(memory_space=pltpu.SMEM),
                  pl.BlockSpec((1, TQ, grp_w), lambda b, h, i: (b, i, h)),
                  pl.BlockSpec((1, TQ, LANES), lambda b, h, i: (b, i, h)),
                  cmp, cmp, kv, kv, kv, kv,
                  pl.BlockSpec((NSA_GROUP, TQ, nc), lambda b, h, i: (h, i, 0)),
                  pl.BlockSpec((NSA_GROUP, TQ, 2 * TQ), lambda b, h, i: (h, 0, 0)),
                  pl.BlockSpec(ovt.shape, lambda b, h, i: (0, 0))],
        out_specs=pl.BlockSpec((1, TQ, NSA_GROUP * HEAD_DIM), lambda b, h, i: (b, i, h)),
        out_shape=jax.ShapeDtypeStruct((bsz, seq, NSA_HEADS * HEAD_DIM), bf16),
        scratch_shapes=[pltpu.VMEM((rows, LANES), bf16),
                        pltpu.VMEM((rows, LANES), bf16),
                        pltpu.VMEM((rows, LANES), f32),
                        pltpu.VMEM((rows, HEAD_DIM), f32),
                        pltpu.VMEM((rows, LANES), f32),
                        pltpu.VMEM((rows, LANES), f32),
                        pltpu.VMEM((rows, LANES), f32),
                        pltpu.VMEM((rows, LANES), f32)],
        compiler_params=_params(3),
        name="nsa",
    )(tbl, qa, gates, kcmp, vcmp, ks, vs, kw, vw, cmp_bias, near_bias, ovt)


def _sb_kernel(q_ref, k_ref, v_ref, tri_ref, o_ref, carry_ref, acc_ref):
    i = pl.program_id(2)
    q = q_ref[0]
    lane = lax.broadcasted_iota(jnp.int32, q.shape, 1)
    zero = jnp.zeros_like(q)
    q_heads = (jnp.where(lane < HEAD_DIM, q, zero), jnp.where(lane >= HEAD_DIM, q, zero))
    carry_ref[...] = jnp.zeros(carry_ref.shape, f32)
    acc_ref[...] = jnp.zeros(acc_ref.shape, f32)
    r = lax.broadcasted_iota(jnp.int32, (TS, TS), 0)
    c = lax.broadcasted_iota(jnp.int32, (TS, TS), 1)
    before = c < r

    def chunks(jobs):
        offs = [pl.multiple_of(off, TS) for off, _ in jobs]
        stage = []
        for (_, diagonal), off in zip(jobs, offs):
            k = k_ref[0, pl.ds(off, TS), :]
            for hh in range(2):
                z = _dot_nt(q_heads[hh], k)
                soft = jnp.log(1.0 + jnp.exp(-jnp.abs(z)))
                log_keep = jnp.minimum(-z, 0.0) - soft
                log_sig = jnp.minimum(z, 0.0) - soft
                if diagonal:
                    log_keep = jnp.where(before, log_keep, 0.0)
                hi, lo = _split(log_keep)
                later = _dot(hi, tri_ref[...]) + _dot(lo, tri_ref[...])
                stage.append((log_sig + later, jnp.sum(log_keep, axis=1, keepdims=True)))
        for hh in range(2):
            carry = carry_ref[hh]
            acc = acc_ref[hh]
            for j, ((_, diagonal), off) in enumerate(zip(jobs, offs)):
                base, total = stage[2 * j + hh]
                a = jnp.exp(base + jnp.concatenate([carry] * (TS // LANES), axis=1))
                if diagonal:
                    a = jnp.where(before, a, 0.0)
                acc = acc + _dot(a.astype(bf16), v_ref[0, pl.ds(off, TS), :])
                carry = carry + total
            carry_ref[hh] = carry
            acc_ref[hh] = acc

    @pl.when(i == 0)
    def _():
        chunks([(0, True)])

    @pl.when(i >= 1)
    def _():
        chunks([(i * TS, True), ((i - 1) * TS, False)])

    def any_live():
        return jnp.max(carry_ref[...]) > EXP_UNDERFLOW

    def more(state):
        n, live = state
        return jnp.logical_and(n < i, live)

    def older(state):
        n, _ = state
        chunks([((i - 1 - n) * TS, False)])
        return n + 1, any_live()

    lax.while_loop(more, older, (jnp.int32(1), any_live()))
    o_ref[0] = jnp.where(lane < HEAD_DIM, acc_ref[0], acc_ref[1]).astype(bf16)


def _sb(qb, kb, vb, tri):
    bsz, seq, width = qb.shape
    q_spec = pl.BlockSpec((1, TS, LANES), lambda b, p, i: (b, i, p))
    kv_spec = pl.BlockSpec((1, seq, LANES), lambda b, p, i: (b, 0, p))
    return pl.pallas_call(
        _sb_kernel,
        grid=(bsz, width // LANES, seq // TS),
        in_specs=[q_spec, kv_spec, kv_spec, pl.BlockSpec(tri.shape, lambda b, p, i: (0, 0))],
        out_specs=q_spec,
        out_shape=jax.ShapeDtypeStruct((bsz, seq, width), bf16),
        scratch_shapes=[pltpu.VMEM((2, TS, LANES), f32), pltpu.VMEM((2, TS, LANES), f32)],
        compiler_params=_params(3),
        name="sb",
    )(qb, kb, vb, tri)


def _merge_kernel(ya_ref, yb_ref, ma_ref, mb_ref, x_ref, gate_ref, wa_ref, wb_ref, wo_ref, o_ref):
    y_a = _dot(ya_ref[0], wa_ref[...])
    y_b = _dot(yb_ref[0], wb_ref[...])
    mixed = ma_ref[0].astype(f32) * y_a + mb_ref[0].astype(f32) * y_b
    o_ref[0] = x_ref[0] + gate_ref[0] * _dot(mixed.astype(bf16), wo_ref[...])


def _merge(ya, yb, ma, mb, x, gate1, wa, wb, wo):
    bsz, seq, d = x.shape
    tok = lambda w: pl.BlockSpec((1, TM, w), lambda b, i: (b, i, 0))
    full = lambda a: pl.BlockSpec(a.shape, lambda b, i: (0,) * a.ndim)
    return pl.pallas_call(
        _merge_kernel,
        grid=(bsz, seq // TM),
        in_specs=[tok(ya.shape[2]), tok(yb.shape[2]), tok(d), tok(d), tok(d),
                  pl.BlockSpec((1, 1, d), lambda b, i: (b, 0, 0)), full(wa), full(wb), full(wo)],
        out_specs=tok(d),
        out_shape=jax.ShapeDtypeStruct((bsz, seq, d), f32),
        compiler_params=_params(2),
        name="merge",
    )(ya, yb, ma, mb, x, gate1, wa, wb, wo)


def _mlp_kernel(h_ref, g2_ref, sc_ref, sh_ref, gate_ref, w1_ref, w2_ref, o_ref):
    hres = h_ref[0]
    d = hres.shape[1]
    ms = jnp.mean(hres * hres, axis=-1, keepdims=True)
    u = (hres * lax.rsqrt(ms + EPS) * g2_ref[...]) * (1.0 + sc_ref[0]) + sh_ref[0]
    ub = u.astype(bf16)
    ff = jnp.zeros(hres.shape, f32)
    for c in range(w1_ref.shape[1] // d):
        hid = jnp.maximum(_dot(ub, w1_ref[:, c * d:(c + 1) * d]), 0.0)
        ff = ff + _dot((hid * hid).astype(bf16), w2_ref[c * d:(c + 1) * d, :])
    o_ref[0] = hres + gate_ref[0] * ff


def _mlp(hres, g2, scale2, shift2, gate2, w1, w2):
    bsz, seq, d = hres.shape
    tok = pl.BlockSpec((1, TM, d), lambda b, i: (b, i, 0))
    mod = pl.BlockSpec((1, 1, d), lambda b, i: (b, 0, 0))
    const = lambda a: pl.BlockSpec(a.shape, lambda b, i: (0,) * a.ndim,
                                   pipeline_mode=pl.Buffered(1))
    return pl.pallas_call(
        _mlp_kernel,
        grid=(bsz, seq // TM),
        in_specs=[tok, pl.BlockSpec(g2.shape, lambda b, i: (0, 0)), mod, mod, mod,
                  const(w1), const(w2)],
        out_specs=tok,
        out_shape=jax.ShapeDtypeStruct((bsz, seq, d), f32),
        compiler_params=_params(2),
        name="mlp",
    )(hres, g2, scale2, shift2, gate2, w1, w2)


def _overlap_t(nc_pad, nsel_pad, nc, nsel):
    c_start = np.arange(nc_pad) * CMP_STRIDE
    s_start = np.arange(nsel_pad) * SEL_BLOCK
    ov = (np.minimum(c_start[None, :] + CMP_BLOCK, s_start[:, None] + SEL_BLOCK)
          - np.maximum(c_start[None, :], s_start[:, None]))
    ov = np.clip(ov, 0, CMP_BLOCK).astype(np.float32) / CMP_BLOCK
    ov[nsel:, :] = 0.0
    ov[:, nc:] = 0.0
    return ov


def _layer(h, mod, rel_tiles, tbl, p):
    bsz, seq, d = h.shape
    shift1, scale1, gate1, shift2, scale2, gate2 = [
        mod[:, k * d:(k + 1) * d].reshape(bsz, 1, d) for k in range(6)]
    cmp_bias, near_bias = rel_tiles

    group_ones = np.kron(np.eye(_QA_W // HEAD_DIM), np.ones((HEAD_DIM, HEAD_DIM)))
    group_ones = group_ones * (np.arange(_QA_W)[None, :] % LANES < HEAD_DIM)
    gsum = jnp.asarray(group_ones, bf16)
    pad_gain = lambda g, n: jnp.tile(jnp.pad(g, (0, LANES - HEAD_DIM)), n).reshape(1, n * LANES)
    gq = pad_gain(p["q_norm_g"], NSA_HEADS)
    gk = jnp.concatenate([pad_gain(p["k_norm_g"][k], NSA_KV_HEADS) for k in range(3)], axis=0)

    (qa, kc, vc, ks, vs, kw, vw, gates, qb, kb, vb, ma, mb) = _inproj(
        h, p["norm1_g"].reshape(1, d), scale1, shift1, _pack_w_in(p["w_in"], d), gsum, gq, gk)

    nch = seq // CMP_STRIDE
    chunks = lambda a: a.reshape(bsz, nch, CMP_STRIDE, NSA_KV_HEADS, HEAD_DIM).transpose(
        0, 3, 1, 2, 4).reshape(bsz, NSA_KV_HEADS, nch, CMP_STRIDE * HEAD_DIM)
    pos2 = p["cmp_pos"].reshape(2, CMP_STRIDE * HEAD_DIM)
    kcmp, vcmp = _compress(chunks(kc), chunks(vc), pos2,
                           p["cmp_k_w1"].astype(bf16), p["cmp_k_w2"].astype(bf16),
                           p["cmp_v_w1"].astype(bf16), p["cmp_v_w2"].astype(bf16),
                           p["k_norm_g"][0].reshape(1, HEAD_DIM))

    nc = (seq - CMP_BLOCK) // CMP_STRIDE + 1
    ovt = jnp.asarray(_overlap_t(nch, HEAD_DIM, nc, seq // SEL_BLOCK), bf16)
    y_nsa = _nsa(tbl, qa, gates, kcmp, vcmp, ks, vs, kw, vw, cmp_bias, near_bias, ovt)

    y_sb = _sb(qb, kb, vb, jnp.asarray(np.tril(np.ones((TS, TS)), -1), bf16))

    h1 = _merge(y_nsa, y_sb, ma, mb, h, gate1, p["w_up_nsa"].astype(bf16),
                p["w_up_sb"].astype(bf16), p["w_out"].astype(bf16))
    return _mlp(h1, p["norm2_g"].reshape(1, d), scale2, shift2, gate2,
                p["mlp_w1"].astype(bf16), p["mlp_w2"].astype(bf16))


def kernel(x, c, rel_bias, ada_w, ada_b, norm1_g, norm2_g, w_in, cmp_pos, cmp_k_w1, cmp_k_w2,
           cmp_v_w1, cmp_v_w2, q_norm_g, k_norm_g, w_up_nsa, w_up_sb, w_out, mlp_w1, mlp_w2):
    bsz, seq, d = x.shape
    assert seq % TM == 0 and seq // SEL_BLOCK <= HEAD_DIM and seq >= WINDOW + TQ
    assert CMP_BLOCK == 2 * CMP_STRIDE and TQ == 2 * SEL_BLOCK
    tbl = rel_bias.astype(f32)
    rel_tiles = _bias_tiles(tbl, seq, seq // CMP_STRIDE)
    stacked = dict(norm1_g=norm1_g, norm2_g=norm2_g, w_in=w_in, cmp_pos=cmp_pos,
                   cmp_k_w1=cmp_k_w1, cmp_k_w2=cmp_k_w2, cmp_v_w1=cmp_v_w1, cmp_v_w2=cmp_v_w2,
                   q_norm_g=q_norm_g, k_norm_g=k_norm_g, w_up_nsa=w_up_nsa, w_up_sb=w_up_sb,
                   w_out=w_out, mlp_w1=mlp_w1, mlp_w2=mlp_w2)
    h = x
    for layer in range(ada_w.shape[0]):
        mod = _adaln(c, ada_w[layer], ada_b[layer])
        h = _layer(h, mod, rel_tiles, tbl, {k: v[layer] for k, v in stacked.items()})
    return h
```

```python
import functools
import math

import numpy as np
import jax
import jax.numpy as jnp
from jax import lax
from jax.experimental import pallas as pl
from jax.experimental.pallas import tpu as pltpu

f32 = jnp.float32
bf16 = jnp.bfloat16

HEAD_DIM = 64
NSA_HEADS = 8
NSA_KV_HEADS = 2
NSA_GROUP = NSA_HEADS // NSA_KV_HEADS
SB_HEADS = 8
CMP_BLOCK = 32
CMP_STRIDE = 16
SEL_BLOCK = 64
SEL_TOPK = 16
WINDOW = 512
N_BUCKETS = 32
MAX_DISTANCE = 128
EPS = 1e-6
FORCED_BONUS = 1e4
NEG_BLOCK = -1e9

LANES = 128
MASKED = -1e30
UNSELECTED = -1e9
EXP_UNDERFLOW = -104.0
VMEM_LIMIT = 56 * 1024 * 1024

TQ = 128
TS = 256
TM = 512


def _bucket_thresholds():
    n = np.arange(0, 4 * MAX_DISTANCE)
    max_exact = N_BUCKETS // 2
    nf = np.maximum(n, 1).astype(np.float32)
    large = max_exact + (np.log(nf / max_exact) / math.log(MAX_DISTANCE / max_exact)
                         * (N_BUCKETS - max_exact)).astype(np.int32)
    large = np.minimum(large, N_BUCKETS - 1)
    b = np.where(n < max_exact, n, large)
    assert np.all(np.diff(b) >= 0) and b[-1] == N_BUCKETS - 1
    return [int(np.argmax(b >= k)) for k in range(N_BUCKETS)]


BUCKET_START = _bucket_thresholds()
assert BUCKET_START[-1] <= LANES


def _dot(a, b):
    return jnp.dot(a, b, preferred_element_type=f32)


def _dot_nt(a, b):
    return lax.dot_general(a, b, (((1,), (1,)), ((), ())), preferred_element_type=f32)


def _split(a):
    hi = a.astype(bf16)
    lo = (a - hi.astype(f32)).astype(bf16)
    return hi, lo


def _params(n_grid):
    return pltpu.CompilerParams(dimension_semantics=("arbitrary",) * n_grid,
                                vmem_limit_bytes=VMEM_LIMIT)


def _adaln_kernel(c_ref, w_ref, b_ref, o_ref):
    c = c_ref[...]
    a = c * jax.nn.sigmoid(c)
    ah, al = _split(a)
    wh, wl = _split(w_ref[...])
    o_ref[...] = _dot(ah, wh) + _dot(ah, wl) + _dot(al, wh) + b_ref[...]


def _adaln(c, w, b):
    bsz, d = c.shape
    n = w.shape[1]
    return pl.pallas_call(
        _adaln_kernel,
        grid=(n // d,),
        in_specs=[pl.BlockSpec((bsz, d), lambda j: (0, 0)),
                  pl.BlockSpec((d, d), lambda j: (0, j)),
                  pl.BlockSpec((1, d), lambda j: (0, j))],
        out_specs=pl.BlockSpec((bsz, d), lambda j: (0, j)),
        out_shape=jax.ShapeDtypeStruct((bsz, n), f32),
        compiler_params=_params(1),
        name="adaln",
    )(c, w, b.reshape(1, n))


def _bias_of_dist(dist, tbl_ref, h):
    out = jnp.full(dist.shape, tbl_ref[0, h], f32)
    for k in range(1, N_BUCKETS):
        out = jnp.where(dist >= BUCKET_START[k], tbl_ref[k, h], out)
    return jnp.where(dist >= 0, out, MASKED)


def _cmp_bias_kernel(tbl_ref, o_ref):
    h = pl.program_id(0)
    i = pl.program_id(1)
    rows, nc = o_ref.shape[1], o_ref.shape[2]
    t = i * rows + lax.broadcasted_iota(jnp.int32, (rows, nc), 0)
    j = lax.broadcasted_iota(jnp.int32, (rows, nc), 1)
    o_ref[0] = _bias_of_dist(t - (j * CMP_STRIDE + CMP_BLOCK - 1), tbl_ref, h)


def _near_bias_kernel(tbl_ref, o_ref):
    h = pl.program_id(0)
    r = lax.broadcasted_iota(jnp.int32, (TQ, 2 * TQ), 0)
    c = lax.broadcasted_iota(jnp.int32, (TQ, 2 * TQ), 1)
    o_ref[0] = _bias_of_dist(r - c + TQ, tbl_ref, h)


def _bias_tiles(rel_bias, seq, nc_pad):
    tbl = rel_bias.astype(f32)
    smem = pl.BlockSpec(memory_space=pltpu.SMEM)
    rows = 512
    cmp_bias = pl.pallas_call(
        _cmp_bias_kernel,
        grid=(NSA_HEADS, seq // rows),
        in_specs=[smem],
        out_specs=pl.BlockSpec((1, rows, nc_pad), lambda h, i: (h, i, 0)),
        out_shape=jax.ShapeDtypeStruct((NSA_HEADS, seq, nc_pad), f32),
        compiler_params=_params(2),
        name="cmp_bias",
    )(tbl)
    near_bias = pl.pallas_call(
        _near_bias_kernel,
        grid=(NSA_HEADS,),
        in_specs=[smem],
        out_specs=pl.BlockSpec((1, TQ, 2 * TQ), lambda h: (h, 0, 0)),
        out_shape=jax.ShapeDtypeStruct((NSA_HEADS, TQ, 2 * TQ), f32),
        compiler_params=_params(1),
        name="near_bias",
    )(tbl)
    return cmp_bias, near_bias


_QA_W = NSA_HEADS * LANES
_KV_W = NSA_KV_HEADS * LANES
_CMP_W = NSA_KV_HEADS * HEAD_DIM
_SB_W = SB_HEADS * HEAD_DIM


def _layout(d_model):
    names = ["qa", "kc", "vc", "ksl", "vsl", "kwn", "vwn", "g", "qb", "kb", "vb", "ma", "mb"]
    widths = [_QA_W, _CMP_W, _CMP_W, _KV_W, _KV_W, _KV_W, _KV_W, _KV_W, _SB_W, _SB_W, _SB_W,
              d_model, d_model]
    offs = np.concatenate([[0], np.cumsum(widths)])
    return {n: (int(offs[i]), int(offs[i + 1])) for i, n in enumerate(names)}, int(offs[-1])


def _pack_w_in(w_in, d_model):
    q_w = NSA_HEADS * HEAD_DIM
    kv_w = NSA_KV_HEADS * HEAD_DIM
    g_w = NSA_HEADS * 3
    sizes = [q_w] + [kv_w] * 6 + [g_w] + [_SB_W] * 3 + [d_model, d_model]
    offs = np.concatenate([[0], np.cumsum(sizes)])
    parts = [w_in[:, int(offs[i]):int(offs[i + 1])] for i in range(len(sizes))]
    d = w_in.shape[0]

    def pad_heads(p, n_heads, width):
        p = p.reshape(d, n_heads, width)
        p = jnp.pad(p, ((0, 0), (0, 0), (0, LANES - width)))
        return p.reshape(d, n_heads * LANES)

    packed = [pad_heads(parts[0], NSA_HEADS, HEAD_DIM), parts[1], parts[2]]
    packed += [pad_heads(parts[k], NSA_KV_HEADS, HEAD_DIM) for k in (3, 4, 5, 6)]
    packed += [pad_heads(parts[7], NSA_KV_HEADS, NSA_GROUP * 3)]
    packed += parts[8:]
    return jnp.concatenate(packed, axis=1).astype(bf16)


def _inproj_kernel(lay, x_ref, g1_ref, sc_ref, sh_ref, w_ref, gq_ref, gk_ref,
                   qa_ref, kc_ref, vc_ref, ks_ref, vs_ref, kw_ref, vw_ref, g_ref,
                   qb_ref, kb_ref, vb_ref, ma_ref, mb_ref):
    i = pl.program_id(1)
    x = x_ref[0]
    ms = jnp.mean(x * x, axis=-1, keepdims=True)
    u = (x * lax.rsqrt(ms + EPS) * g1_ref[...]) * (1.0 + sc_ref[0]) + sh_ref[0]
    ub = u.astype(bf16)

    def proj(name):
        lo, hi = lay[name]
        return _dot(ub, w_ref[:, lo:hi])

    def head_norm(z, gain):
        parts = []
        for g in range(z.shape[1] // LANES):
            zg = z[:, g * LANES:(g + 1) * LANES]
            ss = jnp.sum(zg * zg, axis=1, keepdims=True)
            parts.append(zg * lax.rsqrt(ss * (1.0 / HEAD_DIM) + EPS))
        return jnp.concatenate(parts, axis=1) * gain

    scale = HEAD_DIM ** -0.5
    qa_ref[0] = (head_norm(proj("qa"), gq_ref[...]) * scale).astype(bf16)
    kc_ref[0] = proj("kc")
    vc_ref[0] = proj("vc")

    rows = x.shape[0]
    lane = lax.broadcasted_iota(jnp.int32, (rows, _KV_W), 1)
    tok_blk = (i * rows + lax.broadcasted_iota(jnp.int32, (rows, _KV_W), 0)) // SEL_BLOCK
    in_pad = (lane & HEAD_DIM) != 0
    onehot = in_pad & ((lane & (HEAD_DIM - 1)) == tok_blk)
    ones_col = (lane & (LANES - 1)) == HEAD_DIM

    ks = head_norm(proj("ksl"), gk_ref[1:2, :])
    ks_ref[0] = jnp.where(onehot, 1.0, ks).astype(bf16)
    vs_ref[0] = jnp.where(ones_col, 1.0, proj("vsl")).astype(bf16)
    kw_ref[0] = head_norm(proj("kwn"), gk_ref[2:3, :]).astype(bf16)
    vw_ref[0] = jnp.where(ones_col, 1.0, proj("vwn")).astype(bf16)
    g_ref[0] = jax.nn.sigmoid(proj("g"))
    qb_ref[0] = (proj("qb") * scale).astype(bf16)
    kb_ref[0] = proj("kb").astype(bf16)
    vb_ref[0] = proj("vb").astype(bf16)
    ma_ref[0] = jax.nn.sigmoid(proj("ma")).astype(bf16)
    mb_ref[0] = jax.nn.sigmoid(proj("mb")).astype(bf16)


def _inproj(x, g1, scale1, shift1, w_packed, gq, gk):
    bsz, seq, d = x.shape
    lay, width = _layout(d)
    assert w_packed.shape == (d, width)
    tok = lambda w: pl.BlockSpec((1, TM, w), lambda b, i: (b, i, 0))
    full = lambda a: pl.BlockSpec(a.shape, lambda b, i: (0,) * a.ndim,
                                  pipeline_mode=pl.Buffered(1))
    mod = pl.BlockSpec((1, 1, d), lambda b, i: (b, 0, 0))
    out_w =[(_QA_W, bf16), (_CMP_W, f32), (_CMP_W, f32), (_KV_W, bf16), (_KV_W, bf16),
             (_KV_W, bf16), (_KV_W, bf16), (_KV_W, f32), (_SB_W, bf16), (_SB_W, bf16),
             (_SB_W, bf16), (d, bf16), (d, bf16)]
    return pl.pallas_call(
        functools.partial(_inproj_kernel, lay),
        grid=(bsz, seq // TM),
        in_specs=[tok(d), full(g1), mod, mod, full(w_packed), full(gq), full(gk)],
        out_specs=[tok(w) for w, _ in out_w],
        out_shape=[jax.ShapeDtypeStruct((bsz, seq, w), dt) for w, dt in out_w],
        compiler_params=_params(2),
        name="inproj",
    )(x, g1, scale1, shift1, w_packed, gq, gk)


def _compress_kernel(xk_ref, xv_ref, pos_ref, w1k_ref, w2k_ref, w1v_ref, w2v_ref, gk_ref,
                     ko_ref, vo_ref):
    nch = xk_ref.shape[1] // CMP_STRIDE

    def mlp(x_ref, w1_ref, w2_ref):
        first = jnp.zeros((nch, LANES), f32)
        second = jnp.zeros((nch, LANES), f32)
        for l in range(CMP_STRIDE):
            xl = x_ref[0, pl.ds(l, nch, stride=CMP_STRIDE), :]
            lo = l + CMP_STRIDE
            first = first + _dot((xl + pos_ref[l:l + 1, :]).astype(bf16), w1_ref[l])
            second = second + _dot((xl + pos_ref[lo:lo + 1, :]).astype(bf16), w1_ref[lo])
        pre = first + pltpu.roll(second, nch - 1, 0)
        hid = pre * jax.nn.sigmoid(pre)
        return _dot(hid.astype(bf16), w2_ref[...])

    k = mlp(xk_ref, w1k_ref, w2k_ref)
    v = mlp(xv_ref, w1v_ref, w2v_ref)
    for h in range(NSA_KV_HEADS):
        kh = k[:, h * HEAD_DIM:(h + 1) * HEAD_DIM]
        ms = jnp.mean(kh * kh, axis=-1, keepdims=True)
        ko_ref[0, h] = (kh * lax.rsqrt(ms + EPS) * gk_ref[...]).astype(bf16)
        vo_ref[0, h] = v[:, h * HEAD_DIM:(h + 1) * HEAD_DIM].astype(bf16)


def _compress(xk, xv, pos, w1k, w2k, w1v, w2v, gk0):
    bsz, seq, width = xk.shape
    nch = seq // CMP_STRIDE
    assert width == NSA_KV_HEADS * HEAD_DIM == LANES

    def both_heads(w):
        z = jnp.zeros_like(w)
        return jnp.concatenate([jnp.concatenate([w, z], axis=2),
                                jnp.concatenate([z, w], axis=2)], axis=1).astype(bf16)

    w1 = lambda w: both_heads(w.reshape(CMP_BLOCK, HEAD_DIM, w.shape[1]))
    w2 = lambda w: both_heads(w[None])[0]
    args = (xk, xv, jnp.tile(pos, (1, NSA_KV_HEADS)), w1(w1k), w2(w2k), w1(w1v), w2(w2v), gk0)
    blk = pl.BlockSpec((1, seq, width), lambda b: (b, 0, 0))
    full = lambda a: pl.BlockSpec(a.shape, lambda b: (0,) * a.ndim)
    out = pl.BlockSpec((1, NSA_KV_HEADS, nch, HEAD_DIM), lambda b: (b, 0, 0, 0))
    shape = jax.ShapeDtypeStruct((bsz, NSA_KV_HEADS, nch, HEAD_DIM), bf16)
    return pl.pallas_call(
        _compress_kernel,
        grid=(bsz,),
        in_specs=[blk, blk] + [full(a) for a in args[2:]],
        out_specs=[out, out],
        out_shape=[shape, shape],
        compiler_params=_params(1),
        name="compress",
    )(*args)


def _flash_step(s, v, m_ref, acc_ref):
    m_prev = m_ref[...]
    m_new = jnp.maximum(m_prev, jnp.max(s, axis=1, keepdims=True))
    alpha = jnp.exp(m_prev - m_new)
    p = jnp.exp(s - jnp.concatenate([m_new] * (s.shape[1] // LANES), axis=1))
    acc_ref[...] = alpha * acc_ref[...] + _dot(p.astype(bf16), v)
    m_ref[...] = m_new


def _flash_init(m_ref, acc_ref):
    m_ref[...] = jnp.full(m_ref.shape, MASKED, f32)
    acc_ref[...] = jnp.zeros(acc_ref.shape, f32)


def _flash_out(acc_ref):
    acc = acc_ref[...]
    return acc[:, :HEAD_DIM] / acc[:, HEAD_DIM:HEAD_DIM + 1]


def _nsa_kernel(n_top, tbl_ref, q_ref, g_ref, kc_ref, vc_ref, ks_ref, vs_ref, kw_ref, vw_ref,
                bc_ref, bn_ref, ovt_ref, o_ref, qs_ref, qw_ref, b31_ref, oc_ref,
                m_ref, acc_ref, mw_ref, accw_ref):
    h = pl.program_id(1)
    i = pl.program_id(2)
    rows = NSA_GROUP * TQ
    start = i * TQ
    assert WINDOW == 4 * TQ

    def last_bucket(width):
        return jnp.concatenate([b31_ref[...]] * (width // LANES), axis=1)

    def attend(q_rows_ref, k_ref, v_ref, off, width, bias, state):
        off = pl.multiple_of(off, TQ)
        s = _dot_nt(q_rows_ref[...], k_ref[0, pl.ds(off, width), :]) + bias
        _flash_step(s, v_ref[0, pl.ds(off, width), :], *state)

    def when(cond, guarded):
        return pl.when(cond) if guarded else (lambda fn: fn())

    def near_steps(q_rows_ref, k_ref, v_ref, state, guarded):
        @when(i >= 1, guarded)
        def _():
            attend(q_rows_ref, k_ref, v_ref, start - TQ, 2 * TQ,
                   bn_ref[...].reshape(rows, 2 * TQ), state)

        if guarded:
            @pl.when(i == 0)
            def _():
                attend(q_rows_ref, k_ref, v_ref, start, TQ,
                       bn_ref[:, :, TQ:].reshape(rows, TQ), state)

    def head(guarded):
        qpad = jnp.concatenate(
            [q_ref[0, :, g * LANES:(g + 1) * LANES] for g in range(NSA_GROUP)], axis=0)
        qw_ref[...] = qpad
        b31_ref[...] = jnp.concatenate(
            [jnp.full((TQ, LANES), tbl_ref[N_BUCKETS - 1, h * NSA_GROUP + g], f32)
             for g in range(NSA_GROUP)], axis=0)

        bc = bc_ref[...].reshape(rows, bc_ref.shape[2])
        s_c = _dot_nt(qpad[:, :HEAD_DIM], kc_ref[0, 0]) + bc
        visible = bc > 0.5 * MASKED
        m_c = jnp.max(s_c, axis=1, keepdims=True)
        e_c = jnp.where(visible, jnp.exp(s_c - m_c), 0.0)
        p_c = e_c / jnp.maximum(jnp.sum(e_c, axis=1, keepdims=True), 1e-30)
        oc_ref[...] = _dot(p_c.astype(bf16), vc_ref[0, 0])

        p_sum = p_c[0:TQ]
        for g in range(1, NSA_GROUP):
            p_sum = p_sum + p_c[g * TQ:(g + 1) * TQ]
        p_hi, p_lo = _split(p_sum)
        imp = _dot_nt(ovt_ref[...], p_hi) + _dot_nt(ovt_ref[...], p_lo)
        nblk = imp.shape[0]
        blk = lax.broadcasted_iota(jnp.int32, (nblk, TQ), 0)
        cur = (start + lax.broadcasted_iota(jnp.int32, (nblk, TQ), 1)) // SEL_BLOCK
        forced = (blk == 0) | (blk == cur) | (blk == cur - 1)
        imp = jnp.where(blk > cur, NEG_BLOCK, imp + jnp.where(forced, FORCED_BONUS, 0.0))
        rank = jnp.zeros((nblk, TQ), f32)
        for b2 in range(nblk):
            other = imp[b2:b2 + 1, :]
            rank = rank + jnp.where(blk > b2, jnp.where(other >= imp, 1.0, 0.0),
                                    jnp.where(other > imp, 1.0, 0.0))
        usable = (rank < n_top) & (blk <= cur)
        sel_t = jnp.where(usable, 0.0, UNSELECTED)
        sel_pad = jnp.concatenate([jnp.zeros((LANES - nblk, TQ), f32), sel_t], axis=0).T
        sel_rows = jnp.concatenate([sel_pad.astype(bf16)] * NSA_GROUP, axis=0)
        qs_ref[...] = qpad + sel_rows

        win = (mw_ref, accw_ref)
        _flash_init(*win)

        @when(i >= 4, guarded)
        def _():
            r = lax.broadcasted_iota(jnp.int32, (rows, TQ), 0) & (TQ - 1)
            c = lax.broadcasted_iota(jnp.int32, (rows, TQ), 1)
            attend(qw_ref, kw_ref, vw_ref, start - 4 * TQ, TQ,
                   last_bucket(TQ) + jnp.where(c > r, 0.0, MASKED), win)

        @when(i >= 3, guarded)
        def _():
            attend(qw_ref, kw_ref, vw_ref, start - 3 * TQ, 2 * TQ, last_bucket(2 * TQ), win)

        if guarded:
            @pl.when(i == 2)
            def _():
                attend(qw_ref, kw_ref, vw_ref, start - 2 * TQ, TQ, last_bucket(TQ), win)

        near_steps(qw_ref, kw_ref, vw_ref, win, guarded)
        _flash_init(m_ref, acc_ref)

    def tail(guarded):
        near_steps(qs_ref, ks_ref, vs_ref, (m_ref, acc_ref), guarded)
        o_c = oc_ref[...]
        o_s = _flash_out(acc_ref)
        o_w = _flash_out(accw_ref)
        gates = g_ref[0]
        outs = []
        for g in range(NSA_GROUP):
            sl = slice(g * TQ, (g + 1) * TQ)
            outs.append(gates[:, 3 * g:3 * g + 1] * o_c[sl]
                        + gates[:, 3 * g + 1:3 * g + 2] * o_s[sl]
                        + gates[:, 3 * g + 2:3 * g + 3] * o_w[sl])
        o_ref[0] = jnp.concatenate(outs, axis=1).astype(bf16)

    interior = i >= 4
    pl.when(interior)(lambda: head(False))
    pl.when(jnp.logical_not(interior))(lambda: head(True))

    sel = (m_ref, acc_ref)
    n_far = jnp.maximum(i - 1, 0)

    def far_steps(off, n_steps):
        for step in range(n_steps):
            attend(qs_ref, ks_ref, vs_ref, off + step * (2 * TQ), 2 * TQ, last_bucket(2 * TQ), sel)

    def far_trip(c, carry):
        far_steps(c * (8 * TQ), 4)
        return carry

    lax.fori_loop(0, n_far // 8, far_trip, 0)
    rem_off = (n_far // 8) * (8 * TQ)

    @pl.when((n_far & 4) != 0)
    def _():
        far_steps(rem_off, 2)

    @pl.when((n_far & 2) != 0)
    def _():
        far_steps(rem_off + (n_far & 4) * TQ, 1)

    @pl.when((n_far & 1) != 0)
    def _():
        attend(qs_ref, ks_ref, vs_ref, rem_off + (n_far & 6) * TQ, TQ, last_bucket(TQ), sel)

    pl.when(interior)(lambda: tail(False))
    pl.when(jnp.logical_not(interior))(lambda: tail(True))


def _nsa(tbl, qa, gates, kcmp, vcmp, ks, vs, kw, vw, cmp_bias, near_bias, ovt):
    bsz, seq, _ = qa.shape
    nc = kcmp.shape[2]
    n_top = min(SEL_TOPK, seq // SEL_BLOCK)
    grp_w = NSA_GROUP * LANES
    rows = NSA_GROUP * TQ
    kv = pl.BlockSpec((1, seq, LANES), lambda b, h, i: (b, 0, h))
    cmp = pl.BlockSpec((1, 1, nc, HEAD_DIM), lambda b, h, i: (b, h, 0, 0))
    return pl.pallas_call(
        functools.partial(_nsa_kernel, n_top),
        grid=(bsz, NSA_KV_HEADS, seq // TQ),
        in_specs=[pl.BlockSpec(memory_space=pltpu.SMEM),
                  pl.BlockSpec((1, TQ, grp_w), lambda b, h, i: (b, i, h)),
                  pl.BlockSpec((1, TQ, LANES), lambda b, h, i: (b, i, h)),
                  cmp, cmp, kv, kv, kv, kv,
                  pl.BlockSpec((NSA_GROUP, TQ, nc), lambda b, h, i: (h, i, 0)),
                  pl.BlockSpec((NSA_GROUP, TQ, 2 * TQ), lambda b, h, i: (h, 0, 0)),
                  pl.BlockSpec(ovt.shape, lambda b, h, i: (0, 0))],
        out_specs=pl.BlockSpec((1, TQ, NSA_GROUP * HEAD_DIM), lambda b, h, i: (b, i, h)),
        out_shape=jax.ShapeDtypeStruct((bsz, seq, NSA_HEADS * HEAD_DIM), bf16),
        scratch_shapes=[pltpu.VMEM((rows, LANES), bf16),
                        pltpu.VMEM((rows, LANES), bf16),
                        pltpu.VMEM((rows, LANES), f32),
                        pltpu.VMEM((rows, HEAD_DIM), f32),
                        pltpu.VMEM((rows, LANES), f32),
                        pltpu.VMEM((rows, LANES), f32),
                        pltpu.VMEM((rows, LANES), f32),
                        pltpu.VMEM((rows, LANES), f32)],
        compiler_params=_params(3),
        name="nsa",
    )(tbl, qa, gates, kcmp, vcmp, ks, vs, kw, vw, cmp_bias, near_bias, ovt)


def _sb_kernel(q_ref, k_ref, v_ref, tri_ref, o_ref, carry_ref, acc_ref):
    i = pl.program_id(2)
    q = q_ref[0]
    lane = lax.broadcasted_iota(jnp.int32, q.shape, 1)
    zero = jnp.zeros_like(q)
    q_heads = (jnp.where(lane < HEAD_DIM, q, zero), jnp.where(lane >= HEAD_DIM, q, zero))
    carry_ref[...] = jnp.zeros(carry_ref.shape, f32)
    acc_ref[...] = jnp.zeros(acc_ref.shape, f32)
    r = lax.broadcasted_iota(jnp.int32, (TS, TS), 0)
    c = lax.broadcasted_iota(jnp.int32, (TS, TS), 1)
    before = c < r

    def chunks(jobs):
        offs = [pl.multiple_of(off, TS) for off, _ in jobs]
        stage = []
        for (_, diagonal), off in zip(jobs, offs):
            k = k_ref[0, pl.ds(off, TS), :]
            for hh in range(2):
                z = _dot_nt(q_heads[hh], k)
                soft = jnp.log(1.0 + jnp.exp(-jnp.abs(z)))
                log_keep = jnp.minimum(-z, 0.0) - soft
                log_sig = jnp.minimum(z, 0.0) - soft
                if diagonal:
                    log_keep = jnp.where(before, log_keep, 0.0)
                hi, lo = _split(log_keep)
                later = _dot(hi, tri_ref[...]) + _dot(lo, tri_ref[...])
                stage.append((log_sig + later, jnp.sum(log_keep, axis=1, keepdims=True)))
        for hh in range(2):
            carry = carry_ref[hh]
            acc = acc_ref[hh]
            for j, ((_, diagonal), off) in enumerate(zip(jobs, offs)):
                base, total = stage[2 * j + hh]
                a = jnp.exp(base + jnp.concatenate([carry] * (TS // LANES), axis=1))
                if diagonal:
                    a = jnp.where(before, a, 0.0)
                acc = acc + _dot(a.astype(bf16), v_ref[0, pl.ds(off, TS), :])
                carry = carry + total
            carry_ref[hh] = carry
            acc_ref[hh] = acc

    @pl.when(i == 0)
    def _():
        chunks([(0, True)])

    @pl.when(i >= 1)
    def _():
        chunks([(i * TS, True), ((i - 1) * TS, False)])

    def any_live():
        return jnp.max(carry_ref[...]) > EXP_UNDERFLOW

    def more(state):
        n, live = state
        return jnp.logical_and(n < i, live)

    def older(state):
        n, _ = state
        chunks([((i - 1 - n) * TS, False)])
        return n + 1, any_live()

    lax.while_loop(more, older, (jnp.int32(1), any_live()))
    o_ref[0] = jnp.where(lane < HEAD_DIM, acc_ref[0], acc_ref[1]).astype(bf16)


def _sb(qb, kb, vb, tri):
    bsz, seq, width = qb.shape
    q_spec = pl.BlockSpec((1, TS, LANES), lambda b, p, i: (b, i, p))
    kv_spec = pl.BlockSpec((1, seq, LANES), lambda b, p, i: (b, 0, p))
    return pl.pallas_call(
        _sb_kernel,
        grid=(bsz, width // LANES, seq // TS),
        in_specs=[q_spec, kv_spec, kv_spec, pl.BlockSpec(tri.shape, lambda b, p, i: (0, 0))],
        out_specs=q_spec,
        out_shape=jax.ShapeDtypeStruct((bsz, seq, width), bf16),
        scratch_shapes=[pltpu.VMEM((2, TS, LANES), f32), pltpu.VMEM((2, TS, LANES), f32)],
        compiler_params=_params(3),
        name="sb",
    )(qb, kb, vb, tri)


def _merge_kernel(ya_ref, yb_ref, ma_ref, mb_ref, x_ref, gate_ref, wa_ref, wb_ref, wo_ref, o_ref):
    y_a = _dot(ya_ref[0], wa_ref[...])
    y_b = _dot(yb_ref[0], wb_ref[...])
    mixed = ma_ref[0].astype(f32) * y_a + mb_ref[0].astype(f32) * y_b
    o_ref[0] = x_ref[0] + gate_ref[0] * _dot(mixed.astype(bf16), wo_ref[...])


def _merge(ya, yb, ma, mb, x, gate1, wa, wb, wo):
    bsz, seq, d = x.shape
    tok = lambda w: pl.BlockSpec((1, TM, w), lambda b, i: (b, i, 0))
    full = lambda a: pl.BlockSpec(a.shape, lambda b, i: (0,) * a.ndim)
    return pl.pallas_call(
        _merge_kernel,
        grid=(bsz, seq // TM),
        in_specs=[tok(ya.shape[2]), tok(yb.shape[2]), tok(d), tok(d), tok(d),
                  pl.BlockSpec((1, 1, d), lambda b, i: (b, 0, 0)), full(wa), full(wb), full(wo)],
        out_specs=tok(d),
        out_shape=jax.ShapeDtypeStruct((bsz, seq, d), f32),
        compiler_params=_params(2),
        name="merge",
    )(ya, yb, ma, mb, x, gate1, wa, wb, wo)


def _mlp_kernel(h_ref, g2_ref, sc_ref, sh_ref, gate_ref, w1_ref, w2_ref, o_ref):
    hres = h_ref[0]
    d = hres.shape[1]
    ms = jnp.mean(hres * hres, axis=-1, keepdims=True)
    u = (hres * lax.rsqrt(ms + EPS) * g2_ref[...]) * (1.0 + sc_ref[0]) + sh_ref[0]
    ub = u.astype(bf16)
    ff = jnp.zeros(hres.shape, f32)
    for c in range(w1_ref.shape[1] // d):
        hid = jnp.maximum(_dot(ub, w1_ref[:, c * d:(c + 1) * d]), 0.0)
        ff = ff + _dot((hid * hid).astype(bf16), w2_ref[c * d:(c + 1) * d, :])
    o_ref[0] = hres + gate_ref[0] * ff


def _mlp(hres, g2, scale2, shift2, gate2, w1, w2):
    bsz, seq, d = hres.shape
    tok = pl.BlockSpec((1, TM, d), lambda b, i: (b, i, 0))
    mod = pl.BlockSpec((1, 1, d), lambda b, i: (b, 0, 0))
    const = lambda a: pl.BlockSpec(a.shape, lambda b, i: (0,) * a.ndim,
                                   pipeline_mode=pl.Buffered(1))
    return pl.pallas_call(
        _mlp_kernel,
        grid=(bsz, seq // TM),
        in_specs=[tok, pl.BlockSpec(g2.shape, lambda b, i: (0, 0)), mod, mod, mod,
                  const(w1), const(w2)],
        out_specs=tok,
        out_shape=jax.ShapeDtypeStruct((bsz, seq, d), f32),
        compiler_params=_params(2),
        name="mlp",
    )(hres, g2, scale2, shift2, gate2, w1, w2)


def _overlap_t(nc_pad, nsel_pad, nc, nsel):
    c_start = np.arange(nc_pad) * CMP_STRIDE
    s_start = np.arange(nsel_pad) * SEL_BLOCK
    ov = (np.minimum(c_start[None, :] + CMP_BLOCK, s_start[:, None] + SEL_BLOCK)
          - np.maximum(c_start[None, :], s_start[:, None]))
    ov = np.clip(ov, 0, CMP_BLOCK).astype(np.float32) / CMP_BLOCK
    ov[nsel:, :] = 0.0
    ov[:, nc:] = 0.0
    return ov


def _layer(h, mod, rel_tiles, tbl, p):
    bsz, seq, d = h.shape
    shift1, scale1, gate1, shift2, scale2, gate2 = [
        mod[:, k * d:(k + 1) * d].reshape(bsz, 1, d) for k in range(6)]
    cmp_bias, near_bias = rel_tiles

    pad_gain = lambda g, n: jnp.tile(jnp.pad(g, (0, LANES - HEAD_DIM)), n).reshape(1, n * LANES)
    gq = pad_gain(p["q_norm_g"], NSA_HEADS)
    gk = jnp.concatenate([pad_gain(p["k_norm_g"][k], NSA_KV_HEADS) for k in range(3)], axis=0)

    (qa, kc, vc, ks, vs, kw, vw, gates, qb, kb, vb, ma, mb) = _inproj(
        h, p["norm1_g"].reshape(1, d), scale1, shift1, _pack_w_in(p["w_in"], d), gq, gk)

    nch = seq // CMP_STRIDE
    kcmp, vcmp = _compress(kc, vc, p["cmp_pos"], p["cmp_k_w1"], p["cmp_k_w2"],
                           p["cmp_v_w1"], p["cmp_v_w2"], p["k_norm_g"][0].reshape(1, HEAD_DIM))

    nc = (seq - CMP_BLOCK) // CMP_STRIDE + 1
    ovt = jnp.asarray(_overlap_t(nch, HEAD_DIM, nc, seq // SEL_BLOCK), bf16)
    y_nsa = _nsa(tbl, qa, gates, kcmp, vcmp, ks, vs, kw, vw, cmp_bias, near_bias, ovt)

    y_sb = _sb(qb, kb, vb, jnp.asarray(np.tril(np.ones((TS, TS)), -1), bf16))

    h1 = _merge(y_nsa, y_sb, ma, mb, h, gate1, p["w_up_nsa"].astype(bf16),
                p["w_up_sb"].astype(bf16), p["w_out"].astype(bf16))
    return _mlp(h1, p["norm2_g"].reshape(1, d), scale2, shift2, gate2,
                p["mlp_w1"].astype(bf16), p["mlp_w2"].astype(bf16))


def kernel(x, c, rel_bias, ada_w, ada_b, norm1_g, norm2_g, w_in, cmp_pos, cmp_k_w1, cmp_k_w2,
           cmp_v_w1, cmp_v_w2, q_norm_g, k_norm_g, w_up_nsa, w_up_sb, w_out, mlp_w1, mlp_w2):
    bsz, seq, d = x.shape
    assert seq % TM == 0 and seq // SEL_BLOCK <= HEAD_DIM and seq >= WINDOW + TQ
    assert CMP_BLOCK == 2 * CMP_STRIDE and TQ == 2 * SEL_BLOCK
    tbl = rel_bias.astype(f32)
    rel_tiles = _bias_tiles(tbl, seq, seq // CMP_STRIDE)
    stacked = dict(norm1_g=norm1_g, norm2_g=norm2_g, w_in=w_in, cmp_pos=cmp_pos,
                   cmp_k_w1=cmp_k_w1, cmp_k_w2=cmp_k_w2, cmp_v_w1=cmp_v_w1, cmp_v_w2=cmp_v_w2,
                   q_norm_g=q_norm_g, k_norm_g=k_norm_g, w_up_nsa=w_up_nsa, w_up_sb=w_up_sb,
                   w_out=w_out, mlp_w1=mlp_w1, mlp_w2=mlp_w2)
    h = x
    for layer in range(ada_w.shape[0]):
        mod = _adaln(c, ada_w[layer], ada_b[layer])
        h = _layer(h, mod, rel_tiles, tbl, {k: v[layer] for k, v in stacked.items()})
    return h
```

```python
import functools
import math

import numpy as np
import jax
import jax.numpy as jnp
from jax import lax
from jax.experimental import pallas as pl
from jax.experimental.pallas import tpu as pltpu

f32 = jnp.float32
bf16 = jnp.bfloat16

HEAD_DIM = 64
NSA_HEADS = 8
NSA_KV_HEADS = 2
NSA_GROUP = NSA_HEADS // NSA_KV_HEADS
SB_HEADS = 8
CMP_BLOCK = 32
CMP_STRIDE = 16
SEL_BLOCK = 64
SEL_TOPK = 16
WINDOW = 512
N_BUCKETS = 32
MAX_DISTANCE = 128
EPS = 1e-6
FORCED_BONUS = 1e4
NEG_BLOCK = -1e9

LANES = 128
MASKED = -1e30
UNSELECTED = -1e9
EXP_UNDERFLOW = -104.0
VMEM_LIMIT = 56 * 1024 * 1024

TQ = 128
TS = 256
TM = 512


def _bucket_thresholds():
    n = np.arange(0, 4 * MAX_DISTANCE)
    max_exact = N_BUCKETS // 2
    nf = np.maximum(n, 1).astype(np.float32)
    large = max_exact + (np.log(nf / max_exact) / math.log(MAX_DISTANCE / max_exact)
                         * (N_BUCKETS - max_exact)).astype(np.int32)
    large = np.minimum(large, N_BUCKETS - 1)
    b = np.where(n < max_exact, n, large)
    assert np.all(np.diff(b) >= 0) and b[-1] == N_BUCKETS - 1
    return [int(np.argmax(b >= k)) for k in range(N_BUCKETS)]


BUCKET_START = _bucket_thresholds()
assert BUCKET_START[-1] <= LANES


def _dot(a, b):
    return jnp.dot(a, b, preferred_element_type=f32)


def _dot_nt(a, b):
    return lax.dot_general(a, b, (((1,), (1,)), ((), ())), preferred_element_type=f32)


def _split(a):
    hi = a.astype(bf16)
    lo = (a - hi.astype(f32)).astype(bf16)
    return hi, lo


def _params(n_grid):
    return pltpu.CompilerParams(dimension_semantics=("arbitrary",) * n_grid,
                                vmem_limit_bytes=VMEM_LIMIT)


def _adaln_kernel(c_ref, w_ref, b_ref, o_ref):
    c = c_ref[...]
    a = c * jax.nn.sigmoid(c)
    ah, al = _split(a)
    wh, wl = _split(w_ref[...])
    o_ref[...] = _dot(ah, wh) + _dot(ah, wl) + _dot(al, wh) + b_ref[...]


def _adaln(c, w, b):
    bsz, d = c.shape
    n = w.shape[1]
    return pl.pallas_call(
        _adaln_kernel,
        grid=(n // d,),
        in_specs=[pl.BlockSpec((bsz, d), lambda j: (0, 0)),
                  pl.BlockSpec((d, d), lambda j: (0, j)),
                  pl.BlockSpec((1, d), lambda j: (0, j))],
        out_specs=pl.BlockSpec((bsz, d), lambda j: (0, j)),
        out_shape=jax.ShapeDtypeStruct((bsz, n), f32),
        compiler_params=_params(1),
        name="adaln",
    )(c, w, b.reshape(1, n))


def _bias_of_dist(dist, tbl_ref, h):
    out = jnp.full(dist.shape, tbl_ref[0, h], f32)
    for k in range(1, N_BUCKETS):
        out = jnp.where(dist >= BUCKET_START[k], tbl_ref[k, h], out)
    return jnp.where(dist >= 0, out, MASKED)


def _cmp_bias_kernel(tbl_ref, o_ref):
    h = pl.program_id(0)
    i = pl.program_id(1)
    rows, nc = o_ref.shape[1], o_ref.shape[2]
    t = i * rows + lax.broadcasted_iota(jnp.int32, (rows, nc), 0)
    j = lax.broadcasted_iota(jnp.int32, (rows, nc), 1)
    o_ref[0] = _bias_of_dist(t - (j * CMP_STRIDE + CMP_BLOCK - 1), tbl_ref, h)


def _near_bias_kernel(tbl_ref, o_ref):
    h = pl.program_id(0)
    r = lax.broadcasted_iota(jnp.int32, (TQ, 2 * TQ), 0)
    c = lax.broadcasted_iota(jnp.int32, (TQ, 2 * TQ), 1)
    o_ref[0] = _bias_of_dist(r - c + TQ, tbl_ref, h)


def _bias_tiles(rel_bias, seq, nc_pad):
    tbl = rel_bias.astype(f32)
    smem = pl.BlockSpec(memory_space=pltpu.SMEM)
    rows = 512
    cmp_bias = pl.pallas_call(
        _cmp_bias_kernel,
        grid=(NSA_HEADS, seq // rows),
        in_specs=[smem],
        out_specs=pl.BlockSpec((1, rows, nc_pad), lambda h, i: (h, i, 0)),
        out_shape=jax.ShapeDtypeStruct((NSA_HEADS, seq, nc_pad), f32),
        compiler_params=_params(2),
        name="cmp_bias",
    )(tbl)
    near_bias = pl.pallas_call(
        _near_bias_kernel,
        grid=(NSA_HEADS,),
        in_specs=[smem],
        out_specs=pl.BlockSpec((1, TQ, 2 * TQ), lambda h: (h, 0, 0)),
        out_shape=jax.ShapeDtypeStruct((NSA_HEADS, TQ, 2 * TQ), f32),
        compiler_params=_params(1),
        name="near_bias",
    )(tbl)
    return cmp_bias, near_bias


_QA_W = NSA_HEADS * LANES
_KV_W = NSA_KV_HEADS * LANES
_CMP_W = NSA_KV_HEADS * HEAD_DIM
_SB_W = SB_HEADS * HEAD_DIM


def _layout(d_model):
    names = ["qa", "kc", "vc", "ksl", "vsl", "kwn", "vwn", "g", "qb", "kb", "vb", "ma", "mb"]
    widths = [_QA_W, _CMP_W, _CMP_W, _KV_W, _KV_W, _KV_W, _KV_W, _KV_W, _SB_W, _SB_W, _SB_W,
              d_model, d_model]
    offs = np.concatenate([[0], np.cumsum(widths)])
    return {n: (int(offs[i]), int(offs[i + 1])) for i, n in enumerate(names)}, int(offs[-1])


def _pack_w_in(w_in, d_model):
    q_w = NSA_HEADS * HEAD_DIM
    kv_w = NSA_KV_HEADS * HEAD_DIM
    g_w = NSA_HEADS * 3
    sizes = [q_w] + [kv_w] * 6 + [g_w] + [_SB_W] * 3 + [d_model, d_model]
    offs = np.concatenate([[0], np.cumsum(sizes)])
    parts = [w_in[:, int(offs[i]):int(offs[i + 1])] for i in range(len(sizes))]
    d = w_in.shape[0]

    def pad_heads(p, n_heads, width):
        p = p.reshape(d, n_heads, width)
        p = jnp.pad(p, ((0, 0), (0, 0), (0, LANES - width)))
        return p.reshape(d, n_heads * LANES)

    packed = [pad_heads(parts[0], NSA_HEADS, HEAD_DIM), parts[1], parts[2]]
    packed += [pad_heads(parts[k], NSA_KV_HEADS, HEAD_DIM) for k in (3, 4, 5, 6)]
    packed += [pad_heads(parts[7], NSA_KV_HEADS, NSA_GROUP * 3)]
    packed += parts[8:]
    return jnp.concatenate(packed, axis=1).astype(bf16)


def _inproj_kernel(lay, x_ref, g1_ref, sc_ref, sh_ref, w_ref, gq_ref, gk_ref,
                   qa_ref, kc_ref, vc_ref, ks_ref, vs_ref, kw_ref, vw_ref, g_ref,
                   qb_ref, kb_ref, vb_ref, ma_ref, mb_ref):
    i = pl.program_id(1)
    x = x_ref[0]
    ms = jnp.mean(x * x, axis=-1, keepdims=True)
    u = (x * lax.rsqrt(ms + EPS) * g1_ref[...]) * (1.0 + sc_ref[0]) + sh_ref[0]
    ub = u.astype(bf16)

    def proj(name):
        lo, hi = lay[name]
        return _dot(ub, w_ref[:, lo:hi])

    def head_norm(z, gain):
        parts = []
        for g in range(z.shape[1] // LANES):
            zg = z[:, g * LANES:(g + 1) * LANES]
            ss = jnp.sum(zg * zg, axis=1, keepdims=True)
            parts.append(zg * lax.rsqrt(ss * (1.0 / HEAD_DIM) + EPS))
        return jnp.concatenate(parts, axis=1) * gain

    scale = HEAD_DIM ** -0.5
    qa_ref[0] = (head_norm(proj("qa"), gq_ref[...]) * scale).astype(bf16)
    kc_ref[0] = proj("kc")
    vc_ref[0] = proj("vc")

    rows = x.shape[0]
    lane = lax.broadcasted_iota(jnp.int32, (rows, _KV_W), 1)
    tok_blk = (i * rows + lax.broadcasted_iota(jnp.int32, (rows, _KV_W), 0)) // SEL_BLOCK
    in_pad = (lane & HEAD_DIM) != 0
    onehot = in_pad & ((lane & (HEAD_DIM - 1)) == tok_blk)
    ones_col = (lane & (LANES - 1)) == HEAD_DIM

    ks = head_norm(proj("ksl"), gk_ref[1:2, :])
    ks_ref[0] = jnp.where(onehot, 1.0, ks).astype(bf16)
    vs_ref[0] = jnp.where(ones_col, 1.0, proj("vsl")).astype(bf16)
    kw_ref[0] = head_norm(proj("kwn"), gk_ref[2:3, :]).astype(bf16)
    vw_ref[0] = jnp.where(ones_col, 1.0, proj("vwn")).astype(bf16)
    g_ref[0] = jax.nn.sigmoid(proj("g"))
    qb_ref[0] = (proj("qb") * scale).astype(bf16)
    kb_ref[0] = proj("kb").astype(bf16)
    vb_ref[0] = proj("vb").astype(bf16)
    ma_ref[0] = jax.nn.sigmoid(proj("ma")).astype(bf16)
    mb_ref[0] = jax.nn.sigmoid(proj("mb")).astype(bf16)


def _inproj(x, g1, scale1, shift1, w_packed, gq, gk):
    bsz, seq, d = x.shape
    lay, width = _layout(d)
    assert w_packed.shape == (d, width)
    tok = lambda w: pl.BlockSpec((1, TM, w), lambda b, i: (b, i, 0))
    full = lambda a: pl.BlockSpec(a.shape, lambda b, i: (0,) * a.ndim,
                                  pipeline_mode=pl.Buffered(1))
    mod = pl.BlockSpec((1, 1, d), lambda b, i: (b, 0, 0))
    out_w =[(_QA_W, bf16), (_CMP_W, f32), (_CMP_W, f32), (_KV_W, bf16), (_KV_W, bf16),
             (_KV_W, bf16), (_KV_W, bf16), (_KV_W, f32), (_SB_W, bf16), (_SB_W, bf16),
             (_SB_W, bf16), (d, bf16), (d, bf16)]
    return pl.pallas_call(
        functools.partial(_inproj_kernel, lay),
        grid=(bsz, seq // TM),
        in_specs=[tok(d), full(g1), mod, mod, full(w_packed), full(gq), full(gk)],
        out_specs=[tok(w) for w, _ in out_w],
        out_shape=[jax.ShapeDtypeStruct((bsz, seq, w), dt) for w, dt in out_w],
        compiler_params=_params(2),
        name="inproj",
    )(x, g1, scale1, shift1, w_packed, gq, gk)


def _compress_kernel(xk_ref, xv_ref, pos_ref, w1k_ref, w2k_ref, w1v_ref, w2v_ref, gk_ref,
                     ko_ref, vo_ref):
    nch = xk_ref.shape[1] // CMP_STRIDE

    def mlp(x_ref, w1_ref, w2_ref):
        first = jnp.zeros((nch, LANES), f32)
        second = jnp.zeros((nch, LANES), f32)
        for l in range(CMP_STRIDE):
            xl = x_ref[0, pl.ds(l, nch, stride=CMP_STRIDE), :]
            lo = l + CMP_STRIDE
            first = first + _dot((xl + pos_ref[l:l + 1, :]).astype(bf16), w1_ref[l])
            second = second + _dot((xl + pos_ref[lo:lo + 1, :]).astype(bf16), w1_ref[lo])
        pre = first + pltpu.roll(second, nch - 1, 0)
        hid = pre * jax.nn.sigmoid(pre)
        return _dot(hid.astype(bf16), w2_ref[...])

    k = mlp(xk_ref, w1k_ref, w2k_ref)
    v = mlp(xv_ref, w1v_ref, w2v_ref)
    for h in range(NSA_KV_HEADS):
        kh = k[:, h * HEAD_DIM:(h + 1) * HEAD_DIM]
        ms = jnp.mean(kh * kh, axis=-1, keepdims=True)
        ko_ref[0, h] = (kh * lax.rsqrt(ms + EPS) * gk_ref[...]).astype(bf16)
        vo_ref[0, h] = v[:, h * HEAD_DIM:(h + 1) * HEAD_DIM].astype(bf16)


def _compress(xk, xv, pos, w1k, w2k, w1v, w2v, gk0):
    bsz, seq, width = xk.shape
    nch = seq // CMP_STRIDE
    assert width == NSA_KV_HEADS * HEAD_DIM == LANES

    def both_heads(w):
        z = jnp.zeros_like(w)
        return jnp.concatenate([jnp.concatenate([w, z], axis=2),
                                jnp.concatenate([z, w], axis=2)], axis=1).astype(bf16)

    w1 = lambda w: both_heads(w.reshape(CMP_BLOCK, HEAD_DIM, w.shape[1]))
    w2 = lambda w: both_heads(w[None])[0]
    args = (xk, xv, jnp.tile(pos, (1, NSA_KV_HEADS)), w1(w1k), w2(w2k), w1(w1v), w2(w2v), gk0)
    blk = pl.BlockSpec((1, seq, width), lambda b: (b, 0, 0))
    full = lambda a: pl.BlockSpec(a.shape, lambda b: (0,) * a.ndim)
    out = pl.BlockSpec((1, NSA_KV_HEADS, nch, HEAD_DIM), lambda b: (b, 0, 0, 0))
    shape = jax.ShapeDtypeStruct((bsz, NSA_KV_HEADS, nch, HEAD_DIM), bf16)
    return pl.pallas_call(
        _compress_kernel,
        grid=(bsz,),
        in_specs=[blk, blk] + [full(a) for a in args[2:]],
        out_specs=[out, out],
        out_shape=[shape, shape],
        compiler_params=_params(1),
        name="compress",
    )(*args)


def _flash_step(s, v, m_ref, acc_ref):
    m_prev = m_ref[...]
    m_new = jnp.maximum(m_prev, jnp.max(s, axis=1, keepdims=True))
    alpha = jnp.exp(m_prev - m_new)
    p = jnp.exp(s - jnp.concatenate([m_new] * (s.shape[1] // LANES), axis=1))
    acc_ref[...] = alpha * acc_ref[...] + _dot(p.astype(bf16), v)
    m_ref[...] = m_new


def _flash_init(m_ref, acc_ref):
    m_ref[...] = jnp.full(m_ref.shape, MASKED, f32)
    acc_ref[...] = jnp.zeros(acc_ref.shape, f32)


def _flash_out(acc_ref):
    acc = acc_ref[...]
    return acc[:, :HEAD_DIM] / acc[:, HEAD_DIM:HEAD_DIM + 1]


def _nsa_kernel(n_top, tbl_ref, q_ref, g_ref, kc_ref, vc_ref, ks_ref, vs_ref, kw_ref, vw_ref,
                bc_ref, bn_ref, ovt_ref, o_ref, qs_ref, qw_ref, b31_ref, oc_ref,
                m_ref, acc_ref, mw_ref, accw_ref):
    i = pl.program_id(1)
    kv_heads = range(NSA_KV_HEADS)
    rows = NSA_GROUP * TQ
    start = i * TQ
    assert WINDOW == 4 * TQ

    def last_bucket(hk, width):
        return jnp.concatenate([b31_ref[hk]] * (width // LANES), axis=1)

    def attend(hk, q_rows_ref, k_ref, v_ref, off, width, bias, state):
        off = pl.multiple_of(off, TQ)
        lanes = slice(hk * LANES, (hk + 1) * LANES)
        s = _dot_nt(q_rows_ref[hk], k_ref[0, pl.ds(off, width), lanes]) + bias
        _flash_step(s, v_ref[0, pl.ds(off, width), lanes], state[0].at[hk], state[1].at[hk])

    def when(cond, guarded):
        return pl.when(cond) if guarded else (lambda fn: fn())

    def near_steps(q_rows_ref, k_ref, v_ref, state, guarded):
        group = lambda hk: slice(hk * NSA_GROUP, (hk + 1) * NSA_GROUP)

        @when(i >= 1, guarded)
        def _():
            for hk in kv_heads:
                attend(hk, q_rows_ref, k_ref, v_ref, start - TQ, 2 * TQ,
                       bn_ref[group(hk)].reshape(rows, 2 * TQ), state)

        if guarded:
            @pl.when(i == 0)
            def _():
                for hk in kv_heads:
                    attend(hk, q_rows_ref, k_ref, v_ref, start, TQ,
                           bn_ref[group(hk), :, TQ:].reshape(rows, TQ), state)

    def select(hk):
        heads = range(hk * NSA_GROUP, (hk + 1) * NSA_GROUP)
        qpad = jnp.concatenate([q_ref[0, :, g * LANES:(g + 1) * LANES] for g in heads], axis=0)
        qw_ref[hk] = qpad
        b31_ref[hk] = jnp.concatenate(
            [jnp.full((TQ, LANES), tbl_ref[N_BUCKETS - 1, g], f32) for g in heads], axis=0)

        bc = bc_ref[hk * NSA_GROUP:(hk + 1) * NSA_GROUP].reshape(rows, bc_ref.shape[2])
        s_c = _dot_nt(qpad[:, :HEAD_DIM], kc_ref[0, hk]) + bc
        visible = bc > 0.5 * MASKED
        m_c = jnp.max(s_c, axis=1, keepdims=True)
        e_c = jnp.where(visible, jnp.exp(s_c - m_c), 0.0)
        p_c = e_c / jnp.maximum(jnp.sum(e_c, axis=1, keepdims=True), 1e-30)
        oc_ref[hk] = _dot(p_c.astype(bf16), vc_ref[0, hk])

        p_sum = p_c[0:TQ]
        for g in range(1, NSA_GROUP):
            p_sum = p_sum + p_c[g * TQ:(g + 1) * TQ]
        p_hi, p_lo = _split(p_sum)
        imp = _dot_nt(ovt_ref[...], p_hi) + _dot_nt(ovt_ref[...], p_lo)
        nblk = imp.shape[0]
        blk = lax.broadcasted_iota(jnp.int32, (nblk, TQ), 0)
        cur = (start + lax.broadcasted_iota(jnp.int32, (nblk, TQ), 1)) // SEL_BLOCK
        forced = (blk == 0) | (blk == cur) | (blk == cur - 1)
        imp = jnp.where(blk > cur, NEG_BLOCK, imp + jnp.where(forced, FORCED_BONUS, 0.0))
        rank = jnp.zeros((nblk, TQ), f32)
        for b2 in range(nblk):
            other = imp[b2:b2 + 1, :]
            rank = rank + jnp.where(blk > b2, jnp.where(other >= imp, 1.0, 0.0),
                                    jnp.where(other > imp, 1.0, 0.0))
        usable = (rank < n_top) & (blk <= cur)
        sel_t = jnp.where(usable, 0.0, UNSELECTED)
        sel_pad = jnp.concatenate([jnp.zeros((LANES - nblk, TQ), f32), sel_t], axis=0).T
        sel_rows = jnp.concatenate([sel_pad.astype(bf16)] * NSA_GROUP, axis=0)
        qs_ref[hk] = qpad + sel_rows

    def head(guarded):
        for hk in kv_heads:
            select(hk)

        win = (mw_ref, accw_ref)
        _flash_init(*win)

        @when(i >= 4, guarded)
        def _():
            r = lax.broadcasted_iota(jnp.int32, (rows, TQ), 0) & (TQ - 1)
            c = lax.broadcasted_iota(jnp.int32, (rows, TQ), 1)
            for hk in kv_heads:
                attend(hk, qw_ref, kw_ref, vw_ref, start - 4 * TQ, TQ,
                       last_bucket(hk, TQ) + jnp.where(c > r, 0.0, MASKED), win)

        @when(i >= 3, guarded)
        def _():
            for hk in kv_heads:
                attend(hk, qw_ref, kw_ref, vw_ref, start - 3 * TQ, 2 * TQ,
                       last_bucket(hk, 2 * TQ), win)

        if guarded:
            @pl.when(i == 2)
            def _():
                for hk in kv_heads:
                    attend(hk, qw_ref, kw_ref, vw_ref, start - 2 * TQ, TQ,
                           last_bucket(hk, TQ), win)

        near_steps(qw_ref, kw_ref, vw_ref, win, guarded)
        _flash_init(m_ref, acc_ref)

    def tail(guarded):
        near_steps(qs_ref, ks_ref, vs_ref, (m_ref, acc_ref), guarded)
        outs = []
        for hk in kv_heads:
            o_c = oc_ref[hk]
            o_s = _flash_out(acc_ref.at[hk])
            o_w = _flash_out(accw_ref.at[hk])
            gates = g_ref[0, :, hk * LANES:(hk + 1) * LANES]
            for g in range(NSA_GROUP):
                sl = slice(g * TQ, (g + 1) * TQ)
                outs.append(gates[:, 3 * g:3 * g + 1] * o_c[sl]
                            + gates[:, 3 * g + 1:3 * g + 2] * o_s[sl]
                            + gates[:, 3 * g + 2:3 * g + 3] * o_w[sl])
        o_ref[0] = jnp.concatenate(outs, axis=1).astype(bf16)

    interior = i >= 4
    pl.when(interior)(lambda: head(False))
    pl.when(jnp.logical_not(interior))(lambda: head(True))

    sel = (m_ref, acc_ref)
    n_far = jnp.maximum(i - 1, 0)

    def far_steps(off, n_steps):
        for step in range(n_steps):
            for hk in kv_heads:
                attend(hk, qs_ref, ks_ref, vs_ref, off + step * (2 * TQ), 2 * TQ,
                       last_bucket(hk, 2 * TQ), sel)

    def far_trip(c, carry):
        far_steps(c * (8 * TQ), 4)
        return carry

    lax.fori_loop(0, n_far // 8, far_trip, 0)
    rem_off = (n_far // 8) * (8 * TQ)

    @pl.when((n_far & 4) != 0)
    def _():
        far_steps(rem_off, 2)

    @pl.when((n_far & 2) != 0)
    def _():
        far_steps(rem_off + (n_far & 4) * TQ, 1)

    @pl.when((n_far & 1) != 0)
    def _():
        for hk in kv_heads:
            attend(hk, qs_ref, ks_ref, vs_ref, rem_off + (n_far & 6) * TQ, TQ,
                   last_bucket(hk, TQ), sel)

    pl.when(interior)(lambda: tail(False))
    pl.when(jnp.logical_not(interior))(lambda: tail(True))


def _nsa(tbl, qa, gates, kcmp, vcmp, ks, vs, kw, vw, cmp_bias, near_bias, ovt):
    bsz, seq, _ = qa.shape
    nc = kcmp.shape[2]
    n_top = min(SEL_TOPK, seq // SEL_BLOCK)
    rows = NSA_GROUP * TQ
    tok = lambda w: pl.BlockSpec((1, TQ, w), lambda b, i: (b, i, 0))
    kv = pl.BlockSpec((1, seq, NSA_KV_HEADS * LANES), lambda b, i: (b, 0, 0))
    cmp = pl.BlockSpec((1, NSA_KV_HEADS, nc, HEAD_DIM), lambda b, i: (b, 0, 0, 0))
    per_kv = lambda width, dt: pltpu.VMEM((NSA_KV_HEADS, rows, width), dt)
    return pl.pallas_call(
        functools.partial(_nsa_kernel, n_top),
        grid=(bsz, seq // TQ),
        in_specs=[pl.BlockSpec(memory_space=pltpu.SMEM),
                  tok(NSA_HEADS * LANES), tok(NSA_KV_HEADS * LANES),
                  cmp, cmp, kv, kv, kv, kv,
                  pl.BlockSpec((NSA_HEADS, TQ, nc), lambda b, i: (0, i, 0)),
                  pl.BlockSpec((NSA_HEADS, TQ, 2 * TQ), lambda b, i: (0, 0, 0)),
                  pl.BlockSpec(ovt.shape, lambda b, i: (0, 0))],
        out_specs=tok(NSA_HEADS * HEAD_DIM),
        out_shape=jax.ShapeDtypeStruct((bsz, seq, NSA_HEADS * HEAD_DIM), bf16),
        scratch_shapes=[per_kv(LANES, bf16), per_kv(LANES, bf16), per_kv(LANES, f32),
                        per_kv(HEAD_DIM, f32), per_kv(LANES, f32), per_kv(LANES, f32),
                        per_kv(LANES, f32), per_kv(LANES, f32)],
        compiler_params=_params(2),
        name="nsa",
    )(tbl, qa, gates, kcmp, vcmp, ks, vs, kw, vw, cmp_bias, near_bias, ovt)


def _sb_kernel(q_ref, k_ref, v_ref, tri_ref, o_ref, carry_ref, acc_ref):
    i = pl.program_id(2)
    n_heads = carry_ref.shape[0]
    lane = lax.broadcasted_iota(jnp.int32, (TS, LANES), 1)
    q_heads = []
    for pair in range(n_heads // 2):
        q = q_ref[0, :, pair * LANES:(pair + 1) * LANES]
        zero = jnp.zeros_like(q)
        q_heads += [jnp.where(lane < HEAD_DIM, q, zero), jnp.where(lane >= HEAD_DIM, q, zero)]
    pair_lanes = lambda hh: slice((hh // 2) * LANES, (hh // 2 + 1) * LANES)
    carry_ref[...] = jnp.zeros(carry_ref.shape, f32)
    acc_ref[...] = jnp.zeros(acc_ref.shape, f32)
    r = lax.broadcasted_iota(jnp.int32, (TS, TS), 0)
    c = lax.broadcasted_iota(jnp.int32, (TS, TS), 1)
    before = c < r

    def chunks(jobs):
        offs = [pl.multiple_of(off, TS) for off, _ in jobs]
        stage = []
        for (_, diagonal), off in zip(jobs, offs):
            for hh in range(n_heads):
                z = _dot_nt(q_heads[hh], k_ref[0, pl.ds(off, TS), pair_lanes(hh)])
                soft = jnp.log(1.0 + jnp.exp(-jnp.abs(z)))
                log_keep = jnp.minimum(-z, 0.0) - soft
                log_sig = jnp.minimum(z, 0.0) - soft
                if diagonal:
                    log_keep = jnp.where(before, log_keep, 0.0)
                hi, lo = _split(log_keep)
                later = _dot(hi, tri_ref[...]) + _dot(lo, tri_ref[...])
                stage.append((log_sig + later, jnp.sum(log_keep, axis=1, keepdims=True)))
        for hh in range(n_heads):
            carry = carry_ref[hh]
            acc = acc_ref[hh]
            for j, ((_, diagonal), off) in enumerate(zip(jobs, offs)):
                base, total = stage[n_heads * j + hh]
                a = jnp.exp(base + jnp.concatenate([carry] * (TS // LANES), axis=1))
                if diagonal:
                    a = jnp.where(before, a, 0.0)
                acc = acc + _dot(a.astype(bf16), v_ref[0, pl.ds(off, TS), pair_lanes(hh)])
                carry = carry + total
            carry_ref[hh] = carry
            acc_ref[hh] = acc

    @pl.when(i == 0)
    def _():
        chunks([(0, True)])

    @pl.when(i >= 1)
    def _():
        chunks([(i * TS, True), ((i - 1) * TS, False)])

    def any_live():
        return jnp.max(carry_ref[...]) > EXP_UNDERFLOW

    def more(state):
        n, live = state
        return jnp.logical_and(n < i, live)

    def older(state):
        n, _ = state
        chunks([((i - 1 - n) * TS, False)])
        return n + 1, any_live()

    lax.while_loop(more, older, (jnp.int32(1), any_live()))
    o_ref[0] = jnp.concatenate(
        [jnp.where(lane < HEAD_DIM, acc_ref[2 * pair], acc_ref[2 * pair + 1])
         for pair in range(n_heads // 2)], axis=1).astype(bf16)


SB_STEP_HEADS = 4


def _sb(qb, kb, vb, tri):
    bsz, seq, width = qb.shape
    step_w = SB_STEP_HEADS * HEAD_DIM
    q_spec = pl.BlockSpec((1, TS, step_w), lambda b, p, i: (b, i, p))
    kv_spec = pl.BlockSpec((1, seq, step_w), lambda b, p, i: (b, 0, p))
    state = pltpu.VMEM((SB_STEP_HEADS, TS, LANES), f32)
    return pl.pallas_call(
        _sb_kernel,
        grid=(bsz, width // step_w, seq // TS),
        in_specs=[q_spec, kv_spec, kv_spec, pl.BlockSpec(tri.shape, lambda b, p, i: (0, 0))],
        out_specs=q_spec,
        out_shape=jax.ShapeDtypeStruct((bsz, seq, width), bf16),
        scratch_shapes=[state, state],
        compiler_params=_params(3),
        name="sb",
    )(qb, kb, vb, tri)


def _merge_kernel(ya_ref, yb_ref, ma_ref, mb_ref, x_ref, gate_ref, wa_ref, wb_ref, wo_ref, o_ref):
    y_a = _dot(ya_ref[0], wa_ref[...])
    y_b = _dot(yb_ref[0], wb_ref[...])
    mixed = ma_ref[0].astype(f32) * y_a + mb_ref[0].astype(f32) * y_b
    o_ref[0] = x_ref[0] + gate_ref[0] * _dot(mixed.astype(bf16), wo_ref[...])


def _merge(ya, yb, ma, mb, x, gate1, wa, wb, wo):
    bsz, seq, d = x.shape
    tok = lambda w: pl.BlockSpec((1, TM, w), lambda b, i: (b, i, 0))
    full = lambda a: pl.BlockSpec(a.shape, lambda b, i: (0,) * a.ndim)
    return pl.pallas_call(
        _merge_kernel,
        grid=(bsz, seq // TM),
        in_specs=[tok(ya.shape[2]), tok(yb.shape[2]), tok(d), tok(d), tok(d),
                  pl.BlockSpec((1, 1, d), lambda b, i: (b, 0, 0)), full(wa), full(wb), full(wo)],
        out_specs=tok(d),
        out_shape=jax.ShapeDtypeStruct((bsz, seq, d), f32),
        compiler_params=_params(2),
        name="merge",
    )(ya, yb, ma, mb, x, gate1, wa, wb, wo)


def _mlp_kernel(h_ref, g2_ref, sc_ref, sh_ref, gate_ref, w1_ref, w2_ref, o_ref):
    hres = h_ref[0]
    d = hres.shape[1]
    ms = jnp.mean(hres * hres, axis=-1, keepdims=True)
    u = (hres * lax.rsqrt(ms + EPS) * g2_ref[...]) * (1.0 + sc_ref[0]) + sh_ref[0]
    ub = u.astype(bf16)
    ff = jnp.zeros(hres.shape, f32)
    for c in range(w1_ref.shape[1] // d):
        hid = jnp.maximum(_dot(ub, w1_ref[:, c * d:(c + 1) * d]), 0.0)
        ff = ff + _dot((hid * hid).astype(bf16), w2_ref[c * d:(c + 1) * d, :])
    o_ref[0] = hres + gate_ref[0] * ff


def _mlp(hres, g2, scale2, shift2, gate2, w1, w2):
    bsz, seq, d = hres.shape
    tok = pl.BlockSpec((1, TM, d), lambda b, i: (b, i, 0))
    mod = pl.BlockSpec((1, 1, d), lambda b, i: (b, 0, 0))
    const = lambda a: pl.BlockSpec(a.shape, lambda b, i: (0,) * a.ndim,
                                   pipeline_mode=pl.Buffered(1))
    return pl.pallas_call(
        _mlp_kernel,
        grid=(bsz, seq // TM),
        in_specs=[tok, pl.BlockSpec(g2.shape, lambda b, i: (0, 0)), mod, mod, mod,
                  const(w1), const(w2)],
        out_specs=tok,
        out_shape=jax.ShapeDtypeStruct((bsz, seq, d), f32),
        compiler_params=_params(2),
        name="mlp",
    )(hres, g2, scale2, shift2, gate2, w1, w2)


def _overlap_t(nc_pad, nsel_pad, nc, nsel):
    c_start = np.arange(nc_pad) * CMP_STRIDE
    s_start = np.arange(nsel_pad) * SEL_BLOCK
    ov = (np.minimum(c_start[None, :] + CMP_BLOCK, s_start[:, None] + SEL_BLOCK)
          - np.maximum(c_start[None, :], s_start[:, None]))
    ov = np.clip(ov, 0, CMP_BLOCK).astype(np.float32) / CMP_BLOCK
    ov[nsel:, :] = 0.0
    ov[:, nc:] = 0.0
    return ov


def _layer(h, mod, rel_tiles, tbl, p):
    bsz, seq, d = h.shape
    shift1, scale1, gate1, shift2, scale2, gate2 = [
        mod[:, k * d:(k + 1) * d].reshape(bsz, 1, d) for k in range(6)]
    cmp_bias, near_bias = rel_tiles

    pad_gain = lambda g, n: jnp.tile(jnp.pad(g, (0, LANES - HEAD_DIM)), n).reshape(1, n * LANES)
    gq = pad_gain(p["q_norm_g"], NSA_HEADS)
    gk = jnp.concatenate([pad_gain(p["k_norm_g"][k], NSA_KV_HEADS) for k in range(3)], axis=0)

    (qa, kc, vc, ks, vs, kw, vw, gates, qb, kb, vb, ma, mb) = _inproj(
        h, p["norm1_g"].reshape(1, d), scale1, shift1, _pack_w_in(p["w_in"], d), gq, gk)

    nch = seq // CMP_STRIDE
    kcmp, vcmp = _compress(kc, vc, p["cmp_pos"], p["cmp_k_w1"], p["cmp_k_w2"],
                           p["cmp_v_w1"], p["cmp_v_w2"], p["k_norm_g"][0].reshape(1, HEAD_DIM))

    nc = (seq - CMP_BLOCK) // CMP_STRIDE + 1
    ovt = jnp.asarray(_overlap_t(nch, HEAD_DIM, nc, seq // SEL_BLOCK), bf16)
    y_nsa = _nsa(tbl, qa, gates, kcmp, vcmp, ks, vs, kw, vw, cmp_bias, near_bias, ovt)

    y_sb = _sb(qb, kb, vb, jnp.asarray(np.tril(np.ones((TS, TS)), -1), bf16))

    h1 = _merge(y_nsa, y_sb, ma, mb, h, gate1, p["w_up_nsa"].astype(bf16),
                p["w_up_sb"].astype(bf16), p["w_out"].astype(bf16))
    return _mlp(h1, p["norm2_g"].reshape(1, d), scale2, shift2, gate2,
                p["mlp_w1"].astype(bf16), p["mlp_w2"].astype(bf16))


def kernel(x, c, rel_bias, ada_w, ada_b, norm1_g, norm2_g, w_in, cmp_pos, cmp_k_w1, cmp_k_w2,
           cmp_v_w1, cmp_v_w2, q_norm_g, k_norm_g, w_up_nsa, w_up_sb, w_out, mlp_w1, mlp_w2):
    bsz, seq, d = x.shape
    assert seq % TM == 0 and seq // SEL_BLOCK <= HEAD_DIM and seq >= WINDOW + TQ
    assert CMP_BLOCK == 2 * CMP_STRIDE and TQ == 2 * SEL_BLOCK
    tbl = rel_bias.astype(f32)
    rel_tiles = _bias_tiles(tbl, seq, seq // CMP_STRIDE)
    stacked = dict(norm1_g=norm1_g, norm2_g=norm2_g, w_in=w_in, cmp_pos=cmp_pos,
                   cmp_k_w1=cmp_k_w1, cmp_k_w2=cmp_k_w2, cmp_v_w1=cmp_v_w1, cmp_v_w2=cmp_v_w2,
                   q_norm_g=q_norm_g, k_norm_g=k_norm_g, w_up_nsa=w_up_nsa, w_up_sb=w_up_sb,
                   w_out=w_out, mlp_w1=mlp_w1, mlp_w2=mlp_w2)
    h = x
    for layer in range(ada_w.shape[0]):
        mod = _adaln(c, ada_w[layer], ada_b[layer])
        h = _layer(h, mod, rel_tiles, tbl, {k: v[layer] for k, v in stacked.items()})
    return h
```

```python
import functools
import math

import numpy as np
import jax
import jax.numpy as jnp
from jax import lax
from jax.experimental import pallas as pl
from jax.experimental.pallas import tpu as pltpu

f32 = jnp.float32
bf16 = jnp.bfloat16

HEAD_DIM = 64
NSA_HEADS = 8
NSA_KV_HEADS = 2
NSA_GROUP = NSA_HEADS // NSA_KV_HEADS
SB_HEADS = 8
CMP_BLOCK = 32
CMP_STRIDE = 16
SEL_BLOCK = 64
SEL_TOPK = 16
WINDOW = 512
N_BUCKETS = 32
MAX_DISTANCE = 128
EPS = 1e-6
FORCED_BONUS = 1e4
NEG_BLOCK = -1e9

LANES = 128
MASKED = -1e30
UNSELECTED = -1e9
LOG2E = math.log2(math.e)
EXP2_UNDERFLOW = -150.0
VMEM_LIMIT = 56 * 1024 * 1024

TQ = 128
TS = 256
TM = 512


def _bucket_thresholds():
    n = np.arange(0, 4 * MAX_DISTANCE)
    max_exact = N_BUCKETS // 2
    nf = np.maximum(n, 1).astype(np.float32)
    large = max_exact + (np.log(nf / max_exact) / math.log(MAX_DISTANCE / max_exact)
                         * (N_BUCKETS - max_exact)).astype(np.int32)
    large = np.minimum(large, N_BUCKETS - 1)
    b = np.where(n < max_exact, n, large)
    assert np.all(np.diff(b) >= 0) and b[-1] == N_BUCKETS - 1
    return [int(np.argmax(b >= k)) for k in range(N_BUCKETS)]


BUCKET_START = _bucket_thresholds()
assert BUCKET_START[-1] <= LANES


def _dot(a, b):
    return jnp.dot(a, b, preferred_element_type=f32)


def _dot_nt(a, b):
    return lax.dot_general(a, b, (((1,), (1,)), ((), ())), preferred_element_type=f32)


def _split(a):
    hi = a.astype(bf16)
    lo = (a - hi.astype(f32)).astype(bf16)
    return hi, lo


def _params(n_grid):
    return pltpu.CompilerParams(dimension_semantics=("arbitrary",) * n_grid,
                                vmem_limit_bytes=VMEM_LIMIT)


def _adaln_kernel(c_ref, w_ref, b_ref, o_ref):
    c = c_ref[...]
    a = c * jax.nn.sigmoid(c)
    ah, al = _split(a)
    wh, wl = _split(w_ref[...])
    o_ref[...] = _dot(ah, wh) + _dot(ah, wl) + _dot(al, wh) + b_ref[...]


def _adaln(c, w, b):
    bsz, d = c.shape
    n = w.shape[1]
    return pl.pallas_call(
        _adaln_kernel,
        grid=(n // d,),
        in_specs=[pl.BlockSpec((bsz, d), lambda j: (0, 0)),
                  pl.BlockSpec((d, d), lambda j: (0, j)),
                  pl.BlockSpec((1, d), lambda j: (0, j))],
        out_specs=pl.BlockSpec((bsz, d), lambda j: (0, j)),
        out_shape=jax.ShapeDtypeStruct((bsz, n), f32),
        compiler_params=_params(1),
        name="adaln",
    )(c, w, b.reshape(1, n))


def _bias_of_dist(dist, tbl_ref, h):
    out = jnp.full(dist.shape, tbl_ref[0, h], f32)
    for k in range(1, N_BUCKETS):
        out = jnp.where(dist >= BUCKET_START[k], tbl_ref[k, h], out)
    return jnp.where(dist >= 0, out, MASKED)


def _cmp_bias_kernel(tbl_ref, o_ref):
    h = pl.program_id(0)
    i = pl.program_id(1)
    rows, nc = o_ref.shape[1], o_ref.shape[2]
    t = i * rows + lax.broadcasted_iota(jnp.int32, (rows, nc), 0)
    j = lax.broadcasted_iota(jnp.int32, (rows, nc), 1)
    o_ref[0] = _bias_of_dist(t - (j * CMP_STRIDE + CMP_BLOCK - 1), tbl_ref, h) * LOG2E


def _near_bias_kernel(tbl_ref, o_ref):
    h = pl.program_id(0)
    r = lax.broadcasted_iota(jnp.int32, (TQ, 2 * TQ), 0)
    c = lax.broadcasted_iota(jnp.int32, (TQ, 2 * TQ), 1)
    o_ref[0] = (_bias_of_dist(r - c + TQ, tbl_ref, h) - tbl_ref[N_BUCKETS - 1, h]) * LOG2E


def _bias_tiles(rel_bias, seq, nc_pad):
    tbl = rel_bias.astype(f32)
    smem = pl.BlockSpec(memory_space=pltpu.SMEM)
    rows = 512
    cmp_bias = pl.pallas_call(
        _cmp_bias_kernel,
        grid=(NSA_HEADS, seq // rows),
        in_specs=[smem],
        out_specs=pl.BlockSpec((1, rows, nc_pad), lambda h, i: (h, i, 0)),
        out_shape=jax.ShapeDtypeStruct((NSA_HEADS, seq, nc_pad), f32),
        compiler_params=_params(2),
        name="cmp_bias",
    )(tbl)
    near_bias = pl.pallas_call(
        _near_bias_kernel,
        grid=(NSA_HEADS,),
        in_specs=[smem],
        out_specs=pl.BlockSpec((1, TQ, 2 * TQ), lambda h: (h, 0, 0)),
        out_shape=jax.ShapeDtypeStruct((NSA_HEADS, TQ, 2 * TQ), f32),
        compiler_params=_params(1),
        name="near_bias",
    )(tbl)
    return cmp_bias, near_bias


_QA_W = NSA_HEADS * LANES
_KV_W = NSA_KV_HEADS * LANES
_CMP_W = NSA_KV_HEADS * HEAD_DIM
_SB_W = SB_HEADS * HEAD_DIM


def _layout(d_model):
    names = ["qa", "kc", "vc", "ksl", "vsl", "kwn", "vwn", "g", "qb", "kb", "vb", "ma", "mb"]
    widths = [_QA_W, _CMP_W, _CMP_W, _KV_W, _KV_W, _KV_W, _KV_W, _KV_W, _SB_W, _SB_W, _SB_W,
              d_model, d_model]
    offs = np.concatenate([[0], np.cumsum(widths)])
    return {n: (int(offs[i]), int(offs[i + 1])) for i, n in enumerate(names)}, int(offs[-1])


def _pack_w_in(w_in, d_model):
    q_w = NSA_HEADS * HEAD_DIM
    kv_w = NSA_KV_HEADS * HEAD_DIM
    g_w = NSA_HEADS * 3
    sizes = [q_w] + [kv_w] * 6 + [g_w] + [_SB_W] * 3 + [d_model, d_model]
    offs = np.concatenate([[0], np.cumsum(sizes)])
    parts = [w_in[:, int(offs[i]):int(offs[i + 1])] for i in range(len(sizes))]
    d = w_in.shape[0]

    def pad_heads(p, n_heads, width):
        p = p.reshape(d, n_heads, width)
        p = jnp.pad(p, ((0, 0), (0, 0), (0, LANES - width)))
        return p.reshape(d, n_heads * LANES)

    packed = [pad_heads(parts[0], NSA_HEADS, HEAD_DIM), parts[1], parts[2]]
    packed += [pad_heads(parts[k], NSA_KV_HEADS, HEAD_DIM) for k in (3, 4, 5, 6)]
    packed += [pad_heads(parts[7], NSA_KV_HEADS, NSA_GROUP * 3)]
    packed += parts[8:]
    return jnp.concatenate(packed, axis=1).astype(bf16)


def _inproj_kernel(lay, x_ref, g1_ref, sc_ref, sh_ref, w_ref, gq_ref, gk_ref,
                   qa_ref, kc_ref, vc_ref, ks_ref, vs_ref, kw_ref, vw_ref, g_ref,
                   qb_ref, kb_ref, vb_ref, ma_ref, mb_ref):
    i = pl.program_id(1)
    x = x_ref[0]
    ms = jnp.mean(x * x, axis=-1, keepdims=True)
    u = (x * lax.rsqrt(ms + EPS) * g1_ref[...]) * (1.0 + sc_ref[0]) + sh_ref[0]
    ub = u.astype(bf16)

    def proj(name):
        lo, hi = lay[name]
        return _dot(ub, w_ref[:, lo:hi])

    def head_norm(z, gain):
        parts = []
        for g in range(z.shape[1] // LANES):
            zg = z[:, g * LANES:(g + 1) * LANES]
            ss = jnp.sum(zg * zg, axis=1, keepdims=True)
            parts.append(zg * lax.rsqrt(ss * (1.0 / HEAD_DIM) + EPS))
        return jnp.concatenate(parts, axis=1) * gain

    scale = HEAD_DIM ** -0.5 * LOG2E
    qa_ref[0] = (head_norm(proj("qa"), gq_ref[...]) * scale).astype(bf16)
    kc_ref[0] = proj("kc")
    vc_ref[0] = proj("vc")

    rows = x.shape[0]
    lane = lax.broadcasted_iota(jnp.int32, (rows, _KV_W), 1)
    tok_blk = (i * rows + lax.broadcasted_iota(jnp.int32, (rows, _KV_W), 0)) // SEL_BLOCK
    in_pad = (lane & HEAD_DIM) != 0
    onehot = in_pad & ((lane & (HEAD_DIM - 1)) == tok_blk)
    ones_col = (lane & (LANES - 1)) == HEAD_DIM

    ks = head_norm(proj("ksl"), gk_ref[1:2, :])
    ks_ref[0] = jnp.where(onehot, 1.0, ks).astype(bf16)
    vs_ref[0] = jnp.where(ones_col, 1.0, proj("vsl")).astype(bf16)
    kw_ref[0] = head_norm(proj("kwn"), gk_ref[2:3, :]).astype(bf16)
    vw_ref[0] = jnp.where(ones_col, 1.0, proj("vwn")).astype(bf16)
    g_ref[0] = jax.nn.sigmoid(proj("g"))
    qb_ref[0] = (proj("qb") * scale).astype(bf16)
    kb_ref[0] = proj("kb").astype(bf16)
    vb_ref[0] = proj("vb").astype(bf16)
    ma_ref[0] = jax.nn.sigmoid(proj("ma")).astype(bf16)
    mb_ref[0] = jax.nn.sigmoid(proj("mb")).astype(bf16)


def _inproj(x, g1, scale1, shift1, w_packed, gq, gk):
    bsz, seq, d = x.shape
    lay, width = _layout(d)
    assert w_packed.shape == (d, width)
    tok = lambda w: pl.BlockSpec((1, TM, w), lambda b, i: (b, i, 0))
    full = lambda a: pl.BlockSpec(a.shape, lambda b, i: (0,) * a.ndim,
                                  pipeline_mode=pl.Buffered(1))
    mod = pl.BlockSpec((1, 1, d), lambda b, i: (b, 0, 0))
    out_w =[(_QA_W, bf16), (_CMP_W, f32), (_CMP_W, f32), (_KV_W, bf16), (_KV_W, bf16),
             (_KV_W, bf16), (_KV_W, bf16), (_KV_W, f32), (_SB_W, bf16), (_SB_W, bf16),
             (_SB_W, bf16), (d, bf16), (d, bf16)]
    return pl.pallas_call(
        functools.partial(_inproj_kernel, lay),
        grid=(bsz, seq // TM),
        in_specs=[tok(d), full(g1), mod, mod, full(w_packed), full(gq), full(gk)],
        out_specs=[tok(w) for w, _ in out_w],
        out_shape=[jax.ShapeDtypeStruct((bsz, seq, w), dt) for w, dt in out_w],
        compiler_params=_params(2),
        name="inproj",
    )(x, g1, scale1, shift1, w_packed, gq, gk)


def _compress_kernel(xk_ref, xv_ref, pos_ref, w1k_ref, w2k_ref, w1v_ref, w2v_ref, gk_ref,
                     ko_ref, vo_ref):
    nch = xk_ref.shape[1] // CMP_STRIDE

    def mlp(x_ref, w1_ref, w2_ref):
        first = jnp.zeros((nch, LANES), f32)
        second = jnp.zeros((nch, LANES), f32)
        for l in range(CMP_STRIDE):
            xl = x_ref[0, pl.ds(l, nch, stride=CMP_STRIDE), :]
            lo = l + CMP_STRIDE
            first = first + _dot((xl + pos_ref[l:l + 1, :]).astype(bf16), w1_ref[l])
            second = second + _dot((xl + pos_ref[lo:lo + 1, :]).astype(bf16), w1_ref[lo])
        pre = first + pltpu.roll(second, nch - 1, 0)
        hid = pre * jax.nn.sigmoid(pre)
        return _dot(hid.astype(bf16), w2_ref[...])

    k = mlp(xk_ref, w1k_ref, w2k_ref)
    v = mlp(xv_ref, w1v_ref, w2v_ref)
    for h in range(NSA_KV_HEADS):
        kh = k[:, h * HEAD_DIM:(h + 1) * HEAD_DIM]
        ms = jnp.mean(kh * kh, axis=-1, keepdims=True)
        ko_ref[0, h] = (kh * lax.rsqrt(ms + EPS) * gk_ref[...]).astype(bf16)
        vo_ref[0, h] = v[:, h * HEAD_DIM:(h + 1) * HEAD_DIM].astype(bf16)


def _compress(xk, xv, pos, w1k, w2k, w1v, w2v, gk0):
    bsz, seq, width = xk.shape
    nch = seq // CMP_STRIDE
    assert width == NSA_KV_HEADS * HEAD_DIM == LANES

    def both_heads(w):
        z = jnp.zeros_like(w)
        return jnp.concatenate([jnp.concatenate([w, z], axis=2),
                                jnp.concatenate([z, w], axis=2)], axis=1).astype(bf16)

    w1 = lambda w: both_heads(w.reshape(CMP_BLOCK, HEAD_DIM, w.shape[1]))
    w2 = lambda w: both_heads(w[None])[0]
    args = (xk, xv, jnp.tile(pos, (1, NSA_KV_HEADS)), w1(w1k), w2(w2k), w1(w1v), w2(w2v), gk0)
    blk = pl.BlockSpec((1, seq, width), lambda b: (b, 0, 0))
    full = lambda a: pl.BlockSpec(a.shape, lambda b: (0,) * a.ndim)
    out = pl.BlockSpec((1, NSA_KV_HEADS, nch, HEAD_DIM), lambda b: (b, 0, 0, 0))
    shape = jax.ShapeDtypeStruct((bsz, NSA_KV_HEADS, nch, HEAD_DIM), bf16)
    return pl.pallas_call(
        _compress_kernel,
        grid=(bsz,),
        in_specs=[blk, blk] + [full(a) for a in args[2:]],
        out_specs=[out, out],
        out_shape=[shape, shape],
        compiler_params=_params(1),
        name="compress",
    )(*args)


def _flash_step(s, v, m_ref, acc_ref):
    m_prev = m_ref[...]
    m_new = jnp.maximum(m_prev, jnp.max(s, axis=1, keepdims=True))
    alpha = jnp.exp2(m_prev - m_new)
    p = jnp.exp2(s - jnp.concatenate([m_new] * (s.shape[1] // LANES), axis=1))
    acc_ref[...] = alpha * acc_ref[...] + _dot(p.astype(bf16), v)
    m_ref[...] = m_new


def _flash_init(m_ref, acc_ref):
    m_ref[...] = jnp.full(m_ref.shape, MASKED, f32)
    acc_ref[...] = jnp.zeros(acc_ref.shape, f32)


def _flash_out(acc_ref):
    acc = acc_ref[...]
    return acc[:, :HEAD_DIM] / acc[:, HEAD_DIM:HEAD_DIM + 1]


def _nsa_kernel(n_top, q_ref, g_ref, kc_ref, vc_ref, ks_ref, vs_ref, kw_ref, vw_ref,
                bc_ref, bn_ref, ovt_ref, o_ref, qs_ref, qw_ref, oc_ref,
                m_ref, acc_ref, mw_ref, accw_ref):
    i = pl.program_id(1)
    kv_heads = range(NSA_KV_HEADS)
    rows = NSA_GROUP * TQ
    start = i * TQ
    assert WINDOW == 4 * TQ

    def attend(hk, q_rows_ref, k_ref, v_ref, off, width, bias, state):
        off = pl.multiple_of(off, TQ)
        lanes = slice(hk * LANES, (hk + 1) * LANES)
        s = _dot_nt(q_rows_ref[hk], k_ref[0, pl.ds(off, width), lanes])
        if bias is not None:
            s = s + bias
        _flash_step(s, v_ref[0, pl.ds(off, width), lanes], state[0].at[hk], state[1].at[hk])

    def when(cond, guarded):
        return pl.when(cond) if guarded else (lambda fn: fn())

    def near_steps(q_rows_ref, k_ref, v_ref, state, guarded):
        group = lambda hk: slice(hk * NSA_GROUP, (hk + 1) * NSA_GROUP)

        @when(i >= 1, guarded)
        def _():
            for hk in kv_heads:
                attend(hk, q_rows_ref, k_ref, v_ref, start - TQ, 2 * TQ,
                       bn_ref[group(hk)].reshape(rows, 2 * TQ), state)

        if guarded:
            @pl.when(i == 0)
            def _():
                for hk in kv_heads:
                    attend(hk, q_rows_ref, k_ref, v_ref, start, TQ,
                           bn_ref[group(hk), :, TQ:].reshape(rows, TQ), state)

    def select(hk):
        heads = range(hk * NSA_GROUP, (hk + 1) * NSA_GROUP)
        qpad = jnp.concatenate([q_ref[0, :, g * LANES:(g + 1) * LANES] for g in heads], axis=0)
        qw_ref[hk] = qpad

        bc = bc_ref[hk * NSA_GROUP:(hk + 1) * NSA_GROUP].reshape(rows, bc_ref.shape[2])
        s_c = _dot_nt(qpad[:, :HEAD_DIM], kc_ref[0, hk]) + bc
        visible = bc > 0.5 * MASKED
        m_c = jnp.max(s_c, axis=1, keepdims=True)
        e_c = jnp.where(visible, jnp.exp2(s_c - m_c), 0.0)
        p_c = e_c / jnp.maximum(jnp.sum(e_c, axis=1, keepdims=True), 1e-30)
        oc_ref[hk] = _dot(p_c.astype(bf16), vc_ref[0, hk])

        p_sum = p_c[0:TQ]
        for g in range(1, NSA_GROUP):
            p_sum = p_sum + p_c[g * TQ:(g + 1) * TQ]
        p_hi, p_lo = _split(p_sum)
        imp = _dot_nt(ovt_ref[...], p_hi) + _dot_nt(ovt_ref[...], p_lo)
        nblk = imp.shape[0]
        blk = lax.broadcasted_iota(jnp.int32, (nblk, TQ), 0)
        cur = (start + lax.broadcasted_iota(jnp.int32, (nblk, TQ), 1)) // SEL_BLOCK
        forced = (blk == 0) | (blk == cur) | (blk == cur - 1)
        imp = jnp.where(blk > cur, NEG_BLOCK, imp + jnp.where(forced, FORCED_BONUS, 0.0))
        rank = jnp.zeros((nblk, TQ), f32)
        for b2 in range(nblk):
            other = imp[b2:b2 + 1, :]
            rank = rank + jnp.where(blk > b2, jnp.where(other >= imp, 1.0, 0.0),
                                    jnp.where(other > imp, 1.0, 0.0))
        usable = (rank < n_top) & (blk <= cur)
        sel_t = jnp.where(usable, 0.0, UNSELECTED)
        sel_pad = jnp.concatenate([jnp.zeros((LANES - nblk, TQ), f32), sel_t], axis=0).T
        sel_rows = jnp.concatenate([sel_pad.astype(bf16)] * NSA_GROUP, axis=0)
        qs_ref[hk] = qpad + sel_rows

    def head(guarded):
        for hk in kv_heads:
            select(hk)

        win = (mw_ref, accw_ref)
        _flash_init(*win)

        @when(i >= 4, guarded)
        def _():
            r = lax.broadcasted_iota(jnp.int32, (rows, TQ), 0) & (TQ - 1)
            c = lax.broadcasted_iota(jnp.int32, (rows, TQ), 1)
            for hk in kv_heads:
                attend(hk, qw_ref, kw_ref, vw_ref, start - 4 * TQ, TQ,
                       jnp.where(c > r, 0.0, MASKED), win)

        @when(i >= 3, guarded)
        def _():
            for hk in kv_heads:
                attend(hk, qw_ref, kw_ref, vw_ref, start - 3 * TQ, 2 * TQ, None, win)

        if guarded:
            @pl.when(i == 2)
            def _():
                for hk in kv_heads:
                    attend(hk, qw_ref, kw_ref, vw_ref, start - 2 * TQ, TQ, None, win)

        near_steps(qw_ref, kw_ref, vw_ref, win, guarded)
        _flash_init(m_ref, acc_ref)

    def tail(guarded):
        near_steps(qs_ref, ks_ref, vs_ref, (m_ref, acc_ref), guarded)
        outs = []
        for hk in kv_heads:
            o_c = oc_ref[hk]
            o_s = _flash_out(acc_ref.at[hk])
            o_w = _flash_out(accw_ref.at[hk])
            gates = g_ref[0, :, hk * LANES:(hk + 1) * LANES]
            for g in range(NSA_GROUP):
                sl = slice(g * TQ, (g + 1) * TQ)
                outs.append(gates[:, 3 * g:3 * g + 1] * o_c[sl]
                            + gates[:, 3 * g + 1:3 * g + 2] * o_s[sl]
                            + gates[:, 3 * g + 2:3 * g + 3] * o_w[sl])
        o_ref[0] = jnp.concatenate(outs, axis=1).astype(bf16)

    interior = i >= 4
    pl.when(interior)(lambda: head(False))
    pl.when(jnp.logical_not(interior))(lambda: head(True))

    sel = (m_ref, acc_ref)
    n_far = jnp.maximum(i - 1, 0)

    def far_steps(off, n_steps):
        for step in range(n_steps):
            for hk in kv_heads:
                attend(hk, qs_ref, ks_ref, vs_ref, off + step * (2 * TQ), 2 * TQ, None, sel)

    def far_trip(c, carry):
        far_steps(c * (8 * TQ), 4)
        return carry

    lax.fori_loop(0, n_far // 8, far_trip, 0)
    rem_off = (n_far // 8) * (8 * TQ)

    @pl.when((n_far & 4) != 0)
    def _():
        far_steps(rem_off, 2)

    @pl.when((n_far & 2) != 0)
    def _():
        far_steps(rem_off + (n_far & 4) * TQ, 1)

    @pl.when((n_far & 1) != 0)
    def _():
        for hk in kv_heads:
            attend(hk, qs_ref, ks_ref, vs_ref, rem_off + (n_far & 6) * TQ, TQ, None, sel)

    pl.when(interior)(lambda: tail(False))
    pl.when(jnp.logical_not(interior))(lambda: tail(True))


def _nsa(qa, gates, kcmp, vcmp, ks, vs, kw, vw, cmp_bias, near_bias, ovt):
    bsz, seq, _ = qa.shape
    nc = kcmp.shape[2]
    n_top = min(SEL_TOPK, seq // SEL_BLOCK)
    rows = NSA_GROUP * TQ
    tok = lambda w: pl.BlockSpec((1, TQ, w), lambda b, i: (b, i, 0))
    kv = pl.BlockSpec((1, seq, NSA_KV_HEADS * LANES), lambda b, i: (b, 0, 0))
    cmp = pl.BlockSpec((1, NSA_KV_HEADS, nc, HEAD_DIM), lambda b, i: (b, 0, 0, 0))
    per_kv = lambda width, dt: pltpu.VMEM((NSA_KV_HEADS, rows, width), dt)
    return pl.pallas_call(
        functools.partial(_nsa_kernel, n_top),
        grid=(bsz, seq // TQ),
        in_specs=[tok(NSA_HEADS * LANES), tok(NSA_KV_HEADS * LANES),
                  cmp, cmp, kv, kv, kv, kv,
                  pl.BlockSpec((NSA_HEADS, TQ, nc), lambda b, i: (0, i, 0)),
                  pl.BlockSpec((NSA_HEADS, TQ, 2 * TQ), lambda b, i: (0, 0, 0)),
                  pl.BlockSpec(ovt.shape, lambda b, i: (0, 0))],
        out_specs=tok(NSA_HEADS * HEAD_DIM),
        out_shape=jax.ShapeDtypeStruct((bsz, seq, NSA_HEADS * HEAD_DIM), bf16),
        scratch_shapes=[per_kv(LANES, bf16), per_kv(LANES, bf16),
                        per_kv(HEAD_DIM, f32), per_kv(LANES, f32), per_kv(LANES, f32),
                        per_kv(LANES, f32), per_kv(LANES, f32)],
        compiler_params=_params(2),
        name="nsa",
    )(qa, gates, kcmp, vcmp, ks, vs, kw, vw, cmp_bias, near_bias, ovt)


def _sb_kernel(q_ref, k_ref, v_ref, tri_ref, o_ref, carry_ref, acc_ref):
    i = pl.program_id(2)
    n_heads = carry_ref.shape[0]
    lane = lax.broadcasted_iota(jnp.int32, (TS, LANES), 1)
    q_heads = []
    for pair in range(n_heads // 2):
        q = q_ref[0, :, pair * LANES:(pair + 1) * LANES]
        zero = jnp.zeros_like(q)
        q_heads += [jnp.where(lane < HEAD_DIM, q, zero), jnp.where(lane >= HEAD_DIM, q, zero)]
    pair_lanes = lambda hh: slice((hh // 2) * LANES, (hh // 2 + 1) * LANES)
    carry_ref[...] = jnp.zeros(carry_ref.shape, f32)
    acc_ref[...] = jnp.zeros(acc_ref.shape, f32)
    r = lax.broadcasted_iota(jnp.int32, (TS, TS), 0)
    c = lax.broadcasted_iota(jnp.int32, (TS, TS), 1)
    before = c < r

    def chunks(jobs):
        offs = [pl.multiple_of(off, TS) for off, _ in jobs]
        stage = []
        for (_, diagonal), off in zip(jobs, offs):
            for hh in range(n_heads):
                z = _dot_nt(q_heads[hh], k_ref[0, pl.ds(off, TS), pair_lanes(hh)])
                soft = jnp.log2(1.0 + jnp.exp2(-jnp.abs(z)))
                log_keep = jnp.minimum(-z, 0.0) - soft
                log_sig = jnp.minimum(z, 0.0) - soft
                if diagonal:
                    log_keep = jnp.where(before, log_keep, 0.0)
                hi, lo = _split(log_keep)
                later = _dot(hi, tri_ref[...]) + _dot(lo, tri_ref[...])
                stage.append((log_sig + later, jnp.sum(log_keep, axis=1, keepdims=True)))
        for hh in range(n_heads):
            carry = carry_ref[hh]
            acc = acc_ref[hh]
            for j, ((_, diagonal), off) in enumerate(zip(jobs, offs)):
                base, total = stage[n_heads * j + hh]
                a = jnp.exp2(base + jnp.concatenate([carry] * (TS // LANES), axis=1))
                if diagonal:
                    a = jnp.where(before, a, 0.0)
                acc = acc + _dot(a.astype(bf16), v_ref[0, pl.ds(off, TS), pair_lanes(hh)])
                carry = carry + total
            carry_ref[hh] = carry
            acc_ref[hh] = acc

    @pl.when(i == 0)
    def _():
        chunks([(0, True)])

    @pl.when(i >= 1)
    def _():
        chunks([(i * TS, True), ((i - 1) * TS, False)])

    def any_live():
        return jnp.max(carry_ref[...]) > EXP2_UNDERFLOW

    def more(state):
        n, live = state
        return jnp.logical_and(n < i, live)

    def older(state):
        n, _ = state
        chunks([((i - 1 - n) * TS, False)])
        return n + 1, any_live()

    lax.while_loop(more, older, (jnp.int32(1), any_live()))
    o_ref[0] = jnp.concatenate(
        [jnp.where(lane < HEAD_DIM, acc_ref[2 * pair], acc_ref[2 * pair + 1])
         for pair in range(n_heads // 2)], axis=1).astype(bf16)


SB_STEP_HEADS = 4


def _sb(qb, kb, vb, tri):
    bsz, seq, width = qb.shape
    step_w = SB_STEP_HEADS * HEAD_DIM
    q_spec = pl.BlockSpec((1, TS, step_w), lambda b, p, i: (b, i, p))
    kv_spec = pl.BlockSpec((1, seq, step_w), lambda b, p, i: (b, 0, p))
    state = pltpu.VMEM((SB_STEP_HEADS, TS, LANES), f32)
    return pl.pallas_call(
        _sb_kernel,
        grid=(bsz, width // step_w, seq // TS),
        in_specs=[q_spec, kv_spec, kv_spec, pl.BlockSpec(tri.shape, lambda b, p, i: (0, 0))],
        out_specs=q_spec,
        out_shape=jax.ShapeDtypeStruct((bsz, seq, width), bf16),
        scratch_shapes=[state, state],
        compiler_params=_params(3),
        name="sb",
    )(qb, kb, vb, tri)


def _merge_kernel(ya_ref, yb_ref, ma_ref, mb_ref, x_ref, gate_ref, wa_ref, wb_ref, wo_ref, o_ref):
    y_a = _dot(ya_ref[0], wa_ref[...])
    y_b = _dot(yb_ref[0], wb_ref[...])
    mixed = ma_ref[0].astype(f32) * y_a + mb_ref[0].astype(f32) * y_b
    o_ref[0] = x_ref[0] + gate_ref[0] * _dot(mixed.astype(bf16), wo_ref[...])


def _merge(ya, yb, ma, mb, x, gate1, wa, wb, wo):
    bsz, seq, d = x.shape
    tok = lambda w: pl.BlockSpec((1, TM, w), lambda b, i: (b, i, 0))
    full = lambda a: pl.BlockSpec(a.shape, lambda b, i: (0,) * a.ndim)
    return pl.pallas_call(
        _merge_kernel,
        grid=(bsz, seq // TM),
        in_specs=[tok(ya.shape[2]), tok(yb.shape[2]), tok(d), tok(d), tok(d),
                  pl.BlockSpec((1, 1, d), lambda b, i: (b, 0, 0)), full(wa), full(wb), full(wo)],
        out_specs=tok(d),
        out_shape=jax.ShapeDtypeStruct((bsz, seq, d), f32),
        compiler_params=_params(2),
        name="merge",
    )(ya, yb, ma, mb, x, gate1, wa, wb, wo)


def _mlp_kernel(h_ref, g2_ref, sc_ref, sh_ref, gate_ref, w1_ref, w2_ref, o_ref):
    hres = h_ref[0]
    d = hres.shape[1]
    ms = jnp.mean(hres * hres, axis=-1, keepdims=True)
    u = (hres * lax.rsqrt(ms + EPS) * g2_ref[...]) * (1.0 + sc_ref[0]) + sh_ref[0]
    ub = u.astype(bf16)
    ff = jnp.zeros(hres.shape, f32)
    for c in range(w1_ref.shape[1] // d):
        hid = jnp.maximum(_dot(ub, w1_ref[:, c * d:(c + 1) * d]), 0.0)
        ff = ff + _dot((hid * hid).astype(bf16), w2_ref[c * d:(c + 1) * d, :])
    o_ref[0] = hres + gate_ref[0] * ff


def _mlp(hres, g2, scale2, shift2, gate2, w1, w2):
    bsz, seq, d = hres.shape
    tok = pl.BlockSpec((1, TM, d), lambda b, i: (b, i, 0))
    mod = pl.BlockSpec((1, 1, d), lambda b, i: (b, 0, 0))
    const = lambda a: pl.BlockSpec(a.shape, lambda b, i: (0,) * a.ndim,
                                   pipeline_mode=pl.Buffered(1))
    return pl.pallas_call(
        _mlp_kernel,
        grid=(bsz, seq // TM),
        in_specs=[tok, pl.BlockSpec(g2.shape, lambda b, i: (0, 0)), mod, mod, mod,
                  const(w1), const(w2)],
        out_specs=tok,
        out_shape=jax.ShapeDtypeStruct((bsz, seq, d), f32),
        compiler_params=_params(2),
        name="mlp",
    )(hres, g2, scale2, shift2, gate2, w1, w2)


def _overlap_t(nc_pad, nsel_pad, nc, nsel):
    c_start = np.arange(nc_pad) * CMP_STRIDE
    s_start = np.arange(nsel_pad) * SEL_BLOCK
    ov = (np.minimum(c_start[None, :] + CMP_BLOCK, s_start[:, None] + SEL_BLOCK)
          - np.maximum(c_start[None, :], s_start[:, None]))
    ov = np.clip(ov, 0, CMP_BLOCK).astype(np.float32) / CMP_BLOCK
    ov[nsel:, :] = 0.0
    ov[:, nc:] = 0.0
    return ov


def _layer(h, mod, rel_tiles, p):
    bsz, seq, d = h.shape
    shift1, scale1, gate1, shift2, scale2, gate2 = [
        mod[:, k * d:(k + 1) * d].reshape(bsz, 1, d) for k in range(6)]
    cmp_bias, near_bias = rel_tiles

    pad_gain = lambda g, n: jnp.tile(jnp.pad(g, (0, LANES - HEAD_DIM)), n).reshape(1, n * LANES)
    gq = pad_gain(p["q_norm_g"], NSA_HEADS)
    gk = jnp.concatenate([pad_gain(p["k_norm_g"][k], NSA_KV_HEADS) for k in range(3)], axis=0)

    (qa, kc, vc, ks, vs, kw, vw, gates, qb, kb, vb, ma, mb) = _inproj(
        h, p["norm1_g"].reshape(1, d), scale1, shift1, _pack_w_in(p["w_in"], d), gq, gk)

    nch = seq // CMP_STRIDE
    kcmp, vcmp = _compress(kc, vc, p["cmp_pos"], p["cmp_k_w1"], p["cmp_k_w2"],
                           p["cmp_v_w1"], p["cmp_v_w2"], p["k_norm_g"][0].reshape(1, HEAD_DIM))

    nc = (seq - CMP_BLOCK) // CMP_STRIDE + 1
    ovt = jnp.asarray(_overlap_t(nch, HEAD_DIM, nc, seq // SEL_BLOCK), bf16)
    y_nsa = _nsa(qa, gates, kcmp, vcmp, ks, vs, kw, vw, cmp_bias, near_bias, ovt)

    y_sb = _sb(qb, kb, vb, jnp.asarray(np.tril(np.ones((TS, TS)), -1), bf16))

    h1 = _merge(y_nsa, y_sb, ma, mb, h, gate1, p["w_up_nsa"].astype(bf16),
                p["w_up_sb"].astype(bf16), p["w_out"].astype(bf16))
    return _mlp(h1, p["norm2_g"].reshape(1, d), scale2, shift2, gate2,
                p["mlp_w1"].astype(bf16), p["mlp_w2"].astype(bf16))


def kernel(x, c, rel_bias, ada_w, ada_b, norm1_g, norm2_g, w_in, cmp_pos, cmp_k_w1, cmp_k_w2,
           cmp_v_w1, cmp_v_w2, q_norm_g, k_norm_g, w_up_nsa, w_up_sb, w_out, mlp_w1, mlp_w2):
    bsz, seq, d = x.shape
    assert seq % TM == 0 and seq // SEL_BLOCK <= HEAD_DIM and seq >= WINDOW + TQ
    assert CMP_BLOCK == 2 * CMP_STRIDE and TQ == 2 * SEL_BLOCK
    tbl = rel_bias.astype(f32)
    rel_tiles = _bias_tiles(tbl, seq, seq // CMP_STRIDE)
    stacked = dict(norm1_g=norm1_g, norm2_g=norm2_g, w_in=w_in, cmp_pos=cmp_pos,
                   cmp_k_w1=cmp_k_w1, cmp_k_w2=cmp_k_w2, cmp_v_w1=cmp_v_w1, cmp_v_w2=cmp_v_w2,
                   q_norm_g=q_norm_g, k_norm_g=k_norm_g, w_up_nsa=w_up_nsa, w_up_sb=w_up_sb,
                   w_out=w_out, mlp_w1=mlp_w1, mlp_w2=mlp_w2)
    h = x
    for layer in range(ada_w.shape[0]):
        mod = _adaln(c, ada_w[layer], ada_b[layer])
        h = _layer(h, mod, rel_tiles, {k: v[layer] for k, v in stacked.items()})
    return h
```

```python
import functools
import math

import numpy as np
import jax
import jax.numpy as jnp
from jax import lax
from jax.experimental import pallas as pl
from jax.experimental.pallas import tpu as pltpu

f32 = jnp.float32
bf16 = jnp.bfloat16

HEAD_DIM = 64
NSA_HEADS = 8
NSA_KV_HEADS = 2
NSA_GROUP = NSA_HEADS // NSA_KV_HEADS
SB_HEADS = 8
CMP_BLOCK = 32
CMP_STRIDE = 16
SEL_BLOCK = 64
SEL_TOPK = 16
WINDOW = 512
N_BUCKETS = 32
MAX_DISTANCE = 128
EPS = 1e-6
FORCED_BONUS = 1e4
NEG_BLOCK = -1e9

LANES = 128
MASKED = -1e30
UNSELECTED = -1e9
LOG2E = math.log2(math.e)
EXP2_UNDERFLOW = -150.0
VMEM_LIMIT = 56 * 1024 * 1024

TQ = 128
TS = 256
TM = 512


def _bucket_thresholds():
    n = np.arange(0, 4 * MAX_DISTANCE)
    max_exact = N_BUCKETS // 2
    nf = np.maximum(n, 1).astype(np.float32)
    large = max_exact + (np.log(nf / max_exact) / math.log(MAX_DISTANCE / max_exact)
                         * (N_BUCKETS - max_exact)).astype(np.int32)
    large = np.minimum(large, N_BUCKETS - 1)
    b = np.where(n < max_exact, n, large)
    assert np.all(np.diff(b) >= 0) and b[-1] == N_BUCKETS - 1
    return [int(np.argmax(b >= k)) for k in range(N_BUCKETS)]


BUCKET_START = _bucket_thresholds()
assert BUCKET_START[-1] <= LANES


def _dot(a, b):
    return jnp.dot(a, b, preferred_element_type=f32)


def _dot_nt(a, b):
    return lax.dot_general(a, b, (((1,), (1,)), ((), ())), preferred_element_type=f32)


def _split(a):
    hi = a.astype(bf16)
    lo = (a - hi.astype(f32)).astype(bf16)
    return hi, lo


def _params(n_grid):
    return pltpu.CompilerParams(dimension_semantics=("arbitrary",) * n_grid,
                                vmem_limit_bytes=VMEM_LIMIT)


def _adaln_kernel(c_ref, w_ref, b_ref, o_ref):
    c = c_ref[...]
    a = c * jax.nn.sigmoid(c)
    ah, al = _split(a)
    wh, wl = _split(w_ref[...])
    o_ref[...] = _dot(ah, wh) + _dot(ah, wl) + _dot(al, wh) + b_ref[...]


def _adaln(c, w, b):
    bsz, d = c.shape
    n = w.shape[1]
    return pl.pallas_call(
        _adaln_kernel,
        grid=(n // d,),
        in_specs=[pl.BlockSpec((bsz, d), lambda j: (0, 0)),
                  pl.BlockSpec((d, d), lambda j: (0, j)),
                  pl.BlockSpec((1, d), lambda j: (0, j))],
        out_specs=pl.BlockSpec((bsz, d), lambda j: (0, j)),
        out_shape=jax.ShapeDtypeStruct((bsz, n), f32),
        compiler_params=_params(1),
        name="adaln",
    )(c, w, b.reshape(1, n))


def _bias_of_dist(dist, tbl_ref, h):
    out = jnp.full(dist.shape, tbl_ref[0, h], f32)
    for k in range(1, N_BUCKETS):
        out = jnp.where(dist >= BUCKET_START[k], tbl_ref[k, h], out)
    return jnp.where(dist >= 0, out, MASKED)


def _cmp_bias_kernel(tbl_ref, o_ref):
    h = pl.program_id(0)
    i = pl.program_id(1)
    rows, nc = o_ref.shape[1], o_ref.shape[2]
    t = i * rows + lax.broadcasted_iota(jnp.int32, (rows, nc), 0)
    j = lax.broadcasted_iota(jnp.int32, (rows, nc), 1)
    o_ref[0] = _bias_of_dist(t - (j * CMP_STRIDE + CMP_BLOCK - 1), tbl_ref, h) * LOG2E


def _near_bias_kernel(tbl_ref, o_ref):
    h = pl.program_id(0)
    r = lax.broadcasted_iota(jnp.int32, (TQ, 2 * TQ), 0)
    c = lax.broadcasted_iota(jnp.int32, (TQ, 2 * TQ), 1)
    o_ref[0] = (_bias_of_dist(r - c + TQ, tbl_ref, h) - tbl_ref[N_BUCKETS - 1, h]) * LOG2E


def _bias_tiles(rel_bias, seq, nc_pad):
    tbl = rel_bias.astype(f32)
    smem = pl.BlockSpec(memory_space=pltpu.SMEM)
    rows = 512
    cmp_bias = pl.pallas_call(
        _cmp_bias_kernel,
        grid=(NSA_HEADS, seq // rows),
        in_specs=[smem],
        out_specs=pl.BlockSpec((1, rows, nc_pad), lambda h, i: (h, i, 0)),
        out_shape=jax.ShapeDtypeStruct((NSA_HEADS, seq, nc_pad), f32),
        compiler_params=_params(2),
        name="cmp_bias",
    )(tbl)
    near_bias = pl.pallas_call(
        _near_bias_kernel,
        grid=(NSA_HEADS,),
        in_specs=[smem],
        out_specs=pl.BlockSpec((1, TQ, 2 * TQ), lambda h: (h, 0, 0)),
        out_shape=jax.ShapeDtypeStruct((NSA_HEADS, TQ, 2 * TQ), f32),
        compiler_params=_params(1),
        name="near_bias",
    )(tbl)
    return cmp_bias, near_bias


_QA_W = NSA_HEADS * LANES
_KV_W = NSA_KV_HEADS * LANES
_CMP_W = NSA_KV_HEADS * HEAD_DIM
_SB_W = SB_HEADS * HEAD_DIM


def _layout(d_model):
    names = ["qa", "kc", "vc", "ksl", "vsl", "kwn", "vwn", "g", "qb", "kb", "vb", "ma", "mb"]
    widths = [_QA_W, _CMP_W, _CMP_W, _KV_W, _KV_W, _KV_W, _KV_W, _KV_W, _SB_W, _SB_W, _SB_W,
              d_model, d_model]
    offs = np.concatenate([[0], np.cumsum(widths)])
    return {n: (int(offs[i]), int(offs[i + 1])) for i, n in enumerate(names)}, int(offs[-1])


def _pack_w_in(w_in, d_model):
    q_w = NSA_HEADS * HEAD_DIM
    kv_w = NSA_KV_HEADS * HEAD_DIM
    g_w = NSA_HEADS * 3
    sizes = [q_w] + [kv_w] * 6 + [g_w] + [_SB_W] * 3 + [d_model, d_model]
    offs = np.concatenate([[0], np.cumsum(sizes)])
    parts = [w_in[:, int(offs[i]):int(offs[i + 1])] for i in range(len(sizes))]
    d = w_in.shape[0]

    def pad_heads(p, n_heads, width):
        p = p.reshape(d, n_heads, width)
        p = jnp.pad(p, ((0, 0), (0, 0), (0, LANES - width)))
        return p.reshape(d, n_heads * LANES)

    packed = [pad_heads(parts[0], NSA_HEADS, HEAD_DIM), parts[1], parts[2]]
    packed += [pad_heads(parts[k], NSA_KV_HEADS, HEAD_DIM) for k in (3, 4, 5, 6)]
    packed += [pad_heads(parts[7], NSA_KV_HEADS, NSA_GROUP * 3)]
    packed += parts[8:]
    return jnp.concatenate(packed, axis=1).astype(bf16)


def _inproj_kernel(lay, x_ref, g1_ref, sc_ref, sh_ref, w_ref, gq_ref, gk_ref,
                   qa_ref, kc_ref, vc_ref, ks_ref, vs_ref, kw_ref, vw_ref, g_ref,
                   qb_ref, kb_ref, vb_ref, ma_ref, mb_ref):
    i = pl.program_id(1)
    x = x_ref[0]
    ms = jnp.mean(x * x, axis=-1, keepdims=True)
    u = (x * lax.rsqrt(ms + EPS) * g1_ref[...]) * (1.0 + sc_ref[0]) + sh_ref[0]
    ub = u.astype(bf16)

    def proj(name):
        lo, hi = lay[name]
        return _dot(ub, w_ref[:, lo:hi])

    def head_norm(z, gain):
        parts = []
        for g in range(z.shape[1] // LANES):
            zg = z[:, g * LANES:(g + 1) * LANES]
            ss = jnp.sum(zg * zg, axis=1, keepdims=True)
            parts.append(zg * lax.rsqrt(ss * (1.0 / HEAD_DIM) + EPS))
        return jnp.concatenate(parts, axis=1) * gain

    scale = HEAD_DIM ** -0.5 * LOG2E
    qa_ref[0] = (head_norm(proj("qa"), gq_ref[...]) * scale).astype(bf16)
    kc_ref[0] = proj("kc")
    vc_ref[0] = proj("vc")

    rows = x.shape[0]
    lane = lax.broadcasted_iota(jnp.int32, (rows, _KV_W), 1)
    tok_blk = (i * rows + lax.broadcasted_iota(jnp.int32, (rows, _KV_W), 0)) // SEL_BLOCK
    in_pad = (lane & HEAD_DIM) != 0
    onehot = in_pad & ((lane & (HEAD_DIM - 1)) == tok_blk)
    ones_col = (lane & (LANES - 1)) == HEAD_DIM

    ks = head_norm(proj("ksl"), gk_ref[1:2, :])
    ks_ref[0] = jnp.where(onehot, 1.0, ks).astype(bf16)
    vs_ref[0] = jnp.where(ones_col, 1.0, proj("vsl")).astype(bf16)
    kw_ref[0] = head_norm(proj("kwn"), gk_ref[2:3, :]).astype(bf16)
    vw_ref[0] = jnp.where(ones_col, 1.0, proj("vwn")).astype(bf16)
    g_ref[0] = jax.nn.sigmoid(proj("g")).T
    qb_ref[0] = (proj("qb") * scale).astype(bf16)
    kb_ref[0] = proj("kb").astype(bf16)
    vb_ref[0] = proj("vb").astype(bf16)
    ma_ref[0] = jax.nn.sigmoid(proj("ma")).astype(bf16)
    mb_ref[0] = jax.nn.sigmoid(proj("mb")).astype(bf16)


def _inproj(x, g1, scale1, shift1, w_packed, gq, gk):
    bsz, seq, d = x.shape
    lay, width = _layout(d)
    assert w_packed.shape == (d, width)
    tok = lambda w: pl.BlockSpec((1, TM, w), lambda b, i: (b, i, 0))
    full = lambda a: pl.BlockSpec(a.shape, lambda b, i: (0,) * a.ndim,
                                  pipeline_mode=pl.Buffered(1))
    mod = pl.BlockSpec((1, 1, d), lambda b, i: (b, 0, 0))
    out_w = [(_QA_W, bf16), (_CMP_W, f32), (_CMP_W, f32), (_KV_W, bf16), (_KV_W, bf16),
             (_KV_W, bf16), (_KV_W, bf16), (_KV_W, f32), (_SB_W, bf16), (_SB_W, bf16),
             (_SB_W, bf16), (d, bf16), (d, bf16)]
    gates_at = 7
    out_specs = [tok(w) for w, _ in out_w]
    out_shape = [jax.ShapeDtypeStruct((bsz, seq, w), dt) for w, dt in out_w]
    out_specs[gates_at] = pl.BlockSpec((1, _KV_W, TM), lambda b, i: (b, 0, i))
    out_shape[gates_at] = jax.ShapeDtypeStruct((bsz, _KV_W, seq), f32)
    return pl.pallas_call(
        functools.partial(_inproj_kernel, lay),
        grid=(bsz, seq // TM),
        in_specs=[tok(d), full(g1), mod, mod, full(w_packed), full(gq), full(gk)],
        out_specs=out_specs,
        out_shape=out_shape,
        compiler_params=_params(2),
        name="inproj",
    )(x, g1, scale1, shift1, w_packed, gq, gk)


def _compress_kernel(xk_ref, xv_ref, pos_ref, w1k_ref, w2k_ref, w1v_ref, w2v_ref, gk_ref,
                     ko_ref, vo_ref):
    nch = xk_ref.shape[1] // CMP_STRIDE

    def mlp(x_ref, w1_ref, w2_ref):
        first = jnp.zeros((nch, LANES), f32)
        second = jnp.zeros((nch, LANES), f32)
        for l in range(CMP_STRIDE):
            xl = x_ref[0, pl.ds(l, nch, stride=CMP_STRIDE), :]
            lo = l + CMP_STRIDE
            first = first + _dot((xl + pos_ref[l:l + 1, :]).astype(bf16), w1_ref[l])
            second = second + _dot((xl + pos_ref[lo:lo + 1, :]).astype(bf16), w1_ref[lo])
        pre = first + pltpu.roll(second, nch - 1, 0)
        hid = pre * jax.nn.sigmoid(pre)
        return _dot(hid.astype(bf16), w2_ref[...])

    k = mlp(xk_ref, w1k_ref, w2k_ref)
    v = mlp(xv_ref, w1v_ref, w2v_ref)
    for h in range(NSA_KV_HEADS):
        kh = k[:, h * HEAD_DIM:(h + 1) * HEAD_DIM]
        ms = jnp.mean(kh * kh, axis=-1, keepdims=True)
        ko_ref[0, h] = (kh * lax.rsqrt(ms + EPS) * gk_ref[...]).astype(bf16)
        vh = v[:, h * HEAD_DIM:(h + 1) * HEAD_DIM]
        vo_ref[0, h] = jnp.concatenate([vh, jnp.zeros_like(vh)], axis=1).astype(bf16)


def _compress(xk, xv, pos, w1k, w2k, w1v, w2v, gk0):
    bsz, seq, width = xk.shape
    nch = seq // CMP_STRIDE
    assert width == NSA_KV_HEADS * HEAD_DIM == LANES

    def both_heads(w):
        z = jnp.zeros_like(w)
        return jnp.concatenate([jnp.concatenate([w, z], axis=2),
                                jnp.concatenate([z, w], axis=2)], axis=1).astype(bf16)

    w1 = lambda w: both_heads(w.reshape(CMP_BLOCK, HEAD_DIM, w.shape[1]))
    w2 = lambda w: both_heads(w[None])[0]
    args = (xk, xv, jnp.tile(pos, (1, NSA_KV_HEADS)), w1(w1k), w2(w2k), w1(w1v), w2(w2v), gk0)
    blk = pl.BlockSpec((1, seq, width), lambda b: (b, 0, 0))
    full = lambda a: pl.BlockSpec(a.shape, lambda b: (0,) * a.ndim)
    out = lambda w: pl.BlockSpec((1, NSA_KV_HEADS, nch, w), lambda b: (b, 0, 0, 0))
    shape = lambda w: jax.ShapeDtypeStruct((bsz, NSA_KV_HEADS, nch, w), bf16)
    return pl.pallas_call(
        _compress_kernel,
        grid=(bsz,),
        in_specs=[blk, blk] + [full(a) for a in args[2:]],
        out_specs=[out(HEAD_DIM), out(LANES)],
        out_shape=[shape(HEAD_DIM), shape(LANES)],
        compiler_params=_params(1),
        name="compress",
    )(*args)


def _flash_step(s, v, m_ref, acc_ref):
    m_prev = m_ref[...]
    m_new = jnp.maximum(m_prev, jnp.max(s, axis=1, keepdims=True))
    alpha = jnp.exp2(m_prev - m_new)
    p = jnp.exp2(s - jnp.concatenate([m_new] * (s.shape[1] // LANES), axis=1))
    acc_ref[...] = alpha * acc_ref[...] + _dot(p.astype(bf16), v)
    m_ref[...] = m_new


def _flash_init(m_ref, acc_ref):
    m_ref[...] = jnp.full(m_ref.shape, MASKED, f32)
    acc_ref[...] = jnp.zeros(acc_ref.shape, f32)


def _flash_out_t(acc_ref):
    acc_t = acc_ref[...].T
    return acc_t[:HEAD_DIM] / acc_t[HEAD_DIM:HEAD_DIM + 1]


def _nsa_kernel(n_top, q_ref, g_ref, kc_ref, vc_ref, ks_ref, vs_ref, kw_ref, vw_ref,
                bc_ref, bn_ref, ovt_ref, o_ref, qs_ref, qw_ref, oc_ref,
                m_ref, acc_ref, mw_ref, accw_ref):
    i = pl.program_id(1)
    kv_heads = range(NSA_KV_HEADS)
    rows = NSA_GROUP * TQ
    start = i * TQ
    assert WINDOW == 4 * TQ

    def attend(hk, q_rows_ref, k_ref, v_ref, off, width, bias, state):
        off = pl.multiple_of(off, TQ)
        lanes = slice(hk * LANES, (hk + 1) * LANES)
        s = _dot_nt(q_rows_ref[hk], k_ref[0, pl.ds(off, width), lanes])
        if bias is not None:
            s = s + bias
        _flash_step(s, v_ref[0, pl.ds(off, width), lanes], state[0].at[hk], state[1].at[hk])

    def when(cond, guarded):
        return pl.when(cond) if guarded else (lambda fn: fn())

    def near_steps(q_rows_ref, k_ref, v_ref, state, guarded):
        group = lambda hk: slice(hk * NSA_GROUP, (hk + 1) * NSA_GROUP)

        @when(i >= 1, guarded)
        def _():
            for hk in kv_heads:
                attend(hk, q_rows_ref, k_ref, v_ref, start - TQ, 2 * TQ,
                       bn_ref[group(hk)].reshape(rows, 2 * TQ), state)

        if guarded:
            @pl.when(i == 0)
            def _():
                for hk in kv_heads:
                    attend(hk, q_rows_ref, k_ref, v_ref, start, TQ,
                           bn_ref[group(hk), :, TQ:].reshape(rows, TQ), state)

    def select(hk):
        heads = range(hk * NSA_GROUP, (hk + 1) * NSA_GROUP)
        qpad = jnp.concatenate([q_ref[0, :, g * LANES:(g + 1) * LANES] for g in heads], axis=0)
        qw_ref[hk] = qpad

        bc = bc_ref[hk * NSA_GROUP:(hk + 1) * NSA_GROUP].reshape(rows, bc_ref.shape[2])
        s_c = _dot_nt(qpad[:, :HEAD_DIM], kc_ref[0, hk]) + bc
        visible = bc > 0.5 * MASKED
        m_c = jnp.max(s_c, axis=1, keepdims=True)
        e_c = jnp.where(visible, jnp.exp2(s_c - m_c), 0.0)
        p_c = e_c / jnp.maximum(jnp.sum(e_c, axis=1, keepdims=True), 1e-30)
        oc_ref[hk] = _dot(p_c.astype(bf16), vc_ref[0, hk])

        p_sum = p_c[0:TQ]
        for g in range(1, NSA_GROUP):
            p_sum = p_sum + p_c[g * TQ:(g + 1) * TQ]
        p_hi, p_lo = _split(p_sum)
        imp = _dot_nt(ovt_ref[...], p_hi) + _dot_nt(ovt_ref[...], p_lo)
        nblk = imp.shape[0]
        blk = lax.broadcasted_iota(jnp.int32, (nblk, TQ), 0)
        cur = (start + lax.broadcasted_iota(jnp.int32, (nblk, TQ), 1)) // SEL_BLOCK
        forced = (blk == 0) | (blk == cur) | (blk == cur - 1)
        imp = jnp.where(blk > cur, NEG_BLOCK, imp + jnp.where(forced, FORCED_BONUS, 0.0))
        rank = jnp.zeros((nblk, TQ), f32)
        for b2 in range(nblk):
            other = imp[b2:b2 + 1, :]
            rank = rank + jnp.where(blk > b2, jnp.where(other >= imp, 1.0, 0.0),
                                    jnp.where(other > imp, 1.0, 0.0))
        usable = (rank < n_top) & (blk <= cur)
        sel_t = jnp.where(usable, 0.0, UNSELECTED)
        sel_pad = jnp.concatenate([jnp.zeros((LANES - nblk, TQ), f32), sel_t], axis=0).T
        sel_rows = jnp.concatenate([sel_pad.astype(bf16)] * NSA_GROUP, axis=0)
        qs_ref[hk] = qpad + sel_rows

    def head(guarded):
        for hk in kv_heads:
            select(hk)

        win = (mw_ref, accw_ref)
        _flash_init(*win)

        @when(i >= 4, guarded)
        def _():
            r = lax.broadcasted_iota(jnp.int32, (rows, TQ), 0) & (TQ - 1)
            c = lax.broadcasted_iota(jnp.int32, (rows, TQ), 1)
            for hk in kv_heads:
                attend(hk, qw_ref, kw_ref, vw_ref, start - 4 * TQ, TQ,
                       jnp.where(c > r, 0.0, MASKED), win)

        @when(i >= 3, guarded)
        def _():
            for hk in kv_heads:
                attend(hk, qw_ref, kw_ref, vw_ref, start - 3 * TQ, 2 * TQ, None, win)

        if guarded:
            @pl.when(i == 2)
            def _():
                for hk in kv_heads:
                    attend(hk, qw_ref, kw_ref, vw_ref, start - 2 * TQ, TQ, None, win)

        near_steps(qw_ref, kw_ref, vw_ref, win, guarded)
        _flash_init(m_ref, acc_ref)

    def tail(guarded):
        near_steps(qs_ref, ks_ref, vs_ref, (m_ref, acc_ref), guarded)
        outs = []
        for hk in kv_heads:
            o_c = oc_ref[hk].T[:HEAD_DIM]
            o_s = _flash_out_t(acc_ref.at[hk])
            o_w = _flash_out_t(accw_ref.at[hk])
            gates = g_ref[0, hk * LANES:(hk + 1) * LANES, :]
            for g in range(NSA_GROUP):
                sl = slice(g * TQ, (g + 1) * TQ)
                outs.append(gates[3 * g:3 * g + 1] * o_c[:, sl]
                            + gates[3 * g + 1:3 * g + 2] * o_s[:, sl]
                            + gates[3 * g + 2:3 * g + 3] * o_w[:, sl])
        o_ref[0] = jnp.concatenate(outs, axis=0).T.astype(bf16)

    interior = i >= 4
    pl.when(interior)(lambda: head(False))
    pl.when(jnp.logical_not(interior))(lambda: head(True))

    sel = (m_ref, acc_ref)
    n_far = jnp.maximum(i - 1, 0)

    def far_steps(off, n_steps):
        for step in range(n_steps):
            for hk in kv_heads:
                attend(hk, qs_ref, ks_ref, vs_ref, off + step * (2 * TQ), 2 * TQ, None, sel)

    def far_trip(c, carry):
        far_steps(c * (8 * TQ), 4)
        return carry

    lax.fori_loop(0, n_far // 8, far_trip, 0)
    rem_off = (n_far // 8) * (8 * TQ)

    @pl.when((n_far & 4) != 0)
    def _():
        far_steps(rem_off, 2)

    @pl.when((n_far & 2) != 0)
    def _():
        far_steps(rem_off + (n_far & 4) * TQ, 1)

    @pl.when((n_far & 1) != 0)
    def _():
        for hk in kv_heads:
            attend(hk, qs_ref, ks_ref, vs_ref, rem_off + (n_far & 6) * TQ, TQ, None, sel)

    pl.when(interior)(lambda: tail(False))
    pl.when(jnp.logical_not(interior))(lambda: tail(True))


def _nsa(qa, gates, kcmp, vcmp, ks, vs, kw, vw, cmp_bias, near_bias, ovt):
    bsz, seq, _ = qa.shape
    nc = kcmp.shape[2]
    n_top = min(SEL_TOPK, seq // SEL_BLOCK)
    rows = NSA_GROUP * TQ
    tok = lambda w: pl.BlockSpec((1, TQ, w), lambda b, i: (b, i, 0))
    kv = pl.BlockSpec((1, seq, NSA_KV_HEADS * LANES), lambda b, i: (b, 0, 0))
    cmp = lambda a: pl.BlockSpec((1,) + a.shape[1:], lambda b, i: (b, 0, 0, 0))
    per_kv = lambda width, dt: pltpu.VMEM((NSA_KV_HEADS, rows, width), dt)
    return pl.pallas_call(
        functools.partial(_nsa_kernel, n_top),
        grid=(bsz, seq // TQ),
        in_specs=[tok(NSA_HEADS * LANES),
                  pl.BlockSpec((1, NSA_KV_HEADS * LANES, TQ), lambda b, i: (b, 0, i)),
                  cmp(kcmp), cmp(vcmp), kv, kv, kv, kv,
                  pl.BlockSpec((NSA_HEADS, TQ, nc), lambda b, i: (0, i, 0)),
                  pl.BlockSpec((NSA_HEADS, TQ, 2 * TQ), lambda b, i: (0, 0, 0)),
                  pl.BlockSpec(ovt.shape, lambda b, i: (0, 0))],
        out_specs=tok(NSA_HEADS * HEAD_DIM),
        out_shape=jax.ShapeDtypeStruct((bsz, seq, NSA_HEADS * HEAD_DIM), bf16),
        scratch_shapes=[per_kv(LANES, bf16), per_kv(LANES, bf16),
                        per_kv(LANES, f32), per_kv(LANES, f32), per_kv(LANES, f32),
                        per_kv(LANES, f32), per_kv(LANES, f32)],
        compiler_params=_params(2),
        name="nsa",
    )(qa, gates, kcmp, vcmp, ks, vs, kw, vw, cmp_bias, near_bias, ovt)


def _sb_kernel(q_ref, k_ref, v_ref, tri_ref, o_ref, carry_ref, acc_ref):
    i = pl.program_id(2)
    n_heads = carry_ref.shape[0]
    lane = lax.broadcasted_iota(jnp.int32, (TS, LANES), 1)
    q_heads = []
    for pair in range(n_heads // 2):
        q = q_ref[0, :, pair * LANES:(pair + 1) * LANES]
        zero = jnp.zeros_like(q)
        q_heads += [jnp.where(lane < HEAD_DIM, q, zero), jnp.where(lane >= HEAD_DIM, q, zero)]
    pair_lanes = lambda hh: slice((hh // 2) * LANES, (hh // 2 + 1) * LANES)
    carry_ref[...] = jnp.zeros(carry_ref.shape, f32)
    acc_ref[...] = jnp.zeros(acc_ref.shape, f32)
    r = lax.broadcasted_iota(jnp.int32, (TS, TS), 0)
    c = lax.broadcasted_iota(jnp.int32, (TS, TS), 1)
    before = c < r

    def chunks(jobs):
        offs = [pl.multiple_of(off, TS) for off, _ in jobs]
        stage = []
        for (_, diagonal), off in zip(jobs, offs):
            for hh in range(n_heads):
                z = _dot_nt(q_heads[hh], k_ref[0, pl.ds(off, TS), pair_lanes(hh)])
                soft = jnp.log2(1.0 + jnp.exp2(-jnp.abs(z)))
                log_keep = jnp.minimum(-z, 0.0) - soft
                log_sig = jnp.minimum(z, 0.0) - soft
                if diagonal:
                    log_keep = jnp.where(before, log_keep, 0.0)
                hi, lo = _split(log_keep)
                later = _dot(hi, tri_ref[...]) + _dot(lo, tri_ref[...])
                stage.append((log_sig + later, jnp.sum(log_keep, axis=1, keepdims=True)))
        for hh in range(n_heads):
            carry = carry_ref[hh]
            acc = acc_ref[hh]
            for j, ((_, diagonal), off) in enumerate(zip(jobs, offs)):
                base, total = stage[n_heads * j + hh]
                a = jnp.exp2(base + jnp.concatenate([carry] * (TS // LANES), axis=1))
                if diagonal:
                    a = jnp.where(before, a, 0.0)
                acc = acc + _dot(a.astype(bf16), v_ref[0, pl.ds(off, TS), pair_lanes(hh)])
                carry = carry + total
            carry_ref[hh] = carry
            acc_ref[hh] = acc

    @pl.when(i == 0)
    def _():
        chunks([(0, True)])

    @pl.when(i >= 1)
    def _():
        chunks([(i * TS, True), ((i - 1) * TS, False)])

    def any_live():
        return jnp.max(carry_ref[...]) > EXP2_UNDERFLOW

    def more(state):
        n, live = state
        return jnp.logical_and(n < i, live)

    def older(state):
        n, _ = state
        chunks([((i - 1 - n) * TS, False)])
        return n + 1, any_live()

    lax.while_loop(more, older, (jnp.int32(1), any_live()))
    o_ref[0] = jnp.concatenate(
        [jnp.where(lane < HEAD_DIM, acc_ref[2 * pair], acc_ref[2 * pair + 1])
         for pair in range(n_heads // 2)], axis=1).astype(bf16)


SB_STEP_HEADS = 4


def _sb(qb, kb, vb, tri):
    bsz, seq, width = qb.shape
    step_w = SB_STEP_HEADS * HEAD_DIM
    q_spec = pl.BlockSpec((1, TS, step_w), lambda b, p, i: (b, i, p))
    kv_spec = pl.BlockSpec((1, seq, step_w), lambda b, p, i: (b, 0, p))
    state = pltpu.VMEM((SB_STEP_HEADS, TS, LANES), f32)
    return pl.pallas_call(
        _sb_kernel,
        grid=(bsz, width // step_w, seq // TS),
        in_specs=[q_spec, kv_spec, kv_spec, pl.BlockSpec(tri.shape, lambda b, p, i: (0, 0))],
        out_specs=q_spec,
        out_shape=jax.ShapeDtypeStruct((bsz, seq, width), bf16),
        scratch_shapes=[state, state],
        compiler_params=_params(3),
        name="sb",
    )(qb, kb, vb, tri)


def _merge_kernel(ya_ref, yb_ref, ma_ref, mb_ref, x_ref, gate_ref, wa_ref, wb_ref, wo_ref, o_ref):
    y_a = _dot(ya_ref[0], wa_ref[...])
    y_b = _dot(yb_ref[0], wb_ref[...])
    mixed = ma_ref[0].astype(f32) * y_a + mb_ref[0].astype(f32) * y_b
    o_ref[0] = x_ref[0] + gate_ref[0] * _dot(mixed.astype(bf16), wo_ref[...])


def _merge(ya, yb, ma, mb, x, gate1, wa, wb, wo):
    bsz, seq, d = x.shape
    tok = lambda w: pl.BlockSpec((1, TM, w), lambda b, i: (b, i, 0))
    full = lambda a: pl.BlockSpec(a.shape, lambda b, i: (0,) * a.ndim)
    return pl.pallas_call(
        _merge_kernel,
        grid=(bsz, seq // TM),
        in_specs=[tok(ya.shape[2]), tok(yb.shape[2]), tok(d), tok(d), tok(d),
                  pl.BlockSpec((1, 1, d), lambda b, i: (b, 0, 0)), full(wa), full(wb), full(wo)],
        out_specs=tok(d),
        out_shape=jax.ShapeDtypeStruct((bsz, seq, d), f32),
        compiler_params=_params(2),
        name="merge",
    )(ya, yb, ma, mb, x, gate1, wa, wb, wo)


def _mlp_kernel(h_ref, g2_ref, sc_ref, sh_ref, gate_ref, w1_ref, w2_ref, o_ref):
    hres = h_ref[0]
    d = hres.shape[1]
    ms = jnp.mean(hres * hres, axis=-1, keepdims=True)
    u = (hres * lax.rsqrt(ms + EPS) * g2_ref[...]) * (1.0 + sc_ref[0]) + sh_ref[0]
    ub = u.astype(bf16)
    ff = jnp.zeros(hres.shape, f32)
    for c in range(w1_ref.shape[1] // d):
        hid = jnp.maximum(_dot(ub, w1_ref[:, c * d:(c + 1) * d]), 0.0)
        ff = ff + _dot((hid * hid).astype(bf16), w2_ref[c * d:(c + 1) * d, :])
    o_ref[0] = hres + gate_ref[0] * ff


def _mlp(hres, g2, scale2, shift2, gate2, w1, w2):
    bsz, seq, d = hres.shape
    tok = pl.BlockSpec((1, TM, d), lambda b, i: (b, i, 0))
    mod = pl.BlockSpec((1, 1, d), lambda b, i: (b, 0, 0))
    const = lambda a: pl.BlockSpec(a.shape, lambda b, i: (0,) * a.ndim,
                                   pipeline_mode=pl.Buffered(1))
    return pl.pallas_call(
        _mlp_kernel,
        grid=(bsz, seq // TM),
        in_specs=[tok, pl.BlockSpec(g2.shape, lambda b, i: (0, 0)), mod, mod, mod,
                  const(w1), const(w2)],
        out_specs=tok,
        out_shape=jax.ShapeDtypeStruct((bsz, seq, d), f32),
        compiler_params=_params(2),
        name="mlp",
    )(hres, g2, scale2, shift2, gate2, w1, w2)


def _overlap_t(nc_pad, nsel_pad, nc, nsel):
    c_start = np.arange(nc_pad) * CMP_STRIDE
    s_start = np.arange(nsel_pad) * SEL_BLOCK
    ov = (np.minimum(c_start[None, :] + CMP_BLOCK, s_start[:, None] + SEL_BLOCK)
          - np.maximum(c_start[None, :], s_start[:, None]))
    ov = np.clip(ov, 0, CMP_BLOCK).astype(np.float32) / CMP_BLOCK
    ov[nsel:, :] = 0.0
    ov[:, nc:] = 0.0
    return ov


def _layer(h, mod, rel_tiles, p):
    bsz, seq, d = h.shape
    shift1, scale1, gate1, shift2, scale2, gate2 = [
        mod[:, k * d:(k + 1) * d].reshape(bsz, 1, d) for k in range(6)]
    cmp_bias, near_bias = rel_tiles

    pad_gain = lambda g, n: jnp.tile(jnp.pad(g, (0, LANES - HEAD_DIM)), n).reshape(1, n * LANES)
    gq = pad_gain(p["q_norm_g"], NSA_HEADS)
    gk = jnp.concatenate([pad_gain(p["k_norm_g"][k], NSA_KV_HEADS) for k in range(3)], axis=0)

    (qa, kc, vc, ks, vs, kw, vw, gates, qb, kb, vb, ma, mb) = _inproj(
        h, p["norm1_g"].reshape(1, d), scale1, shift1, _pack_w_in(p["w_in"], d), gq, gk)

    nch = seq // CMP_STRIDE
    kcmp, vcmp = _compress(kc, vc, p["cmp_pos"], p["cmp_k_w1"], p["cmp_k_w2"],
                           p["cmp_v_w1"], p["cmp_v_w2"], p["k_norm_g"][0].reshape(1, HEAD_DIM))

    nc = (seq - CMP_BLOCK) // CMP_STRIDE + 1
    ovt = jnp.asarray(_overlap_t(nch, HEAD_DIM, nc, seq // SEL_BLOCK), bf16)
    y_nsa = _nsa(qa, gates, kcmp, vcmp, ks, vs, kw, vw, cmp_bias, near_bias, ovt)

    y_sb = _sb(qb, kb, vb, jnp.asarray(np.tril(np.ones((TS, TS)), -1), bf16))

    h1 = _merge(y_nsa, y_sb, ma, mb, h, gate1, p["w_up_nsa"].astype(bf16),
                p["w_up_sb"].astype(bf16), p["w_out"].astype(bf16))
    return _mlp(h1, p["norm2_g"].reshape(1, d), scale2, shift2, gate2,
                p["mlp_w1"].astype(bf16), p["mlp_w2"].astype(bf16))


def kernel(x, c, rel_bias, ada_w, ada_b, norm1_g, norm2_g, w_in, cmp_pos, cmp_k_w1, cmp_k_w2,
           cmp_v_w1, cmp_v_w2, q_norm_g, k_norm_g, w_up_nsa, w_up_sb, w_out, mlp_w1, mlp_w2):
    bsz, seq, d = x.shape
    assert seq % TM == 0 and seq // SEL_BLOCK <= HEAD_DIM and seq >= WINDOW + TQ
    assert CMP_BLOCK == 2 * CMP_STRIDE and TQ == 2 * SEL_BLOCK
    tbl = rel_bias.astype(f32)
    rel_tiles = _bias_tiles(tbl, seq, seq // CMP_STRIDE)
    stacked = dict(norm1_g=norm1_g, norm2_g=norm2_g, w_in=w_in, cmp_pos=cmp_pos,
                   cmp_k_w1=cmp_k_w1, cmp_k_w2=cmp_k_w2, cmp_v_w1=cmp_v_w1, cmp_v_w2=cmp_v_w2,
                   q_norm_g=q_norm_g, k_norm_g=k_norm_g, w_up_nsa=w_up_nsa, w_up_sb=w_up_sb,
                   w_out=w_out, mlp_w1=mlp_w1, mlp_w2=mlp_w2)
    h = x
    for layer in range(ada_w.shape[0]):
        mod = _adaln(c, ada_w[layer], ada_b[layer])
        h = _layer(h, mod, rel_tiles, {k: v[layer] for k, v in stacked.items()})
    return h
```

```python
import functools
import math

import numpy as np
import jax
import jax.numpy as jnp
from jax import lax
from jax.experimental import pallas as pl
from jax.experimental.pallas import tpu as pltpu

f32 = jnp.float32
bf16 = jnp.bfloat16

HEAD_DIM = 64
NSA_HEADS = 8
NSA_KV_HEADS = 2
NSA_GROUP = NSA_HEADS // NSA_KV_HEADS
SB_HEADS = 8
CMP_BLOCK = 32
CMP_STRIDE = 16
SEL_BLOCK = 64
SEL_TOPK = 16
WINDOW = 512
N_BUCKETS = 32
MAX_DISTANCE = 128
EPS = 1e-6
FORCED_BONUS = 1e4
NEG_BLOCK = -1e9

LANES = 128
MASKED = -1e30
UNSELECTED = -1e9
LOG2E = math.log2(math.e)
EXP2_UNDERFLOW = -150.0
VMEM_LIMIT = 56 * 1024 * 1024

TQ = 128
TS = 256
TM = 512


def _bucket_thresholds():
    n = np.arange(0, 4 * MAX_DISTANCE)
    max_exact = N_BUCKETS // 2
    nf = np.maximum(n, 1).astype(np.float32)
    large = max_exact + (np.log(nf / max_exact) / math.log(MAX_DISTANCE / max_exact)
                         * (N_BUCKETS - max_exact)).astype(np.int32)
    large = np.minimum(large, N_BUCKETS - 1)
    b = np.where(n < max_exact, n, large)
    assert np.all(np.diff(b) >= 0) and b[-1] == N_BUCKETS - 1
    return [int(np.argmax(b >= k)) for k in range(N_BUCKETS)]


BUCKET_START = _bucket_thresholds()
assert BUCKET_START[-1] <= LANES


def _dot(a, b):
    return jnp.dot(a, b, preferred_element_type=f32)


def _dot_nt(a, b):
    return lax.dot_general(a, b, (((1,), (1,)), ((), ())), preferred_element_type=f32)


def _split(a):
    hi = a.astype(bf16)
    lo = (a - hi.astype(f32)).astype(bf16)
    return hi, lo


def _params(n_grid):
    return pltpu.CompilerParams(dimension_semantics=("arbitrary",) * n_grid,
                                vmem_limit_bytes=VMEM_LIMIT)


def _adaln_kernel(c_ref, w_ref, b_ref, o_ref):
    c = c_ref[...]
    a = c * jax.nn.sigmoid(c)
    ah, al = _split(a)
    wh, wl = _split(w_ref[...])
    o_ref[...] = _dot(ah, wh) + _dot(ah, wl) + _dot(al, wh) + b_ref[...]


def _adaln(c, w, b):
    bsz, d = c.shape
    n = w.shape[1]
    return pl.pallas_call(
        _adaln_kernel,
        grid=(n // d,),
        in_specs=[pl.BlockSpec((bsz, d), lambda j: (0, 0)),
                  pl.BlockSpec((d, d), lambda j: (0, j)),
                  pl.BlockSpec((1, d), lambda j: (0, j))],
        out_specs=pl.BlockSpec((bsz, d), lambda j: (0, j)),
        out_shape=jax.ShapeDtypeStruct((bsz, n), f32),
        compiler_params=_params(1),
        name="adaln",
    )(c, w, b.reshape(1, n))


def _bias_of_dist(dist, tbl_ref, h):
    out = jnp.full(dist.shape, tbl_ref[0, h], f32)
    for k in range(1, N_BUCKETS):
        out = jnp.where(dist >= BUCKET_START[k], tbl_ref[k, h], out)
    return jnp.where(dist >= 0, out, MASKED)


def _cmp_bias_kernel(tbl_ref, o_ref):
    h = pl.program_id(0)
    i = pl.program_id(1)
    rows, nc = o_ref.shape[1], o_ref.shape[2]
    t = i * rows + lax.broadcasted_iota(jnp.int32, (rows, nc), 0)
    j = lax.broadcasted_iota(jnp.int32, (rows, nc), 1)
    o_ref[0] = _bias_of_dist(t - (j * CMP_STRIDE + CMP_BLOCK - 1), tbl_ref, h) * LOG2E


def _near_bias_kernel(tbl_ref, o_ref):
    h = pl.program_id(0)
    r = lax.broadcasted_iota(jnp.int32, (TQ, 2 * TQ), 0)
    c = lax.broadcasted_iota(jnp.int32, (TQ, 2 * TQ), 1)
    o_ref[0] = (_bias_of_dist(r - c + TQ, tbl_ref, h) - tbl_ref[N_BUCKETS - 1, h]) * LOG2E


def _bias_tiles(rel_bias, seq, nc_pad):
    tbl = rel_bias.astype(f32)
    smem = pl.BlockSpec(memory_space=pltpu.SMEM)
    rows = 512
    cmp_bias = pl.pallas_call(
        _cmp_bias_kernel,
        grid=(NSA_HEADS, seq // rows),
        in_specs=[smem],
        out_specs=pl.BlockSpec((1, rows, nc_pad), lambda h, i: (h, i, 0)),
        out_shape=jax.ShapeDtypeStruct((NSA_HEADS, seq, nc_pad), f32),
        compiler_params=_params(2),
        name="cmp_bias",
    )(tbl)
    near_bias = pl.pallas_call(
        _near_bias_kernel,
        grid=(NSA_HEADS,),
        in_specs=[smem],
        out_specs=pl.BlockSpec((1, TQ, 2 * TQ), lambda h: (h, 0, 0)),
        out_shape=jax.ShapeDtypeStruct((NSA_HEADS, TQ, 2 * TQ), f32),
        compiler_params=_params(1),
        name="near_bias",
    )(tbl)
    return cmp_bias, near_bias


_QA_W = NSA_HEADS * LANES
_KV_W = NSA_KV_HEADS * LANES
_CMP_W = NSA_KV_HEADS * HEAD_DIM
_SB_W = SB_HEADS * HEAD_DIM


def _layout(d_model):
    names = ["qa", "kc", "vc", "ksl", "vsl", "kwn", "vwn", "g", "qb", "kb", "vb", "ma", "mb"]
    widths = [_QA_W, _CMP_W, _CMP_W, _KV_W, _KV_W, _KV_W, _KV_W, _KV_W, _SB_W, _SB_W, _SB_W,
              d_model, d_model]
    offs = np.concatenate([[0], np.cumsum(widths)])
    return {n: (int(offs[i]), int(offs[i + 1])) for i, n in enumerate(names)}, int(offs[-1])


def _pack_w_in(w_in, d_model):
    q_w = NSA_HEADS * HEAD_DIM
    kv_w = NSA_KV_HEADS * HEAD_DIM
    g_w = NSA_HEADS * 3
    sizes = [q_w] + [kv_w] * 6 + [g_w] + [_SB_W] * 3 + [d_model, d_model]
    offs = np.concatenate([[0], np.cumsum(sizes)])
    parts = [w_in[:, int(offs[i]):int(offs[i + 1])] for i in range(len(sizes))]
    d = w_in.shape[0]

    def pad_heads(p, n_heads, width):
        p = p.reshape(d, n_heads, width)
        p = jnp.pad(p, ((0, 0), (0, 0), (0, LANES - width)))
        return p.reshape(d, n_heads * LANES)

    packed = [pad_heads(parts[0], NSA_HEADS, HEAD_DIM), parts[1], parts[2]]
    packed += [pad_heads(parts[k], NSA_KV_HEADS, HEAD_DIM) for k in (3, 4, 5, 6)]
    packed += [pad_heads(parts[7], NSA_KV_HEADS, NSA_GROUP * 3)]
    packed += parts[8:]
    return jnp.concatenate(packed, axis=1).astype(bf16)


def _inproj_kernel(lay, x_ref, g1_ref, sc_ref, sh_ref, w_ref, gq_ref, gk_ref,
                   qa_ref, kc_ref, vc_ref, ks_ref, vs_ref, kw_ref, vw_ref, g_ref,
                   qb_ref, kb_ref, vb_ref, ma_ref, mb_ref):
    i = pl.program_id(1)
    x = x_ref[0]
    ms = jnp.mean(x * x, axis=-1, keepdims=True)
    u = (x * lax.rsqrt(ms + EPS) * g1_ref[...]) * (1.0 + sc_ref[0]) + sh_ref[0]
    ub = u.astype(bf16)

    def proj(name):
        lo, hi = lay[name]
        return _dot(ub, w_ref[:, lo:hi])

    def head_norm(z, gain):
        parts = []
        for g in range(z.shape[1] // LANES):
            zg = z[:, g * LANES:(g + 1) * LANES]
            ss = jnp.sum(zg * zg, axis=1, keepdims=True)
            parts.append(zg * lax.rsqrt(ss * (1.0 / HEAD_DIM) + EPS))
        return jnp.concatenate(parts, axis=1) * gain

    scale = HEAD_DIM ** -0.5 * LOG2E
    qa_ref[0] = (head_norm(proj("qa"), gq_ref[...]) * scale).astype(bf16)
    kc_ref[0] = proj("kc")
    vc_ref[0] = proj("vc")

    rows = x.shape[0]
    lane = lax.broadcasted_iota(jnp.int32, (rows, _KV_W), 1)
    tok_blk = (i * rows + lax.broadcasted_iota(jnp.int32, (rows, _KV_W), 0)) // SEL_BLOCK
    in_pad = (lane & HEAD_DIM) != 0
    onehot = in_pad & ((lane & (HEAD_DIM - 1)) == tok_blk)
    ones_col = (lane & (LANES - 1)) == HEAD_DIM

    ks = head_norm(proj("ksl"), gk_ref[1:2, :])
    ks_ref[0] = jnp.where(onehot, 1.0, ks).astype(bf16)
    vs_ref[0] = jnp.where(ones_col, 1.0, proj("vsl")).astype(bf16)
    kw_ref[0] = head_norm(proj("kwn"), gk_ref[2:3, :]).astype(bf16)
    vw_ref[0] = jnp.where(ones_col, 1.0, proj("vwn")).astype(bf16)
    g_ref[0] = jax.nn.sigmoid(proj("g")).T
    qb_ref[0] = (proj("qb") * scale).astype(bf16)
    kb_ref[0] = proj("kb").astype(bf16)
    vb_ref[0] = proj("vb").astype(bf16)
    ma_ref[0] = jax.nn.sigmoid(proj("ma")).astype(bf16)
    mb_ref[0] = jax.nn.sigmoid(proj("mb")).astype(bf16)


def _inproj(x, g1, scale1, shift1, w_packed, gq, gk):
    bsz, seq, d = x.shape
    lay, width = _layout(d)
    assert w_packed.shape == (d, width)
    tok = lambda w: pl.BlockSpec((1, TM, w), lambda b, i: (b, i, 0))
    full = lambda a: pl.BlockSpec(a.shape, lambda b, i: (0,) * a.ndim,
                                  pipeline_mode=pl.Buffered(1))
    mod = pl.BlockSpec((1, 1, d), lambda b, i: (b, 0, 0))
    out_w = [(_QA_W, bf16), (_CMP_W, f32), (_CMP_W, f32), (_KV_W, bf16), (_KV_W, bf16),
             (_KV_W, bf16), (_KV_W, bf16), (_KV_W, f32), (_SB_W, bf16), (_SB_W, bf16),
             (_SB_W, bf16), (d, bf16), (d, bf16)]
    gates_at = 7
    out_specs = [tok(w) for w, _ in out_w]
    out_shape = [jax.ShapeDtypeStruct((bsz, seq, w), dt) for w, dt in out_w]
    out_specs[gates_at] = pl.BlockSpec((1, _KV_W, TM), lambda b, i: (b, 0, i))
    out_shape[gates_at] = jax.ShapeDtypeStruct((bsz, _KV_W, seq), f32)
    return pl.pallas_call(
        functools.partial(_inproj_kernel, lay),
        grid=(bsz, seq // TM),
        in_specs=[tok(d), full(g1), mod, mod, full(w_packed), full(gq), full(gk)],
        out_specs=out_specs,
        out_shape=out_shape,
        compiler_params=_params(2),
        name="inproj",
    )(x, g1, scale1, shift1, w_packed, gq, gk)


def _compress_kernel(xk_ref, xv_ref, pos_ref, w1k_ref, w2k_ref, w1v_ref, w2v_ref, gk_ref,
                     ko_ref, vo_ref):
    nch = xk_ref.shape[1] // CMP_STRIDE

    def mlp(x_ref, w1_ref, w2_ref):
        first = jnp.zeros((nch, LANES), f32)
        second = jnp.zeros((nch, LANES), f32)
        for l in range(CMP_STRIDE):
            xl = x_ref[0, pl.ds(l, nch, stride=CMP_STRIDE), :]
            lo = l + CMP_STRIDE
            first = first + _dot((xl + pos_ref[l:l + 1, :]).astype(bf16), w1_ref[l])
            second = second + _dot((xl + pos_ref[lo:lo + 1, :]).astype(bf16), w1_ref[lo])
        pre = first + pltpu.roll(second, nch - 1, 0)
        hid = pre * jax.nn.sigmoid(pre)
        return _dot(hid.astype(bf16), w2_ref[...])

    k = mlp(xk_ref, w1k_ref, w2k_ref)
    v = mlp(xv_ref, w1v_ref, w2v_ref)
    for h in range(NSA_KV_HEADS):
        kh = k[:, h * HEAD_DIM:(h + 1) * HEAD_DIM]
        ms = jnp.mean(kh * kh, axis=-1, keepdims=True)
        ko_ref[0, h] = (kh * lax.rsqrt(ms + EPS) * gk_ref[...]).astype(bf16)
        vh = v[:, h * HEAD_DIM:(h + 1) * HEAD_DIM]
        vo_ref[0, h] = jnp.concatenate([vh, jnp.zeros_like(vh)], axis=1).astype(bf16)


def _compress(xk, xv, pos, w1k, w2k, w1v, w2v, gk0):
    bsz, seq, width = xk.shape
    nch = seq // CMP_STRIDE
    assert width == NSA_KV_HEADS * HEAD_DIM == LANES

    def both_heads(w):
        z = jnp.zeros_like(w)
        return jnp.concatenate([jnp.concatenate([w, z], axis=2),
                                jnp.concatenate([z, w], axis=2)], axis=1).astype(bf16)

    w1 = lambda w: both_heads(w.reshape(CMP_BLOCK, HEAD_DIM, w.shape[1]))
    w2 = lambda w: both_heads(w[None])[0]
    args = (xk, xv, jnp.tile(pos, (1, NSA_KV_HEADS)), w1(w1k), w2(w2k), w1(w1v), w2(w2v), gk0)
    blk = pl.BlockSpec((1, seq, width), lambda b: (b, 0, 0))
    full = lambda a: pl.BlockSpec(a.shape, lambda b: (0,) * a.ndim)
    out = lambda w: pl.BlockSpec((1, NSA_KV_HEADS, nch, w), lambda b: (b, 0, 0, 0))
    shape = lambda w: jax.ShapeDtypeStruct((bsz, NSA_KV_HEADS, nch, w), bf16)
    return pl.pallas_call(
        _compress_kernel,
        grid=(bsz,),
        in_specs=[blk, blk] + [full(a) for a in args[2:]],
        out_specs=[out(HEAD_DIM), out(LANES)],
        out_shape=[shape(HEAD_DIM), shape(LANES)],
        compiler_params=_params(1),
        name="compress",
    )(*args)


def _flash_step(s, v, m_ref, acc_ref):
    m_prev = m_ref[...]
    m_new = jnp.maximum(m_prev, jnp.max(s, axis=1, keepdims=True))
    alpha = jnp.exp2(m_prev - m_new)
    p = jnp.exp2(s - jnp.concatenate([m_new] * (s.shape[1] // LANES), axis=1))
    acc_ref[...] = alpha * acc_ref[...] + _dot(p.astype(bf16), v)
    m_ref[...] = m_new


def _flash_init(m_ref, acc_ref):
    m_ref[...] = jnp.full(m_ref.shape, MASKED, f32)
    acc_ref[...] = jnp.zeros(acc_ref.shape, f32)


def _flash_out_t(acc_ref):
    acc_t = acc_ref[...].T
    return acc_t[:HEAD_DIM] / acc_t[HEAD_DIM:HEAD_DIM + 1]


def _nsa_kernel(n_top, q_ref, g_ref, kc_ref, vc_ref, ks_ref, vs_ref, kw_ref, vw_ref,
                bc_ref, bn_ref, ovt_ref, o_ref, qs_ref, qw_ref, oc_ref,
                m_ref, acc_ref, mw_ref, accw_ref):
    i = pl.program_id(1)
    kv_heads = range(NSA_KV_HEADS)
    rows = NSA_GROUP * TQ
    start = i * TQ
    assert WINDOW == 4 * TQ

    def attend(hk, q_rows_ref, k_ref, v_ref, off, width, bias, state):
        off = pl.multiple_of(off, TQ)
        lanes = slice(hk * LANES, (hk + 1) * LANES)
        s = _dot_nt(q_rows_ref[hk], k_ref[0, pl.ds(off, width), lanes])
        if bias is not None:
            s = s + bias
        _flash_step(s, v_ref[0, pl.ds(off, width), lanes], state[0].at[hk], state[1].at[hk])

    def when(cond, guarded):
        return pl.when(cond) if guarded else (lambda fn: fn())

    def near_steps(q_rows_ref, k_ref, v_ref, state, guarded):
        group = lambda hk: slice(hk * NSA_GROUP, (hk + 1) * NSA_GROUP)

        @when(i >= 1, guarded)
        def _():
            for hk in kv_heads:
                attend(hk, q_rows_ref, k_ref, v_ref, start - TQ, 2 * TQ,
                       bn_ref[group(hk)].reshape(rows, 2 * TQ), state)

        if guarded:
            @pl.when(i == 0)
            def _():
                for hk in kv_heads:
                    attend(hk, q_rows_ref, k_ref, v_ref, start, TQ,
                           bn_ref[group(hk), :, TQ:].reshape(rows, TQ), state)

    def select(hk):
        heads = range(hk * NSA_GROUP, (hk + 1) * NSA_GROUP)
        qpad = jnp.concatenate([q_ref[0, :, g * LANES:(g + 1) * LANES] for g in heads], axis=0)
        qw_ref[hk] = qpad

        bc = bc_ref[hk * NSA_GROUP:(hk + 1) * NSA_GROUP].reshape(rows, bc_ref.shape[2])
        s_c = _dot_nt(qpad[:, :HEAD_DIM], kc_ref[0, hk]) + bc
        visible = bc > 0.5 * MASKED
        m_c = jnp.max(s_c, axis=1, keepdims=True)
        e_c = jnp.where(visible, jnp.exp2(s_c - m_c), 0.0)
        p_c = e_c / jnp.maximum(jnp.sum(e_c, axis=1, keepdims=True), 1e-30)
        oc_ref[hk] = _dot(p_c.astype(bf16), vc_ref[0, hk])

        p_sum = p_c[0:TQ]
        for g in range(1, NSA_GROUP):
            p_sum = p_sum + p_c[g * TQ:(g + 1) * TQ]
        p_hi, p_lo = _split(p_sum)
        imp = _dot_nt(ovt_ref[...], p_hi) + _dot_nt(ovt_ref[...], p_lo)
        nblk = imp.shape[0]
        blk = lax.broadcasted_iota(jnp.int32, (nblk, TQ), 0)
        cur = (start + lax.broadcasted_iota(jnp.int32, (nblk, TQ), 1)) // SEL_BLOCK
        forced = (blk == 0) | (blk == cur) | (blk == cur - 1)
        imp = jnp.where(blk > cur, NEG_BLOCK, imp + jnp.where(forced, FORCED_BONUS, 0.0))
        sub = 8
        groups = [imp[lo:lo + sub] for lo in range(0, nblk, sub)]
        ranks = [jnp.zeros((sub, TQ), f32) for _ in groups]
        row = lax.broadcasted_iota(jnp.int32, (sub, TQ), 0)
        for b2 in range(nblk):
            other = imp[b2:b2 + 1, :]
            for gi, grp in enumerate(groups):
                lo = gi * sub
                if lo > b2:
                    ranks[gi] = jnp.where(other >= grp, ranks[gi] + 1.0, ranks[gi])
                elif lo + sub - 1 < b2:
                    ranks[gi] = jnp.where(other > grp, ranks[gi] + 1.0, ranks[gi])
                else:
                    ranks[gi] = ranks[gi] + jnp.where(row + lo > b2,
                                                      jnp.where(other >= grp, 1.0, 0.0),
                                                      jnp.where(other > grp, 1.0, 0.0))
        rank = jnp.concatenate(ranks, axis=0)
        usable = (rank < n_top) & (blk <= cur)
        sel_t = jnp.where(usable, 0.0, UNSELECTED)
        sel_pad = jnp.concatenate([jnp.zeros((LANES - nblk, TQ), f32), sel_t], axis=0).T
        sel_rows = jnp.concatenate([sel_pad.astype(bf16)] * NSA_GROUP, axis=0)
        qs_ref[hk] = qpad + sel_rows

    def head(guarded):
        for hk in kv_heads:
            select(hk)

        win = (mw_ref, accw_ref)
        _flash_init(*win)

        @when(i >= 4, guarded)
        def _():
            r = lax.broadcasted_iota(jnp.int32, (rows, TQ), 0) & (TQ - 1)
            c = lax.broadcasted_iota(jnp.int32, (rows, TQ), 1)
            for hk in kv_heads:
                attend(hk, qw_ref, kw_ref, vw_ref, start - 4 * TQ, TQ,
                       jnp.where(c > r, 0.0, MASKED), win)

        @when(i >= 3, guarded)
        def _():
            for hk in kv_heads:
                attend(hk, qw_ref, kw_ref, vw_ref, start - 3 * TQ, 2 * TQ, None, win)

        if guarded:
            @pl.when(i == 2)
            def _():
                for hk in kv_heads:
                    attend(hk, qw_ref, kw_ref, vw_ref, start - 2 * TQ, TQ, None, win)

        near_steps(qw_ref, kw_ref, vw_ref, win, guarded)
        _flash_init(m_ref, acc_ref)

    def tail(guarded):
        near_steps(qs_ref, ks_ref, vs_ref, (m_ref, acc_ref), guarded)
        outs = []
        for hk in kv_heads:
            o_c = oc_ref[hk].T[:HEAD_DIM]
            o_s = _flash_out_t(acc_ref.at[hk])
            o_w = _flash_out_t(accw_ref.at[hk])
            gates = g_ref[0, hk * LANES:(hk + 1) * LANES, :]
            for g in range(NSA_GROUP):
                sl = slice(g * TQ, (g + 1) * TQ)
                outs.append(gates[3 * g:3 * g + 1] * o_c[:, sl]
                            + gates[3 * g + 1:3 * g + 2] * o_s[:, sl]
                            + gates[3 * g + 2:3 * g + 3] * o_w[:, sl])
        o_ref[0] = jnp.concatenate(outs, axis=0).T.astype(bf16)

    interior = i >= 4
    pl.when(interior)(lambda: head(False))
    pl.when(jnp.logical_not(interior))(lambda: head(True))

    sel = (m_ref, acc_ref)
    n_far = jnp.maximum(i - 1, 0)

    def far_steps(off, n_steps):
        for step in range(n_steps):
            for hk in kv_heads:
                attend(hk, qs_ref, ks_ref, vs_ref, off + step * (2 * TQ), 2 * TQ, None, sel)

    def far_trip(c, carry):
        far_steps(c * (8 * TQ), 4)
        return carry

    lax.fori_loop(0, n_far // 8, far_trip, 0)
    rem_off = (n_far // 8) * (8 * TQ)

    @pl.when((n_far & 4) != 0)
    def _():
        far_steps(rem_off, 2)

    @pl.when((n_far & 2) != 0)
    def _():
        far_steps(rem_off + (n_far & 4) * TQ, 1)

    @pl.when((n_far & 1) != 0)
    def _():
        for hk in kv_heads:
            attend(hk, qs_ref, ks_ref, vs_ref, rem_off + (n_far & 6) * TQ, TQ, None, sel)

    pl.when(interior)(lambda: tail(False))
    pl.when(jnp.logical_not(interior))(lambda: tail(True))


def _nsa(qa, gates, kcmp, vcmp, ks, vs, kw, vw, cmp_bias, near_bias, ovt):
    bsz, seq, _ = qa.shape
    nc = kcmp.shape[2]
    n_top = min(SEL_TOPK, seq // SEL_BLOCK)
    rows = NSA_GROUP * TQ
    tok = lambda w: pl.BlockSpec((1, TQ, w), lambda b, i: (b, i, 0))
    kv = pl.BlockSpec((1, seq, NSA_KV_HEADS * LANES), lambda b, i: (b, 0, 0))
    cmp = lambda a: pl.BlockSpec((1,) + a.shape[1:], lambda b, i: (b, 0, 0, 0))
    per_kv = lambda width, dt: pltpu.VMEM((NSA_KV_HEADS, rows, width), dt)
    return pl.pallas_call(
        functools.partial(_nsa_kernel, n_top),
        grid=(bsz, seq // TQ),
        in_specs=[tok(NSA_HEADS * LANES),
                  pl.BlockSpec((1, NSA_KV_HEADS * LANES, TQ), lambda b, i: (b, 0, i)),
                  cmp(kcmp), cmp(vcmp), kv, kv, kv, kv,
                  pl.BlockSpec((NSA_HEADS, TQ, nc), lambda b, i: (0, i, 0)),
                  pl.BlockSpec((NSA_HEADS, TQ, 2 * TQ), lambda b, i: (0, 0, 0)),
                  pl.BlockSpec(ovt.shape, lambda b, i: (0, 0))],
        out_specs=tok(NSA_HEADS * HEAD_DIM),
        out_shape=jax.ShapeDtypeStruct((bsz, seq, NSA_HEADS * HEAD_DIM), bf16),
        scratch_shapes=[per_kv(LANES, bf16), per_kv(LANES, bf16),
                        per_kv(LANES, f32), per_kv(LANES, f32), per_kv(LANES, f32),
                        per_kv(LANES, f32), per_kv(LANES, f32)],
        compiler_params=_params(2),
        name="nsa",
    )(qa, gates, kcmp, vcmp, ks, vs, kw, vw, cmp_bias, near_bias, ovt)


def _sb_kernel(q_ref, k_ref, v_ref, tri_ref, o_ref, carry_ref, acc_ref):
    i = pl.program_id(2)
    n_heads = carry_ref.shape[0]
    lane = lax.broadcasted_iota(jnp.int32, (TS, LANES), 1)
    q_heads = []
    for pair in range(n_heads // 2):
        q = q_ref[0, :, pair * LANES:(pair + 1) * LANES]
        zero = jnp.zeros_like(q)
        q_heads += [jnp.where(lane < HEAD_DIM, q, zero), jnp.where(lane >= HEAD_DIM, q, zero)]
    pair_lanes = lambda hh: slice((hh // 2) * LANES, (hh // 2 + 1) * LANES)
    carry_ref[...] = jnp.zeros(carry_ref.shape, f32)
    acc_ref[...] = jnp.zeros(acc_ref.shape, f32)
    r = lax.broadcasted_iota(jnp.int32, (TS, TS), 0)
    c = lax.broadcasted_iota(jnp.int32, (TS, TS), 1)
    before = c < r

    def chunks(jobs):
        offs = [pl.multiple_of(off, TS) for off, _ in jobs]
        stage = []
        for (_, diagonal), off in zip(jobs, offs):
            for hh in range(n_heads):
                z = _dot_nt(q_heads[hh], k_ref[0, pl.ds(off, TS), pair_lanes(hh)])
                soft = jnp.log2(1.0 + jnp.exp2(-jnp.abs(z)))
                log_keep = jnp.minimum(-z, 0.0) - soft
                log_sig = jnp.minimum(z, 0.0) - soft
                if diagonal:
                    log_keep = jnp.where(before, log_keep, 0.0)
                hi, lo = _split(log_keep)
                later = _dot(hi, tri_ref[...]) + _dot(lo, tri_ref[...])
                stage.append((log_sig + later, jnp.sum(log_keep, axis=1, keepdims=True)))
        for hh in range(n_heads):
            carry = carry_ref[hh]
            acc = acc_ref[hh]
            for j, ((_, diagonal), off) in enumerate(zip(jobs, offs)):
                base, total = stage[n_heads * j + hh]
                a = jnp.exp2(base + jnp.concatenate([carry] * (TS // LANES), axis=1))
                if diagonal:
                    a = jnp.where(before, a, 0.0)
                acc = acc + _dot(a.astype(bf16), v_ref[0, pl.ds(off, TS), pair_lanes(hh)])
                carry = carry + total
            carry_ref[hh] = carry
            acc_ref[hh] = acc

    @pl.when(i == 0)
    def _():
        chunks([(0, True)])

    @pl.when(i >= 1)
    def _():
        chunks([(i * TS, True), ((i - 1) * TS, False)])

    def any_live():
        return jnp.max(carry_ref[...]) > EXP2_UNDERFLOW

    def more(state):
        n, live = state
        return jnp.logical_and(n < i, live)

    def older(state):
        n, _ = state
        chunks([((i - 1 - n) * TS, False)])
        return n + 1, any_live()

    lax.while_loop(more, older, (jnp.int32(1), any_live()))
    o_ref[0] = jnp.concatenate(
        [jnp.where(lane < HEAD_DIM, acc_ref[2 * pair], acc_ref[2 * pair + 1])
         for pair in range(n_heads // 2)], axis=1).astype(bf16)


SB_STEP_HEADS = 8


def _sb(qb, kb, vb, tri):
    bsz, seq, width = qb.shape
    step_w = SB_STEP_HEADS * HEAD_DIM
    q_spec = pl.BlockSpec((1, TS, step_w), lambda b, p, i: (b, i, p))
    kv_spec = pl.BlockSpec((1, seq, step_w), lambda b, p, i: (b, 0, p))
    state = pltpu.VMEM((SB_STEP_HEADS, TS, LANES), f32)
    return pl.pallas_call(
        _sb_kernel,
        grid=(bsz, width // step_w, seq // TS),
        in_specs=[q_spec, kv_spec, kv_spec, pl.BlockSpec(tri.shape, lambda b, p, i: (0, 0))],
        out_specs=q_spec,
        out_shape=jax.ShapeDtypeStruct((bsz, seq, width), bf16),
        scratch_shapes=[state, state],
        compiler_params=_params(3),
        name="sb",
    )(qb, kb, vb, tri)


def _merge_kernel(ya_ref, yb_ref, ma_ref, mb_ref, x_ref, gate_ref, wa_ref, wb_ref, wo_ref, o_ref):
    y_a = _dot(ya_ref[0], wa_ref[...])
    y_b = _dot(yb_ref[0], wb_ref[...])
    mixed = ma_ref[0].astype(f32) * y_a + mb_ref[0].astype(f32) * y_b
    o_ref[0] = x_ref[0] + gate_ref[0] * _dot(mixed.astype(bf16), wo_ref[...])


def _merge(ya, yb, ma, mb, x, gate1, wa, wb, wo):
    bsz, seq, d = x.shape
    tok = lambda w: pl.BlockSpec((1, TM, w), lambda b, i: (b, i, 0))
    full = lambda a: pl.BlockSpec(a.shape, lambda b, i: (0,) * a.ndim)
    return pl.pallas_call(
        _merge_kernel,
        grid=(bsz, seq // TM),
        in_specs=[tok(ya.shape[2]), tok(yb.shape[2]), tok(d), tok(d), tok(d),
                  pl.BlockSpec((1, 1, d), lambda b, i: (b, 0, 0)), full(wa), full(wb), full(wo)],
        out_specs=tok(d),
        out_shape=jax.ShapeDtypeStruct((bsz, seq, d), f32),
        compiler_params=_params(2),
        name="merge",
    )(ya, yb, ma, mb, x, gate1, wa, wb, wo)


def _mlp_kernel(h_ref, g2_ref, sc_ref, sh_ref, gate_ref, w1_ref, w2_ref, o_ref):
    hres = h_ref[0]
    d = hres.shape[1]
    ms = jnp.mean(hres * hres, axis=-1, keepdims=True)
    u = (hres * lax.rsqrt(ms + EPS) * g2_ref[...]) * (1.0 + sc_ref[0]) + sh_ref[0]
    ub = u.astype(bf16)
    ff = jnp.zeros(hres.shape, f32)
    for c in range(w1_ref.shape[1] // d):
        hid = jnp.maximum(_dot(ub, w1_ref[:, c * d:(c + 1) * d]), 0.0)
        ff = ff + _dot((hid * hid).astype(bf16), w2_ref[c * d:(c + 1) * d, :])
    o_ref[0] = hres + gate_ref[0] * ff


def _mlp(hres, g2, scale2, shift2, gate2, w1, w2):
    bsz, seq, d = hres.shape
    tok = pl.BlockSpec((1, TM, d), lambda b, i: (b, i, 0))
    mod = pl.BlockSpec((1, 1, d), lambda b, i: (b, 0, 0))
    const = lambda a: pl.BlockSpec(a.shape, lambda b, i: (0,) * a.ndim,
                                   pipeline_mode=pl.Buffered(1))
    return pl.pallas_call(
        _mlp_kernel,
        grid=(bsz, seq // TM),
        in_specs=[tok, pl.BlockSpec(g2.shape, lambda b, i: (0, 0)), mod, mod, mod,
                  const(w1), const(w2)],
        out_specs=tok,
        out_shape=jax.ShapeDtypeStruct((bsz, seq, d), f32),
        compiler_params=_params(2),
        name="mlp",
    )(hres, g2, scale2, shift2, gate2, w1, w2)


def _overlap_t(nc_pad, nsel_pad, nc, nsel):
    c_start = np.arange(nc_pad) * CMP_STRIDE
    s_start = np.arange(nsel_pad) * SEL_BLOCK
    ov = (np.minimum(c_start[None, :] + CMP_BLOCK, s_start[:, None] + SEL_BLOCK)
          - np.maximum(c_start[None, :], s_start[:, None]))
    ov = np.clip(ov, 0, CMP_BLOCK).astype(np.float32) / CMP_BLOCK
    ov[nsel:, :] = 0.0
    ov[:, nc:] = 0.0
    return ov


def _layer(h, mod, rel_tiles, p):
    bsz, seq, d = h.shape
    shift1, scale1, gate1, shift2, scale2, gate2 = [
        mod[:, k * d:(k + 1) * d].reshape(bsz, 1, d) for k in range(6)]
    cmp_bias, near_bias = rel_tiles

    pad_gain = lambda g, n: jnp.tile(jnp.pad(g, (0, LANES - HEAD_DIM)), n).reshape(1, n * LANES)
    gq = pad_gain(p["q_norm_g"], NSA_HEADS)
    gk = jnp.concatenate([pad_gain(p["k_norm_g"][k], NSA_KV_HEADS) for k in range(3)], axis=0)

    (qa, kc, vc, ks, vs, kw, vw, gates, qb, kb, vb, ma, mb) = _inproj(
        h, p["norm1_g"].reshape(1, d), scale1, shift1, _pack_w_in(p["w_in"], d), gq, gk)

    nch = seq // CMP_STRIDE
    kcmp, vcmp = _compress(kc, vc, p["cmp_pos"], p["cmp_k_w1"], p["cmp_k_w2"],
                           p["cmp_v_w1"], p["cmp_v_w2"], p["k_norm_g"][0].reshape(1, HEAD_DIM))

    nc = (seq - CMP_BLOCK) // CMP_STRIDE + 1
    ovt = jnp.asarray(_overlap_t(nch, HEAD_DIM, nc, seq // SEL_BLOCK), bf16)
    y_nsa = _nsa(qa, gates, kcmp, vcmp, ks, vs, kw, vw, cmp_bias, near_bias, ovt)

    y_sb = _sb(qb, kb, vb, jnp.asarray(np.tril(np.ones((TS, TS)), -1), bf16))

    h1 = _merge(y_nsa, y_sb, ma, mb, h, gate1, p["w_up_nsa"].astype(bf16),
                p["w_up_sb"].astype(bf16), p["w_out"].astype(bf16))
    return _mlp(h1, p["norm2_g"].reshape(1, d), scale2, shift2, gate2,
                p["mlp_w1"].astype(bf16), p["mlp_w2"].astype(bf16))


def kernel(x, c, rel_bias, ada_w, ada_b, norm1_g, norm2_g, w_in, cmp_pos, cmp_k_w1, cmp_k_w2,
           cmp_v_w1, cmp_v_w2, q_norm_g, k_norm_g, w_up_nsa, w_up_sb, w_out, mlp_w1, mlp_w2):
    bsz, seq, d = x.shape
    assert seq % TM == 0 and seq // SEL_BLOCK <= HEAD_DIM and seq >= WINDOW + TQ
    assert CMP_BLOCK == 2 * CMP_STRIDE and TQ == 2 * SEL_BLOCK
    tbl = rel_bias.astype(f32)
    rel_tiles = _bias_tiles(tbl, seq, seq // CMP_STRIDE)
    stacked = dict(norm1_g=norm1_g, norm2_g=norm2_g, w_in=w_in, cmp_pos=cmp_pos,
                   cmp_k_w1=cmp_k_w1, cmp_k_w2=cmp_k_w2, cmp_v_w1=cmp_v_w1, cmp_v_w2=cmp_v_w2,
                   q_norm_g=q_norm_g, k_norm_g=k_norm_g, w_up_nsa=w_up_nsa, w_up_sb=w_up_sb,
                   w_out=w_out, mlp_w1=mlp_w1, mlp_w2=mlp_w2)
    h = x
    for layer in range(ada_w.shape[0]):
        mod = _adaln(c, ada_w[layer], ada_b[layer])
        h = _layer(h, mod, rel_tiles, {k: v[layer] for k, v in stacked.items()})
    return h
```

```python
import functools
import math

import numpy as np
import jax
import jax.numpy as jnp
from jax import lax
from jax.experimental import pallas as pl
from jax.experimental.pallas import tpu as pltpu

f32 = jnp.float32
bf16 = jnp.bfloat16

HEAD_DIM = 64
NSA_HEADS = 8
NSA_KV_HEADS = 2
NSA_GROUP = NSA_HEADS // NSA_KV_HEADS
SB_HEADS = 8
CMP_BLOCK = 32
CMP_STRIDE = 16
SEL_BLOCK = 64
SEL_TOPK = 16
WINDOW = 512
N_BUCKETS = 32
MAX_DISTANCE = 128
EPS = 1e-6
FORCED_BONUS = 1e4
NEG_BLOCK = -1e9

LANES = 128
MASKED = -1e30
UNSELECTED = -1e9
LOG2E = math.log2(math.e)
EXP2_UNDERFLOW = -150.0
VMEM_LIMIT = 56 * 1024 * 1024

TQ = 128
TS = 256
TM = 512


def _bucket_thresholds():
    n = np.arange(0, 4 * MAX_DISTANCE)
    max_exact = N_BUCKETS // 2
    nf = np.maximum(n, 1).astype(np.float32)
    large = max_exact + (np.log(nf / max_exact) / math.log(MAX_DISTANCE / max_exact)
                         * (N_BUCKETS - max_exact)).astype(np.int32)
    large = np.minimum(large, N_BUCKETS - 1)
    b = np.where(n < max_exact, n, large)
    assert np.all(np.diff(b) >= 0) and b[-1] == N_BUCKETS - 1
    return [int(np.argmax(b >= k)) for k in range(N_BUCKETS)]


BUCKET_START = _bucket_thresholds()
assert BUCKET_START[-1] <= LANES


def _dot(a, b):
    return jnp.dot(a, b, preferred_element_type=f32)


def _dot_nt(a, b):
    return lax.dot_general(a, b, (((1,), (1,)), ((), ())), preferred_element_type=f32)


def _split(a):
    hi = a.astype(bf16)
    lo = (a - hi.astype(f32)).astype(bf16)
    return hi, lo


def _params(n_grid):
    return pltpu.CompilerParams(dimension_semantics=("arbitrary",) * n_grid,
                                vmem_limit_bytes=VMEM_LIMIT)


def _adaln_kernel(c_ref, w_ref, b_ref, o_ref):
    c = c_ref[...]
    a = c * jax.nn.sigmoid(c)
    ah, al = _split(a)
    wh, wl = _split(w_ref[...])
    o_ref[...] = _dot(ah, wh) + _dot(ah, wl) + _dot(al, wh) + b_ref[...]


def _adaln(c, w, b):
    bsz, d = c.shape
    n = w.shape[1]
    return pl.pallas_call(
        _adaln_kernel,
        grid=(n // d,),
        in_specs=[pl.BlockSpec((bsz, d), lambda j: (0, 0)),
                  pl.BlockSpec((d, d), lambda j: (0, j)),
                  pl.BlockSpec((1, d), lambda j: (0, j))],
        out_specs=pl.BlockSpec((bsz, d), lambda j: (0, j)),
        out_shape=jax.ShapeDtypeStruct((bsz, n), f32),
        compiler_params=_params(1),
        name="adaln",
    )(c, w, b.reshape(1, n))


def _bias_of_dist(dist, tbl_ref, h):
    out = jnp.full(dist.shape, tbl_ref[0, h], f32)
    for k in range(1, N_BUCKETS):
        out = jnp.where(dist >= BUCKET_START[k], tbl_ref[k, h], out)
    return jnp.where(dist >= 0, out, MASKED)


def _cmp_bias_kernel(tbl_ref, o_ref):
    h = pl.program_id(0)
    i = pl.program_id(1)
    rows, nc = o_ref.shape[1], o_ref.shape[2]
    t = i * rows + lax.broadcasted_iota(jnp.int32, (rows, nc), 0)
    j = lax.broadcasted_iota(jnp.int32, (rows, nc), 1)
    o_ref[0] = _bias_of_dist(t - (j * CMP_STRIDE + CMP_BLOCK - 1), tbl_ref, h) * LOG2E


def _near_bias_kernel(tbl_ref, o_ref):
    h = pl.program_id(0)
    r = lax.broadcasted_iota(jnp.int32, (TQ, 2 * TQ), 0)
    c = lax.broadcasted_iota(jnp.int32, (TQ, 2 * TQ), 1)
    o_ref[0] = (_bias_of_dist(r - c + TQ, tbl_ref, h) - tbl_ref[N_BUCKETS - 1, h]) * LOG2E


def _bias_tiles(rel_bias, seq, nc_pad):
    tbl = rel_bias.astype(f32)
    smem = pl.BlockSpec(memory_space=pltpu.SMEM)
    rows = 512
    cmp_bias = pl.pallas_call(
        _cmp_bias_kernel,
        grid=(NSA_HEADS, seq // rows),
        in_specs=[smem],
        out_specs=pl.BlockSpec((1, rows, nc_pad), lambda h, i: (h, i, 0)),
        out_shape=jax.ShapeDtypeStruct((NSA_HEADS, seq, nc_pad), f32),
        compiler_params=_params(2),
        name="cmp_bias",
    )(tbl)
    near_bias = pl.pallas_call(
        _near_bias_kernel,
        grid=(NSA_HEADS,),
        in_specs=[smem],
        out_specs=pl.BlockSpec((1, TQ, 2 * TQ), lambda h: (h, 0, 0)),
        out_shape=jax.ShapeDtypeStruct((NSA_HEADS, TQ, 2 * TQ), f32),
        compiler_params=_params(1),
        name="near_bias",
    )(tbl)
    return cmp_bias, near_bias


_QA_W = NSA_HEADS * LANES
_KV_W = NSA_KV_HEADS * LANES
_CMP_W = NSA_KV_HEADS * HEAD_DIM
_SB_W = SB_HEADS * HEAD_DIM


def _layout(d_model):
    names = ["qa", "kc", "vc", "ksl", "vsl", "kwn", "vwn", "g", "qb", "kb", "vb", "ma", "mb"]
    widths = [_QA_W, _CMP_W, _CMP_W, _KV_W, _KV_W, _KV_W, _KV_W, _KV_W, _SB_W, _SB_W, _SB_W,
              d_model, d_model]
    offs = np.concatenate([[0], np.cumsum(widths)])
    return {n: (int(offs[i]), int(offs[i + 1])) for i, n in enumerate(names)}, int(offs[-1])


def _pack_w_in(w_in, d_model):
    q_w = NSA_HEADS * HEAD_DIM
    kv_w = NSA_KV_HEADS * HEAD_DIM
    g_w = NSA_HEADS * 3
    sizes = [q_w] + [kv_w] * 6 + [g_w] + [_SB_W] * 3 + [d_model, d_model]
    offs = np.concatenate([[0], np.cumsum(sizes)])
    parts = [w_in[:, int(offs[i]):int(offs[i + 1])] for i in range(len(sizes))]
    d = w_in.shape[0]

    def pad_heads(p, n_heads, width):
        p = p.reshape(d, n_heads, width)
        p = jnp.pad(p, ((0, 0), (0, 0), (0, LANES - width)))
        return p.reshape(d, n_heads * LANES)

    packed = [pad_heads(parts[0], NSA_HEADS, HEAD_DIM), parts[1], parts[2]]
    packed += [pad_heads(parts[k], NSA_KV_HEADS, HEAD_DIM) for k in (3, 4, 5, 6)]
    packed += [pad_heads(parts[7], NSA_KV_HEADS, NSA_GROUP * 3)]
    packed += parts[8:]
    return jnp.concatenate(packed, axis=1).astype(bf16)


def _inproj_kernel(lay, x_ref, g1_ref, sc_ref, sh_ref, w_ref, gq_ref, gk_ref,
                   qa_ref, kc_ref, vc_ref, ks_ref, vs_ref, kw_ref, vw_ref, g_ref,
                   qb_ref, kb_ref, vb_ref, ma_ref, mb_ref):
    i = pl.program_id(1)
    x = x_ref[0]
    ms = jnp.mean(x * x, axis=-1, keepdims=True)
    u = (x * lax.rsqrt(ms + EPS) * g1_ref[...]) * (1.0 + sc_ref[0]) + sh_ref[0]
    ub = u.astype(bf16)

    def proj(name):
        lo, hi = lay[name]
        return _dot(ub, w_ref[:, lo:hi])

    def head_norm(z, gain):
        parts = []
        for g in range(z.shape[1] // LANES):
            zg = z[:, g * LANES:(g + 1) * LANES]
            ss = jnp.sum(zg * zg, axis=1, keepdims=True)
            parts.append(zg * lax.rsqrt(ss * (1.0 / HEAD_DIM) + EPS))
        return jnp.concatenate(parts, axis=1) * gain

    scale = HEAD_DIM ** -0.5 * LOG2E
    qa_ref[0] = (head_norm(proj("qa"), gq_ref[...]) * scale).astype(bf16)
    kc_ref[0] = proj("kc")
    vc_ref[0] = proj("vc")

    rows = x.shape[0]
    lane = lax.broadcasted_iota(jnp.int32, (rows, _KV_W), 1)
    tok_blk = (i * rows + lax.broadcasted_iota(jnp.int32, (rows, _KV_W), 0)) // SEL_BLOCK
    in_pad = (lane & HEAD_DIM) != 0
    onehot = in_pad & ((lane & (HEAD_DIM - 1)) == tok_blk)
    ones_col = (lane & (LANES - 1)) == HEAD_DIM

    ks = head_norm(proj("ksl"), gk_ref[1:2, :])
    ks_ref[0] = jnp.where(onehot, 1.0, ks).astype(bf16)
    vs_ref[0] = jnp.where(ones_col, 1.0, proj("vsl")).astype(bf16)
    kw_ref[0] = head_norm(proj("kwn"), gk_ref[2:3, :]).astype(bf16)
    vw_ref[0] = jnp.where(ones_col, 1.0, proj("vwn")).astype(bf16)
    g_ref[0] = jax.nn.sigmoid(proj("g")).T
    qb_ref[0] = (proj("qb") * scale).astype(bf16)
    kb_ref[0] = proj("kb").astype(bf16)
    vb_ref[0] = proj("vb").astype(bf16)
    ma_ref[0] = jax.nn.sigmoid(proj("ma")).astype(bf16)
    mb_ref[0] = jax.nn.sigmoid(proj("mb")).astype(bf16)


def _inproj(x, g1, scale1, shift1, w_packed, gq, gk):
    bsz, seq, d = x.shape
    lay, width = _layout(d)
    assert w_packed.shape == (d, width)
    tok = lambda w: pl.BlockSpec((1, TM, w), lambda b, i: (b, i, 0))
    full = lambda a: pl.BlockSpec(a.shape, lambda b, i: (0,) * a.ndim,
                                  pipeline_mode=pl.Buffered(1))
    mod = pl.BlockSpec((1, 1, d), lambda b, i: (b, 0, 0))
    out_w = [(_QA_W, bf16), (_CMP_W, f32), (_CMP_W, f32), (_KV_W, bf16), (_KV_W, bf16),
             (_KV_W, bf16), (_KV_W, bf16), (_KV_W, f32), (_SB_W, bf16), (_SB_W, bf16),
             (_SB_W, bf16), (d, bf16), (d, bf16)]
    gates_at = 7
    out_specs = [tok(w) for w, _ in out_w]
    out_shape = [jax.ShapeDtypeStruct((bsz, seq, w), dt) for w, dt in out_w]
    out_specs[gates_at] = pl.BlockSpec((1, _KV_W, TM), lambda b, i: (b, 0, i))
    out_shape[gates_at] = jax.ShapeDtypeStruct((bsz, _KV_W, seq), f32)
    return pl.pallas_call(
        functools.partial(_inproj_kernel, lay),
        grid=(bsz, seq // TM),
        in_specs=[tok(d), full(g1), mod, mod, full(w_packed), full(gq), full(gk)],
        out_specs=out_specs,
        out_shape=out_shape,
        compiler_params=_params(2),
        name="inproj",
    )(x, g1, scale1, shift1, w_packed, gq, gk)


def _compress_kernel(xk_ref, xv_ref, pos_ref, w1k_ref, w2k_ref, w1v_ref, w2v_ref, gk_ref,
                     ko_ref, vo_ref):
    nch = xk_ref.shape[1] // CMP_STRIDE

    def mlp(x_ref, w1_ref, w2_ref):
        first = jnp.zeros((nch, LANES), f32)
        second = jnp.zeros((nch, LANES), f32)
        for l in range(CMP_STRIDE):
            xl = x_ref[0, pl.ds(l, nch, stride=CMP_STRIDE), :]
            lo = l + CMP_STRIDE
            first = first + _dot((xl + pos_ref[l:l + 1, :]).astype(bf16), w1_ref[l])
            second = second + _dot((xl + pos_ref[lo:lo + 1, :]).astype(bf16), w1_ref[lo])
        pre = first + pltpu.roll(second, nch - 1, 0)
        hid = pre * jax.nn.sigmoid(pre)
        return _dot(hid.astype(bf16), w2_ref[...])

    k = mlp(xk_ref, w1k_ref, w2k_ref)
    v = mlp(xv_ref, w1v_ref, w2v_ref)
    for h in range(NSA_KV_HEADS):
        kh = k[:, h * HEAD_DIM:(h + 1) * HEAD_DIM]
        ms = jnp.mean(kh * kh, axis=-1, keepdims=True)
        ko_ref[0, h] = (kh * lax.rsqrt(ms + EPS) * gk_ref[...]).astype(bf16)
        vh = v[:, h * HEAD_DIM:(h + 1) * HEAD_DIM]
        vo_ref[0, h] = jnp.concatenate([vh, jnp.zeros_like(vh)], axis=1).astype(bf16)


def _compress(xk, xv, pos, w1k, w2k, w1v, w2v, gk0):
    bsz, seq, width = xk.shape
    nch = seq // CMP_STRIDE
    assert width == NSA_KV_HEADS * HEAD_DIM == LANES

    def both_heads(w):
        z = jnp.zeros_like(w)
        return jnp.concatenate([jnp.concatenate([w, z], axis=2),
                                jnp.concatenate([z, w], axis=2)], axis=1).astype(bf16)

    w1 = lambda w: both_heads(w.reshape(CMP_BLOCK, HEAD_DIM, w.shape[1]))
    w2 = lambda w: both_heads(w[None])[0]
    args = (xk, xv, jnp.tile(pos, (1, NSA_KV_HEADS)), w1(w1k), w2(w2k), w1(w1v), w2(w2v), gk0)
    blk = pl.BlockSpec((1, seq, width), lambda b: (b, 0, 0))
    full = lambda a: pl.BlockSpec(a.shape, lambda b: (0,) * a.ndim)
    out = lambda w: pl.BlockSpec((1, NSA_KV_HEADS, nch, w), lambda b: (b, 0, 0, 0))
    shape = lambda w: jax.ShapeDtypeStruct((bsz, NSA_KV_HEADS, nch, w), bf16)
    return pl.pallas_call(
        _compress_kernel,
        grid=(bsz,),
        in_specs=[blk, blk] + [full(a) for a in args[2:]],
        out_specs=[out(HEAD_DIM), out(LANES)],
        out_shape=[shape(HEAD_DIM), shape(LANES)],
        compiler_params=_params(1),
        name="compress",
    )(*args)


def _flash_step(s, v, m_ref, acc_ref):
    m_prev = m_ref[...]
    m_new = jnp.maximum(m_prev, jnp.max(s, axis=1, keepdims=True))
    alpha = jnp.exp2(m_prev - m_new)
    p = jnp.exp2(s - jnp.concatenate([m_new] * (s.shape[1] // LANES), axis=1))
    acc_ref[...] = alpha * acc_ref[...] + _dot(p.astype(bf16), v)
    m_ref[...] = m_new


def _flash_init(m_ref, acc_ref):
    m_ref[...] = jnp.full(m_ref.shape, MASKED, f32)
    acc_ref[...] = jnp.zeros(acc_ref.shape, f32)


def _flash_out_t(acc_ref):
    acc_t = acc_ref[...].T
    return acc_t[:HEAD_DIM] / acc_t[HEAD_DIM:HEAD_DIM + 1]


def _nsa_kernel(n_top, q_ref, g_ref, kc_ref, vc_ref, ks_ref, vs_ref, kw_ref, vw_ref,
                bc_ref, bn_ref, ovt_ref, o_ref, qs_ref, qw_ref, oc_ref,
                m_ref, acc_ref, mw_ref, accw_ref):
    i = pl.program_id(1)
    kv_heads = range(NSA_KV_HEADS)
    rows = NSA_GROUP * TQ
    start = i * TQ
    assert WINDOW == 4 * TQ

    def attend(hk, q_rows_ref, k_ref, v_ref, off, width, bias, state):
        off = pl.multiple_of(off, TQ)
        lanes = slice(hk * LANES, (hk + 1) * LANES)
        s = _dot_nt(q_rows_ref[hk], k_ref[0, pl.ds(off, width), lanes])
        if bias is not None:
            s = s + bias
        _flash_step(s, v_ref[0, pl.ds(off, width), lanes], state[0].at[hk], state[1].at[hk])

    def when(cond, guarded):
        return pl.when(cond) if guarded else (lambda fn: fn())

    def near_steps(q_rows_ref, k_ref, v_ref, state, guarded):
        group = lambda hk: slice(hk * NSA_GROUP, (hk + 1) * NSA_GROUP)

        @when(i >= 1, guarded)
        def _():
            for hk in kv_heads:
                attend(hk, q_rows_ref, k_ref, v_ref, start - TQ, 2 * TQ,
                       bn_ref[group(hk)].reshape(rows, 2 * TQ), state)

        if guarded:
            @pl.when(i == 0)
            def _():
                for hk in kv_heads:
                    attend(hk, q_rows_ref, k_ref, v_ref, start, TQ,
                           bn_ref[group(hk), :, TQ:].reshape(rows, TQ), state)

    def select(hk):
        heads = range(hk * NSA_GROUP, (hk + 1) * NSA_GROUP)
        qpad = jnp.concatenate([q_ref[0, :, g * LANES:(g + 1) * LANES] for g in heads], axis=0)
        qw_ref[hk] = qpad

        bc = bc_ref[hk * NSA_GROUP:(hk + 1) * NSA_GROUP].reshape(rows, bc_ref.shape[2])
        s_c = _dot_nt(qpad[:, :HEAD_DIM], kc_ref[0, hk]) + bc
        visible = bc > 0.5 * MASKED
        m_c = jnp.max(s_c, axis=1, keepdims=True)
        e_c = jnp.where(visible, jnp.exp2(s_c - m_c), 0.0)
        p_c = e_c / jnp.maximum(jnp.sum(e_c, axis=1, keepdims=True), 1e-30)
        oc_ref[hk] = _dot(p_c.astype(bf16), vc_ref[0, hk])

        p_sum = p_c[0:TQ]
        for g in range(1, NSA_GROUP):
            p_sum = p_sum + p_c[g * TQ:(g + 1) * TQ]
        p_hi, p_lo = _split(p_sum)
        imp = _dot_nt(ovt_ref[...], p_hi) + _dot_nt(ovt_ref[...], p_lo)
        nblk = imp.shape[0]
        blk = lax.broadcasted_iota(jnp.int32, (nblk, TQ), 0)
        cur = (start + lax.broadcasted_iota(jnp.int32, (nblk, TQ), 1)) // SEL_BLOCK
        forced = (blk == 0) | (blk == cur) | (blk == cur - 1)
        imp = jnp.where(blk > cur, NEG_BLOCK, imp + jnp.where(forced, FORCED_BONUS, 0.0))
        sub = 8
        groups = [imp[lo:lo + sub] for lo in range(0, nblk, sub)]
        ranks = [jnp.zeros((sub, TQ), f32) for _ in groups]
        row = lax.broadcasted_iota(jnp.int32, (sub, TQ), 0)
        for b2 in range(nblk):
            other = imp[b2:b2 + 1, :]
            for gi, grp in enumerate(groups):
                lo = gi * sub
                if lo > b2:
                    ranks[gi] = jnp.where(other >= grp, ranks[gi] + 1.0, ranks[gi])
                elif lo + sub - 1 < b2:
                    ranks[gi] = jnp.where(other > grp, ranks[gi] + 1.0, ranks[gi])
                else:
                    ranks[gi] = ranks[gi] + jnp.where(row + lo > b2,
                                                      jnp.where(other >= grp, 1.0, 0.0),
                                                      jnp.where(other > grp, 1.0, 0.0))
        rank = jnp.concatenate(ranks, axis=0)
        usable = (rank < n_top) & (blk <= cur)
        sel_t = jnp.where(usable, 0.0, UNSELECTED)
        sel_pad = jnp.concatenate([jnp.zeros((LANES - nblk, TQ), f32), sel_t], axis=0).T
        sel_rows = jnp.concatenate([sel_pad.astype(bf16)] * NSA_GROUP, axis=0)
        qs_ref[hk] = qpad + sel_rows

    def head(guarded):
        for hk in kv_heads:
            select(hk)

        win = (mw_ref, accw_ref)
        _flash_init(*win)

        @when(i >= 4, guarded)
        def _():
            r = lax.broadcasted_iota(jnp.int32, (rows, TQ), 0) & (TQ - 1)
            c = lax.broadcasted_iota(jnp.int32, (rows, TQ), 1)
            for hk in kv_heads:
                attend(hk, qw_ref, kw_ref, vw_ref, start - 4 * TQ, TQ,
                       jnp.where(c > r, 0.0, MASKED), win)

        @when(i >= 3, guarded)
        def _():
            for hk in kv_heads:
                attend(hk, qw_ref, kw_ref, vw_ref, start - 3 * TQ, 2 * TQ, None, win)

        if guarded:
            @pl.when(i == 2)
            def _():
                for hk in kv_heads:
                    attend(hk, qw_ref, kw_ref, vw_ref, start - 2 * TQ, TQ, None, win)

        near_steps(qw_ref, kw_ref, vw_ref, win, guarded)
        _flash_init(m_ref, acc_ref)

    def tail(guarded):
        near_steps(qs_ref, ks_ref, vs_ref, (m_ref, acc_ref), guarded)
        outs = []
        for hk in kv_heads:
            o_c = oc_ref[hk].T[:HEAD_DIM]
            o_s = _flash_out_t(acc_ref.at[hk])
            o_w = _flash_out_t(accw_ref.at[hk])
            gates = g_ref[0, hk * LANES:(hk + 1) * LANES, :]
            for g in range(NSA_GROUP):
                sl = slice(g * TQ, (g + 1) * TQ)
                outs.append(gates[3 * g:3 * g + 1] * o_c[:, sl]
                            + gates[3 * g + 1:3 * g + 2] * o_s[:, sl]
                            + gates[3 * g + 2:3 * g + 3] * o_w[:, sl])
        o_ref[0] = jnp.concatenate(outs, axis=0).T.astype(bf16)

    interior = i >= 4
    pl.when(interior)(lambda: head(False))
    pl.when(jnp.logical_not(interior))(lambda: head(True))

    sel = (m_ref, acc_ref)
    n_far = jnp.maximum(i - 1, 0)

    def far_steps(off, n_steps):
        for step in range(n_steps):
            for hk in kv_heads:
                attend(hk, qs_ref, ks_ref, vs_ref, off + step * (2 * TQ), 2 * TQ, None, sel)

    def far_trip(c, carry):
        far_steps(c * (8 * TQ), 4)
        return carry

    lax.fori_loop(0, n_far // 8, far_trip, 0)
    rem_off = (n_far // 8) * (8 * TQ)

    @pl.when((n_far & 4) != 0)
    def _():
        far_steps(rem_off, 2)

    @pl.when((n_far & 2) != 0)
    def _():
        far_steps(rem_off + (n_far & 4) * TQ, 1)

    @pl.when((n_far & 1) != 0)
    def _():
        for hk in kv_heads:
            attend(hk, qs_ref, ks_ref, vs_ref, rem_off + (n_far & 6) * TQ, TQ, None, sel)

    pl.when(interior)(lambda: tail(False))
    pl.when(jnp.logical_not(interior))(lambda: tail(True))


def _nsa(qa, gates, kcmp, vcmp, ks, vs, kw, vw, cmp_bias, near_bias, ovt):
    bsz, seq, _ = qa.shape
    nc = kcmp.shape[2]
    n_top = min(SEL_TOPK, seq // SEL_BLOCK)
    rows = NSA_GROUP * TQ
    tok = lambda w: pl.BlockSpec((1, TQ, w), lambda b, i: (b, i, 0))
    kv = pl.BlockSpec((1, seq, NSA_KV_HEADS * LANES), lambda b, i: (b, 0, 0))
    cmp = lambda a: pl.BlockSpec((1,) + a.shape[1:], lambda b, i: (b, 0, 0, 0))
    per_kv = lambda width, dt: pltpu.VMEM((NSA_KV_HEADS, rows, width), dt)
    return pl.pallas_call(
        functools.partial(_nsa_kernel, n_top),
        grid=(bsz, seq // TQ),
        in_specs=[tok(NSA_HEADS * LANES),
                  pl.BlockSpec((1, NSA_KV_HEADS * LANES, TQ), lambda b, i: (b, 0, i)),
                  cmp(kcmp), cmp(vcmp), kv, kv, kv, kv,
                  pl.BlockSpec((NSA_HEADS, TQ, nc), lambda b, i: (0, i, 0)),
                  pl.BlockSpec((NSA_HEADS, TQ, 2 * TQ), lambda b, i: (0, 0, 0)),
                  pl.BlockSpec(ovt.shape, lambda b, i: (0, 0))],
        out_specs=tok(NSA_HEADS * HEAD_DIM),
        out_shape=jax.ShapeDtypeStruct((bsz, seq, NSA_HEADS * HEAD_DIM), bf16),
        scratch_shapes=[per_kv(LANES, bf16), per_kv(LANES, bf16),
                        per_kv(LANES, f32), per_kv(LANES, f32), per_kv(LANES, f32),
                        per_kv(LANES, f32), per_kv(LANES, f32)],
        compiler_params=_params(2),
        name="nsa",
    )(qa, gates, kcmp, vcmp, ks, vs, kw, vw, cmp_bias, near_bias, ovt)


def _sb_kernel(q_ref, k_ref, v_ref, tri_ref, o_ref, carry_ref, acc_ref):
    i = pl.program_id(2)
    n_heads = carry_ref.shape[0]
    lane = lax.broadcasted_iota(jnp.int32, (TS, LANES), 1)
    q_heads = []
    for pair in range(n_heads // 2):
        q = q_ref[0, :, pair * LANES:(pair + 1) * LANES]
        zero = jnp.zeros_like(q)
        q_heads += [jnp.where(lane < HEAD_DIM, q, zero), jnp.where(lane >= HEAD_DIM, q, zero)]
    pair_lanes = lambda hh: slice((hh // 2) * LANES, (hh // 2 + 1) * LANES)
    carry_ref[...] = jnp.zeros(carry_ref.shape, f32)
    acc_ref[...] = jnp.zeros(acc_ref.shape, f32)
    r = lax.broadcasted_iota(jnp.int32, (TS, TS), 0)
    c = lax.broadcasted_iota(jnp.int32, (TS, TS), 1)
    before = c < r

    def chunks(jobs):
        offs = [pl.multiple_of(off, TS) for off, _ in jobs]
        stage = []
        for (_, diagonal), off in zip(jobs, offs):
            for hh in range(n_heads):
                z = _dot_nt(q_heads[hh], k_ref[0, pl.ds(off, TS), pair_lanes(hh)])
                soft = jnp.log2(1.0 + jnp.exp2(-jnp.abs(z)))
                log_keep = jnp.minimum(-z, 0.0) - soft
                log_sig = jnp.minimum(z, 0.0) - soft
                if diagonal:
                    log_keep = jnp.where(before, log_keep, 0.0)
                hi, lo = _split(log_keep)
                later = _dot(hi, tri_ref[...]) + _dot(lo, tri_ref[...])
                stage.append((log_sig + later, jnp.sum(log_keep, axis=1, keepdims=True)))
        for hh in range(n_heads):
            carry = carry_ref[hh]
            acc = acc_ref[hh]
            for j, ((_, diagonal), off) in enumerate(zip(jobs, offs)):
                base, total = stage[n_heads * j + hh]
                a = jnp.exp2(base + jnp.concatenate([carry] * (TS // LANES), axis=1))
                if diagonal:
                    a = jnp.where(before, a, 0.0)
                acc = acc + _dot(a.astype(bf16), v_ref[0, pl.ds(off, TS), pair_lanes(hh)])
                carry = carry + total
            carry_ref[hh] = carry
            acc_ref[hh] = acc

    @pl.when(i == 0)
    def _():
        chunks([(0, True)])

    @pl.when(i >= 1)
    def _():
        chunks([(i * TS, True), ((i - 1) * TS, False)])

    def any_live():
        return jnp.max(carry_ref[...]) > EXP2_UNDERFLOW

    def more(state):
        n, live = state
        return jnp.logical_and(n < i, live)

    def older(state):
        n, _ = state
        chunks([((i - 1 - n) * TS, False)])
        return n + 1, any_live()

    lax.while_loop(more, older, (jnp.int32(1), any_live()))
    o_ref[0] = jnp.concatenate(
        [jnp.where(lane < HEAD_DIM, acc_ref[2 * pair], acc_ref[2 * pair + 1])
         for pair in range(n_heads // 2)], axis=1).astype(bf16)


SB_STEP_HEADS = 8


def _sb(qb, kb, vb, tri):
    bsz, seq, width = qb.shape
    step_w = SB_STEP_HEADS * HEAD_DIM
    q_spec = pl.BlockSpec((1, TS, step_w), lambda b, p, i: (b, i, p))
    kv_spec = pl.BlockSpec((1, seq, step_w), lambda b, p, i: (b, 0, p))
    state = pltpu.VMEM((SB_STEP_HEADS, TS, LANES), f32)
    return pl.pallas_call(
        _sb_kernel,
        grid=(bsz, width // step_w, seq // TS),
        in_specs=[q_spec, kv_spec, kv_spec, pl.BlockSpec(tri.shape, lambda b, p, i: (0, 0))],
        out_specs=q_spec,
        out_shape=jax.ShapeDtypeStruct((bsz, seq, width), bf16),
        scratch_shapes=[state, state],
        compiler_params=_params(3),
        name="sb",
    )(qb, kb, vb, tri)


def _post_kernel(ya_ref, yb_ref, ma_ref, mb_ref, x_ref, gate1_ref, g2_ref, sc_ref, sh_ref,
                 gate_ref, wa_ref, wb_ref, wo_ref, w1_ref, w2_ref, o_ref):
    y_a = _dot(ya_ref[0], wa_ref[...])
    y_b = _dot(yb_ref[0], wb_ref[...])
    mixed = ma_ref[0].astype(f32) * y_a + mb_ref[0].astype(f32) * y_b
    hres = x_ref[0] + gate1_ref[0] * _dot(mixed.astype(bf16), wo_ref[...])
    d = hres.shape[1]
    ms = jnp.mean(hres * hres, axis=-1, keepdims=True)
    u = (hres * lax.rsqrt(ms + EPS) * g2_ref[...]) * (1.0 + sc_ref[0]) + sh_ref[0]
    ub = u.astype(bf16)
    ff = jnp.zeros(hres.shape, f32)
    for c in range(w1_ref.shape[1] // d):
        hid = jnp.maximum(_dot(ub, w1_ref[:, c * d:(c + 1) * d]), 0.0)
        ff = ff + _dot((hid * hid).astype(bf16), w2_ref[c * d:(c + 1) * d, :])
    o_ref[0] = hres + gate_ref[0] * ff


def _post(ya, yb, ma, mb, x, gate1, g2, scale2, shift2, gate2, wa, wb, wo, w1, w2):
    bsz, seq, d = x.shape
    tok = lambda w: pl.BlockSpec((1, TM, w), lambda b, i: (b, i, 0))
    mod = pl.BlockSpec((1, 1, d), lambda b, i: (b, 0, 0))
    const = lambda a: pl.BlockSpec(a.shape, lambda b, i: (0,) * a.ndim,
                                   pipeline_mode=pl.Buffered(1))
    return pl.pallas_call(
        _post_kernel,
        grid=(bsz, seq // TM),
        in_specs=[tok(ya.shape[2]), tok(yb.shape[2]), tok(d), tok(d), tok(d), mod,
                  const(g2), mod, mod, mod, const(wa), const(wb), const(wo), const(w1), const(w2)],
        out_specs=tok(d),
        out_shape=jax.ShapeDtypeStruct((bsz, seq, d), f32),
        compiler_params=_params(2),
        name="post",
    )(ya, yb, ma, mb, x, gate1, g2, scale2, shift2, gate2, wa, wb, wo, w1, w2)


def _overlap_t(nc_pad, nsel_pad, nc, nsel):
    c_start = np.arange(nc_pad) * CMP_STRIDE
    s_start = np.arange(nsel_pad) * SEL_BLOCK
    ov = (np.minimum(c_start[None, :] + CMP_BLOCK, s_start[:, None] + SEL_BLOCK)
          - np.maximum(c_start[None, :], s_start[:, None]))
    ov = np.clip(ov, 0, CMP_BLOCK).astype(np.float32) / CMP_BLOCK
    ov[nsel:, :] = 0.0
    ov[:, nc:] = 0.0
    return ov


def _layer(h, mod, rel_tiles, p):
    bsz, seq, d = h.shape
    shift1, scale1, gate1, shift2, scale2, gate2 = [
        mod[:, k * d:(k + 1) * d].reshape(bsz, 1, d) for k in range(6)]
    cmp_bias, near_bias = rel_tiles

    pad_gain = lambda g, n: jnp.tile(jnp.pad(g, (0, LANES - HEAD_DIM)), n).reshape(1, n * LANES)
    gq = pad_gain(p["q_norm_g"], NSA_HEADS)
    gk = jnp.concatenate([pad_gain(p["k_norm_g"][k], NSA_KV_HEADS) for k in range(3)], axis=0)

    (qa, kc, vc, ks, vs, kw, vw, gates, qb, kb, vb, ma, mb) = _inproj(
        h, p["norm1_g"].reshape(1, d), scale1, shift1, _pack_w_in(p["w_in"], d), gq, gk)

    nch = seq // CMP_STRIDE
    kcmp, vcmp = _compress(kc, vc, p["cmp_pos"], p["cmp_k_w1"], p["cmp_k_w2"],
                           p["cmp_v_w1"], p["cmp_v_w2"], p["k_norm_g"][0].reshape(1, HEAD_DIM))

    nc = (seq - CMP_BLOCK) // CMP_STRIDE + 1
    ovt = jnp.asarray(_overlap_t(nch, HEAD_DIM, nc, seq // SEL_BLOCK), bf16)
    y_nsa = _nsa(qa, gates, kcmp, vcmp, ks, vs, kw, vw, cmp_bias, near_bias, ovt)

    y_sb = _sb(qb, kb, vb, jnp.asarray(np.tril(np.ones((TS, TS)), -1), bf16))

    return _post(y_nsa, y_sb, ma, mb, h, gate1, p["norm2_g"].reshape(1, d), scale2, shift2, gate2,
                 p["w_up_nsa"].astype(bf16), p["w_up_sb"].astype(bf16), p["w_out"].astype(bf16),
                 p["mlp_w1"].astype(bf16), p["mlp_w2"].astype(bf16))


def kernel(x, c, rel_bias, ada_w, ada_b, norm1_g, norm2_g, w_in, cmp_pos, cmp_k_w1, cmp_k_w2,
           cmp_v_w1, cmp_v_w2, q_norm_g, k_norm_g, w_up_nsa, w_up_sb, w_out, mlp_w1, mlp_w2):
    bsz, seq, d = x.shape
    assert seq % TM == 0 and seq // SEL_BLOCK <= HEAD_DIM and seq >= WINDOW + TQ
    assert CMP_BLOCK == 2 * CMP_STRIDE and TQ == 2 * SEL_BLOCK
    tbl = rel_bias.astype(f32)
    rel_tiles = _bias_tiles(tbl, seq, seq // CMP_STRIDE)
    stacked = dict(norm1_g=norm1_g, norm2_g=norm2_g, w_in=w_in, cmp_pos=cmp_pos,
                   cmp_k_w1=cmp_k_w1, cmp_k_w2=cmp_k_w2, cmp_v_w1=cmp_v_w1, cmp_v_w2=cmp_v_w2,
                   q_norm_g=q_norm_g, k_norm_g=k_norm_g, w_up_nsa=w_up_nsa, w_up_sb=w_up_sb,
                   w_out=w_out, mlp_w1=mlp_w1, mlp_w2=mlp_w2)
    h = x
    for layer in range(ada_w.shape[0]):
        mod = _adaln(c, ada_w[layer], ada_b[layer])
        h = _layer(h, mod, rel_tiles, {k: v[layer] for k, v in stacked.items()})
    return h
```

```python
import functools
import math

import numpy as np
import jax
import jax.numpy as jnp
from jax import lax
from jax.experimental import pallas as pl
from jax.experimental.pallas import tpu as pltpu

f32 = jnp.float32
bf16 = jnp.bfloat16

HEAD_DIM = 64
NSA_HEADS = 8
NSA_KV_HEADS = 2
NSA_GROUP = NSA_HEADS // NSA_KV_HEADS
SB_HEADS = 8
CMP_BLOCK = 32
CMP_STRIDE = 16
SEL_BLOCK = 64
SEL_TOPK = 16
WINDOW = 512
N_BUCKETS = 32
MAX_DISTANCE = 128
EPS = 1e-6
FORCED_BONUS = 1e4
NEG_BLOCK = -1e9

LANES = 128
MASKED = -1e30
UNSELECTED = -1e9
LOG2E = math.log2(math.e)
EXP2_UNDERFLOW = -150.0
VMEM_LIMIT = 56 * 1024 * 1024

TQ = 128
TS = 256
TM = 512


def _bucket_thresholds():
    n = np.arange(0, 4 * MAX_DISTANCE)
    max_exact = N_BUCKETS // 2
    nf = np.maximum(n, 1).astype(np.float32)
    large = max_exact + (np.log(nf / max_exact) / math.log(MAX_DISTANCE / max_exact)
                         * (N_BUCKETS - max_exact)).astype(np.int32)
    large = np.minimum(large, N_BUCKETS - 1)
    b = np.where(n < max_exact, n, large)
    assert np.all(np.diff(b) >= 0) and b[-1] == N_BUCKETS - 1
    return [int(np.argmax(b >= k)) for k in range(N_BUCKETS)]


BUCKET_START = _bucket_thresholds()
assert BUCKET_START[-1] <= LANES


def _dot(a, b):
    return jnp.dot(a, b, preferred_element_type=f32)


def _dot_nt(a, b):
    return lax.dot_general(a, b, (((1,), (1,)), ((), ())), preferred_element_type=f32)


def _split(a):
    hi = a.astype(bf16)
    lo = (a - hi.astype(f32)).astype(bf16)
    return hi, lo


def _params(n_grid):
    return pltpu.CompilerParams(dimension_semantics=("arbitrary",) * n_grid,
                                vmem_limit_bytes=VMEM_LIMIT)


def _adaln_kernel(c_ref, w_ref, b_ref, o_ref):
    c = c_ref[...]
    a = c * jax.nn.sigmoid(c)
    ah, al = _split(a)
    wh, wl = _split(w_ref[...])
    o_ref[...] = _dot(ah, wh) + _dot(ah, wl) + _dot(al, wh) + b_ref[...]


def _adaln(c, w, b):
    bsz, d = c.shape
    n = w.shape[1]
    return pl.pallas_call(
        _adaln_kernel,
        grid=(n // d,),
        in_specs=[pl.BlockSpec((bsz, d), lambda j: (0, 0)),
                  pl.BlockSpec((d, d), lambda j: (0, j)),
                  pl.BlockSpec((1, d), lambda j: (0, j))],
        out_specs=pl.BlockSpec((bsz, d), lambda j: (0, j)),
        out_shape=jax.ShapeDtypeStruct((bsz, n), f32),
        compiler_params=_params(1),
        name="adaln",
    )(c, w, b.reshape(1, n))


def _bias_of_dist(dist, tbl_ref, h):
    out = jnp.full(dist.shape, tbl_ref[0, h], f32)
    for k in range(1, N_BUCKETS):
        out = jnp.where(dist >= BUCKET_START[k], tbl_ref[k, h], out)
    return jnp.where(dist >= 0, out, MASKED)


def _cmp_bias_kernel(tbl_ref, o_ref):
    h = pl.program_id(0)
    i = pl.program_id(1)
    rows, nc = o_ref.shape[1], o_ref.shape[2]
    t = i * rows + lax.broadcasted_iota(jnp.int32, (rows, nc), 0)
    j = lax.broadcasted_iota(jnp.int32, (rows, nc), 1)
    o_ref[0] = _bias_of_dist(t - (j * CMP_STRIDE + CMP_BLOCK - 1), tbl_ref, h) * LOG2E


def _near_bias_kernel(tbl_ref, o_ref):
    h = pl.program_id(0)
    r = lax.broadcasted_iota(jnp.int32, (TQ, 2 * TQ), 0)
    c = lax.broadcasted_iota(jnp.int32, (TQ, 2 * TQ), 1)
    o_ref[0] = (_bias_of_dist(r - c + TQ, tbl_ref, h) - tbl_ref[N_BUCKETS - 1, h]) * LOG2E


def _bias_tiles(rel_bias, seq, nc_pad):
    tbl = rel_bias.astype(f32)
    smem = pl.BlockSpec(memory_space=pltpu.SMEM)
    rows = 512
    cmp_bias = pl.pallas_call(
        _cmp_bias_kernel,
        grid=(NSA_HEADS, seq // rows),
        in_specs=[smem],
        out_specs=pl.BlockSpec((1, rows, nc_pad), lambda h, i: (h, i, 0)),
        out_shape=jax.ShapeDtypeStruct((NSA_HEADS, seq, nc_pad), f32),
        compiler_params=_params(2),
        name="cmp_bias",
    )(tbl)
    near_bias = pl.pallas_call(
        _near_bias_kernel,
        grid=(NSA_HEADS,),
        in_specs=[smem],
        out_specs=pl.BlockSpec((1, TQ, 2 * TQ), lambda h: (h, 0, 0)),
        out_shape=jax.ShapeDtypeStruct((NSA_HEADS, TQ, 2 * TQ), f32),
        compiler_params=_params(1),
        name="near_bias",
    )(tbl)
    return cmp_bias, near_bias


_QA_W = NSA_HEADS * LANES
_KV_W = NSA_KV_HEADS * LANES
_CMP_W = NSA_KV_HEADS * HEAD_DIM
_SB_W = SB_HEADS * HEAD_DIM
_GATE_ROWS = NSA_GROUP * 3


def _layout(d_model):
    names = ["qa", "kc", "vc", "ksl", "vsl", "kwn", "vwn", "g", "qb", "kb", "vb", "ma", "mb"]
    widths = [NSA_HEADS * HEAD_DIM] + [_CMP_W] * 6 + [LANES] + [_SB_W] * 3 + [d_model, d_model]
    offs = np.concatenate([[0], np.cumsum(widths)])
    return {n: (int(offs[i]), int(offs[i + 1])) for i, n in enumerate(names)}, int(offs[-1])


def _pack_w_in(w_in, d_model):
    q_w = NSA_HEADS * HEAD_DIM
    kv_w = NSA_KV_HEADS * HEAD_DIM
    g_w = NSA_HEADS * 3
    sizes = [q_w] + [kv_w] * 6 + [g_w] + [_SB_W] * 3 + [d_model, d_model]
    offs = np.concatenate([[0], np.cumsum(sizes)])
    parts = [w_in[:, int(offs[i]):int(offs[i + 1])] for i in range(len(sizes))]
    parts[7] = jnp.pad(parts[7], ((0, 0), (0, LANES - g_w)))
    return jnp.concatenate(parts, axis=1).astype(bf16)


def _inproj_kernel(lay, x_ref, g1_ref, sc_ref, sh_ref, w_ref, gq_ref, gk_ref,
                   qa_ref, kc_ref, vc_ref, ks_ref, vs_ref, kw_ref, vw_ref, g_ref,
                   qb_ref, kb_ref, vb_ref, ma_ref, mb_ref):
    i = pl.program_id(1)
    x = x_ref[0]
    ms = jnp.mean(x * x, axis=-1, keepdims=True)
    u = (x * lax.rsqrt(ms + EPS) * g1_ref[...]) * (1.0 + sc_ref[0]) + sh_ref[0]
    ub = u.astype(bf16)

    narrow = ("kc", "vc", "ksl", "vsl", "kwn", "vwn", "g")
    narrow_lo = lay[narrow[0]][0]
    z_narrow = _dot(ub, w_ref[:, narrow_lo:lay[narrow[-1]][1]])

    def proj(name):
        lo, hi = lay[name]
        if name in narrow:
            return z_narrow[:, lo - narrow_lo:hi - narrow_lo]
        return _dot(ub, w_ref[:, lo:hi])

    rows = x.shape[0]
    lane = lax.broadcasted_iota(jnp.int32, (rows, LANES), 1)
    low = lane < HEAD_DIM

    def pair_norm(z, gain):
        sq = z * z
        ss_a = jnp.sum(jnp.where(low, sq, 0.0), axis=1, keepdims=True)
        ss_b = jnp.sum(jnp.where(low, 0.0, sq), axis=1, keepdims=True)
        inv = lax.rsqrt(jnp.where(low, ss_a, ss_b) * (1.0 / HEAD_DIM) + EPS)
        return z * inv * gain

    def spread(z, extra):
        return jnp.concatenate([jnp.where(low, z, extra),
                                jnp.where(low, pltpu.roll(z, HEAD_DIM, 1), extra)], axis=1)

    scale = HEAD_DIM ** -0.5 * LOG2E
    zq = proj("qa")
    qa_ref[0] = jnp.concatenate(
        [spread(pair_norm(zq[:, j * LANES:(j + 1) * LANES], gq_ref[...]) * scale, 0.0)
         for j in range(NSA_HEADS // 2)], axis=1).astype(bf16)
    kc_ref[0] = proj("kc")
    vc_ref[0] = proj("vc")

    tok_blk = (i * rows + lax.broadcasted_iota(jnp.int32, (rows, LANES), 0)) // SEL_BLOCK
    onehot = jnp.where(lane - HEAD_DIM == tok_blk, 1.0, 0.0)
    ones_col = jnp.where(lane == HEAD_DIM, 1.0, 0.0)

    ks_ref[0] = spread(pair_norm(proj("ksl"), gk_ref[1:2, :]), onehot).astype(bf16)
    vs_ref[0] = spread(proj("vsl"), ones_col).astype(bf16)
    kw_ref[0] = spread(pair_norm(proj("kwn"), gk_ref[2:3, :]), 0.0).astype(bf16)
    vw_ref[0] = spread(proj("vwn"), ones_col).astype(bf16)
    g_ref[0] = jax.nn.sigmoid(proj("g")).T
    qb_ref[0] = (proj("qb") * scale).astype(bf16)
    kb_ref[0] = proj("kb").astype(bf16)
    vb_ref[0] = proj("vb").astype(bf16)
    ma_ref[0] = jax.nn.sigmoid(proj("ma")).astype(bf16)
    mb_ref[0] = jax.nn.sigmoid(proj("mb")).astype(bf16)


def _inproj(x, g1, scale1, shift1, w_packed, gq, gk):
    bsz, seq, d = x.shape
    lay, width = _layout(d)
    assert w_packed.shape == (d, width)
    tok = lambda w: pl.BlockSpec((1, TM, w), lambda b, i: (b, i, 0))
    full = lambda a: pl.BlockSpec(a.shape, lambda b, i: (0,) * a.ndim,
                                  pipeline_mode=pl.Buffered(1))
    mod = pl.BlockSpec((1, 1, d), lambda b, i: (b, 0, 0))
    out_w = [(_QA_W, bf16), (_CMP_W, f32), (_CMP_W, f32), (_KV_W, bf16), (_KV_W, bf16),
             (_KV_W, bf16), (_KV_W, bf16), (LANES, f32), (_SB_W, bf16), (_SB_W, bf16),
             (_SB_W, bf16), (d, bf16), (d, bf16)]
    gates_at = 7
    out_specs = [tok(w) for w, _ in out_w]
    out_shape = [jax.ShapeDtypeStruct((bsz, seq, w), dt) for w, dt in out_w]
    out_specs[gates_at] = pl.BlockSpec((1, LANES, TM), lambda b, i: (b, 0, i))
    out_shape[gates_at] = jax.ShapeDtypeStruct((bsz, LANES, seq), f32)
    return pl.pallas_call(
        functools.partial(_inproj_kernel, lay),
        grid=(bsz, seq // TM),
        in_specs=[tok(d), full(g1), mod, mod, full(w_packed), full(gq), full(gk)],
        out_specs=out_specs,
        out_shape=out_shape,
        compiler_params=_params(2),
        name="inproj",
    )(x, g1, scale1, shift1, w_packed, gq, gk)


def _compress_kernel(xk_ref, xv_ref, pos_ref, w1k_ref, w2k_ref, w1v_ref, w2v_ref, gk_ref,
                     ko_ref, vo_ref):
    nch = xk_ref.shape[1] // CMP_STRIDE

    def mlp(x_ref, w1_ref, w2_ref):
        first = jnp.zeros((nch, LANES), f32)
        second = jnp.zeros((nch, LANES), f32)
        for l in range(CMP_STRIDE):
            xl = x_ref[0, pl.ds(l, nch, stride=CMP_STRIDE), :]
            lo = l + CMP_STRIDE
            first = first + _dot((xl + pos_ref[l:l + 1, :]).astype(bf16), w1_ref[l])
            second = second + _dot((xl + pos_ref[lo:lo + 1, :]).astype(bf16), w1_ref[lo])
        pre = first + pltpu.roll(second, nch - 1, 0)
        hid = pre * jax.nn.sigmoid(pre)
        return _dot(hid.astype(bf16), w2_ref[...])

    k = mlp(xk_ref, w1k_ref, w2k_ref)
    v = mlp(xv_ref, w1v_ref, w2v_ref)
    for h in range(NSA_KV_HEADS):
        kh = k[:, h * HEAD_DIM:(h + 1) * HEAD_DIM]
        ms = jnp.mean(kh * kh, axis=-1, keepdims=True)
        ko_ref[0, h] = (kh * lax.rsqrt(ms + EPS) * gk_ref[...]).astype(bf16)
        vh = v[:, h * HEAD_DIM:(h + 1) * HEAD_DIM]
        vo_ref[0, h] = jnp.concatenate([vh, jnp.zeros_like(vh)], axis=1).astype(bf16)


def _compress(xk, xv, pos, w1k, w2k, w1v, w2v, gk0):
    bsz, seq, width = xk.shape
    nch = seq // CMP_STRIDE
    assert width == NSA_KV_HEADS * HEAD_DIM == LANES

    def both_heads(w):
        z = jnp.zeros_like(w)
        return jnp.concatenate([jnp.concatenate([w, z], axis=2),
                                jnp.concatenate([z, w], axis=2)], axis=1).astype(bf16)

    w1 = lambda w: both_heads(w.reshape(CMP_BLOCK, HEAD_DIM, w.shape[1]))
    w2 = lambda w: both_heads(w[None])[0]
    args = (xk, xv, jnp.tile(pos, (1, NSA_KV_HEADS)), w1(w1k), w2(w2k), w1(w1v), w2(w2v), gk0)
    blk = pl.BlockSpec((1, seq, width), lambda b: (b, 0, 0))
    full = lambda a: pl.BlockSpec(a.shape, lambda b: (0,) * a.ndim)
    out = lambda w: pl.BlockSpec((1, NSA_KV_HEADS, nch, w), lambda b: (b, 0, 0, 0))
    shape = lambda w: jax.ShapeDtypeStruct((bsz, NSA_KV_HEADS, nch, w), bf16)
    return pl.pallas_call(
        _compress_kernel,
        grid=(bsz,),
        in_specs=[blk, blk] + [full(a) for a in args[2:]],
        out_specs=[out(HEAD_DIM), out(LANES)],
        out_shape=[shape(HEAD_DIM), shape(LANES)],
        compiler_params=_params(1),
        name="compress",
    )(*args)


def _flash_step(s, v, m_ref, acc_ref):
    m_prev = m_ref[...]
    m_new = jnp.maximum(m_prev, jnp.max(s, axis=1, keepdims=True))
    alpha = jnp.exp2(m_prev - m_new)
    p = jnp.exp2(s - jnp.concatenate([m_new] * (s.shape[1] // LANES), axis=1))
    acc_ref[...] = alpha * acc_ref[...] + _dot(p.astype(bf16), v)
    m_ref[...] = m_new


def _flash_init(m_ref, acc_ref):
    m_ref[...] = jnp.full(m_ref.shape, MASKED, f32)
    acc_ref[...] = jnp.zeros(acc_ref.shape, f32)


def _flash_out_t(acc_ref):
    acc_t = acc_ref[...].T
    return acc_t[:HEAD_DIM] / acc_t[HEAD_DIM:HEAD_DIM + 1]


def _nsa_kernel(n_top, q_ref, g_ref, kc_ref, vc_ref, ks_ref, vs_ref, kw_ref, vw_ref,
                bc_ref, bn_ref, ovt_ref, o_ref, qs_ref, qw_ref, oc_ref,
                m_ref, acc_ref, mw_ref, accw_ref):
    i = pl.program_id(1)
    kv_heads = range(NSA_KV_HEADS)
    rows = NSA_GROUP * TQ
    start = i * TQ
    assert WINDOW == 4 * TQ

    def attend(hk, q_rows_ref, k_ref, v_ref, off, width, bias, state):
        off = pl.multiple_of(off, TQ)
        lanes = slice(hk * LANES, (hk + 1) * LANES)
        s = _dot_nt(q_rows_ref[hk], k_ref[0, pl.ds(off, width), lanes])
        if bias is not None:
            s = s + bias
        _flash_step(s, v_ref[0, pl.ds(off, width), lanes], state[0].at[hk], state[1].at[hk])

    def when(cond, guarded):
        return pl.when(cond) if guarded else (lambda fn: fn())

    def near_steps(q_rows_ref, k_ref, v_ref, state, guarded):
        group = lambda hk: slice(hk * NSA_GROUP, (hk + 1) * NSA_GROUP)

        @when(i >= 1, guarded)
        def _():
            for hk in kv_heads:
                attend(hk, q_rows_ref, k_ref, v_ref, start - TQ, 2 * TQ,
                       bn_ref[group(hk)].reshape(rows, 2 * TQ), state)

        if guarded:
            @pl.when(i == 0)
            def _():
                for hk in kv_heads:
                    attend(hk, q_rows_ref, k_ref, v_ref, start, TQ,
                           bn_ref[group(hk), :, TQ:].reshape(rows, TQ), state)

    def select(hk):
        heads = range(hk * NSA_GROUP, (hk + 1) * NSA_GROUP)
        qpad = jnp.concatenate([q_ref[0, :, g * LANES:(g + 1) * LANES] for g in heads], axis=0)
        qw_ref[hk] = qpad

        bc = bc_ref[hk * NSA_GROUP:(hk + 1) * NSA_GROUP].reshape(rows, bc_ref.shape[2])
        s_c = _dot_nt(qpad[:, :HEAD_DIM], kc_ref[0, hk]) + bc
        visible = bc > 0.5 * MASKED
        m_c = jnp.max(s_c, axis=1, keepdims=True)
        e_c = jnp.where(visible, jnp.exp2(s_c - m_c), 0.0)
        p_c = e_c / jnp.maximum(jnp.sum(e_c, axis=1, keepdims=True), 1e-30)
        oc_ref[hk] = _dot(p_c.astype(bf16), vc_ref[0, hk])

        p_sum = p_c[0:TQ]
        for g in range(1, NSA_GROUP):
            p_sum = p_sum + p_c[g * TQ:(g + 1) * TQ]
        p_hi, p_lo = _split(p_sum)
        imp = _dot_nt(ovt_ref[...], p_hi) + _dot_nt(ovt_ref[...], p_lo)
        nblk = imp.shape[0]
        blk = lax.broadcasted_iota(jnp.int32, (nblk, TQ), 0)
        cur = (start + lax.broadcasted_iota(jnp.int32, (nblk, TQ), 1)) // SEL_BLOCK
        forced = (blk == 0) | (blk == cur) | (blk == cur - 1)
        imp = jnp.where(blk > cur, NEG_BLOCK, imp + jnp.where(forced, FORCED_BONUS, 0.0))
        sub = 8
        groups = [imp[lo:lo + sub] for lo in range(0, nblk, sub)]
        ranks = [jnp.zeros((sub, TQ), f32) for _ in groups]
        row = lax.broadcasted_iota(jnp.int32, (sub, TQ), 0)
        for b2 in range(nblk):
            other = imp[b2:b2 + 1, :]
            for gi, grp in enumerate(groups):
                lo = gi * sub
                if lo > b2:
                    ranks[gi] = jnp.where(other >= grp, ranks[gi] + 1.0, ranks[gi])
                elif lo + sub - 1 < b2:
                    ranks[gi] = jnp.where(other > grp, ranks[gi] + 1.0, ranks[gi])
                else:
                    ranks[gi] = ranks[gi] + jnp.where(row + lo > b2,
                                                      jnp.where(other >= grp, 1.0, 0.0),
                                                      jnp.where(other > grp, 1.0, 0.0))
        rank = jnp.concatenate(ranks, axis=0)
        usable = (rank < n_top) & (blk <= cur)
        sel_t = jnp.where(usable, 0.0, UNSELECTED)
        sel_pad = jnp.concatenate([jnp.zeros((LANES - nblk, TQ), f32), sel_t], axis=0).T
        sel_rows = jnp.concatenate([sel_pad.astype(bf16)] * NSA_GROUP, axis=0)
        qs_ref[hk] = qpad + sel_rows

    def head(guarded):
        for hk in kv_heads:
            select(hk)

        win = (mw_ref, accw_ref)
        _flash_init(*win)

        @when(i >= 4, guarded)
        def _():
            r = lax.broadcasted_iota(jnp.int32, (rows, TQ), 0) & (TQ - 1)
            c = lax.broadcasted_iota(jnp.int32, (rows, TQ), 1)
            for hk in kv_heads:
                attend(hk, qw_ref, kw_ref, vw_ref, start - 4 * TQ, TQ,
                       jnp.where(c > r, 0.0, MASKED), win)

        @when(i >= 3, guarded)
        def _():
            for hk in kv_heads:
                attend(hk, qw_ref, kw_ref, vw_ref, start - 3 * TQ, 2 * TQ, None, win)

        if guarded:
            @pl.when(i == 2)
            def _():
                for hk in kv_heads:
                    attend(hk, qw_ref, kw_ref, vw_ref, start - 2 * TQ, TQ, None, win)

        near_steps(qw_ref, kw_ref, vw_ref, win, guarded)
        _flash_init(m_ref, acc_ref)

    def tail(guarded):
        near_steps(qs_ref, ks_ref, vs_ref, (m_ref, acc_ref), guarded)
        outs = []
        for hk in kv_heads:
            o_c = oc_ref[hk].T[:HEAD_DIM]
            o_s = _flash_out_t(acc_ref.at[hk])
            o_w = _flash_out_t(accw_ref.at[hk])
            gates = g_ref[0]
            for g in range(NSA_GROUP):
                sl = slice(g * TQ, (g + 1) * TQ)
                at = hk * _GATE_ROWS + 3 * g
                outs.append(gates[at:at + 1] * o_c[:, sl]
                            + gates[at + 1:at + 2] * o_s[:, sl]
                            + gates[at + 2:at + 3] * o_w[:, sl])
        o_ref[0] = jnp.concatenate(outs, axis=0).T.astype(bf16)

    interior = i >= 4
    pl.when(interior)(lambda: head(False))
    pl.when(jnp.logical_not(interior))(lambda: head(True))

    sel = (m_ref, acc_ref)
    n_far = jnp.maximum(i - 1, 0)

    def far_steps(off, n_steps):
        for step in range(n_steps):
            for hk in kv_heads:
                attend(hk, qs_ref, ks_ref, vs_ref, off + step * (2 * TQ), 2 * TQ, None, sel)

    def far_trip(c, carry):
        far_steps(c * (8 * TQ), 4)
        return carry

    lax.fori_loop(0, n_far // 8, far_trip, 0)
    rem_off = (n_far // 8) * (8 * TQ)

    @pl.when((n_far & 4) != 0)
    def _():
        far_steps(rem_off, 2)

    @pl.when((n_far & 2) != 0)
    def _():
        far_steps(rem_off + (n_far & 4) * TQ, 1)

    @pl.when((n_far & 1) != 0)
    def _():
        for hk in kv_heads:
            attend(hk, qs_ref, ks_ref, vs_ref, rem_off + (n_far & 6) * TQ, TQ, None, sel)

    pl.when(interior)(lambda: tail(False))
    pl.when(jnp.logical_not(interior))(lambda: tail(True))


def _nsa(qa, gates, kcmp, vcmp, ks, vs, kw, vw, cmp_bias, near_bias, ovt):
    bsz, seq, _ = qa.shape
    nc = kcmp.shape[2]
    n_top = min(SEL_TOPK, seq // SEL_BLOCK)
    rows = NSA_GROUP * TQ
    tok = lambda w: pl.BlockSpec((1, TQ, w), lambda b, i: (b, i, 0))
    kv = pl.BlockSpec((1, seq, NSA_KV_HEADS * LANES), lambda b, i: (b, 0, 0))
    cmp = lambda a: pl.BlockSpec((1,) + a.shape[1:], lambda b, i: (b, 0, 0, 0))
    per_kv = lambda width, dt: pltpu.VMEM((NSA_KV_HEADS, rows, width), dt)
    return pl.pallas_call(
        functools.partial(_nsa_kernel, n_top),
        grid=(bsz, seq // TQ),
        in_specs=[tok(NSA_HEADS * LANES),
                  pl.BlockSpec((1, LANES, TQ), lambda b, i: (b, 0, i)),
                  cmp(kcmp), cmp(vcmp), kv, kv, kv, kv,
                  pl.BlockSpec((NSA_HEADS, TQ, nc), lambda b, i: (0, i, 0)),
                  pl.BlockSpec((NSA_HEADS, TQ, 2 * TQ), lambda b, i: (0, 0, 0)),
                  pl.BlockSpec(ovt.shape, lambda b, i: (0, 0))],
        out_specs=tok(NSA_HEADS * HEAD_DIM),
        out_shape=jax.ShapeDtypeStruct((bsz, seq, NSA_HEADS * HEAD_DIM), bf16),
        scratch_shapes=[per_kv(LANES, bf16), per_kv(LANES, bf16),
                        per_kv(LANES, f32), per_kv(LANES, f32), per_kv(LANES, f32),
                        per_kv(LANES, f32), per_kv(LANES, f32)],
        compiler_params=_params(2),
        name="nsa",
    )(qa, gates, kcmp, vcmp, ks, vs, kw, vw, cmp_bias, near_bias, ovt)


def _sb_kernel(q_ref, k_ref, v_ref, tri_ref, o_ref, carry_ref, acc_ref):
    i = pl.program_id(2)
    n_heads = carry_ref.shape[0]
    lane = lax.broadcasted_iota(jnp.int32, (TS, LANES), 1)
    q_heads = []
    for pair in range(n_heads // 2):
        q = q_ref[0, :, pair * LANES:(pair + 1) * LANES]
        zero = jnp.zeros_like(q)
        q_heads += [jnp.where(lane < HEAD_DIM, q, zero), jnp.where(lane >= HEAD_DIM, q, zero)]
    pair_lanes = lambda hh: slice((hh // 2) * LANES, (hh // 2 + 1) * LANES)
    carry_ref[...] = jnp.zeros(carry_ref.shape, f32)
    acc_ref[...] = jnp.zeros(acc_ref.shape, f32)
    r = lax.broadcasted_iota(jnp.int32, (TS, TS), 0)
    c = lax.broadcasted_iota(jnp.int32, (TS, TS), 1)
    before = c < r

    def chunks(jobs):
        offs = [pl.multiple_of(off, TS) for off, _ in jobs]
        stage = []
        for (_, diagonal), off in zip(jobs, offs):
            for hh in range(n_heads):
                z = _dot_nt(q_heads[hh], k_ref[0, pl.ds(off, TS), pair_lanes(hh)])
                soft = jnp.log2(1.0 + jnp.exp2(-jnp.abs(z)))
                log_keep = jnp.minimum(-z, 0.0) - soft
                log_sig = jnp.minimum(z, 0.0) - soft
                if diagonal:
                    log_keep = jnp.where(before, log_keep, 0.0)
                hi, lo = _split(log_keep)
                later = _dot(hi, tri_ref[...]) + _dot(lo, tri_ref[...])
                stage.append((log_sig + later, jnp.sum(log_keep, axis=1, keepdims=True)))
        for hh in range(n_heads):
            carry = carry_ref[hh]
            acc = acc_ref[hh]
            for j, ((_, diagonal), off) in enumerate(zip(jobs, offs)):
                base, total = stage[n_heads * j + hh]
                a = jnp.exp2(base + jnp.concatenate([carry] * (TS // LANES), axis=1))
                if diagonal:
                    a = jnp.where(before, a, 0.0)
                acc = acc + _dot(a.astype(bf16), v_ref[0, pl.ds(off, TS), pair_lanes(hh)])
                carry = carry + total
            carry_ref[hh] = carry
            acc_ref[hh] = acc

    @pl.when(i == 0)
    def _():
        chunks([(0, True)])

    @pl.when(i >= 1)
    def _():
        chunks([(i * TS, True), ((i - 1) * TS, False)])

    def any_live():
        return jnp.max(carry_ref[...]) > EXP2_UNDERFLOW

    def more(state):
        n, live = state
        return jnp.logical_and(n < i, live)

    def older(state):
        n, _ = state
        chunks([((i - 1 - n) * TS, False)])
        return n + 1, any_live()

    lax.while_loop(more, older, (jnp.int32(1), any_live()))
    o_ref[0] = jnp.concatenate(
        [jnp.where(lane < HEAD_DIM, acc_ref[2 * pair], acc_ref[2 * pair + 1])
         for pair in range(n_heads // 2)], axis=1).astype(bf16)


SB_STEP_HEADS = 8


def _sb(qb, kb, vb, tri):
    bsz, seq, width = qb.shape
    step_w = SB_STEP_HEADS * HEAD_DIM
    q_spec = pl.BlockSpec((1, TS, step_w), lambda b, p, i: (b, i, p))
    kv_spec = pl.BlockSpec((1, seq, step_w), lambda b, p, i: (b, 0, p))
    state = pltpu.VMEM((SB_STEP_HEADS, TS, LANES), f32)
    return pl.pallas_call(
        _sb_kernel,
        grid=(bsz, width // step_w, seq // TS),
        in_specs=[q_spec, kv_spec, kv_spec, pl.BlockSpec(tri.shape, lambda b, p, i: (0, 0))],
        out_specs=q_spec,
        out_shape=jax.ShapeDtypeStruct((bsz, seq, width), bf16),
        scratch_shapes=[state, state],
        compiler_params=_params(3),
        name="sb",
    )(qb, kb, vb, tri)


def _post_kernel(ya_ref, yb_ref, ma_ref, mb_ref, x_ref, gate1_ref, g2_ref, sc_ref, sh_ref,
                 gate_ref, wa_ref, wb_ref, wo_ref, w1_ref, w2_ref, o_ref):
    y_a = _dot(ya_ref[0], wa_ref[...])
    y_b = _dot(yb_ref[0], wb_ref[...])
    mixed = ma_ref[0].astype(f32) * y_a + mb_ref[0].astype(f32) * y_b
    hres = x_ref[0] + gate1_ref[0] * _dot(mixed.astype(bf16), wo_ref[...])
    d = hres.shape[1]
    ms = jnp.mean(hres * hres, axis=-1, keepdims=True)
    u = (hres * lax.rsqrt(ms + EPS) * g2_ref[...]) * (1.0 + sc_ref[0]) + sh_ref[0]
    ub = u.astype(bf16)
    ff = jnp.zeros(hres.shape, f32)
    for c in range(w1_ref.shape[1] // d):
        hid = jnp.maximum(_dot(ub, w1_ref[:, c * d:(c + 1) * d]), 0.0)
        ff = ff + _dot((hid * hid).astype(bf16), w2_ref[c * d:(c + 1) * d, :])
    o_ref[0] = hres + gate_ref[0] * ff


def _post(ya, yb, ma, mb, x, gate1, g2, scale2, shift2, gate2, wa, wb, wo, w1, w2):
    bsz, seq, d = x.shape
    tok = lambda w: pl.BlockSpec((1, TM, w), lambda b, i: (b, i, 0))
    mod = pl.BlockSpec((1, 1, d), lambda b, i: (b, 0, 0))
    const = lambda a: pl.BlockSpec(a.shape, lambda b, i: (0,) * a.ndim,
                                   pipeline_mode=pl.Buffered(1))
    return pl.pallas_call(
        _post_kernel,
        grid=(bsz, seq // TM),
        in_specs=[tok(ya.shape[2]), tok(yb.shape[2]), tok(d), tok(d), tok(d), mod,
                  const(g2), mod, mod, mod, const(wa), const(wb), const(wo), const(w1), const(w2)],
        out_specs=tok(d),
        out_shape=jax.ShapeDtypeStruct((bsz, seq, d), f32),
        compiler_params=_params(2),
        name="post",
    )(ya, yb, ma, mb, x, gate1, g2, scale2, shift2, gate2, wa, wb, wo, w1, w2)


def _overlap_t(nc_pad, nsel_pad, nc, nsel):
    c_start = np.arange(nc_pad) * CMP_STRIDE
    s_start = np.arange(nsel_pad) * SEL_BLOCK
    ov = (np.minimum(c_start[None, :] + CMP_BLOCK, s_start[:, None] + SEL_BLOCK)
          - np.maximum(c_start[None, :], s_start[:, None]))
    ov = np.clip(ov, 0, CMP_BLOCK).astype(np.float32) / CMP_BLOCK
    ov[nsel:, :] = 0.0
    ov[:, nc:] = 0.0
    return ov


def _layer(h, mod, rel_tiles, p):
    bsz, seq, d = h.shape
    shift1, scale1, gate1, shift2, scale2, gate2 = [
        mod[:, k * d:(k + 1) * d].reshape(bsz, 1, d) for k in range(6)]
    cmp_bias, near_bias = rel_tiles

    gq = jnp.tile(p["q_norm_g"], 2).reshape(1, LANES)
    gk = jnp.tile(p["k_norm_g"], (1, 2))

    (qa, kc, vc, ks, vs, kw, vw, gates, qb, kb, vb, ma, mb) = _inproj(
        h, p["norm1_g"].reshape(1, d), scale1, shift1, _pack_w_in(p["w_in"], d), gq, gk)

    nch = seq // CMP_STRIDE
    kcmp, vcmp = _compress(kc, vc, p["cmp_pos"], p["cmp_k_w1"], p["cmp_k_w2"],
                           p["cmp_v_w1"], p["cmp_v_w2"], p["k_norm_g"][0].reshape(1, HEAD_DIM))

    nc = (seq - CMP_BLOCK) // CMP_STRIDE + 1
    ovt = jnp.asarray(_overlap_t(nch, HEAD_DIM, nc, seq // SEL_BLOCK), bf16)
    y_nsa = _nsa(qa, gates, kcmp, vcmp, ks, vs, kw, vw, cmp_bias, near_bias, ovt)

    y_sb = _sb(qb, kb, vb, jnp.asarray(np.tril(np.ones((TS, TS)), -1), bf16))

    return _post(y_nsa, y_sb, ma, mb, h, gate1, p["norm2_g"].reshape(1, d), scale2, shift2, gate2,
                 p["w_up_nsa"].astype(bf16), p["w_up_sb"].astype(bf16), p["w_out"].astype(bf16),
                 p["mlp_w1"].astype(bf16), p["mlp_w2"].astype(bf16))


def kernel(x, c, rel_bias, ada_w, ada_b, norm1_g, norm2_g, w_in, cmp_pos, cmp_k_w1, cmp_k_w2,
           cmp_v_w1, cmp_v_w2, q_norm_g, k_norm_g, w_up_nsa, w_up_sb, w_out, mlp_w1, mlp_w2):
    bsz, seq, d = x.shape
    assert seq % TM == 0 and seq // SEL_BLOCK <= HEAD_DIM and seq >= WINDOW + TQ
    assert CMP_BLOCK == 2 * CMP_STRIDE and TQ == 2 * SEL_BLOCK
    tbl = rel_bias.astype(f32)
    rel_tiles = _bias_tiles(tbl, seq, seq // CMP_STRIDE)
    stacked = dict(norm1_g=norm1_g, norm2_g=norm2_g, w_in=w_in, cmp_pos=cmp_pos,
                   cmp_k_w1=cmp_k_w1, cmp_k_w2=cmp_k_w2, cmp_v_w1=cmp_v_w1, cmp_v_w2=cmp_v_w2,
                   q_norm_g=q_norm_g, k_norm_g=k_norm_g, w_up_nsa=w_up_nsa, w_up_sb=w_up_sb,
                   w_out=w_out, mlp_w1=mlp_w1, mlp_w2=mlp_w2)
    h = x
    for layer in range(ada_w.shape[0]):
        mod = _adaln(c, ada_w[layer], ada_b[layer])
        h = _layer(h, mod, rel_tiles, {k: v[layer] for k, v in stacked.items()})
    return h
```

```python
import functools
import math

import numpy as np
import jax
import jax.numpy as jnp
from jax import lax
from jax.experimental import pallas as pl
from jax.experimental.pallas import tpu as pltpu

f32 = jnp.float32
bf16 = jnp.bfloat16

HEAD_DIM = 64
NSA_HEADS = 8
NSA_KV_HEADS = 2
NSA_GROUP = NSA_HEADS // NSA_KV_HEADS
SB_HEADS = 8
CMP_BLOCK = 32
CMP_STRIDE = 16
SEL_BLOCK = 64
SEL_TOPK = 16
WINDOW = 512
N_BUCKETS = 32
MAX_DISTANCE = 128
EPS = 1e-6
FORCED_BONUS = 1e4
NEG_BLOCK = -1e9

LANES = 128
MASKED = -1e30
UNSELECTED = -1e9
LOG2E = math.log2(math.e)
EXP2_UNDERFLOW = -150.0
VMEM_LIMIT = 56 * 1024 * 1024

TQ = 128
TS = 256
TM = 512


def _bucket_thresholds():
    n = np.arange(0, 4 * MAX_DISTANCE)
    max_exact = N_BUCKETS // 2
    nf = np.maximum(n, 1).astype(np.float32)
    large = max_exact + (np.log(nf / max_exact) / math.log(MAX_DISTANCE / max_exact)
                         * (N_BUCKETS - max_exact)).astype(np.int32)
    large = np.minimum(large, N_BUCKETS - 1)
    b = np.where(n < max_exact, n, large)
    assert np.all(np.diff(b) >= 0) and b[-1] == N_BUCKETS - 1
    return [int(np.argmax(b >= k)) for k in range(N_BUCKETS)]


BUCKET_START = _bucket_thresholds()
assert BUCKET_START[-1] <= LANES


def _dot(a, b):
    return jnp.dot(a, b, preferred_element_type=f32)


def _dot_nt(a, b):
    return lax.dot_general(a, b, (((1,), (1,)), ((), ())), preferred_element_type=f32)


def _split(a):
    hi = a.astype(bf16)
    lo = (a - hi.astype(f32)).astype(bf16)
    return hi, lo


def _params(n_grid):
    return pltpu.CompilerParams(dimension_semantics=("arbitrary",) * n_grid,
                                vmem_limit_bytes=VMEM_LIMIT)


def _adaln_kernel(c_ref, w_ref, b_ref, o_ref):
    c = c_ref[...]
    a = c * jax.nn.sigmoid(c)
    ah, al = _split(a)
    wh, wl = _split(w_ref[...])
    o_ref[...] = _dot(ah, wh) + _dot(ah, wl) + _dot(al, wh) + b_ref[...]


def _adaln(c, w, b):
    bsz, d = c.shape
    n = w.shape[1]
    return pl.pallas_call(
        _adaln_kernel,
        grid=(n // d,),
        in_specs=[pl.BlockSpec((bsz, d), lambda j: (0, 0)),
                  pl.BlockSpec((d, d), lambda j: (0, j)),
                  pl.BlockSpec((1, d), lambda j: (0, j))],
        out_specs=pl.BlockSpec((bsz, d), lambda j: (0, j)),
        out_shape=jax.ShapeDtypeStruct((bsz, n), f32),
        compiler_params=_params(1),
        name="adaln",
    )(c, w, b.reshape(1, n))


def _bias_of_dist(dist, tbl_ref, h):
    out = jnp.full(dist.shape, tbl_ref[0, h], f32)
    for k in range(1, N_BUCKETS):
        out = jnp.where(dist >= BUCKET_START[k], tbl_ref[k, h], out)
    return jnp.where(dist >= 0, out, MASKED)


def _cmp_bias_kernel(tbl_ref, o_ref):
    h = pl.program_id(0)
    i = pl.program_id(1)
    rows, nc = o_ref.shape[1], o_ref.shape[2]
    t = i * rows + lax.broadcasted_iota(jnp.int32, (rows, nc), 0)
    j = lax.broadcasted_iota(jnp.int32, (rows, nc), 1)
    o_ref[0] = _bias_of_dist(t - (j * CMP_STRIDE + CMP_BLOCK - 1), tbl_ref, h) * LOG2E


def _near_bias_kernel(tbl_ref, o_ref):
    h = pl.program_id(0)
    r = lax.broadcasted_iota(jnp.int32, (TQ, 2 * TQ), 0)
    c = lax.broadcasted_iota(jnp.int32, (TQ, 2 * TQ), 1)
    o_ref[0] = (_bias_of_dist(r - c + TQ, tbl_ref, h) - tbl_ref[N_BUCKETS - 1, h]) * LOG2E


def _bias_tiles(rel_bias, seq, nc_pad):
    tbl = rel_bias.astype(f32)
    smem = pl.BlockSpec(memory_space=pltpu.SMEM)
    rows = 512
    cmp_bias = pl.pallas_call(
        _cmp_bias_kernel,
        grid=(NSA_HEADS, seq // rows),
        in_specs=[smem],
        out_specs=pl.BlockSpec((1, rows, nc_pad), lambda h, i: (h, i, 0)),
        out_shape=jax.ShapeDtypeStruct((NSA_HEADS, seq, nc_pad), f32),
        compiler_params=_params(2),
        name="cmp_bias",
    )(tbl)
    near_bias = pl.pallas_call(
        _near_bias_kernel,
        grid=(NSA_HEADS,),
        in_specs=[smem],
        out_specs=pl.BlockSpec((1, TQ, 2 * TQ), lambda h: (h, 0, 0)),
        out_shape=jax.ShapeDtypeStruct((NSA_HEADS, TQ, 2 * TQ), f32),
        compiler_params=_params(1),
        name="near_bias",
    )(tbl)
    return cmp_bias, near_bias


_QA_W = NSA_HEADS * LANES
_KV_W = NSA_KV_HEADS * LANES
_CMP_W = NSA_KV_HEADS * HEAD_DIM
_SB_W = SB_HEADS * HEAD_DIM
_GATE_ROWS = NSA_GROUP * 3


def _layout(d_model):
    names = ["qa", "kc", "vc", "ksl", "vsl", "kwn", "vwn", "g", "qb", "kb", "vb", "ma", "mb"]
    widths = [NSA_HEADS * HEAD_DIM] + [_CMP_W] * 6 + [LANES] + [_SB_W] * 3 + [d_model, d_model]
    offs = np.concatenate([[0], np.cumsum(widths)])
    return {n: (int(offs[i]), int(offs[i + 1])) for i, n in enumerate(names)}, int(offs[-1])


def _pack_w_in(w_in, d_model):
    q_w = NSA_HEADS * HEAD_DIM
    kv_w = NSA_KV_HEADS * HEAD_DIM
    g_w = NSA_HEADS * 3
    sizes = [q_w] + [kv_w] * 6 + [g_w] + [_SB_W] * 3 + [d_model, d_model]
    offs = np.concatenate([[0], np.cumsum(sizes)])
    parts = [w_in[:, int(offs[i]):int(offs[i + 1])] for i in range(len(sizes))]
    parts[7] = jnp.pad(parts[7], ((0, 0), (0, LANES - g_w)))
    return jnp.concatenate(parts, axis=1).astype(bf16)


def _inproj_kernel(lay, x_ref, g1_ref, sc_ref, sh_ref, w_ref, gq_ref, gk_ref,
                   qa_ref, kc_ref, vc_ref, ks_ref, vs_ref, kw_ref, vw_ref, g_ref,
                   qb_ref, kb_ref, vb_ref, ma_ref, mb_ref):
    i = pl.program_id(1)
    x = x_ref[0]
    ms = jnp.mean(x * x, axis=-1, keepdims=True)
    u = (x * lax.rsqrt(ms + EPS) * g1_ref[...]) * (1.0 + sc_ref[0]) + sh_ref[0]
    ub = u.astype(bf16)

    narrow = ("kc", "vc", "ksl", "vsl", "kwn", "vwn", "g")
    narrow_lo = lay[narrow[0]][0]
    z_narrow = _dot(ub, w_ref[:, narrow_lo:lay[narrow[-1]][1]])

    def proj(name):
        lo, hi = lay[name]
        if name in narrow:
            return z_narrow[:, lo - narrow_lo:hi - narrow_lo]
        return _dot(ub, w_ref[:, lo:hi])

    rows = x.shape[0]
    lane = lax.broadcasted_iota(jnp.int32, (rows, LANES), 1)
    low = lane < HEAD_DIM

    def pair_norm(z, gain):
        sq = z * z
        ss_a = jnp.sum(jnp.where(low, sq, 0.0), axis=1, keepdims=True)
        ss_b = jnp.sum(jnp.where(low, 0.0, sq), axis=1, keepdims=True)
        inv = lax.rsqrt(jnp.where(low, ss_a, ss_b) * (1.0 / HEAD_DIM) + EPS)
        return z * inv * gain

    def spread(z, extra):
        return jnp.concatenate([jnp.where(low, z, extra),
                                jnp.where(low, pltpu.roll(z, HEAD_DIM, 1), extra)], axis=1)

    scale = HEAD_DIM ** -0.5 * LOG2E
    zq = proj("qa")
    qa_ref[0] = jnp.concatenate(
        [spread(pair_norm(zq[:, j * LANES:(j + 1) * LANES], gq_ref[...]) * scale, 0.0)
         for j in range(NSA_HEADS // 2)], axis=1).astype(bf16)
    kc_ref[0] = proj("kc")
    vc_ref[0] = proj("vc")

    tok_blk = (i * rows + lax.broadcasted_iota(jnp.int32, (rows, LANES), 0)) // SEL_BLOCK
    onehot = jnp.where(lane - HEAD_DIM == tok_blk, 1.0, 0.0)
    ones_col = jnp.where(lane == HEAD_DIM, 1.0, 0.0)

    ks_ref[0] = spread(pair_norm(proj("ksl"), gk_ref[1:2, :]), onehot).astype(bf16)
    vs_ref[0] = spread(proj("vsl"), ones_col).astype(bf16)
    kw_ref[0] = spread(pair_norm(proj("kwn"), gk_ref[2:3, :]), 0.0).astype(bf16)
    vw_ref[0] = spread(proj("vwn"), ones_col).astype(bf16)
    g_ref[0] = jax.nn.sigmoid(proj("g")).T
    qb_ref[0] = (proj("qb") * scale).astype(bf16)
    kb_ref[0] = proj("kb").astype(bf16)
    vb_ref[0] = proj("vb").astype(bf16)
    ma_ref[0] = jax.nn.sigmoid(proj("ma")).astype(bf16)
    mb_ref[0] = jax.nn.sigmoid(proj("mb")).astype(bf16)


def _inproj(x, g1, scale1, shift1, w_packed, gq, gk):
    bsz, seq, d = x.shape
    lay, width = _layout(d)
    assert w_packed.shape == (d, width)
    tok = lambda w: pl.BlockSpec((1, TM, w), lambda b, i: (b, i, 0))
    full = lambda a: pl.BlockSpec(a.shape, lambda b, i: (0,) * a.ndim,
                                  pipeline_mode=pl.Buffered(1))
    mod = pl.BlockSpec((1, 1, d), lambda b, i: (b, 0, 0))
    out_w = [(_QA_W, bf16), (_CMP_W, f32), (_CMP_W, f32), (_KV_W, bf16), (_KV_W, bf16),
             (_KV_W, bf16), (_KV_W, bf16), (LANES, f32), (_SB_W, bf16), (_SB_W, bf16),
             (_SB_W, bf16), (d, bf16), (d, bf16)]
    gates_at = 7
    out_specs = [tok(w) for w, _ in out_w]
    out_shape = [jax.ShapeDtypeStruct((bsz, seq, w), dt) for w, dt in out_w]
    out_specs[gates_at] = pl.BlockSpec((1, LANES, TM), lambda b, i: (b, 0, i))
    out_shape[gates_at] = jax.ShapeDtypeStruct((bsz, LANES, seq), f32)
    return pl.pallas_call(
        functools.partial(_inproj_kernel, lay),
        grid=(bsz, seq // TM),
        in_specs=[tok(d), full(g1), mod, mod, full(w_packed), full(gq), full(gk)],
        out_specs=out_specs,
        out_shape=out_shape,
        compiler_params=_params(2),
        name="inproj",
    )(x, g1, scale1, shift1, w_packed, gq, gk)


def _compress_kernel(xk_ref, xv_ref, pos_ref, w1k_ref, w2k_ref, w1v_ref, w2v_ref, gk_ref,
                     ko_ref, vo_ref):
    nch = xk_ref.shape[1] // CMP_STRIDE

    def mlp(x_ref, w1_ref, w2_ref):
        first = jnp.zeros((nch, LANES), f32)
        second = jnp.zeros((nch, LANES), f32)
        for l in range(CMP_STRIDE):
            xl = x_ref[0, pl.ds(l, nch, stride=CMP_STRIDE), :]
            lo = l + CMP_STRIDE
            first = first + _dot((xl + pos_ref[l:l + 1, :]).astype(bf16), w1_ref[l])
            second = second + _dot((xl + pos_ref[lo:lo + 1, :]).astype(bf16), w1_ref[lo])
        pre = first + pltpu.roll(second, nch - 1, 0)
        hid = pre * jax.nn.sigmoid(pre)
        return _dot(hid.astype(bf16), w2_ref[...])

    k = mlp(xk_ref, w1k_ref, w2k_ref)
    v = mlp(xv_ref, w1v_ref, w2v_ref)
    for h in range(NSA_KV_HEADS):
        kh = k[:, h * HEAD_DIM:(h + 1) * HEAD_DIM]
        ms = jnp.mean(kh * kh, axis=-1, keepdims=True)
        ko_ref[0, h] = (kh * lax.rsqrt(ms + EPS) * gk_ref[...]).astype(bf16)
        vh = v[:, h * HEAD_DIM:(h + 1) * HEAD_DIM]
        vo_ref[0, h] = jnp.concatenate([vh, jnp.zeros_like(vh)], axis=1).astype(bf16)


def _compress(xk, xv, pos, w1k, w2k, w1v, w2v, gk0):
    bsz, seq, width = xk.shape
    nch = seq // CMP_STRIDE
    assert width == NSA_KV_HEADS * HEAD_DIM == LANES

    def both_heads(w):
        z = jnp.zeros_like(w)
        return jnp.concatenate([jnp.concatenate([w, z], axis=2),
                                jnp.concatenate([z, w], axis=2)], axis=1).astype(bf16)

    w1 = lambda w: both_heads(w.reshape(CMP_BLOCK, HEAD_DIM, w.shape[1]))
    w2 = lambda w: both_heads(w[None])[0]
    args = (xk, xv, jnp.tile(pos, (1, NSA_KV_HEADS)), w1(w1k), w2(w2k), w1(w1v), w2(w2v), gk0)
    blk = pl.BlockSpec((1, seq, width), lambda b: (b, 0, 0))
    full = lambda a: pl.BlockSpec(a.shape, lambda b: (0,) * a.ndim)
    out = lambda w: pl.BlockSpec((1, NSA_KV_HEADS, nch, w), lambda b: (b, 0, 0, 0))
    shape = lambda w: jax.ShapeDtypeStruct((bsz, NSA_KV_HEADS, nch, w), bf16)
    return pl.pallas_call(
        _compress_kernel,
        grid=(bsz,),
        in_specs=[blk, blk] + [full(a) for a in args[2:]],
        out_specs=[out(HEAD_DIM), out(LANES)],
        out_shape=[shape(HEAD_DIM), shape(LANES)],
        compiler_params=_params(1),
        name="compress",
    )(*args)


def _flash_step(s, v, m_ref, acc_ref):
    m_prev = m_ref[...]
    m_new = jnp.maximum(m_prev, jnp.max(s, axis=1, keepdims=True))
    alpha = jnp.exp2(m_prev - m_new)
    p = jnp.exp2(s - jnp.concatenate([m_new] * (s.shape[1] // LANES), axis=1))
    acc_ref[...] = alpha * acc_ref[...] + _dot(p.astype(bf16), v)
    m_ref[...] = m_new


def _flash_init(m_ref, acc_ref):
    m_ref[...] = jnp.full(m_ref.shape, MASKED, f32)
    acc_ref[...] = jnp.zeros(acc_ref.shape, f32)


def _flash_out_t(acc_ref):
    acc_t = acc_ref[...].T
    return acc_t[:HEAD_DIM] / acc_t[HEAD_DIM:HEAD_DIM + 1]


def _nsa_kernel(n_top, q_ref, g_ref, kc_ref, vc_ref, ks_ref, vs_ref, kw_ref, vw_ref,
                bc_ref, bn_ref, ovt_ref, o_ref, qs_ref, qw_ref, oc_ref,
                m_ref, acc_ref, mw_ref, accw_ref):
    i = pl.program_id(1)
    kv_heads = range(NSA_KV_HEADS)
    rows = NSA_GROUP * TQ
    start = i * TQ
    assert WINDOW == 4 * TQ

    def attend(hk, q_rows_ref, k_ref, v_ref, off, width, bias, state):
        off = pl.multiple_of(off, TQ)
        lanes = slice(hk * LANES, (hk + 1) * LANES)
        s = _dot_nt(q_rows_ref[hk], k_ref[0, pl.ds(off, width), lanes])
        if bias is not None:
            s = s + bias
        _flash_step(s, v_ref[0, pl.ds(off, width), lanes], state[0].at[hk], state[1].at[hk])

    def when(cond, guarded):
        return pl.when(cond) if guarded else (lambda fn: fn())

    def near_steps(q_rows_ref, k_ref, v_ref, state, guarded, extra=0):
        group = lambda hk: slice(hk * NSA_GROUP, (hk + 1) * NSA_GROUP)

        @when(i >= 1, guarded)
        def _():
            for hk in kv_heads:
                bias = bn_ref[group(hk)].reshape(rows, 2 * TQ)
                if extra:
                    bias = jnp.concatenate([jnp.zeros((rows, extra * TQ), f32), bias], axis=1)
                attend(hk, q_rows_ref, k_ref, v_ref, start - (1 + extra) * TQ, (2 + extra) * TQ,
                       bias, state)

        if guarded:
            @pl.when(i == 0)
            def _():
                for hk in kv_heads:
                    attend(hk, q_rows_ref, k_ref, v_ref, start, TQ,
                           bn_ref[group(hk), :, TQ:].reshape(rows, TQ), state)

    def select(hk):
        heads = range(hk * NSA_GROUP, (hk + 1) * NSA_GROUP)
        qpad = jnp.concatenate([q_ref[0, :, g * LANES:(g + 1) * LANES] for g in heads], axis=0)
        qw_ref[hk] = qpad

        bc = bc_ref[hk * NSA_GROUP:(hk + 1) * NSA_GROUP].reshape(rows, bc_ref.shape[2])
        s_c = _dot_nt(qpad[:, :HEAD_DIM], kc_ref[0, hk]) + bc
        visible = bc > 0.5 * MASKED
        m_c = jnp.max(s_c, axis=1, keepdims=True)
        e_c = jnp.where(visible, jnp.exp2(s_c - m_c), 0.0)
        p_c = e_c / jnp.maximum(jnp.sum(e_c, axis=1, keepdims=True), 1e-30)
        oc_ref[hk] = _dot(p_c.astype(bf16), vc_ref[0, hk])

        p_sum = p_c[0:TQ]
        for g in range(1, NSA_GROUP):
            p_sum = p_sum + p_c[g * TQ:(g + 1) * TQ]
        p_hi, p_lo = _split(p_sum)
        imp = _dot_nt(ovt_ref[...], p_hi) + _dot_nt(ovt_ref[...], p_lo)
        nblk = imp.shape[0]
        blk = lax.broadcasted_iota(jnp.int32, (nblk, TQ), 0)
        cur = (start + lax.broadcasted_iota(jnp.int32, (nblk, TQ), 1)) // SEL_BLOCK
        forced = (blk == 0) | (blk == cur) | (blk == cur - 1)
        imp = jnp.where(blk > cur, NEG_BLOCK, imp + jnp.where(forced, FORCED_BONUS, 0.0))
        sub = 8
        groups = [imp[lo:lo + sub] for lo in range(0, nblk, sub)]
        ranks = [jnp.zeros((sub, TQ), f32) for _ in groups]
        row = lax.broadcasted_iota(jnp.int32, (sub, TQ), 0)
        for b2 in range(nblk):
            other = imp[b2:b2 + 1, :]
            for gi, grp in enumerate(groups):
                lo = gi * sub
                if lo > b2:
                    ranks[gi] = jnp.where(other >= grp, ranks[gi] + 1.0, ranks[gi])
                elif lo + sub - 1 < b2:
                    ranks[gi] = jnp.where(other > grp, ranks[gi] + 1.0, ranks[gi])
                else:
                    ranks[gi] = ranks[gi] + jnp.where(row + lo > b2,
                                                      jnp.where(other >= grp, 1.0, 0.0),
                                                      jnp.where(other > grp, 1.0, 0.0))
        rank = jnp.concatenate(ranks, axis=0)
        usable = (rank < n_top) & (blk <= cur)
        sel_t = jnp.where(usable, 0.0, UNSELECTED)
        sel_pad = jnp.concatenate([jnp.zeros((LANES - nblk, TQ), f32), sel_t], axis=0).T
        sel_rows = jnp.concatenate([sel_pad.astype(bf16)] * NSA_GROUP, axis=0)
        qs_ref[hk] = qpad + sel_rows

    def head(guarded):
        for hk in kv_heads:
            select(hk)

        win = (mw_ref, accw_ref)
        _flash_init(*win)

        @when(i >= 4, guarded)
        def _():
            r = lax.broadcasted_iota(jnp.int32, (rows, TQ), 0) & (TQ - 1)
            c = lax.broadcasted_iota(jnp.int32, (rows, TQ), 1)
            for hk in kv_heads:
                attend(hk, qw_ref, kw_ref, vw_ref, start - 4 * TQ, TQ,
                       jnp.where(c > r, 0.0, MASKED), win)

        @when(i >= 3, guarded)
        def _():
            for hk in kv_heads:
                attend(hk, qw_ref, kw_ref, vw_ref, start - 3 * TQ, 2 * TQ, None, win)

        if guarded:
            @pl.when(i == 2)
            def _():
                for hk in kv_heads:
                    attend(hk, qw_ref, kw_ref, vw_ref, start - 2 * TQ, TQ, None, win)

        near_steps(qw_ref, kw_ref, vw_ref, win, guarded)
        _flash_init(m_ref, acc_ref)

    def tail(guarded, extra=0):
        near_steps(qs_ref, ks_ref, vs_ref, (m_ref, acc_ref), guarded, extra)
        outs = []
        for hk in kv_heads:
            o_c = oc_ref[hk].T[:HEAD_DIM]
            o_s = _flash_out_t(acc_ref.at[hk])
            o_w = _flash_out_t(accw_ref.at[hk])
            gates = g_ref[0]
            for g in range(NSA_GROUP):
                sl = slice(g * TQ, (g + 1) * TQ)
                at = hk * _GATE_ROWS + 3 * g
                outs.append(gates[at:at + 1] * o_c[:, sl]
                            + gates[at + 1:at + 2] * o_s[:, sl]
                            + gates[at + 2:at + 3] * o_w[:, sl])
        o_ref[0] = jnp.concatenate(outs, axis=0).T.astype(bf16)

    interior = i >= 4
    pl.when(interior)(lambda: head(False))
    pl.when(jnp.logical_not(interior))(lambda: head(True))

    sel = (m_ref, acc_ref)
    n_far = jnp.maximum(i - 1, 0)

    def far_steps(off, n_steps):
        for step in range(n_steps):
            for hk in kv_heads:
                attend(hk, qs_ref, ks_ref, vs_ref, off + step * (2 * TQ), 2 * TQ, None, sel)

    def far_trip(c, carry):
        far_steps(c * (8 * TQ), 4)
        return carry

    lax.fori_loop(0, n_far // 8, far_trip, 0)
    rem_off = (n_far // 8) * (8 * TQ)

    @pl.when((n_far & 4) != 0)
    def _():
        far_steps(rem_off, 2)

    edge = jnp.logical_not(interior)

    @pl.when(edge & ((n_far & 2) != 0))
    def _():
        far_steps(rem_off + (n_far & 4) * TQ, 1)

    @pl.when(edge & ((n_far & 1) != 0))
    def _():
        for hk in kv_heads:
            attend(hk, qs_ref, ks_ref, vs_ref, rem_off + (n_far & 6) * TQ, TQ, None, sel)

    for extra in range(4):
        pl.when(interior & ((n_far & 3) == extra))(functools.partial(tail, False, extra))
    pl.when(edge)(lambda: tail(True))


def _nsa(qa, gates, kcmp, vcmp, ks, vs, kw, vw, cmp_bias, near_bias, ovt):
    bsz, seq, _ = qa.shape
    nc = kcmp.shape[2]
    n_top = min(SEL_TOPK, seq // SEL_BLOCK)
    rows = NSA_GROUP * TQ
    tok = lambda w: pl.BlockSpec((1, TQ, w), lambda b, i: (b, i, 0))
    kv = pl.BlockSpec((1, seq, NSA_KV_HEADS * LANES), lambda b, i: (b, 0, 0))
    cmp = lambda a: pl.BlockSpec((1,) + a.shape[1:], lambda b, i: (b, 0, 0, 0))
    per_kv = lambda width, dt: pltpu.VMEM((NSA_KV_HEADS, rows, width), dt)
    return pl.pallas_call(
        functools.partial(_nsa_kernel, n_top),
        grid=(bsz, seq // TQ),
        in_specs=[tok(NSA_HEADS * LANES),
                  pl.BlockSpec((1, LANES, TQ), lambda b, i: (b, 0, i)),
                  cmp(kcmp), cmp(vcmp), kv, kv, kv, kv,
                  pl.BlockSpec((NSA_HEADS, TQ, nc), lambda b, i: (0, i, 0)),
                  pl.BlockSpec((NSA_HEADS, TQ, 2 * TQ), lambda b, i: (0, 0, 0)),
                  pl.BlockSpec(ovt.shape, lambda b, i: (0, 0))],
        out_specs=tok(NSA_HEADS * HEAD_DIM),
        out_shape=jax.ShapeDtypeStruct((bsz, seq, NSA_HEADS * HEAD_DIM), bf16),
        scratch_shapes=[per_kv(LANES, bf16), per_kv(LANES, bf16),
                        per_kv(LANES, f32), per_kv(LANES, f32), per_kv(LANES, f32),
                        per_kv(LANES, f32), per_kv(LANES, f32)],
        compiler_params=_params(2),
        name="nsa",
    )(qa, gates, kcmp, vcmp, ks, vs, kw, vw, cmp_bias, near_bias, ovt)


def _sb_kernel(q_ref, k_ref, v_ref, tri_ref, o_ref, carry_ref, acc_ref):
    i = pl.program_id(2)
    n_heads = carry_ref.shape[0]
    lane = lax.broadcasted_iota(jnp.int32, (TS, LANES), 1)
    q_heads = []
    for pair in range(n_heads // 2):
        q = q_ref[0, :, pair * LANES:(pair + 1) * LANES]
        zero = jnp.zeros_like(q)
        q_heads += [jnp.where(lane < HEAD_DIM, q, zero), jnp.where(lane >= HEAD_DIM, q, zero)]
    pair_lanes = lambda hh: slice((hh // 2) * LANES, (hh // 2 + 1) * LANES)
    carry_ref[...] = jnp.zeros(carry_ref.shape, f32)
    acc_ref[...] = jnp.zeros(acc_ref.shape, f32)
    r = lax.broadcasted_iota(jnp.int32, (TS, TS), 0)
    c = lax.broadcasted_iota(jnp.int32, (TS, TS), 1)
    before = c < r

    def chunks(jobs):
        offs = [pl.multiple_of(off, TS) for off, _ in jobs]
        stage = []
        for (_, diagonal), off in zip(jobs, offs):
            for hh in range(n_heads):
                z = _dot_nt(q_heads[hh], k_ref[0, pl.ds(off, TS), pair_lanes(hh)])
                soft = jnp.log2(1.0 + jnp.exp2(-jnp.abs(z)))
                log_keep = jnp.minimum(-z, 0.0) - soft
                log_sig = log_keep + z
                if diagonal:
                    log_keep = jnp.where(before, log_keep, 0.0)
                hi, lo = _split(log_keep)
                later = _dot(hi, tri_ref[...]) + _dot(lo, tri_ref[...])
                stage.append((log_sig + later, jnp.sum(log_keep, axis=1, keepdims=True)))
        for hh in range(n_heads):
            carry = carry_ref[hh]
            acc = acc_ref[hh]
            for j, ((_, diagonal), off) in enumerate(zip(jobs, offs)):
                base, total = stage[n_heads * j + hh]
                a = jnp.exp2(base + jnp.concatenate([carry] * (TS // LANES), axis=1))
                if diagonal:
                    a = jnp.where(before, a, 0.0)
                acc = acc + _dot(a.astype(bf16), v_ref[0, pl.ds(off, TS), pair_lanes(hh)])
                carry = carry + total
            carry_ref[hh] = carry
            acc_ref[hh] = acc

    @pl.when(i == 0)
    def _():
        chunks([(0, True)])

    @pl.when(i >= 1)
    def _():
        chunks([(i * TS, True), ((i - 1) * TS, False)])

    def any_live():
        return jnp.max(carry_ref[...]) > EXP2_UNDERFLOW

    def more(state):
        n, live = state
        return jnp.logical_and(n < i, live)

    def older(state):
        n, _ = state
        chunks([((i - 1 - n) * TS, False)])
        return n + 1, any_live()

    lax.while_loop(more, older, (jnp.int32(1), any_live()))
    o_ref[0] = jnp.concatenate(
        [jnp.where(lane < HEAD_DIM, acc_ref[2 * pair], acc_ref[2 * pair + 1])
         for pair in range(n_heads // 2)], axis=1).astype(bf16)


SB_STEP_HEADS = 8


def _sb(qb, kb, vb, tri):
    bsz, seq, width = qb.shape
    step_w = SB_STEP_HEADS * HEAD_DIM
    q_spec = pl.BlockSpec((1, TS, step_w), lambda b, p, i: (b, i, p))
    kv_spec = pl.BlockSpec((1, seq, step_w), lambda b, p, i: (b, 0, p))
    state = pltpu.VMEM((SB_STEP_HEADS, TS, LANES), f32)
    return pl.pallas_call(
        _sb_kernel,
        grid=(bsz, width // step_w, seq // TS),
        in_specs=[q_spec, kv_spec, kv_spec, pl.BlockSpec(tri.shape, lambda b, p, i: (0, 0))],
        out_specs=q_spec,
        out_shape=jax.ShapeDtypeStruct((bsz, seq, width), bf16),
        scratch_shapes=[state, state],
        compiler_params=_params(3),
        name="sb",
    )(qb, kb, vb, tri)


def _post_kernel(ya_ref, yb_ref, ma_ref, mb_ref, x_ref, gate1_ref, g2_ref, sc_ref, sh_ref,
                 gate_ref, wa_ref, wb_ref, wo_ref, w1_ref, w2_ref, o_ref):
    y_a = _dot(ya_ref[0], wa_ref[...])
    y_b = _dot(yb_ref[0], wb_ref[...])
    mixed = ma_ref[0].astype(f32) * y_a + mb_ref[0].astype(f32) * y_b
    hres = x_ref[0] + gate1_ref[0] * _dot(mixed.astype(bf16), wo_ref[...])
    d = hres.shape[1]
    ms = jnp.mean(hres * hres, axis=-1, keepdims=True)
    u = (hres * lax.rsqrt(ms + EPS) * g2_ref[...]) * (1.0 + sc_ref[0]) + sh_ref[0]
    ub = u.astype(bf16)
    ff = jnp.zeros(hres.shape, f32)
    for c in range(w1_ref.shape[1] // d):
        hid = jnp.maximum(_dot(ub, w1_ref[:, c * d:(c + 1) * d]), 0.0)
        ff = ff + _dot((hid * hid).astype(bf16), w2_ref[c * d:(c + 1) * d, :])
    o_ref[0] = hres + gate_ref[0] * ff


def _post(ya, yb, ma, mb, x, gate1, g2, scale2, shift2, gate2, wa, wb, wo, w1, w2):
    bsz, seq, d = x.shape
    tok = lambda w: pl.BlockSpec((1, TM, w), lambda b, i: (b, i, 0))
    mod = pl.BlockSpec((1, 1, d), lambda b, i: (b, 0, 0))
    const = lambda a: pl.BlockSpec(a.shape, lambda b, i: (0,) * a.ndim,
                                   pipeline_mode=pl.Buffered(1))
    return pl.pallas_call(
        _post_kernel,
        grid=(bsz, seq // TM),
        in_specs=[tok(ya.shape[2]), tok(yb.shape[2]), tok(d), tok(d), tok(d), mod,
                  const(g2), mod, mod, mod, const(wa), const(wb), const(wo), const(w1), const(w2)],
        out_specs=tok(d),
        out_shape=jax.ShapeDtypeStruct((bsz, seq, d), f32),
        compiler_params=_params(2),
        name="post",
    )(ya, yb, ma, mb, x, gate1, g2, scale2, shift2, gate2, wa, wb, wo, w1, w2)


def _overlap_t(nc_pad, nsel_pad, nc, nsel):
    c_start = np.arange(nc_pad) * CMP_STRIDE
    s_start = np.arange(nsel_pad) * SEL_BLOCK
    ov = (np.minimum(c_start[None, :] + CMP_BLOCK, s_start[:, None] + SEL_BLOCK)
          - np.maximum(c_start[None, :], s_start[:, None]))
    ov = np.clip(ov, 0, CMP_BLOCK).astype(np.float32) / CMP_BLOCK
    ov[nsel:, :] = 0.0
    ov[:, nc:] = 0.0
    return ov


def _layer(h, mod, rel_tiles, p):
    bsz, seq, d = h.shape
    shift1, scale1, gate1, shift2, scale2, gate2 = [
        mod[:, k * d:(k + 1) * d].reshape(bsz, 1, d) for k in range(6)]
    cmp_bias, near_bias = rel_tiles

    gq = jnp.tile(p["q_norm_g"], 2).reshape(1, LANES)
    gk = jnp.tile(p["k_norm_g"], (1, 2))

    (qa, kc, vc, ks, vs, kw, vw, gates, qb, kb, vb, ma, mb) = _inproj(
        h, p["norm1_g"].reshape(1, d), scale1, shift1, _pack_w_in(p["w_in"], d), gq, gk)

    nch = seq // CMP_STRIDE
    kcmp, vcmp = _compress(kc, vc, p["cmp_pos"], p["cmp_k_w1"], p["cmp_k_w2"],
                           p["cmp_v_w1"], p["cmp_v_w2"], p["k_norm_g"][0].reshape(1, HEAD_DIM))

    nc = (seq - CMP_BLOCK) // CMP_STRIDE + 1
    ovt = jnp.asarray(_overlap_t(nch, HEAD_DIM, nc, seq // SEL_BLOCK), bf16)
    y_nsa = _nsa(qa, gates, kcmp, vcmp, ks, vs, kw, vw, cmp_bias, near_bias, ovt)

    y_sb = _sb(qb, kb, vb, jnp.asarray(np.tril(np.ones((TS, TS)), -1), bf16))

    return _post(y_nsa, y_sb, ma, mb, h, gate1, p["norm2_g"].reshape(1, d), scale2, shift2, gate2,
                 p["w_up_nsa"].astype(bf16), p["w_up_sb"].astype(bf16), p["w_out"].astype(bf16),
                 p["mlp_w1"].astype(bf16), p["mlp_w2"].astype(bf16))


def kernel(x, c, rel_bias, ada_w, ada_b, norm1_g, norm2_g, w_in, cmp_pos, cmp_k_w1, cmp_k_w2,
           cmp_v_w1, cmp_v_w2, q_norm_g, k_norm_g, w_up_nsa, w_up_sb, w_out, mlp_w1, mlp_w2):
    bsz, seq, d = x.shape
    assert seq % TM == 0 and seq // SEL_BLOCK <= HEAD_DIM and seq >= WINDOW + TQ
    assert CMP_BLOCK == 2 * CMP_STRIDE and TQ == 2 * SEL_BLOCK
    tbl = rel_bias.astype(f32)
    rel_tiles = _bias_tiles(tbl, seq, seq // CMP_STRIDE)
    stacked = dict(norm1_g=norm1_g, norm2_g=norm2_g, w_in=w_in, cmp_pos=cmp_pos,
                   cmp_k_w1=cmp_k_w1, cmp_k_w2=cmp_k_w2, cmp_v_w1=cmp_v_w1, cmp_v_w2=cmp_v_w2,
                   q_norm_g=q_norm_g, k_norm_g=k_norm_g, w_up_nsa=w_up_nsa, w_up_sb=w_up_sb,
                   w_out=w_out, mlp_w1=mlp_w1, mlp_w2=mlp_w2)
    h = x
    for layer in range(ada_w.shape[0]):
        mod = _adaln(c, ada_w[layer], ada_b[layer])
        h = _layer(h, mod, rel_tiles, {k: v[layer] for k, v in stacked.items()})
    return h
```

```python
import functools
import math

import numpy as np
import jax
import jax.numpy as jnp
from jax import lax
from jax.experimental import pallas as pl
from jax.experimental.pallas import tpu as pltpu

f32 = jnp.float32
bf16 = jnp.bfloat16

HEAD_DIM = 64
NSA_HEADS = 8
NSA_KV_HEADS = 2
NSA_GROUP = NSA_HEADS // NSA_KV_HEADS
SB_HEADS = 8
CMP_BLOCK = 32
CMP_STRIDE = 16
SEL_BLOCK = 64
SEL_TOPK = 16
WINDOW = 512
N_BUCKETS = 32
MAX_DISTANCE = 128
EPS = 1e-6
FORCED_BONUS = 1e4
NEG_BLOCK = -1e9

LANES = 128
MASKED = -1e30
UNSELECTED = -1e9
LOG2E = math.log2(math.e)
EXP2_UNDERFLOW = -150.0
VMEM_LIMIT = 56 * 1024 * 1024

TQ = 256
KU = 128
TS = 256
TM = 512


def _bucket_thresholds():
    n = np.arange(0, 4 * MAX_DISTANCE)
    max_exact = N_BUCKETS // 2
    nf = np.maximum(n, 1).astype(np.float32)
    large = max_exact + (np.log(nf / max_exact) / math.log(MAX_DISTANCE / max_exact)
                         * (N_BUCKETS - max_exact)).astype(np.int32)
    large = np.minimum(large, N_BUCKETS - 1)
    b = np.where(n < max_exact, n, large)
    assert np.all(np.diff(b) >= 0) and b[-1] == N_BUCKETS - 1
    return [int(np.argmax(b >= k)) for k in range(N_BUCKETS)]


BUCKET_START = _bucket_thresholds()
assert BUCKET_START[-1] <= KU


def _dot(a, b):
    return jnp.dot(a, b, preferred_element_type=f32)


def _dot_nt(a, b):
    return lax.dot_general(a, b, (((1,), (1,)), ((), ())), preferred_element_type=f32)


def _split(a):
    hi = a.astype(bf16)
    lo = (a - hi.astype(f32)).astype(bf16)
    return hi, lo


def _params(n_grid):
    return pltpu.CompilerParams(dimension_semantics=("arbitrary",) * n_grid,
                                vmem_limit_bytes=VMEM_LIMIT)


def _adaln_kernel(c_ref, w_ref, b_ref, o_ref):
    c = c_ref[...]
    a = c * jax.nn.sigmoid(c)
    ah, al = _split(a)
    wh, wl = _split(w_ref[...])
    o_ref[...] = _dot(ah, wh) + _dot(ah, wl) + _dot(al, wh) + b_ref[...]


def _adaln(c, w, b):
    bsz, d = c.shape
    n = w.shape[1]
    return pl.pallas_call(
        _adaln_kernel,
        grid=(n // d,),
        in_specs=[pl.BlockSpec((bsz, d), lambda j: (0, 0)),
                  pl.BlockSpec((d, d), lambda j: (0, j)),
                  pl.BlockSpec((1, d), lambda j: (0, j))],
        out_specs=pl.BlockSpec((bsz, d), lambda j: (0, j)),
        out_shape=jax.ShapeDtypeStruct((bsz, n), f32),
        compiler_params=_params(1),
        name="adaln",
    )(c, w, b.reshape(1, n))


def _bias_of_dist(dist, tbl_ref, h):
    out = jnp.full(dist.shape, tbl_ref[0, h], f32)
    for k in range(1, N_BUCKETS):
        out = jnp.where(dist >= BUCKET_START[k], tbl_ref[k, h], out)
    return jnp.where(dist >= 0, out, MASKED)


def _cmp_bias_kernel(tbl_ref, o_ref):
    h = pl.program_id(0)
    i = pl.program_id(1)
    rows, nc = o_ref.shape[1], o_ref.shape[2]
    t = i * rows + lax.broadcasted_iota(jnp.int32, (rows, nc), 0)
    j = lax.broadcasted_iota(jnp.int32, (rows, nc), 1)
    o_ref[0] = _bias_of_dist(t - (j * CMP_STRIDE + CMP_BLOCK - 1), tbl_ref, h) * LOG2E


def _near_bias_kernel(tbl_ref, o_ref):
    h = pl.program_id(0)
    r = lax.broadcasted_iota(jnp.int32, (TQ, KU + TQ), 0)
    c = lax.broadcasted_iota(jnp.int32, (TQ, KU + TQ), 1)
    o_ref[0] = (_bias_of_dist(r - c + KU, tbl_ref, h) - tbl_ref[N_BUCKETS - 1, h]) * LOG2E


def _bias_tiles(rel_bias, seq, nc_pad):
    tbl = rel_bias.astype(f32)
    smem = pl.BlockSpec(memory_space=pltpu.SMEM)
    rows = 512
    cmp_bias = pl.pallas_call(
        _cmp_bias_kernel,
        grid=(NSA_HEADS, seq // rows),
        in_specs=[smem],
        out_specs=pl.BlockSpec((1, rows, nc_pad), lambda h, i: (h, i, 0)),
        out_shape=jax.ShapeDtypeStruct((NSA_HEADS, seq, nc_pad), f32),
        compiler_params=_params(2),
        name="cmp_bias",
    )(tbl)
    near_bias = pl.pallas_call(
        _near_bias_kernel,
        grid=(NSA_HEADS,),
        in_specs=[smem],
        out_specs=pl.BlockSpec((1, TQ, KU + TQ), lambda h: (h, 0, 0)),
        out_shape=jax.ShapeDtypeStruct((NSA_HEADS, TQ, KU + TQ), f32),
        compiler_params=_params(1),
        name="near_bias",
    )(tbl)
    return cmp_bias, near_bias


_QA_W = NSA_HEADS * LANES
_KV_W = NSA_KV_HEADS * LANES
_CMP_W = NSA_KV_HEADS * HEAD_DIM
_SB_W = SB_HEADS * HEAD_DIM
_GATE_ROWS = NSA_GROUP * 3


def _layout(d_model):
    names = ["qa", "kc", "vc", "ksl", "vsl", "kwn", "vwn", "g", "qb", "kb", "vb", "ma", "mb"]
    widths = [NSA_HEADS * HEAD_DIM] + [_CMP_W] * 6 + [LANES] + [_SB_W] * 3 + [d_model, d_model]
    offs = np.concatenate([[0], np.cumsum(widths)])
    return {n: (int(offs[i]), int(offs[i + 1])) for i, n in enumerate(names)}, int(offs[-1])


def _pack_w_in(w_in, d_model):
    q_w = NSA_HEADS * HEAD_DIM
    kv_w = NSA_KV_HEADS * HEAD_DIM
    g_w = NSA_HEADS * 3
    sizes = [q_w] + [kv_w] * 6 + [g_w] + [_SB_W] * 3 + [d_model, d_model]
    offs = np.concatenate([[0], np.cumsum(sizes)])
    parts = [w_in[:, int(offs[i]):int(offs[i + 1])] for i in range(len(sizes))]
    parts[7] = jnp.pad(parts[7], ((0, 0), (0, LANES - g_w)))
    return jnp.concatenate(parts, axis=1).astype(bf16)


def _inproj_kernel(lay, x_ref, g1_ref, sc_ref, sh_ref, w_ref, gq_ref, gk_ref,
                   qa_ref, kc_ref, vc_ref, ks_ref, vs_ref, kw_ref, vw_ref, g_ref,
                   qb_ref, kb_ref, vb_ref, ma_ref, mb_ref):
    i = pl.program_id(1)
    x = x_ref[0]
    ms = jnp.mean(x * x, axis=-1, keepdims=True)
    u = (x * lax.rsqrt(ms + EPS) * g1_ref[...]) * (1.0 + sc_ref[0]) + sh_ref[0]
    ub = u.astype(bf16)

    narrow = ("kc", "vc", "ksl", "vsl", "kwn", "vwn", "g")
    narrow_lo = lay[narrow[0]][0]
    z_narrow = _dot(ub, w_ref[:, narrow_lo:lay[narrow[-1]][1]])

    def proj(name):
        lo, hi = lay[name]
        if name in narrow:
            return z_narrow[:, lo - narrow_lo:hi - narrow_lo]
        return _dot(ub, w_ref[:, lo:hi])

    rows = x.shape[0]
    lane = lax.broadcasted_iota(jnp.int32, (rows, LANES), 1)
    low = lane < HEAD_DIM

    def pair_norm(z, gain):
        sq = z * z
        ss_a = jnp.sum(jnp.where(low, sq, 0.0), axis=1, keepdims=True)
        ss_b = jnp.sum(jnp.where(low, 0.0, sq), axis=1, keepdims=True)
        inv = lax.rsqrt(jnp.where(low, ss_a, ss_b) * (1.0 / HEAD_DIM) + EPS)
        return z * inv * gain

    def spread(z, extra):
        return jnp.concatenate([jnp.where(low, z, extra),
                                jnp.where(low, pltpu.roll(z, HEAD_DIM, 1), extra)], axis=1)

    scale = HEAD_DIM ** -0.5 * LOG2E
    zq = proj("qa")
    qa_ref[0] = jnp.concatenate(
        [spread(pair_norm(zq[:, j * LANES:(j + 1) * LANES], gq_ref[...]) * scale, 0.0)
         for j in range(NSA_HEADS // 2)], axis=1).astype(bf16)
    kc_ref[0] = proj("kc")
    vc_ref[0] = proj("vc")

    tok_blk = (i * rows + lax.broadcasted_iota(jnp.int32, (rows, LANES), 0)) // SEL_BLOCK
    onehot = jnp.where(lane - HEAD_DIM == tok_blk, 1.0, 0.0)
    ones_col = jnp.where(lane == HEAD_DIM, 1.0, 0.0)

    ks_ref[0] = spread(pair_norm(proj("ksl"), gk_ref[1:2, :]), onehot).astype(bf16)
    vs_ref[0] = spread(proj("vsl"), ones_col).astype(bf16)
    kw_ref[0] = spread(pair_norm(proj("kwn"), gk_ref[2:3, :]), 0.0).astype(bf16)
    vw_ref[0] = spread(proj("vwn"), ones_col).astype(bf16)
    g_ref[0] = jax.nn.sigmoid(proj("g")).T
    qb_ref[0] = (proj("qb") * scale).astype(bf16)
    kb_ref[0] = proj("kb").astype(bf16)
    vb_ref[0] = proj("vb").astype(bf16)
    ma_ref[0] = jax.nn.sigmoid(proj("ma")).astype(bf16)
    mb_ref[0] = jax.nn.sigmoid(proj("mb")).astype(bf16)


def _inproj(x, g1, scale1, shift1, w_packed, gq, gk):
    bsz, seq, d = x.shape
    lay, width = _layout(d)
    assert w_packed.shape == (d, width)
    tok = lambda w: pl.BlockSpec((1, TM, w), lambda b, i: (b, i, 0))
    full = lambda a: pl.BlockSpec(a.shape, lambda b, i: (0,) * a.ndim,
                                  pipeline_mode=pl.Buffered(1))
    mod = pl.BlockSpec((1, 1, d), lambda b, i: (b, 0, 0))
    out_w = [(_QA_W, bf16), (_CMP_W, f32), (_CMP_W, f32), (_KV_W, bf16), (_KV_W, bf16),
             (_KV_W, bf16), (_KV_W, bf16), (LANES, f32), (_SB_W, bf16), (_SB_W, bf16),
             (_SB_W, bf16), (d, bf16), (d, bf16)]
    gates_at = 7
    out_specs = [tok(w) for w, _ in out_w]
    out_shape = [jax.ShapeDtypeStruct((bsz, seq, w), dt) for w, dt in out_w]
    out_specs[gates_at] = pl.BlockSpec((1, LANES, TM), lambda b, i: (b, 0, i))
    out_shape[gates_at] = jax.ShapeDtypeStruct((bsz, LANES, seq), f32)
    return pl.pallas_call(
        functools.partial(_inproj_kernel, lay),
        grid=(bsz, seq // TM),
        in_specs=[tok(d), full(g1), mod, mod, full(w_packed), full(gq), full(gk)],
        out_specs=out_specs,
        out_shape=out_shape,
        compiler_params=_params(2),
        name="inproj",
    )(x, g1, scale1, shift1, w_packed, gq, gk)


def _compress_kernel(xk_ref, xv_ref, pos_ref, w1k_ref, w2k_ref, w1v_ref, w2v_ref, gk_ref,
                     ko_ref, vo_ref):
    nch = xk_ref.shape[1] // CMP_STRIDE

    def mlp(x_ref, w1_ref, w2_ref):
        first = jnp.zeros((nch, LANES), f32)
        second = jnp.zeros((nch, LANES), f32)
        for l in range(CMP_STRIDE):
            xl = x_ref[0, pl.ds(l, nch, stride=CMP_STRIDE), :]
            lo = l + CMP_STRIDE
            first = first + _dot((xl + pos_ref[l:l + 1, :]).astype(bf16), w1_ref[l])
            second = second + _dot((xl + pos_ref[lo:lo + 1, :]).astype(bf16), w1_ref[lo])
        pre = first + pltpu.roll(second, nch - 1, 0)
        hid = pre * jax.nn.sigmoid(pre)
        return _dot(hid.astype(bf16), w2_ref[...])

    k = mlp(xk_ref, w1k_ref, w2k_ref)
    v = mlp(xv_ref, w1v_ref, w2v_ref)
    for h in range(NSA_KV_HEADS):
        kh = k[:, h * HEAD_DIM:(h + 1) * HEAD_DIM]
        ms = jnp.mean(kh * kh, axis=-1, keepdims=True)
        ko_ref[0, h] = (kh * lax.rsqrt(ms + EPS) * gk_ref[...]).astype(bf16)
        vh = v[:, h * HEAD_DIM:(h + 1) * HEAD_DIM]
        vo_ref[0, h] = jnp.concatenate([vh, jnp.zeros_like(vh)], axis=1).astype(bf16)


def _compress(xk, xv, pos, w1k, w2k, w1v, w2v, gk0):
    bsz, seq, width = xk.shape
    nch = seq // CMP_STRIDE
    assert width == NSA_KV_HEADS * HEAD_DIM == LANES

    def both_heads(w):
        z = jnp.zeros_like(w)
        return jnp.concatenate([jnp.concatenate([w, z], axis=2),
                                jnp.concatenate([z, w], axis=2)], axis=1).astype(bf16)

    w1 = lambda w: both_heads(w.reshape(CMP_BLOCK, HEAD_DIM, w.shape[1]))
    w2 = lambda w: both_heads(w[None])[0]
    args = (xk, xv, jnp.tile(pos, (1, NSA_KV_HEADS)), w1(w1k), w2(w2k), w1(w1v), w2(w2v), gk0)
    blk = pl.BlockSpec((1, seq, width), lambda b: (b, 0, 0))
    full = lambda a: pl.BlockSpec(a.shape, lambda b: (0,) * a.ndim)
    out = lambda w: pl.BlockSpec((1, NSA_KV_HEADS, nch, w), lambda b: (b, 0, 0, 0))
    shape = lambda w: jax.ShapeDtypeStruct((bsz, NSA_KV_HEADS, nch, w), bf16)
    return pl.pallas_call(
        _compress_kernel,
        grid=(bsz,),
        in_specs=[blk, blk] + [full(a) for a in args[2:]],
        out_specs=[out(HEAD_DIM), out(LANES)],
        out_shape=[shape(HEAD_DIM), shape(LANES)],
        compiler_params=_params(1),
        name="compress",
    )(*args)


def _flash_step(s, v, m_ref, acc_ref):
    m_prev = m_ref[...]
    m_new = jnp.maximum(m_prev, jnp.max(s, axis=1, keepdims=True))
    alpha = jnp.exp2(m_prev - m_new)
    p = jnp.exp2(s - jnp.concatenate([m_new] * (s.shape[1] // LANES), axis=1))
    acc_ref[...] = alpha * acc_ref[...] + _dot(p.astype(bf16), v)
    m_ref[...] = m_new


def _flash_init(m_ref, acc_ref):
    m_ref[...] = jnp.full(m_ref.shape, MASKED, f32)
    acc_ref[...] = jnp.zeros(acc_ref.shape, f32)


def _flash_out_t(acc_ref):
    acc_t = acc_ref[...].T
    return acc_t[:HEAD_DIM] / acc_t[HEAD_DIM:HEAD_DIM + 1]


def _nsa_kernel(n_top, q_ref, g_ref, kc_ref, vc_ref, ks_ref, vs_ref, kw_ref, vw_ref,
                bc_ref, bn_ref, ovt_ref, o_ref, qs_ref, qw_ref, oc_ref,
                m_ref, acc_ref, mw_ref, accw_ref):
    i = pl.program_id(1)
    kv_heads = range(NSA_KV_HEADS)
    rows = NSA_GROUP * TQ
    start = i * TQ
    assert WINDOW % TQ == 0 and TQ % KU == 0
    first_interior = WINDOW // TQ
    window_extra = (WINDOW - TQ) // KU - 1

    def attend(hk, q_rows_ref, k_ref, v_ref, off, width, bias, state):
        off = pl.multiple_of(off, KU)
        lanes = slice(hk * LANES, (hk + 1) * LANES)
        s = _dot_nt(q_rows_ref[hk], k_ref[0, pl.ds(off, width), lanes])
        if bias is not None:
            s = s + bias
        _flash_step(s, v_ref[0, pl.ds(off, width), lanes], state[0].at[hk], state[1].at[hk])

    def when(cond, guarded):
        return pl.when(cond) if guarded else (lambda fn: fn())

    def near_steps(q_rows_ref, k_ref, v_ref, state, guarded, extra=0):
        group = lambda hk: slice(hk * NSA_GROUP, (hk + 1) * NSA_GROUP)

        @when(i >= 1, guarded)
        def _():
            for hk in kv_heads:
                bias = bn_ref[group(hk)].reshape(rows, KU + TQ)
                if extra:
                    bias = jnp.concatenate([jnp.zeros((rows, extra * KU), f32), bias], axis=1)
                attend(hk, q_rows_ref, k_ref, v_ref, start - (1 + extra) * KU,
                       TQ + (1 + extra) * KU, bias, state)

        if guarded:
            @pl.when(i == 0)
            def _():
                for hk in kv_heads:
                    attend(hk, q_rows_ref, k_ref, v_ref, start, TQ,
                           bn_ref[group(hk), :, KU:].reshape(rows, TQ), state)

    def select(hk):
        heads = range(hk * NSA_GROUP, (hk + 1) * NSA_GROUP)
        qpad = jnp.concatenate([q_ref[0, :, g * LANES:(g + 1) * LANES] for g in heads], axis=0)
        qw_ref[hk] = qpad

        bc = bc_ref[hk * NSA_GROUP:(hk + 1) * NSA_GROUP].reshape(rows, bc_ref.shape[2])
        s_c = _dot_nt(qpad[:, :HEAD_DIM], kc_ref[0, hk]) + bc
        visible = bc > 0.5 * MASKED
        m_c = jnp.max(s_c, axis=1, keepdims=True)
        e_c = jnp.where(visible, jnp.exp2(s_c - m_c), 0.0)
        p_c = e_c / jnp.maximum(jnp.sum(e_c, axis=1, keepdims=True), 1e-30)
        oc_ref[hk] = _dot(p_c.astype(bf16), vc_ref[0, hk])

        p_sum = p_c[0:TQ]
        for g in range(1, NSA_GROUP):
            p_sum = p_sum + p_c[g * TQ:(g + 1) * TQ]
        p_hi, p_lo = _split(p_sum)
        imp = _dot_nt(ovt_ref[...], p_hi) + _dot_nt(ovt_ref[...], p_lo)
        nblk = imp.shape[0]
        blk = lax.broadcasted_iota(jnp.int32, (nblk, TQ), 0)
        cur = (start + lax.broadcasted_iota(jnp.int32, (nblk, TQ), 1)) // SEL_BLOCK
        forced = (blk == 0) | (blk == cur) | (blk == cur - 1)
        imp = jnp.where(blk > cur, NEG_BLOCK, imp + jnp.where(forced, FORCED_BONUS, 0.0))
        sub = 8
        groups = [imp[lo:lo + sub] for lo in range(0, nblk, sub)]
        ranks = [jnp.zeros((sub, TQ), f32) for _ in groups]
        row = lax.broadcasted_iota(jnp.int32, (sub, TQ), 0)
        for b2 in range(nblk):
            other = imp[b2:b2 + 1, :]
            for gi, grp in enumerate(groups):
                lo = gi * sub
                if lo > b2:
                    ranks[gi] = jnp.where(other >= grp, ranks[gi] + 1.0, ranks[gi])
                elif lo + sub - 1 < b2:
                    ranks[gi] = jnp.where(other > grp, ranks[gi] + 1.0, ranks[gi])
                else:
                    ranks[gi] = ranks[gi] + jnp.where(row + lo > b2,
                                                      jnp.where(other >= grp, 1.0, 0.0),
                                                      jnp.where(other > grp, 1.0, 0.0))
        rank = jnp.concatenate(ranks, axis=0)
        usable = (rank < n_top) & (blk <= cur)
        sel_t = jnp.where(usable, 0.0, UNSELECTED)
        sel_pad = jnp.concatenate([jnp.zeros((LANES - nblk, TQ), f32), sel_t], axis=0).T
        sel_rows = jnp.concatenate([sel_pad.astype(bf16)] * NSA_GROUP, axis=0)
        qs_ref[hk] = qpad + sel_rows

    def head(guarded):
        for hk in kv_heads:
            select(hk)

        win = (mw_ref, accw_ref)
        _flash_init(*win)

        @when(i >= first_interior, guarded)
        def _():
            r = lax.broadcasted_iota(jnp.int32, (rows, TQ), 0) & (TQ - 1)
            c = lax.broadcasted_iota(jnp.int32, (rows, TQ), 1)
            for hk in kv_heads:
                attend(hk, qw_ref, kw_ref, vw_ref, start - WINDOW, TQ,
                       jnp.where(c > r, 0.0, MASKED), win)

        near_steps(qw_ref, kw_ref, vw_ref, win, guarded, window_extra)
        _flash_init(m_ref, acc_ref)

    def tail(guarded, extra=0):
        near_steps(qs_ref, ks_ref, vs_ref, (m_ref, acc_ref), guarded, extra)
        outs = []
        for hk in kv_heads:
            o_c = oc_ref[hk].T[:HEAD_DIM]
            o_s = _flash_out_t(acc_ref.at[hk])
            o_w = _flash_out_t(accw_ref.at[hk])
            gates = g_ref[0]
            for g in range(NSA_GROUP):
                sl = slice(g * TQ, (g + 1) * TQ)
                at = hk * _GATE_ROWS + 3 * g
                outs.append(gates[at:at + 1] * o_c[:, sl]
                            + gates[at + 1:at + 2] * o_s[:, sl]
                            + gates[at + 2:at + 3] * o_w[:, sl])
        o_ref[0] = jnp.concatenate(outs, axis=0).T.astype(bf16)

    interior = i >= first_interior
    pl.when(interior)(lambda: head(False))
    pl.when(jnp.logical_not(interior))(lambda: head(True))

    sel = (m_ref, acc_ref)
    n_far = jnp.maximum(i * (TQ // KU) - 1, 0)

    def far_steps(off, n_steps):
        for step in range(n_steps):
            for hk in kv_heads:
                attend(hk, qs_ref, ks_ref, vs_ref, off + step * (2 * KU), 2 * KU, None, sel)

    def far_trip(c, carry):
        far_steps(c * (8 * KU), 4)
        return carry

    lax.fori_loop(0, n_far // 8, far_trip, 0)
    rem_off = (n_far // 8) * (8 * KU)

    @pl.when((n_far & 4) != 0)
    def _():
        far_steps(rem_off, 2)

    edge = jnp.logical_not(interior)

    @pl.when(edge & ((n_far & 2) != 0))
    def _():
        far_steps(rem_off + (n_far & 4) * KU, 1)

    @pl.when(edge & ((n_far & 1) != 0))
    def _():
        for hk in kv_heads:
            attend(hk, qs_ref, ks_ref, vs_ref, rem_off + (n_far & 6) * KU, KU, None, sel)

    for extra in range(4):
        pl.when(interior & ((n_far & 3) == extra))(functools.partial(tail, False, extra))
    pl.when(edge)(lambda: tail(True))


def _nsa(qa, gates, kcmp, vcmp, ks, vs, kw, vw, cmp_bias, near_bias, ovt):
    bsz, seq, _ = qa.shape
    nc = kcmp.shape[2]
    n_top = min(SEL_TOPK, seq // SEL_BLOCK)
    rows = NSA_GROUP * TQ
    tok = lambda w: pl.BlockSpec((1, TQ, w), lambda b, i: (b, i, 0))
    kv = pl.BlockSpec((1, seq, NSA_KV_HEADS * LANES), lambda b, i: (b, 0, 0))
    cmp = lambda a: pl.BlockSpec((1,) + a.shape[1:], lambda b, i: (b, 0, 0, 0))
    per_kv = lambda width, dt: pltpu.VMEM((NSA_KV_HEADS, rows, width), dt)
    return pl.pallas_call(
        functools.partial(_nsa_kernel, n_top),
        grid=(bsz, seq // TQ),
        in_specs=[tok(NSA_HEADS * LANES),
                  pl.BlockSpec((1, LANES, TQ), lambda b, i: (b, 0, i)),
                  cmp(kcmp), cmp(vcmp), kv, kv, kv, kv,
                  pl.BlockSpec((NSA_HEADS, TQ, nc), lambda b, i: (0, i, 0)),
                  pl.BlockSpec((NSA_HEADS, TQ, KU + TQ), lambda b, i: (0, 0, 0)),
                  pl.BlockSpec(ovt.shape, lambda b, i: (0, 0))],
        out_specs=tok(NSA_HEADS * HEAD_DIM),
        out_shape=jax.ShapeDtypeStruct((bsz, seq, NSA_HEADS * HEAD_DIM), bf16),
        scratch_shapes=[per_kv(LANES, bf16), per_kv(LANES, bf16),
                        per_kv(LANES, f32), per_kv(LANES, f32), per_kv(LANES, f32),
                        per_kv(LANES, f32), per_kv(LANES, f32)],
        compiler_params=_params(2),
        name="nsa",
    )(qa, gates, kcmp, vcmp, ks, vs, kw, vw, cmp_bias, near_bias, ovt)


def _sb_kernel(q_ref, k_ref, v_ref, tri_ref, o_ref, carry_ref, acc_ref):
    i = pl.program_id(2)
    n_heads = carry_ref.shape[0]
    lane = lax.broadcasted_iota(jnp.int32, (TS, LANES), 1)
    q_heads = []
    for pair in range(n_heads // 2):
        q = q_ref[0, :, pair * LANES:(pair + 1) * LANES]
        zero = jnp.zeros_like(q)
        q_heads += [jnp.where(lane < HEAD_DIM, q, zero), jnp.where(lane >= HEAD_DIM, q, zero)]
    pair_lanes = lambda hh: slice((hh // 2) * LANES, (hh // 2 + 1) * LANES)
    carry_ref[...] = jnp.zeros(carry_ref.shape, f32)
    acc_ref[...] = jnp.zeros(acc_ref.shape, f32)
    r = lax.broadcasted_iota(jnp.int32, (TS, TS), 0)
    c = lax.broadcasted_iota(jnp.int32, (TS, TS), 1)
    before = c < r

    def chunks(jobs):
        offs = [pl.multiple_of(off, TS) for off, _ in jobs]
        stage = []
        for (_, diagonal), off in zip(jobs, offs):
            for hh in range(n_heads):
                z = _dot_nt(q_heads[hh], k_ref[0, pl.ds(off, TS), pair_lanes(hh)])
                soft = jnp.log2(1.0 + jnp.exp2(-jnp.abs(z)))
                log_keep = jnp.minimum(-z, 0.0) - soft
                log_sig = log_keep + z
                if diagonal:
                    log_keep = jnp.where(before, log_keep, 0.0)
                hi, lo = _split(log_keep)
                later = _dot(hi, tri_ref[...]) + _dot(lo, tri_ref[...])
                stage.append((log_sig + later, jnp.sum(log_keep, axis=1, keepdims=True)))
        for hh in range(n_heads):
            carry = carry_ref[hh]
            acc = acc_ref[hh]
            for j, ((_, diagonal), off) in enumerate(zip(jobs, offs)):
                base, total = stage[n_heads * j + hh]
                a = jnp.exp2(base + jnp.concatenate([carry] * (TS // LANES), axis=1))
                if diagonal:
                    a = jnp.where(before, a, 0.0)
                acc = acc + _dot(a.astype(bf16), v_ref[0, pl.ds(off, TS), pair_lanes(hh)])
                carry = carry + total
            carry_ref[hh] = carry
            acc_ref[hh] = acc

    @pl.when(i == 0)
    def _():
        chunks([(0, True)])

    @pl.when(i >= 1)
    def _():
        chunks([(i * TS, True), ((i - 1) * TS, False)])

    def any_live():
        return jnp.max(carry_ref[...]) > EXP2_UNDERFLOW

    def more(state):
        n, live = state
        return jnp.logical_and(n < i, live)

    def older(state):
        n, _ = state
        chunks([((i - 1 - n) * TS, False)])
        return n + 1, any_live()

    lax.while_loop(more, older, (jnp.int32(1), any_live()))
    o_ref[0] = jnp.concatenate(
        [jnp.where(lane < HEAD_DIM, acc_ref[2 * pair], acc_ref[2 * pair + 1])
         for pair in range(n_heads // 2)], axis=1).astype(bf16)


SB_STEP_HEADS = 8


def _sb(qb, kb, vb, tri):
    bsz, seq, width = qb.shape
    step_w = SB_STEP_HEADS * HEAD_DIM
    q_spec = pl.BlockSpec((1, TS, step_w), lambda b, p, i: (b, i, p))
    kv_spec = pl.BlockSpec((1, seq, step_w), lambda b, p, i: (b, 0, p))
    state = pltpu.VMEM((SB_STEP_HEADS, TS, LANES), f32)
    return pl.pallas_call(
        _sb_kernel,
        grid=(bsz, width // step_w, seq // TS),
        in_specs=[q_spec, kv_spec, kv_spec, pl.BlockSpec(tri.shape, lambda b, p, i: (0, 0))],
        out_specs=q_spec,
        out_shape=jax.ShapeDtypeStruct((bsz, seq, width), bf16),
        scratch_shapes=[state, state],
        compiler_params=_params(3),
        name="sb",
    )(qb, kb, vb, tri)


def _post_kernel(ya_ref, yb_ref, ma_ref, mb_ref, x_ref, gate1_ref, g2_ref, sc_ref, sh_ref,
                 gate_ref, wa_ref, wb_ref, wo_ref, w1_ref, w2_ref, o_ref):
    y_a = _dot(ya_ref[0], wa_ref[...])
    y_b = _dot(yb_ref[0], wb_ref[...])
    mixed = ma_ref[0].astype(f32) * y_a + mb_ref[0].astype(f32) * y_b
    hres = x_ref[0] + gate1_ref[0] * _dot(mixed.astype(bf16), wo_ref[...])
    d = hres.shape[1]
    ms = jnp.mean(hres * hres, axis=-1, keepdims=True)
    u = (hres * lax.rsqrt(ms + EPS) * g2_ref[...]) * (1.0 + sc_ref[0]) + sh_ref[0]
    ub = u.astype(bf16)
    ff = jnp.zeros(hres.shape, f32)
    for c in range(w1_ref.shape[1] // d):
        hid = jnp.maximum(_dot(ub, w1_ref[:, c * d:(c + 1) * d]), 0.0)
        ff = ff + _dot((hid * hid).astype(bf16), w2_ref[c * d:(c + 1) * d, :])
    o_ref[0] = hres + gate_ref[0] * ff


def _post(ya, yb, ma, mb, x, gate1, g2, scale2, shift2, gate2, wa, wb, wo, w1, w2):
    bsz, seq, d = x.shape
    tok = lambda w: pl.BlockSpec((1, TM, w), lambda b, i: (b, i, 0))
    mod = pl.BlockSpec((1, 1, d), lambda b, i: (b, 0, 0))
    const = lambda a: pl.BlockSpec(a.shape, lambda b, i: (0,) * a.ndim,
                                   pipeline_mode=pl.Buffered(1))
    return pl.pallas_call(
        _post_kernel,
        grid=(bsz, seq // TM),
        in_specs=[tok(ya.shape[2]), tok(yb.shape[2]), tok(d), tok(d), tok(d), mod,
                  const(g2), mod, mod, mod, const(wa), const(wb), const(wo), const(w1), const(w2)],
        out_specs=tok(d),
        out_shape=jax.ShapeDtypeStruct((bsz, seq, d), f32),
        compiler_params=_params(2),
        name="post",
    )(ya, yb, ma, mb, x, gate1, g2, scale2, shift2, gate2, wa, wb, wo, w1, w2)


def _overlap_t(nc_pad, nsel_pad, nc, nsel):
    c_start = np.arange(nc_pad) * CMP_STRIDE
    s_start = np.arange(nsel_pad) * SEL_BLOCK
    ov = (np.minimum(c_start[None, :] + CMP_BLOCK, s_start[:, None] + SEL_BLOCK)
          - np.maximum(c_start[None, :], s_start[:, None]))
    ov = np.clip(ov, 0, CMP_BLOCK).astype(np.float32) / CMP_BLOCK
    ov[nsel:, :] = 0.0
    ov[:, nc:] = 0.0
    return ov


def _layer(h, mod, rel_tiles, p):
    bsz, seq, d = h.shape
    shift1, scale1, gate1, shift2, scale2, gate2 = [
        mod[:, k * d:(k + 1) * d].reshape(bsz, 1, d) for k in range(6)]
    cmp_bias, near_bias = rel_tiles

    gq = jnp.tile(p["q_norm_g"], 2).reshape(1, LANES)
    gk = jnp.tile(p["k_norm_g"], (1, 2))

    (qa, kc, vc, ks, vs, kw, vw, gates, qb, kb, vb, ma, mb) = _inproj(
        h, p["norm1_g"].reshape(1, d), scale1, shift1, _pack_w_in(p["w_in"], d), gq, gk)

    nch = seq // CMP_STRIDE
    kcmp, vcmp = _compress(kc, vc, p["cmp_pos"], p["cmp_k_w1"], p["cmp_k_w2"],
                           p["cmp_v_w1"], p["cmp_v_w2"], p["k_norm_g"][0].reshape(1, HEAD_DIM))

    nc = (seq - CMP_BLOCK) // CMP_STRIDE + 1
    ovt = jnp.asarray(_overlap_t(nch, HEAD_DIM, nc, seq // SEL_BLOCK), bf16)
    y_nsa = _nsa(qa, gates, kcmp, vcmp, ks, vs, kw, vw, cmp_bias, near_bias, ovt)

    y_sb = _sb(qb, kb, vb, jnp.asarray(np.tril(np.ones((TS, TS)), -1), bf16))

    return _post(y_nsa, y_sb, ma, mb, h, gate1, p["norm2_g"].reshape(1, d), scale2, shift2, gate2,
                 p["w_up_nsa"].astype(bf16), p["w_up_sb"].astype(bf16), p["w_out"].astype(bf16),
                 p["mlp_w1"].astype(bf16), p["mlp_w2"].astype(bf16))


def kernel(x, c, rel_bias, ada_w, ada_b, norm1_g, norm2_g, w_in, cmp_pos, cmp_k_w1, cmp_k_w2,
           cmp_v_w1, cmp_v_w2, q_norm_g, k_norm_g, w_up_nsa, w_up_sb, w_out, mlp_w1, mlp_w2):
    bsz, seq, d = x.shape
    assert seq % TM == 0 and seq // SEL_BLOCK <= HEAD_DIM and seq >= WINDOW + TQ
    assert CMP_BLOCK == 2 * CMP_STRIDE and KU % SEL_BLOCK == 0
    tbl = rel_bias.astype(f32)
    rel_tiles = _bias_tiles(tbl, seq, seq // CMP_STRIDE)
    stacked = dict(norm1_g=norm1_g, norm2_g=norm2_g, w_in=w_in, cmp_pos=cmp_pos,
                   cmp_k_w1=cmp_k_w1, cmp_k_w2=cmp_k_w2, cmp_v_w1=cmp_v_w1, cmp_v_w2=cmp_v_w2,
                   q_norm_g=q_norm_g, k_norm_g=k_norm_g, w_up_nsa=w_up_nsa, w_up_sb=w_up_sb,
                   w_out=w_out, mlp_w1=mlp_w1, mlp_w2=mlp_w2)
    h = x
    for layer in range(ada_w.shape[0]):
        mod = _adaln(c, ada_w[layer], ada_b[layer])
        h = _layer(h, mod, rel_tiles, {k: v[layer] for k, v in stacked.items()})
    return h
```

```python
import functools
import math

import numpy as np
import jax
import jax.numpy as jnp
from jax import lax
from jax.experimental import pallas as pl
from jax.experimental.pallas import tpu as pltpu

f32 = jnp.float32
bf16 = jnp.bfloat16

HEAD_DIM = 64
NSA_HEADS = 8
NSA_KV_HEADS = 2
NSA_GROUP = NSA_HEADS // NSA_KV_HEADS
SB_HEADS = 8
CMP_BLOCK = 32
CMP_STRIDE = 16
SEL_BLOCK = 64
SEL_TOPK = 16
WINDOW = 512
N_BUCKETS = 32
MAX_DISTANCE = 128
EPS = 1e-6
FORCED_BONUS = 1e4
NEG_BLOCK = -1e9

LANES = 128
MASKED = -1e30
UNSELECTED = -1e9
LOG2E = math.log2(math.e)
EXP2_UNDERFLOW = -150.0
VMEM_LIMIT = 56 * 1024 * 1024

TQ = 256
KU = 128
TS = 256
TM = 512


def _bucket_thresholds():
    n = np.arange(0, 4 * MAX_DISTANCE)
    max_exact = N_BUCKETS // 2
    nf = np.maximum(n, 1).astype(np.float32)
    large = max_exact + (np.log(nf / max_exact) / math.log(MAX_DISTANCE / max_exact)
                         * (N_BUCKETS - max_exact)).astype(np.int32)
    large = np.minimum(large, N_BUCKETS - 1)
    b = np.where(n < max_exact, n, large)
    assert np.all(np.diff(b) >= 0) and b[-1] == N_BUCKETS - 1
    return [int(np.argmax(b >= k)) for k in range(N_BUCKETS)]


BUCKET_START = _bucket_thresholds()
assert BUCKET_START[-1] <= KU


def _dot(a, b):
    return jnp.dot(a, b, preferred_element_type=f32)


def _dot_nt(a, b):
    return lax.dot_general(a, b, (((1,), (1,)), ((), ())), preferred_element_type=f32)


def _split(a):
    hi = a.astype(bf16)
    lo = (a - hi.astype(f32)).astype(bf16)
    return hi, lo


def _params(n_grid):
    return pltpu.CompilerParams(dimension_semantics=("arbitrary",) * n_grid,
                                vmem_limit_bytes=VMEM_LIMIT)


def _adaln_kernel(c_ref, w_ref, b_ref, o_ref):
    c = c_ref[...]
    a = c * jax.nn.sigmoid(c)
    ah, al = _split(a)
    wh, wl = _split(w_ref[...])
    o_ref[...] = _dot(ah, wh) + _dot(ah, wl) + _dot(al, wh) + b_ref[...]


def _adaln(c, w, b):
    bsz, d = c.shape
    n = w.shape[1]
    return pl.pallas_call(
        _adaln_kernel,
        grid=(n // d,),
        in_specs=[pl.BlockSpec((bsz, d), lambda j: (0, 0)),
                  pl.BlockSpec((d, d), lambda j: (0, j)),
                  pl.BlockSpec((1, d), lambda j: (0, j))],
        out_specs=pl.BlockSpec((bsz, d), lambda j: (0, j)),
        out_shape=jax.ShapeDtypeStruct((bsz, n), f32),
        compiler_params=_params(1),
        name="adaln",
    )(c, w, b.reshape(1, n))


def _bias_of_dist(dist, tbl_ref, h):
    out = jnp.full(dist.shape, tbl_ref[0, h], f32)
    for k in range(1, N_BUCKETS):
        out = jnp.where(dist >= BUCKET_START[k], tbl_ref[k, h], out)
    return jnp.where(dist >= 0, out, MASKED)


def _cmp_bias_kernel(tbl_ref, o_ref):
    h = pl.program_id(0)
    i = pl.program_id(1)
    rows, nc = o_ref.shape[1], o_ref.shape[2]
    t = i * rows + lax.broadcasted_iota(jnp.int32, (rows, nc), 0)
    j = lax.broadcasted_iota(jnp.int32, (rows, nc), 1)
    o_ref[0] = _bias_of_dist(t - (j * CMP_STRIDE + CMP_BLOCK - 1), tbl_ref, h) * LOG2E


def _near_bias_kernel(tbl_ref, o_ref):
    h = pl.program_id(0)
    r = lax.broadcasted_iota(jnp.int32, (TQ, KU + TQ), 0)
    c = lax.broadcasted_iota(jnp.int32, (TQ, KU + TQ), 1)
    o_ref[0] = (_bias_of_dist(r - c + KU, tbl_ref, h) - tbl_ref[N_BUCKETS - 1, h]) * LOG2E


def _bias_tiles(rel_bias, seq, nc_pad):
    tbl = rel_bias.astype(f32)
    smem = pl.BlockSpec(memory_space=pltpu.SMEM)
    rows = 512
    cmp_bias = pl.pallas_call(
        _cmp_bias_kernel,
        grid=(NSA_HEADS, seq // rows),
        in_specs=[smem],
        out_specs=pl.BlockSpec((1, rows, nc_pad), lambda h, i: (h, i, 0)),
        out_shape=jax.ShapeDtypeStruct((NSA_HEADS, seq, nc_pad), f32),
        compiler_params=_params(2),
        name="cmp_bias",
    )(tbl)
    near_bias = pl.pallas_call(
        _near_bias_kernel,
        grid=(NSA_HEADS,),
        in_specs=[smem],
        out_specs=pl.BlockSpec((1, TQ, KU + TQ), lambda h: (h, 0, 0)),
        out_shape=jax.ShapeDtypeStruct((NSA_HEADS, TQ, KU + TQ), f32),
        compiler_params=_params(1),
        name="near_bias",
    )(tbl)
    return cmp_bias, near_bias


_QA_W = NSA_HEADS * LANES
_KV_W = NSA_KV_HEADS * LANES
_CMP_W = NSA_KV_HEADS * HEAD_DIM
_SB_W = SB_HEADS * HEAD_DIM
_GATE_ROWS = NSA_GROUP * 3


def _layout(d_model):
    names = ["qa", "kc", "vc", "ksl", "vsl", "kwn", "vwn", "g", "qb", "kb", "vb", "ma", "mb"]
    widths = [NSA_HEADS * HEAD_DIM] + [_CMP_W] * 6 + [LANES] + [_SB_W] * 3 + [d_model, d_model]
    offs = np.concatenate([[0], np.cumsum(widths)])
    return {n: (int(offs[i]), int(offs[i + 1])) for i, n in enumerate(names)}, int(offs[-1])


def _pack_w_in(w_in, d_model):
    q_w = NSA_HEADS * HEAD_DIM
    kv_w = NSA_KV_HEADS * HEAD_DIM
    g_w = NSA_HEADS * 3
    sizes = [q_w] + [kv_w] * 6 + [g_w] + [_SB_W] * 3 + [d_model, d_model]
    offs = np.concatenate([[0], np.cumsum(sizes)])
    parts = [w_in[:, int(offs[i]):int(offs[i + 1])] for i in range(len(sizes))]
    parts[7] = jnp.pad(parts[7], ((0, 0), (0, LANES - g_w)))
    return jnp.concatenate(parts, axis=1).astype(bf16)


def _inproj_kernel(lay, x_ref, g1_ref, sc_ref, sh_ref, w_ref, gq_ref, gk_ref,
                   qa_ref, kc_ref, vc_ref, ks_ref, vs_ref, kw_ref, vw_ref, g_ref,
                   qb_ref, kb_ref, vb_ref, ma_ref, mb_ref):
    i = pl.program_id(1)
    x = x_ref[0]
    ms = jnp.mean(x * x, axis=-1, keepdims=True)
    u = (x * lax.rsqrt(ms + EPS) * g1_ref[...]) * (1.0 + sc_ref[0]) + sh_ref[0]
    ub = u.astype(bf16)

    narrow = ("kc", "vc", "ksl", "vsl", "kwn", "vwn", "g")
    narrow_lo = lay[narrow[0]][0]
    z_narrow = _dot(ub, w_ref[:, narrow_lo:lay[narrow[-1]][1]])

    def proj(name):
        lo, hi = lay[name]
        if name in narrow:
            return z_narrow[:, lo - narrow_lo:hi - narrow_lo]
        return _dot(ub, w_ref[:, lo:hi])

    rows = x.shape[0]
    lane = lax.broadcasted_iota(jnp.int32, (rows, LANES), 1)
    low = lane < HEAD_DIM

    def pair_norm(z, gain):
        sq = z * z
        ss_a = jnp.sum(jnp.where(low, sq, 0.0), axis=1, keepdims=True)
        ss_b = jnp.sum(jnp.where(low, 0.0, sq), axis=1, keepdims=True)
        inv = lax.rsqrt(jnp.where(low, ss_a, ss_b) * (1.0 / HEAD_DIM) + EPS)
        return z * inv * gain

    def spread(z, extra):
        return jnp.concatenate([jnp.where(low, z, extra),
                                jnp.where(low, pltpu.roll(z, HEAD_DIM, 1), extra)], axis=1)

    scale = HEAD_DIM ** -0.5 * LOG2E
    zq = proj("qa")
    qa_ref[0] = jnp.concatenate(
        [spread(pair_norm(zq[:, j * LANES:(j + 1) * LANES], gq_ref[...]) * scale, 0.0)
         for j in range(NSA_HEADS // 2)], axis=1).astype(bf16)
    kc_ref[0] = proj("kc")
    vc_ref[0] = proj("vc")

    tok_blk = (i * rows + lax.broadcasted_iota(jnp.int32, (rows, LANES), 0)) // SEL_BLOCK
    onehot = jnp.where(lane - HEAD_DIM == tok_blk, 1.0, 0.0)
    ones_col = jnp.where(lane == HEAD_DIM, 1.0, 0.0)

    ks_ref[0] = spread(pair_norm(proj("ksl"), gk_ref[1:2, :]), onehot).astype(bf16)
    vs_ref[0] = spread(proj("vsl"), ones_col).astype(bf16)
    kw_ref[0] = spread(pair_norm(proj("kwn"), gk_ref[2:3, :]), 0.0).astype(bf16)
    vw_ref[0] = spread(proj("vwn"), ones_col).astype(bf16)
    g_ref[0] = jax.nn.sigmoid(proj("g")).T
    qb_ref[0] = (proj("qb") * scale).astype(bf16)
    kb_ref[0] = proj("kb").astype(bf16)
    vb_ref[0] = proj("vb").astype(bf16)
    ma_ref[0] = jax.nn.sigmoid(proj("ma")).astype(bf16)
    mb_ref[0] = jax.nn.sigmoid(proj("mb")).astype(bf16)


def _inproj(x, g1, scale1, shift1, w_packed, gq, gk):
    bsz, seq, d = x.shape
    lay, width = _layout(d)
    assert w_packed.shape == (d, width)
    tok = lambda w: pl.BlockSpec((1, TM, w), lambda b, i: (b, i, 0))
    full = lambda a: pl.BlockSpec(a.shape, lambda b, i: (0,) * a.ndim,
                                  pipeline_mode=pl.Buffered(1))
    mod = pl.BlockSpec((1, 1, d), lambda b, i: (b, 0, 0))
    out_w = [(_QA_W, bf16), (_CMP_W, f32), (_CMP_W, f32), (_KV_W, bf16), (_KV_W, bf16),
             (_KV_W, bf16), (_KV_W, bf16), (LANES, f32), (_SB_W, bf16), (_SB_W, bf16),
             (_SB_W, bf16), (d, bf16), (d, bf16)]
    gates_at = 7
    out_specs = [tok(w) for w, _ in out_w]
    out_shape = [jax.ShapeDtypeStruct((bsz, seq, w), dt) for w, dt in out_w]
    out_specs[gates_at] = pl.BlockSpec((1, LANES, TM), lambda b, i: (b, 0, i))
    out_shape[gates_at] = jax.ShapeDtypeStruct((bsz, LANES, seq), f32)
    return pl.pallas_call(
        functools.partial(_inproj_kernel, lay),
        grid=(bsz, seq // TM),
        in_specs=[tok(d), full(g1), mod, mod, full(w_packed), full(gq), full(gk)],
        out_specs=out_specs,
        out_shape=out_shape,
        compiler_params=_params(2),
        name="inproj",
    )(x, g1, scale1, shift1, w_packed, gq, gk)


def _compress_kernel(xk_ref, xv_ref, pos_ref, w1k_ref, w2k_ref, w1v_ref, w2v_ref, gk_ref,
                     ko_ref, vo_ref):
    nch = xk_ref.shape[1] // CMP_STRIDE

    def mlp(x_ref, w1_ref, w2_ref):
        first = jnp.zeros((nch, LANES), f32)
        second = jnp.zeros((nch, LANES), f32)
        for l in range(CMP_STRIDE):
            xl = x_ref[0, pl.ds(l, nch, stride=CMP_STRIDE), :]
            lo = l + CMP_STRIDE
            first = first + _dot((xl + pos_ref[l:l + 1, :]).astype(bf16), w1_ref[l])
            second = second + _dot((xl + pos_ref[lo:lo + 1, :]).astype(bf16), w1_ref[lo])
        pre = first + pltpu.roll(second, nch - 1, 0)
        hid = pre * jax.nn.sigmoid(pre)
        return _dot(hid.astype(bf16), w2_ref[...])

    k = mlp(xk_ref, w1k_ref, w2k_ref)
    v = mlp(xv_ref, w1v_ref, w2v_ref)
    for h in range(NSA_KV_HEADS):
        kh = k[:, h * HEAD_DIM:(h + 1) * HEAD_DIM]
        ms = jnp.mean(kh * kh, axis=-1, keepdims=True)
        ko_ref[0, h] = (kh * lax.rsqrt(ms + EPS) * gk_ref[...]).astype(bf16)
        vh = v[:, h * HEAD_DIM:(h + 1) * HEAD_DIM]
        vo_ref[0, h] = jnp.concatenate([vh, jnp.zeros_like(vh)], axis=1).astype(bf16)


def _compress(xk, xv, pos, w1k, w2k, w1v, w2v, gk0):
    bsz, seq, width = xk.shape
    nch = seq // CMP_STRIDE
    assert width == NSA_KV_HEADS * HEAD_DIM == LANES

    def both_heads(w):
        z = jnp.zeros_like(w)
        return jnp.concatenate([jnp.concatenate([w, z], axis=2),
                                jnp.concatenate([z, w], axis=2)], axis=1).astype(bf16)

    w1 = lambda w: both_heads(w.reshape(CMP_BLOCK, HEAD_DIM, w.shape[1]))
    w2 = lambda w: both_heads(w[None])[0]
    args = (xk, xv, jnp.tile(pos, (1, NSA_KV_HEADS)), w1(w1k), w2(w2k), w1(w1v), w2(w2v), gk0)
    blk = pl.BlockSpec((1, seq, width), lambda b: (b, 0, 0))
    full = lambda a: pl.BlockSpec(a.shape, lambda b: (0,) * a.ndim)
    out = lambda w: pl.BlockSpec((1, NSA_KV_HEADS, nch, w), lambda b: (b, 0, 0, 0))
    shape = lambda w: jax.ShapeDtypeStruct((bsz, NSA_KV_HEADS, nch, w), bf16)
    return pl.pallas_call(
        _compress_kernel,
        grid=(bsz,),
        in_specs=[blk, blk] + [full(a) for a in args[2:]],
        out_specs=[out(HEAD_DIM), out(LANES)],
        out_shape=[shape(HEAD_DIM), shape(LANES)],
        compiler_params=_params(1),
        name="compress",
    )(*args)


def _flash_step(s, v, m_ref, acc_ref):
    m_prev = m_ref[...]
    m_new = jnp.maximum(m_prev, jnp.max(s, axis=1, keepdims=True))
    alpha = jnp.exp2(m_prev - m_new)
    p = jnp.exp2(s - jnp.concatenate([m_new] * (s.shape[1] // LANES), axis=1))
    acc_ref[...] = alpha * acc_ref[...] + _dot(p.astype(bf16), v)
    m_ref[...] = m_new


def _flash_init(m_ref, acc_ref):
    m_ref[...] = jnp.full(m_ref.shape, MASKED, f32)
    acc_ref[...] = jnp.zeros(acc_ref.shape, f32)


def _flash_out_t(acc_ref):
    acc_t = acc_ref[...].T
    return acc_t[:HEAD_DIM] / acc_t[HEAD_DIM:HEAD_DIM + 1]


def _nsa_kernel(n_top, q_ref, g_ref, kc_ref, vc_ref, ks_ref, vs_ref, kw_ref, vw_ref,
                bc_ref, bn_ref, ovt_ref, o_ref, qs_ref, qw_ref, oc_ref,
                m_ref, acc_ref, mw_ref, accw_ref):
    i = pl.program_id(1)
    kv_heads = range(NSA_KV_HEADS)
    rows = NSA_GROUP * TQ
    start = i * TQ
    assert WINDOW % TQ == 0 and TQ % KU == 0
    first_interior = WINDOW // TQ
    window_extra = (WINDOW - TQ) // KU - 1

    def attend(hk, q_rows_ref, k_ref, v_ref, off, width, bias, state):
        off = pl.multiple_of(off, KU)
        lanes = slice(hk * LANES, (hk + 1) * LANES)
        s = _dot_nt(q_rows_ref[hk], k_ref[0, pl.ds(off, width), lanes])
        if bias is not None:
            s = s + bias
        _flash_step(s, v_ref[0, pl.ds(off, width), lanes], state[0].at[hk], state[1].at[hk])

    def when(cond, guarded):
        return pl.when(cond) if guarded else (lambda fn: fn())

    def near_steps(q_rows_ref, k_ref, v_ref, state, guarded, extra=0):
        group = lambda hk: slice(hk * NSA_GROUP, (hk + 1) * NSA_GROUP)

        @when(i >= 1, guarded)
        def _():
            for hk in kv_heads:
                bias = bn_ref[group(hk)].reshape(rows, KU + TQ)
                if extra:
                    bias = jnp.concatenate([jnp.zeros((rows, extra * KU), f32), bias], axis=1)
                attend(hk, q_rows_ref, k_ref, v_ref, start - (1 + extra) * KU,
                       TQ + (1 + extra) * KU, bias, state)

        if guarded:
            @pl.when(i == 0)
            def _():
                for hk in kv_heads:
                    attend(hk, q_rows_ref, k_ref, v_ref, start, TQ,
                           bn_ref[group(hk), :, KU:].reshape(rows, TQ), state)

    def select(hk):
        heads = range(hk * NSA_GROUP, (hk + 1) * NSA_GROUP)
        qpad = jnp.concatenate([q_ref[0, :, g * LANES:(g + 1) * LANES] for g in heads], axis=0)
        qw_ref[hk] = qpad

        bc = bc_ref[hk * NSA_GROUP:(hk + 1) * NSA_GROUP].reshape(rows, bc_ref.shape[2])
        s_c = _dot_nt(qpad[:, :HEAD_DIM], kc_ref[0, hk]) + bc
        visible = bc > 0.5 * MASKED
        m_c = jnp.max(s_c, axis=1, keepdims=True)
        e_c = jnp.where(visible, jnp.exp2(s_c - m_c), 0.0)
        p_c = e_c / jnp.maximum(jnp.sum(e_c, axis=1, keepdims=True), 1e-30)
        oc_ref[hk] = _dot(p_c.astype(bf16), vc_ref[0, hk])

        p_sum = p_c[0:TQ]
        for g in range(1, NSA_GROUP):
            p_sum = p_sum + p_c[g * TQ:(g + 1) * TQ]
        p_hi, p_lo = _split(p_sum)
        imp = _dot_nt(ovt_ref[...], p_hi) + _dot_nt(ovt_ref[...], p_lo)
        nblk = imp.shape[0]
        blk = lax.broadcasted_iota(jnp.int32, (nblk, TQ), 0)
        cur = (start + lax.broadcasted_iota(jnp.int32, (nblk, TQ), 1)) // SEL_BLOCK
        forced = (blk == 0) | (blk == cur) | (blk == cur - 1)
        imp = jnp.where(blk > cur, NEG_BLOCK, imp + jnp.where(forced, FORCED_BONUS, 0.0))
        sub = 8
        groups = [imp[lo:lo + sub] for lo in range(0, nblk, sub)]
        ranks = [jnp.zeros((sub, TQ), f32) for _ in groups]
        row = lax.broadcasted_iota(jnp.int32, (sub, TQ), 0)
        for b2 in range(nblk):
            other = imp[b2:b2 + 1, :]
            for gi, grp in enumerate(groups):
                lo = gi * sub
                if lo > b2:
                    ranks[gi] = jnp.where(other >= grp, ranks[gi] + 1.0, ranks[gi])
                elif lo + sub - 1 < b2:
                    ranks[gi] = jnp.where(other > grp, ranks[gi] + 1.0, ranks[gi])
                else:
                    ranks[gi] = ranks[gi] + jnp.where(row + lo > b2,
                                                      jnp.where(other >= grp, 1.0, 0.0),
                                                      jnp.where(other > grp, 1.0, 0.0))
        rank = jnp.concatenate(ranks, axis=0)
        usable = (rank < n_top) & (blk <= cur)
        sel_t = jnp.where(usable, 0.0, UNSELECTED)
        sel_pad = jnp.concatenate([jnp.zeros((LANES - nblk, TQ), f32), sel_t], axis=0).T
        sel_rows = jnp.concatenate([sel_pad.astype(bf16)] * NSA_GROUP, axis=0)
        qs_ref[hk] = qpad + sel_rows

    def head(guarded):
        for hk in kv_heads:
            select(hk)

        win = (mw_ref, accw_ref)
        _flash_init(*win)

        @when(i >= first_interior, guarded)
        def _():
            r = lax.broadcasted_iota(jnp.int32, (rows, TQ), 0) & (TQ - 1)
            c = lax.broadcasted_iota(jnp.int32, (rows, TQ), 1)
            for hk in kv_heads:
                attend(hk, qw_ref, kw_ref, vw_ref, start - WINDOW, TQ,
                       jnp.where(c > r, 0.0, MASKED), win)

        near_steps(qw_ref, kw_ref, vw_ref, win, guarded, window_extra)
        _flash_init(m_ref, acc_ref)

    def tail(guarded, extra=0):
        near_steps(qs_ref, ks_ref, vs_ref, (m_ref, acc_ref), guarded, extra)
        outs = []
        for hk in kv_heads:
            o_c = oc_ref[hk].T[:HEAD_DIM]
            o_s = _flash_out_t(acc_ref.at[hk])
            o_w = _flash_out_t(accw_ref.at[hk])
            gates = g_ref[0]
            for g in range(NSA_GROUP):
                sl = slice(g * TQ, (g + 1) * TQ)
                at = hk * _GATE_ROWS + 3 * g
                outs.append(gates[at:at + 1] * o_c[:, sl]
                            + gates[at + 1:at + 2] * o_s[:, sl]
                            + gates[at + 2:at + 3] * o_w[:, sl])
        o_ref[0] = jnp.concatenate(outs, axis=0).T.astype(bf16)

    interior = i >= first_interior
    pl.when(interior)(lambda: head(False))
    pl.when(jnp.logical_not(interior))(lambda: head(True))

    sel = (m_ref, acc_ref)
    n_far = jnp.maximum(i * (TQ // KU) - 1, 0)

    def far_steps(off, n_steps):
        for step in range(n_steps):
            for hk in kv_heads:
                attend(hk, qs_ref, ks_ref, vs_ref, off + step * (2 * KU), 2 * KU, None, sel)

    def far_trip(c, carry):
        far_steps(c * (8 * KU), 4)
        return carry

    lax.fori_loop(0, n_far // 8, far_trip, 0)
    rem_off = (n_far // 8) * (8 * KU)

    @pl.when((n_far & 4) != 0)
    def _():
        far_steps(rem_off, 2)

    edge = jnp.logical_not(interior)

    @pl.when(edge & ((n_far & 2) != 0))
    def _():
        far_steps(rem_off + (n_far & 4) * KU, 1)

    @pl.when(edge & ((n_far & 1) != 0))
    def _():
        for hk in kv_heads:
            attend(hk, qs_ref, ks_ref, vs_ref, rem_off + (n_far & 6) * KU, KU, None, sel)

    for extra in range(4):
        pl.when(interior & ((n_far & 3) == extra))(functools.partial(tail, False, extra))
    pl.when(edge)(lambda: tail(True))


def _nsa(qa, gates, kcmp, vcmp, ks, vs, kw, vw, cmp_bias, near_bias, ovt):
    bsz, seq, _ = qa.shape
    nc = kcmp.shape[2]
    n_top = min(SEL_TOPK, seq // SEL_BLOCK)
    rows = NSA_GROUP * TQ
    tok = lambda w: pl.BlockSpec((1, TQ, w), lambda b, i: (b, i, 0))
    kv = pl.BlockSpec((1, seq, NSA_KV_HEADS * LANES), lambda b, i: (b, 0, 0))
    cmp = lambda a: pl.BlockSpec((1,) + a.shape[1:], lambda b, i: (b, 0, 0, 0))
    per_kv = lambda width, dt: pltpu.VMEM((NSA_KV_HEADS, rows, width), dt)
    return pl.pallas_call(
        functools.partial(_nsa_kernel, n_top),
        grid=(bsz, seq // TQ),
        in_specs=[tok(NSA_HEADS * LANES),
                  pl.BlockSpec((1, LANES, TQ), lambda b, i: (b, 0, i)),
                  cmp(kcmp), cmp(vcmp), kv, kv, kv, kv,
                  pl.BlockSpec((NSA_HEADS, TQ, nc), lambda b, i: (0, i, 0)),
                  pl.BlockSpec((NSA_HEADS, TQ, KU + TQ), lambda b, i: (0, 0, 0)),
                  pl.BlockSpec(ovt.shape, lambda b, i: (0, 0))],
        out_specs=tok(NSA_HEADS * HEAD_DIM),
        out_shape=jax.ShapeDtypeStruct((bsz, seq, NSA_HEADS * HEAD_DIM), bf16),
        scratch_shapes=[per_kv(LANES, bf16), per_kv(LANES, bf16),
                        per_kv(LANES, f32), per_kv(LANES, f32), per_kv(LANES, f32),
                        per_kv(LANES, f32), per_kv(LANES, f32)],
        compiler_params=_params(2),
        name="nsa",
    )(qa, gates, kcmp, vcmp, ks, vs, kw, vw, cmp_bias, near_bias, ovt)


def _sb_kernel(q_ref, k_ref, v_ref, tri_ref, o_ref, carry_ref, acc_ref):
    i = pl.program_id(2)
    n_heads = carry_ref.shape[0]
    lane = lax.broadcasted_iota(jnp.int32, (TS, LANES), 1)
    q_heads = []
    for pair in range(n_heads // 2):
        q = q_ref[0, :, pair * LANES:(pair + 1) * LANES]
        zero = jnp.zeros_like(q)
        q_heads += [jnp.where(lane < HEAD_DIM, q, zero), jnp.where(lane >= HEAD_DIM, q, zero)]
    pair_lanes = lambda hh: slice((hh // 2) * LANES, (hh // 2 + 1) * LANES)
    carry_ref[...] = jnp.zeros(carry_ref.shape, f32)
    acc_ref[...] = jnp.zeros(acc_ref.shape, f32)
    r = lax.broadcasted_iota(jnp.int32, (TS, TS), 0)
    c = lax.broadcasted_iota(jnp.int32, (TS, TS), 1)
    before = c < r

    def chunks(jobs):
        offs = [pl.multiple_of(off, TS) for off, _ in jobs]
        stage = []
        for (_, diagonal), off in zip(jobs, offs):
            for hh in range(n_heads):
                z = _dot_nt(q_heads[hh], k_ref[0, pl.ds(off, TS), pair_lanes(hh)])
                soft = jnp.log2(1.0 + jnp.exp2(-jnp.abs(z)))
                log_keep = jnp.minimum(-z, 0.0) - soft
                log_sig = log_keep + z
                if diagonal:
                    log_keep = jnp.where(before, log_keep, 0.0)
                later = _dot(log_keep.astype(bf16), tri_ref[...])
                stage.append((log_sig + later, jnp.sum(log_keep, axis=1, keepdims=True)))
        for hh in range(n_heads):
            carry = carry_ref[hh]
            acc = acc_ref[hh]
            for j, ((_, diagonal), off) in enumerate(zip(jobs, offs)):
                base, total = stage[n_heads * j + hh]
                a = jnp.exp2(base + jnp.concatenate([carry] * (TS // LANES), axis=1))
                if diagonal:
                    a = jnp.where(before, a, 0.0)
                acc = acc + _dot(a.astype(bf16), v_ref[0, pl.ds(off, TS), pair_lanes(hh)])
                carry = carry + total
            carry_ref[hh] = carry
            acc_ref[hh] = acc

    @pl.when(i == 0)
    def _():
        chunks([(0, True)])

    @pl.when(i >= 1)
    def _():
        chunks([(i * TS, True), ((i - 1) * TS, False)])

    def any_live():
        return jnp.max(carry_ref[...]) > EXP2_UNDERFLOW

    def more(state):
        n, live = state
        return jnp.logical_and(n < i, live)

    def older(state):
        n, _ = state
        chunks([((i - 1 - n) * TS, False)])
        return n + 1, any_live()

    lax.while_loop(more, older, (jnp.int32(1), any_live()))
    o_ref[0] = jnp.concatenate(
        [jnp.where(lane < HEAD_DIM, acc_ref[2 * pair], acc_ref[2 * pair + 1])
         for pair in range(n_heads // 2)], axis=1).astype(bf16)


SB_STEP_HEADS = 8


def _sb(qb, kb, vb, tri):
    bsz, seq, width = qb.shape
    step_w = SB_STEP_HEADS * HEAD_DIM
    q_spec = pl.BlockSpec((1, TS, step_w), lambda b, p, i: (b, i, p))
    kv_spec = pl.BlockSpec((1, seq, step_w), lambda b, p, i: (b, 0, p))
    state = pltpu.VMEM((SB_STEP_HEADS, TS, LANES), f32)
    return pl.pallas_call(
        _sb_kernel,
        grid=(bsz, width // step_w, seq // TS),
        in_specs=[q_spec, kv_spec, kv_spec, pl.BlockSpec(tri.shape, lambda b, p, i: (0, 0))],
        out_specs=q_spec,
        out_shape=jax.ShapeDtypeStruct((bsz, seq, width), bf16),
        scratch_shapes=[state, state],
        compiler_params=_params(3),
        name="sb",
    )(qb, kb, vb, tri)


def _post_kernel(ya_ref, yb_ref, ma_ref, mb_ref, x_ref, gate1_ref, g2_ref, sc_ref, sh_ref,
                 gate_ref, wa_ref, wb_ref, wo_ref, w1_ref, w2_ref, o_ref):
    y_a = _dot(ya_ref[0], wa_ref[...])
    y_b = _dot(yb_ref[0], wb_ref[...])
    mixed = ma_ref[0].astype(f32) * y_a + mb_ref[0].astype(f32) * y_b
    hres = x_ref[0] + gate1_ref[0] * _dot(mixed.astype(bf16), wo_ref[...])
    d = hres.shape[1]
    ms = jnp.mean(hres * hres, axis=-1, keepdims=True)
    u = (hres * lax.rsqrt(ms + EPS) * g2_ref[...]) * (1.0 + sc_ref[0]) + sh_ref[0]
    ub = u.astype(bf16)
    ff = jnp.zeros(hres.shape, f32)
    for c in range(w1_ref.shape[1] // d):
        hid = jnp.maximum(_dot(ub, w1_ref[:, c * d:(c + 1) * d]), 0.0)
        ff = ff + _dot((hid * hid).astype(bf16), w2_ref[c * d:(c + 1) * d, :])
    o_ref[0] = hres + gate_ref[0] * ff


def _post(ya, yb, ma, mb, x, gate1, g2, scale2, shift2, gate2, wa, wb, wo, w1, w2):
    bsz, seq, d = x.shape
    tok = lambda w: pl.BlockSpec((1, TM, w), lambda b, i: (b, i, 0))
    mod = pl.BlockSpec((1, 1, d), lambda b, i: (b, 0, 0))
    const = lambda a: pl.BlockSpec(a.shape, lambda b, i: (0,) * a.ndim,
                                   pipeline_mode=pl.Buffered(1))
    return pl.pallas_call(
        _post_kernel,
        grid=(bsz, seq // TM),
        in_specs=[tok(ya.shape[2]), tok(yb.shape[2]), tok(d), tok(d), tok(d), mod,
                  const(g2), mod, mod, mod, const(wa), const(wb), const(wo), const(w1), const(w2)],
        out_specs=tok(d),
        out_shape=jax.ShapeDtypeStruct((bsz, seq, d), f32),
        compiler_params=_params(2),
        name="post",
    )(ya, yb, ma, mb, x, gate1, g2, scale2, shift2, gate2, wa, wb, wo, w1, w2)


def _overlap_t(nc_pad, nsel_pad, nc, nsel):
    c_start = np.arange(nc_pad) * CMP_STRIDE
    s_start = np.arange(nsel_pad) * SEL_BLOCK
    ov = (np.minimum(c_start[None, :] + CMP_BLOCK, s_start[:, None] + SEL_BLOCK)
          - np.maximum(c_start[None, :], s_start[:, None]))
    ov = np.clip(ov, 0, CMP_BLOCK).astype(np.float32) / CMP_BLOCK
    ov[nsel:, :] = 0.0
    ov[:, nc:] = 0.0
    return ov


def _layer(h, mod, rel_tiles, p):
    bsz, seq, d = h.shape
    shift1, scale1, gate1, shift2, scale2, gate2 = [
        mod[:, k * d:(k + 1) * d].reshape(bsz, 1, d) for k in range(6)]
    cmp_bias, near_bias = rel_tiles

    gq = jnp.tile(p["q_norm_g"], 2).reshape(1, LANES)
    gk = jnp.tile(p["k_norm_g"], (1, 2))

    (qa, kc, vc, ks, vs, kw, vw, gates, qb, kb, vb, ma, mb) = _inproj(
        h, p["norm1_g"].reshape(1, d), scale1, shift1, _pack_w_in(p["w_in"], d), gq, gk)

    nch = seq // CMP_STRIDE
    kcmp, vcmp = _compress(kc, vc, p["cmp_pos"], p["cmp_k_w1"], p["cmp_k_w2"],
                           p["cmp_v_w1"], p["cmp_v_w2"], p["k_norm_g"][0].reshape(1, HEAD_DIM))

    nc = (seq - CMP_BLOCK) // CMP_STRIDE + 1
    ovt = jnp.asarray(_overlap_t(nch, HEAD_DIM, nc, seq // SEL_BLOCK), bf16)
    y_nsa = _nsa(qa, gates, kcmp, vcmp, ks, vs, kw, vw, cmp_bias, near_bias, ovt)

    y_sb = _sb(qb, kb, vb, jnp.asarray(np.tril(np.ones((TS, TS)), -1), bf16))

    return _post(y_nsa, y_sb, ma, mb, h, gate1, p["norm2_g"].reshape(1, d), scale2, shift2, gate2,
                 p["w_up_nsa"].astype(bf16), p["w_up_sb"].astype(bf16), p["w_out"].astype(bf16),
                 p["mlp_w1"].astype(bf16), p["mlp_w2"].astype(bf16))


def kernel(x, c, rel_bias, ada_w, ada_b, norm1_g, norm2_g, w_in, cmp_pos, cmp_k_w1, cmp_k_w2,
           cmp_v_w1, cmp_v_w2, q_norm_g, k_norm_g, w_up_nsa, w_up_sb, w_out, mlp_w1, mlp_w2):
    bsz, seq, d = x.shape
    assert seq % TM == 0 and seq // SEL_BLOCK <= HEAD_DIM and seq >= WINDOW + TQ
    assert CMP_BLOCK == 2 * CMP_STRIDE and KU % SEL_BLOCK == 0
    tbl = rel_bias.astype(f32)
    rel_tiles = _bias_tiles(tbl, seq, seq // CMP_STRIDE)
    stacked = dict(norm1_g=norm1_g, norm2_g=norm2_g, w_in=w_in, cmp_pos=cmp_pos,
                   cmp_k_w1=cmp_k_w1, cmp_k_w2=cmp_k_w2, cmp_v_w1=cmp_v_w1, cmp_v_w2=cmp_v_w2,
                   q_norm_g=q_norm_g, k_norm_g=k_norm_g, w_up_nsa=w_up_nsa, w_up_sb=w_up_sb,
                   w_out=w_out, mlp_w1=mlp_w1, mlp_w2=mlp_w2)
    h = x
    for layer in range(ada_w.shape[0]):
        mod = _adaln(c, ada_w[layer], ada_b[layer])
        h = _layer(h, mod, rel_tiles, {k: v[layer] for k, v in stacked.items()})
    return h
```

```python
import functools
import math

import numpy as np
import jax
import jax.numpy as jnp
from jax import lax
from jax.experimental import pallas as pl
from jax.experimental.pallas import tpu as pltpu

f32 = jnp.float32
bf16 = jnp.bfloat16

HEAD_DIM = 64
NSA_HEADS = 8
NSA_KV_HEADS = 2
NSA_GROUP = NSA_HEADS // NSA_KV_HEADS
SB_HEADS = 8
CMP_BLOCK = 32
CMP_STRIDE = 16
SEL_BLOCK = 64
SEL_TOPK = 16
WINDOW = 512
N_BUCKETS = 32
MAX_DISTANCE = 128
EPS = 1e-6
FORCED_BONUS = 1e4
NEG_BLOCK = -1e9

LANES = 128
MASKED = -1e30
UNSELECTED = -1e9
LOG2E = math.log2(math.e)
EXP2_UNDERFLOW = -150.0
VMEM_LIMIT = 56 * 1024 * 1024

TQ = 256
KU = 128
TS = 256
TM = 512


def _bucket_thresholds():
    n = np.arange(0, 4 * MAX_DISTANCE)
    max_exact = N_BUCKETS // 2
    nf = np.maximum(n, 1).astype(np.float32)
    large = max_exact + (np.log(nf / max_exact) / math.log(MAX_DISTANCE / max_exact)
                         * (N_BUCKETS - max_exact)).astype(np.int32)
    large = np.minimum(large, N_BUCKETS - 1)
    b = np.where(n < max_exact, n, large)
    assert np.all(np.diff(b) >= 0) and b[-1] == N_BUCKETS - 1
    return [int(np.argmax(b >= k)) for k in range(N_BUCKETS)]


BUCKET_START = _bucket_thresholds()
assert BUCKET_START[-1] <= KU


def _dot(a, b):
    return jnp.dot(a, b, preferred_element_type=f32)


def _dot_nt(a, b):
    return lax.dot_general(a, b, (((1,), (1,)), ((), ())), preferred_element_type=f32)


def _split(a):
    hi = a.astype(bf16)
    lo = (a - hi.astype(f32)).astype(bf16)
    return hi, lo


def _params(n_grid):
    return pltpu.CompilerParams(dimension_semantics=("arbitrary",) * n_grid,
                                vmem_limit_bytes=VMEM_LIMIT)


def _adaln_kernel(c_ref, w_ref, b_ref, o_ref):
    c = c_ref[...]
    a = c * jax.nn.sigmoid(c)
    ah, al = _split(a)
    wh, wl = _split(w_ref[...])
    o_ref[...] = _dot(ah, wh) + _dot(ah, wl) + _dot(al, wh) + b_ref[...]


def _adaln(c, w, b):
    bsz, d = c.shape
    n = w.shape[1]
    return pl.pallas_call(
        _adaln_kernel,
        grid=(n // d,),
        in_specs=[pl.BlockSpec((bsz, d), lambda j: (0, 0)),
                  pl.BlockSpec((d, d), lambda j: (0, j)),
                  pl.BlockSpec((1, d), lambda j: (0, j))],
        out_specs=pl.BlockSpec((bsz, d), lambda j: (0, j)),
        out_shape=jax.ShapeDtypeStruct((bsz, n), f32),
        compiler_params=_params(1),
        name="adaln",
    )(c, w, b.reshape(1, n))


def _bias_of_dist(dist, tbl_ref, h):
    out = jnp.full(dist.shape, tbl_ref[0, h], f32)
    for k in range(1, N_BUCKETS):
        out = jnp.where(dist >= BUCKET_START[k], tbl_ref[k, h], out)
    return jnp.where(dist >= 0, out, MASKED)


def _cmp_bias_kernel(tbl_ref, o_ref):
    h = pl.program_id(0)
    i = pl.program_id(1)
    rows, nc = o_ref.shape[1], o_ref.shape[2]
    t = i * rows + lax.broadcasted_iota(jnp.int32, (rows, nc), 0)
    j = lax.broadcasted_iota(jnp.int32, (rows, nc), 1)
    o_ref[0] = _bias_of_dist(t - (j * CMP_STRIDE + CMP_BLOCK - 1), tbl_ref, h) * LOG2E


def _near_bias_kernel(tbl_ref, o_ref):
    h = pl.program_id(0)
    r = lax.broadcasted_iota(jnp.int32, (TQ, KU + TQ), 0)
    c = lax.broadcasted_iota(jnp.int32, (TQ, KU + TQ), 1)
    o_ref[0] = (_bias_of_dist(r - c + KU, tbl_ref, h) - tbl_ref[N_BUCKETS - 1, h]) * LOG2E


def _bias_tiles(rel_bias, seq, nc_pad):
    tbl = rel_bias.astype(f32)
    smem = pl.BlockSpec(memory_space=pltpu.SMEM)
    rows = 512
    cmp_bias = pl.pallas_call(
        _cmp_bias_kernel,
        grid=(NSA_HEADS, seq // rows),
        in_specs=[smem],
        out_specs=pl.BlockSpec((1, rows, nc_pad), lambda h, i: (h, i, 0)),
        out_shape=jax.ShapeDtypeStruct((NSA_HEADS, seq, nc_pad), f32),
        compiler_params=_params(2),
        name="cmp_bias",
    )(tbl)
    near_bias = pl.pallas_call(
        _near_bias_kernel,
        grid=(NSA_HEADS,),
        in_specs=[smem],
        out_specs=pl.BlockSpec((1, TQ, KU + TQ), lambda h: (h, 0, 0)),
        out_shape=jax.ShapeDtypeStruct((NSA_HEADS, TQ, KU + TQ), f32),
        compiler_params=_params(1),
        name="near_bias",
    )(tbl)
    return cmp_bias, near_bias


_QA_W = NSA_HEADS * LANES
_KV_W = NSA_KV_HEADS * LANES
_CMP_W = NSA_KV_HEADS * HEAD_DIM
_SB_W = SB_HEADS * HEAD_DIM
_GATE_ROWS = NSA_GROUP * 3


def _layout(d_model):
    names = ["qa", "kc", "vc", "ksl", "vsl", "kwn", "vwn", "g", "qb", "kb", "vb", "ma", "mb"]
    widths = [NSA_HEADS * HEAD_DIM] + [_CMP_W] * 6 + [LANES] + [_SB_W] * 3 + [d_model, d_model]
    offs = np.concatenate([[0], np.cumsum(widths)])
    return {n: (int(offs[i]), int(offs[i + 1])) for i, n in enumerate(names)}, int(offs[-1])


def _pack_w_in(w_in, d_model):
    q_w = NSA_HEADS * HEAD_DIM
    kv_w = NSA_KV_HEADS * HEAD_DIM
    g_w = NSA_HEADS * 3
    sizes = [q_w] + [kv_w] * 6 + [g_w] + [_SB_W] * 3 + [d_model, d_model]
    offs = np.concatenate([[0], np.cumsum(sizes)])
    parts = [w_in[:, int(offs[i]):int(offs[i + 1])] for i in range(len(sizes))]
    parts[7] = jnp.pad(parts[7], ((0, 0), (0, LANES - g_w)))
    return jnp.concatenate(parts, axis=1).astype(bf16)


def _inproj_kernel(lay, x_ref, g1_ref, sc_ref, sh_ref, w_ref, gq_ref, gk_ref,
                   qa_ref, kc_ref, vc_ref, ks_ref, vs_ref, kw_ref, vw_ref, g_ref,
                   qb_ref, kb_ref, vb_ref, ma_ref, mb_ref):
    i = pl.program_id(1)
    x = x_ref[0]
    ms = jnp.mean(x * x, axis=-1, keepdims=True)
    u = (x * lax.rsqrt(ms + EPS) * g1_ref[...]) * (1.0 + sc_ref[0]) + sh_ref[0]
    ub = u.astype(bf16)

    narrow = ("kc", "vc", "ksl", "vsl", "kwn", "vwn", "g")
    narrow_lo = lay[narrow[0]][0]
    z_narrow = _dot(ub, w_ref[:, narrow_lo:lay[narrow[-1]][1]])

    def proj(name):
        lo, hi = lay[name]
        if name in narrow:
            return z_narrow[:, lo - narrow_lo:hi - narrow_lo]
        return _dot(ub, w_ref[:, lo:hi])

    rows = x.shape[0]
    lane = lax.broadcasted_iota(jnp.int32, (rows, LANES), 1)
    low = lane < HEAD_DIM

    def pair_norm(z, gain):
        sq = z * z
        ss_a = jnp.sum(jnp.where(low, sq, 0.0), axis=1, keepdims=True)
        ss_b = jnp.sum(jnp.where(low, 0.0, sq), axis=1, keepdims=True)
        inv = lax.rsqrt(jnp.where(low, ss_a, ss_b) * (1.0 / HEAD_DIM) + EPS)
        return z * inv * gain

    def spread(z, extra):
        return jnp.concatenate([jnp.where(low, z, extra),
                                jnp.where(low, pltpu.roll(z, HEAD_DIM, 1), extra)], axis=1)

    scale = HEAD_DIM ** -0.5 * LOG2E
    zq = proj("qa")
    qa_ref[0] = jnp.concatenate(
        [spread(pair_norm(zq[:, j * LANES:(j + 1) * LANES], gq_ref[...]) * scale, 0.0)
         for j in range(NSA_HEADS // 2)], axis=1).astype(bf16)
    kc_ref[0] = proj("kc")
    vc_ref[0] = proj("vc")

    tok_blk = (i * rows + lax.broadcasted_iota(jnp.int32, (rows, LANES), 0)) // SEL_BLOCK
    onehot = jnp.where(lane - HEAD_DIM == tok_blk, 1.0, 0.0)
    ones_col = jnp.where(lane == HEAD_DIM, 1.0, 0.0)

    ks_ref[0] = spread(pair_norm(proj("ksl"), gk_ref[1:2, :]), onehot).astype(bf16)
    vs_ref[0] = spread(proj("vsl"), ones_col).astype(bf16)
    kw_ref[0] = spread(pair_norm(proj("kwn"), gk_ref[2:3, :]), 0.0).astype(bf16)
    vw_ref[0] = spread(proj("vwn"), ones_col).astype(bf16)
    g_ref[0] = jax.nn.sigmoid(proj("g")).T
    qb_ref[0] = (proj("qb") * scale).astype(bf16)
    kb_ref[0] = proj("kb").astype(bf16)
    vb_ref[0] = proj("vb").astype(bf16)
    ma_ref[0] = jax.nn.sigmoid(proj("ma")).astype(bf16)
    mb_ref[0] = jax.nn.sigmoid(proj("mb")).astype(bf16)


def _inproj(x, g1, scale1, shift1, w_packed, gq, gk):
    bsz, seq, d = x.shape
    lay, width = _layout(d)
    assert w_packed.shape == (d, width)
    tok = lambda w: pl.BlockSpec((1, TM, w), lambda b, i: (b, i, 0))
    full = lambda a: pl.BlockSpec(a.shape, lambda b, i: (0,) * a.ndim,
                                  pipeline_mode=pl.Buffered(1))
    mod = pl.BlockSpec((1, 1, d), lambda b, i: (b, 0, 0))
    out_w = [(_QA_W, bf16), (_CMP_W, f32), (_CMP_W, f32), (_KV_W, bf16), (_KV_W, bf16),
             (_KV_W, bf16), (_KV_W, bf16), (LANES, f32), (_SB_W, bf16), (_SB_W, bf16),
             (_SB_W, bf16), (d, bf16), (d, bf16)]
    gates_at = 7
    out_specs = [tok(w) for w, _ in out_w]
    out_shape = [jax.ShapeDtypeStruct((bsz, seq, w), dt) for w, dt in out_w]
    out_specs[gates_at] = pl.BlockSpec((1, LANES, TM), lambda b, i: (b, 0, i))
    out_shape[gates_at] = jax.ShapeDtypeStruct((bsz, LANES, seq), f32)
    return pl.pallas_call(
        functools.partial(_inproj_kernel, lay),
        grid=(bsz, seq // TM),
        in_specs=[tok(d), full(g1), mod, mod, full(w_packed), full(gq), full(gk)],
        out_specs=out_specs,
        out_shape=out_shape,
        compiler_params=_params(2),
        name="inproj",
    )(x, g1, scale1, shift1, w_packed, gq, gk)


def _compress_kernel(xk_ref, xv_ref, pos_ref, w1k_ref, w2k_ref, w1v_ref, w2v_ref, gk_ref,
                     ko_ref, vo_ref):
    nch = xk_ref.shape[1] // CMP_STRIDE

    def mlp(x_ref, w1_ref, w2_ref):
        first = jnp.zeros((nch, LANES), f32)
        second = jnp.zeros((nch, LANES), f32)
        for l in range(CMP_STRIDE):
            xl = x_ref[0, pl.ds(l, nch, stride=CMP_STRIDE), :]
            lo = l + CMP_STRIDE
            first = first + _dot((xl + pos_ref[l:l + 1, :]).astype(bf16), w1_ref[l])
            second = second + _dot((xl + pos_ref[lo:lo + 1, :]).astype(bf16), w1_ref[lo])
        pre = first + pltpu.roll(second, nch - 1, 0)
        hid = pre * jax.nn.sigmoid(pre)
        return _dot(hid.astype(bf16), w2_ref[...])

    k = mlp(xk_ref, w1k_ref, w2k_ref)
    v = mlp(xv_ref, w1v_ref, w2v_ref)
    for h in range(NSA_KV_HEADS):
        kh = k[:, h * HEAD_DIM:(h + 1) * HEAD_DIM]
        ms = jnp.mean(kh * kh, axis=-1, keepdims=True)
        ko_ref[0, h] = (kh * lax.rsqrt(ms + EPS) * gk_ref[...]).astype(bf16)
        vh = v[:, h * HEAD_DIM:(h + 1) * HEAD_DIM]
        ones_col = jnp.where(lax.broadcasted_iota(jnp.int32, vh.shape, 1) == 0, 1.0, 0.0)
        vo_ref[0, h] = jnp.concatenate([vh, ones_col], axis=1).astype(bf16)


def _compress(xk, xv, pos, w1k, w2k, w1v, w2v, gk0):
    bsz, seq, width = xk.shape
    nch = seq // CMP_STRIDE
    assert width == NSA_KV_HEADS * HEAD_DIM == LANES

    def both_heads(w):
        z = jnp.zeros_like(w)
        return jnp.concatenate([jnp.concatenate([w, z], axis=2),
                                jnp.concatenate([z, w], axis=2)], axis=1).astype(bf16)

    w1 = lambda w: both_heads(w.reshape(CMP_BLOCK, HEAD_DIM, w.shape[1]))
    w2 = lambda w: both_heads(w[None])[0]
    args = (xk, xv, jnp.tile(pos, (1, NSA_KV_HEADS)), w1(w1k), w2(w2k), w1(w1v), w2(w2v), gk0)
    blk = pl.BlockSpec((1, seq, width), lambda b: (b, 0, 0))
    full = lambda a: pl.BlockSpec(a.shape, lambda b: (0,) * a.ndim)
    out = lambda w: pl.BlockSpec((1, NSA_KV_HEADS, nch, w), lambda b: (b, 0, 0, 0))
    shape = lambda w: jax.ShapeDtypeStruct((bsz, NSA_KV_HEADS, nch, w), bf16)
    return pl.pallas_call(
        _compress_kernel,
        grid=(bsz,),
        in_specs=[blk, blk] + [full(a) for a in args[2:]],
        out_specs=[out(HEAD_DIM), out(LANES)],
        out_shape=[shape(HEAD_DIM), shape(LANES)],
        compiler_params=_params(1),
        name="compress",
    )(*args)


def _flash_step(s, v, m_ref, acc_ref):
    m_prev = m_ref[...]
    m_new = jnp.maximum(m_prev, jnp.max(s, axis=1, keepdims=True))
    alpha = jnp.exp2(m_prev - m_new)
    p = jnp.exp2(s - jnp.concatenate([m_new] * (s.shape[1] // LANES), axis=1))
    acc_ref[...] = alpha * acc_ref[...] + _dot(p.astype(bf16), v)
    m_ref[...] = m_new


def _flash_init(m_ref, acc_ref):
    m_ref[...] = jnp.full(m_ref.shape, MASKED, f32)
    acc_ref[...] = jnp.zeros(acc_ref.shape, f32)


def _flash_out_t(acc_ref):
    acc_t = acc_ref[...].T
    return acc_t[:HEAD_DIM] / acc_t[HEAD_DIM:HEAD_DIM + 1]


def _nsa_kernel(n_top, q_ref, g_ref, kc_ref, vc_ref, ks_ref, vs_ref, kw_ref, vw_ref,
                bc_ref, bn_ref, ovt_ref, o_ref, qs_ref, qw_ref, oc_ref,
                m_ref, acc_ref, mw_ref, accw_ref):
    i = pl.program_id(1)
    kv_heads = range(NSA_KV_HEADS)
    rows = NSA_GROUP * TQ
    start = i * TQ
    assert WINDOW % TQ == 0 and TQ % KU == 0
    first_interior = WINDOW // TQ
    window_extra = (WINDOW - TQ) // KU - 1

    def attend(hk, q_rows_ref, k_ref, v_ref, off, width, bias, state):
        off = pl.multiple_of(off, KU)
        lanes = slice(hk * LANES, (hk + 1) * LANES)
        s = _dot_nt(q_rows_ref[hk], k_ref[0, pl.ds(off, width), lanes])
        if bias is not None:
            s = s + bias
        _flash_step(s, v_ref[0, pl.ds(off, width), lanes], state[0].at[hk], state[1].at[hk])

    def when(cond, guarded):
        return pl.when(cond) if guarded else (lambda fn: fn())

    def near_steps(q_rows_ref, k_ref, v_ref, state, guarded, extra=0):
        group = lambda hk: slice(hk * NSA_GROUP, (hk + 1) * NSA_GROUP)

        @when(i >= 1, guarded)
        def _():
            for hk in kv_heads:
                bias = bn_ref[group(hk)].reshape(rows, KU + TQ)
                if extra:
                    bias = jnp.concatenate([jnp.zeros((rows, extra * KU), f32), bias], axis=1)
                attend(hk, q_rows_ref, k_ref, v_ref, start - (1 + extra) * KU,
                       TQ + (1 + extra) * KU, bias, state)

        if guarded:
            @pl.when(i == 0)
            def _():
                for hk in kv_heads:
                    attend(hk, q_rows_ref, k_ref, v_ref, start, TQ,
                           bn_ref[group(hk), :, KU:].reshape(rows, TQ), state)

    def select(hk):
        heads = range(hk * NSA_GROUP, (hk + 1) * NSA_GROUP)
        qpad = jnp.concatenate([q_ref[0, :, g * LANES:(g + 1) * LANES] for g in heads], axis=0)
        qw_ref[hk] = qpad

        bc = bc_ref[hk * NSA_GROUP:(hk + 1) * NSA_GROUP].reshape(rows, bc_ref.shape[2])
        s_c = _dot_nt(qpad[:, :HEAD_DIM], kc_ref[0, hk]) + bc
        m_c = jnp.maximum(jnp.max(s_c, axis=1, keepdims=True), 0.1 * MASKED)
        e_c = jnp.exp2(s_c - m_c).astype(bf16)
        oc_ref[hk] = _dot(e_c, vc_ref[0, hk])

        nblk = ovt_ref.shape[0] - 8
        imp = jnp.zeros((nblk, TQ), f32)
        for g in range(NSA_GROUP):
            t = _dot_nt(ovt_ref[...], e_c[g * TQ:(g + 1) * TQ])
            imp = imp + t[:nblk] / jnp.maximum(t[nblk:nblk + 1], 1e-30)
        blk = lax.broadcasted_iota(jnp.int32, (nblk, TQ), 0)
        cur = (start + lax.broadcasted_iota(jnp.int32, (nblk, TQ), 1)) // SEL_BLOCK
        forced = (blk == 0) | (blk == cur) | (blk == cur - 1)
        imp = jnp.where(blk > cur, NEG_BLOCK, imp + jnp.where(forced, FORCED_BONUS, 0.0))
        sub = 8
        groups = [imp[lo:lo + sub] for lo in range(0, nblk, sub)]
        ranks = [jnp.zeros((sub, TQ), f32) for _ in groups]
        row = lax.broadcasted_iota(jnp.int32, (sub, TQ), 0)
        for b2 in range(nblk):
            other = imp[b2:b2 + 1, :]
            for gi, grp in enumerate(groups):
                lo = gi * sub
                if lo > b2:
                    ranks[gi] = jnp.where(other >= grp, ranks[gi] + 1.0, ranks[gi])
                elif lo + sub - 1 < b2:
                    ranks[gi] = jnp.where(other > grp, ranks[gi] + 1.0, ranks[gi])
                else:
                    ranks[gi] = ranks[gi] + jnp.where(row + lo > b2,
                                                      jnp.where(other >= grp, 1.0, 0.0),
                                                      jnp.where(other > grp, 1.0, 0.0))
        rank = jnp.concatenate(ranks, axis=0)
        usable = (rank < n_top) & (blk <= cur)
        sel_t = jnp.where(usable, 0.0, UNSELECTED)
        sel_pad = jnp.concatenate([jnp.zeros((LANES - nblk, TQ), f32), sel_t], axis=0).T
        sel_rows = jnp.concatenate([sel_pad.astype(bf16)] * NSA_GROUP, axis=0)
        qs_ref[hk] = qpad + sel_rows

    def head(guarded):
        for hk in kv_heads:
            select(hk)

        win = (mw_ref, accw_ref)
        _flash_init(*win)

        @when(i >= first_interior, guarded)
        def _():
            r = lax.broadcasted_iota(jnp.int32, (rows, TQ), 0) & (TQ - 1)
            c = lax.broadcasted_iota(jnp.int32, (rows, TQ), 1)
            for hk in kv_heads:
                attend(hk, qw_ref, kw_ref, vw_ref, start - WINDOW, TQ,
                       jnp.where(c > r, 0.0, MASKED), win)

        near_steps(qw_ref, kw_ref, vw_ref, win, guarded, window_extra)
        _flash_init(m_ref, acc_ref)

    def tail(guarded, extra=0):
        near_steps(qs_ref, ks_ref, vs_ref, (m_ref, acc_ref), guarded, extra)
        outs = []
        for hk in kv_heads:
            oc_t = oc_ref[hk].T
            o_c = oc_t[:HEAD_DIM] / jnp.maximum(oc_t[HEAD_DIM:HEAD_DIM + 1], 1e-30)
            o_s = _flash_out_t(acc_ref.at[hk])
            o_w = _flash_out_t(accw_ref.at[hk])
            gates = g_ref[0]
            for g in range(NSA_GROUP):
                sl = slice(g * TQ, (g + 1) * TQ)
                at = hk * _GATE_ROWS + 3 * g
                outs.append(gates[at:at + 1] * o_c[:, sl]
                            + gates[at + 1:at + 2] * o_s[:, sl]
                            + gates[at + 2:at + 3] * o_w[:, sl])
        o_ref[0] = jnp.concatenate(outs, axis=0).T.astype(bf16)

    interior = i >= first_interior
    pl.when(interior)(lambda: head(False))
    pl.when(jnp.logical_not(interior))(lambda: head(True))

    sel = (m_ref, acc_ref)
    n_far = jnp.maximum(i * (TQ // KU) - 1, 0)

    def far_steps(off, n_steps):
        for step in range(n_steps):
            for hk in kv_heads:
                attend(hk, qs_ref, ks_ref, vs_ref, off + step * (2 * KU), 2 * KU, None, sel)

    def far_trip(c, carry):
        far_steps(c * (8 * KU), 4)
        return carry

    lax.fori_loop(0, n_far // 8, far_trip, 0)
    rem_off = (n_far // 8) * (8 * KU)

    @pl.when((n_far & 4) != 0)
    def _():
        far_steps(rem_off, 2)

    edge = jnp.logical_not(interior)

    @pl.when(edge & ((n_far & 2) != 0))
    def _():
        far_steps(rem_off + (n_far & 4) * KU, 1)

    @pl.when(edge & ((n_far & 1) != 0))
    def _():
        for hk in kv_heads:
            attend(hk, qs_ref, ks_ref, vs_ref, rem_off + (n_far & 6) * KU, KU, None, sel)

    for extra in range(4):
        pl.when(interior & ((n_far & 3) == extra))(functools.partial(tail, False, extra))
    pl.when(edge)(lambda: tail(True))


def _nsa(qa, gates, kcmp, vcmp, ks, vs, kw, vw, cmp_bias, near_bias, ovt):
    bsz, seq, _ = qa.shape
    nc = kcmp.shape[2]
    n_top = min(SEL_TOPK, seq // SEL_BLOCK)
    rows = NSA_GROUP * TQ
    tok = lambda w: pl.BlockSpec((1, TQ, w), lambda b, i: (b, i, 0))
    kv = pl.BlockSpec((1, seq, NSA_KV_HEADS * LANES), lambda b, i: (b, 0, 0))
    cmp = lambda a: pl.BlockSpec((1,) + a.shape[1:], lambda b, i: (b, 0, 0, 0))
    per_kv = lambda width, dt: pltpu.VMEM((NSA_KV_HEADS, rows, width), dt)
    return pl.pallas_call(
        functools.partial(_nsa_kernel, n_top),
        grid=(bsz, seq // TQ),
        in_specs=[tok(NSA_HEADS * LANES),
                  pl.BlockSpec((1, LANES, TQ), lambda b, i: (b, 0, i)),
                  cmp(kcmp), cmp(vcmp), kv, kv, kv, kv,
                  pl.BlockSpec((NSA_HEADS, TQ, nc), lambda b, i: (0, i, 0)),
                  pl.BlockSpec((NSA_HEADS, TQ, KU + TQ), lambda b, i: (0, 0, 0)),
                  pl.BlockSpec(ovt.shape, lambda b, i: (0, 0))],
        out_specs=tok(NSA_HEADS * HEAD_DIM),
        out_shape=jax.ShapeDtypeStruct((bsz, seq, NSA_HEADS * HEAD_DIM), bf16),
        scratch_shapes=[per_kv(LANES, bf16), per_kv(LANES, bf16),
                        per_kv(LANES, f32), per_kv(LANES, f32), per_kv(LANES, f32),
                        per_kv(LANES, f32), per_kv(LANES, f32)],
        compiler_params=_params(2),
        name="nsa",
    )(qa, gates, kcmp, vcmp, ks, vs, kw, vw, cmp_bias, near_bias, ovt)


def _sb_kernel(q_ref, k_ref, v_ref, tri_ref, o_ref, carry_ref, acc_ref):
    i = pl.program_id(2)
    n_heads = carry_ref.shape[0]
    lane = lax.broadcasted_iota(jnp.int32, (TS, LANES), 1)
    q_heads = []
    for pair in range(n_heads // 2):
        q = q_ref[0, :, pair * LANES:(pair + 1) * LANES]
        zero = jnp.zeros_like(q)
        q_heads += [jnp.where(lane < HEAD_DIM, q, zero), jnp.where(lane >= HEAD_DIM, q, zero)]
    pair_lanes = lambda hh: slice((hh // 2) * LANES, (hh // 2 + 1) * LANES)
    carry_ref[...] = jnp.zeros(carry_ref.shape, f32)
    acc_ref[...] = jnp.zeros(acc_ref.shape, f32)
    r = lax.broadcasted_iota(jnp.int32, (TS, TS), 0)
    c = lax.broadcasted_iota(jnp.int32, (TS, TS), 1)
    before = c < r

    def chunks(jobs):
        offs = [pl.multiple_of(off, TS) for off, _ in jobs]
        stage = []
        for (_, diagonal), off in zip(jobs, offs):
            for hh in range(n_heads):
                z = _dot_nt(q_heads[hh], k_ref[0, pl.ds(off, TS), pair_lanes(hh)])
                soft = jnp.log2(1.0 + jnp.exp2(-jnp.abs(z)))
                log_keep = jnp.minimum(-z, 0.0) - soft
                log_sig = log_keep + z
                if diagonal:
                    log_keep = jnp.where(before, log_keep, 0.0)
                later = _dot(log_keep.astype(bf16), tri_ref[...])
                stage.append((log_sig + later, jnp.sum(log_keep, axis=1, keepdims=True)))
        for hh in range(n_heads):
            carry = carry_ref[hh]
            acc = acc_ref[hh]
            for j, ((_, diagonal), off) in enumerate(zip(jobs, offs)):
                base, total = stage[n_heads * j + hh]
                a = jnp.exp2(base + jnp.concatenate([carry] * (TS // LANES), axis=1))
                if diagonal:
                    a = jnp.where(before, a, 0.0)
                acc = acc + _dot(a.astype(bf16), v_ref[0, pl.ds(off, TS), pair_lanes(hh)])
                carry = carry + total
            carry_ref[hh] = carry
            acc_ref[hh] = acc

    @pl.when(i == 0)
    def _():
        chunks([(0, True)])

    @pl.when(i >= 1)
    def _():
        chunks([(i * TS, True), ((i - 1) * TS, False)])

    def any_live():
        return jnp.max(carry_ref[...]) > EXP2_UNDERFLOW

    def more(state):
        n, live = state
        return jnp.logical_and(n < i, live)

    def older(state):
        n, _ = state
        chunks([((i - 1 - n) * TS, False)])
        return n + 1, any_live()

    lax.while_loop(more, older, (jnp.int32(1), any_live()))
    o_ref[0] = jnp.concatenate(
        [jnp.where(lane < HEAD_DIM, acc_ref[2 * pair], acc_ref[2 * pair + 1])
         for pair in range(n_heads // 2)], axis=1).astype(bf16)


SB_STEP_HEADS = 8


def _sb(qb, kb, vb, tri):
    bsz, seq, width = qb.shape
    step_w = SB_STEP_HEADS * HEAD_DIM
    q_spec = pl.BlockSpec((1, TS, step_w), lambda b, p, i: (b, i, p))
    kv_spec = pl.BlockSpec((1, seq, step_w), lambda b, p, i: (b, 0, p))
    state = pltpu.VMEM((SB_STEP_HEADS, TS, LANES), f32)
    return pl.pallas_call(
        _sb_kernel,
        grid=(bsz, width // step_w, seq // TS),
        in_specs=[q_spec, kv_spec, kv_spec, pl.BlockSpec(tri.shape, lambda b, p, i: (0, 0))],
        out_specs=q_spec,
        out_shape=jax.ShapeDtypeStruct((bsz, seq, width), bf16),
        scratch_shapes=[state, state],
        compiler_params=_params(3),
        name="sb",
    )(qb, kb, vb, tri)


def _post_kernel(ya_ref, yb_ref, ma_ref, mb_ref, x_ref, gate1_ref, g2_ref, sc_ref, sh_ref,
                 gate_ref, wa_ref, wb_ref, wo_ref, w1_ref, w2_ref, o_ref):
    y_a = _dot(ya_ref[0], wa_ref[...])
    y_b = _dot(yb_ref[0], wb_ref[...])
    mixed = ma_ref[0].astype(f32) * y_a + mb_ref[0].astype(f32) * y_b
    hres = x_ref[0] + gate1_ref[0] * _dot(mixed.astype(bf16), wo_ref[...])
    d = hres.shape[1]
    ms = jnp.mean(hres * hres, axis=-1, keepdims=True)
    u = (hres * lax.rsqrt(ms + EPS) * g2_ref[...]) * (1.0 + sc_ref[0]) + sh_ref[0]
    ub = u.astype(bf16)
    ff = jnp.zeros(hres.shape, f32)
    for c in range(w1_ref.shape[1] // d):
        hid = jnp.maximum(_dot(ub, w1_ref[:, c * d:(c + 1) * d]), 0.0)
        ff = ff + _dot((hid * hid).astype(bf16), w2_ref[c * d:(c + 1) * d, :])
    o_ref[0] = hres + gate_ref[0] * ff


def _post(ya, yb, ma, mb, x, gate1, g2, scale2, shift2, gate2, wa, wb, wo, w1, w2):
    bsz, seq, d = x.shape
    tok = lambda w: pl.BlockSpec((1, TM, w), lambda b, i: (b, i, 0))
    mod = pl.BlockSpec((1, 1, d), lambda b, i: (b, 0, 0))
    const = lambda a: pl.BlockSpec(a.shape, lambda b, i: (0,) * a.ndim,
                                   pipeline_mode=pl.Buffered(1))
    return pl.pallas_call(
        _post_kernel,
        grid=(bsz, seq // TM),
        in_specs=[tok(ya.shape[2]), tok(yb.shape[2]), tok(d), tok(d), tok(d), mod,
                  const(g2), mod, mod, mod, const(wa), const(wb), const(wo), const(w1), const(w2)],
        out_specs=tok(d),
        out_shape=jax.ShapeDtypeStruct((bsz, seq, d), f32),
        compiler_params=_params(2),
        name="post",
    )(ya, yb, ma, mb, x, gate1, g2, scale2, shift2, gate2, wa, wb, wo, w1, w2)


def _overlap_t(nc_pad, nsel_pad, nc, nsel):
    c_start = np.arange(nc_pad) * CMP_STRIDE
    s_start = np.arange(nsel_pad) * SEL_BLOCK
    ov = (np.minimum(c_start[None, :] + CMP_BLOCK, s_start[:, None] + SEL_BLOCK)
          - np.maximum(c_start[None, :], s_start[:, None]))
    ov = np.clip(ov, 0, CMP_BLOCK).astype(np.float32) / CMP_BLOCK
    ov[nsel:, :] = 0.0
    ov[:, nc:] = 0.0
    return ov


def _layer(h, mod, rel_tiles, p):
    bsz, seq, d = h.shape
    shift1, scale1, gate1, shift2, scale2, gate2 = [
        mod[:, k * d:(k + 1) * d].reshape(bsz, 1, d) for k in range(6)]
    cmp_bias, near_bias = rel_tiles

    gq = jnp.tile(p["q_norm_g"], 2).reshape(1, LANES)
    gk = jnp.tile(p["k_norm_g"], (1, 2))

    (qa, kc, vc, ks, vs, kw, vw, gates, qb, kb, vb, ma, mb) = _inproj(
        h, p["norm1_g"].reshape(1, d), scale1, shift1, _pack_w_in(p["w_in"], d), gq, gk)

    nch = seq // CMP_STRIDE
    kcmp, vcmp = _compress(kc, vc, p["cmp_pos"], p["cmp_k_w1"], p["cmp_k_w2"],
                           p["cmp_v_w1"], p["cmp_v_w2"], p["k_norm_g"][0].reshape(1, HEAD_DIM))

    nc = (seq - CMP_BLOCK) // CMP_STRIDE + 1
    ovt = jnp.asarray(np.concatenate([_overlap_t(nch, HEAD_DIM, nc, seq // SEL_BLOCK),
                                      np.ones((8, nch), np.float32)], axis=0), bf16)
    y_nsa = _nsa(qa, gates, kcmp, vcmp, ks, vs, kw, vw, cmp_bias, near_bias, ovt)

    y_sb = _sb(qb, kb, vb, jnp.asarray(np.tril(np.ones((TS, TS)), -1), bf16))

    return _post(y_nsa, y_sb, ma, mb, h, gate1, p["norm2_g"].reshape(1, d), scale2, shift2, gate2,
                 p["w_up_nsa"].astype(bf16), p["w_up_sb"].astype(bf16), p["w_out"].astype(bf16),
                 p["mlp_w1"].astype(bf16), p["mlp_w2"].astype(bf16))


def kernel(x, c, rel_bias, ada_w, ada_b, norm1_g, norm2_g, w_in, cmp_pos, cmp_k_w1, cmp_k_w2,
           cmp_v_w1, cmp_v_w2, q_norm_g, k_norm_g, w_up_nsa, w_up_sb, w_out, mlp_w1, mlp_w2):
    bsz, seq, d = x.shape
    assert seq % TM == 0 and seq // SEL_BLOCK <= HEAD_DIM and seq >= WINDOW + TQ
    assert CMP_BLOCK == 2 * CMP_STRIDE and KU % SEL_BLOCK == 0
    tbl = rel_bias.astype(f32)
    rel_tiles = _bias_tiles(tbl, seq, seq // CMP_STRIDE)
    stacked = dict(norm1_g=norm1_g, norm2_g=norm2_g, w_in=w_in, cmp_pos=cmp_pos,
                   cmp_k_w1=cmp_k_w1, cmp_k_w2=cmp_k_w2, cmp_v_w1=cmp_v_w1, cmp_v_w2=cmp_v_w2,
                   q_norm_g=q_norm_g, k_norm_g=k_norm_g, w_up_nsa=w_up_nsa, w_up_sb=w_up_sb,
                   w_out=w_out, mlp_w1=mlp_w1, mlp_w2=mlp_w2)
    h = x
    for layer in range(ada_w.shape[0]):
        mod = _adaln(c, ada_w[layer], ada_b[layer])
        h = _layer(h, mod, rel_tiles, {k: v[layer] for k, v in stacked.items()})
    return h
```

```python
import functools
import math

import numpy as np
import jax
import jax.numpy as jnp
from jax import lax
from jax.experimental import pallas as pl
from jax.experimental.pallas import tpu as pltpu

f32 = jnp.float32
bf16 = jnp.bfloat16

HEAD_DIM = 64
NSA_HEADS = 8
NSA_KV_HEADS = 2
NSA_GROUP = NSA_HEADS // NSA_KV_HEADS
SB_HEADS = 8
CMP_BLOCK = 32
CMP_STRIDE = 16
SEL_BLOCK = 64
SEL_TOPK = 16
WINDOW = 512
N_BUCKETS = 32
MAX_DISTANCE = 128
EPS = 1e-6
FORCED_BONUS = 1e4
NEG_BLOCK = -1e9

LANES = 128
MASKED = -1e30
UNSELECTED = -1e9
LOG2E = math.log2(math.e)
EXP2_UNDERFLOW = -150.0
VMEM_LIMIT = 56 * 1024 * 1024

TQ = 256
KU = 128
TS = 256
TM = 512


def _bucket_thresholds():
    n = np.arange(0, 4 * MAX_DISTANCE)
    max_exact = N_BUCKETS // 2
    nf = np.maximum(n, 1).astype(np.float32)
    large = max_exact + (np.log(nf / max_exact) / math.log(MAX_DISTANCE / max_exact)
                         * (N_BUCKETS - max_exact)).astype(np.int32)
    large = np.minimum(large, N_BUCKETS - 1)
    b = np.where(n < max_exact, n, large)
    assert np.all(np.diff(b) >= 0) and b[-1] == N_BUCKETS - 1
    return [int(np.argmax(b >= k)) for k in range(N_BUCKETS)]


BUCKET_START = _bucket_thresholds()
assert BUCKET_START[-1] <= KU


def _dot(a, b):
    return jnp.dot(a, b, preferred_element_type=f32)


def _dot_nt(a, b):
    return lax.dot_general(a, b, (((1,), (1,)), ((), ())), preferred_element_type=f32)


def _split(a):
    hi = a.astype(bf16)
    lo = (a - hi.astype(f32)).astype(bf16)
    return hi, lo


def _params(n_grid):
    return pltpu.CompilerParams(dimension_semantics=("arbitrary",) * n_grid,
                                vmem_limit_bytes=VMEM_LIMIT)


def _adaln_kernel(c_ref, w_ref, b_ref, o_ref):
    c = c_ref[...]
    a = c * jax.nn.sigmoid(c)
    ah, al = _split(a)
    wh, wl = _split(w_ref[...])
    o_ref[...] = _dot(ah, wh) + _dot(ah, wl) + _dot(al, wh) + b_ref[...]


def _adaln(c, w, b):
    bsz, d = c.shape
    n = w.shape[1]
    return pl.pallas_call(
        _adaln_kernel,
        grid=(n // d,),
        in_specs=[pl.BlockSpec((bsz, d), lambda j: (0, 0)),
                  pl.BlockSpec((d, d), lambda j: (0, j)),
                  pl.BlockSpec((1, d), lambda j: (0, j))],
        out_specs=pl.BlockSpec((bsz, d), lambda j: (0, j)),
        out_shape=jax.ShapeDtypeStruct((bsz, n), f32),
        compiler_params=_params(1),
        name="adaln",
    )(c, w, b.reshape(1, n))


def _bias_of_dist(dist, tbl_ref, h):
    out = jnp.full(dist.shape, tbl_ref[0, h], f32)
    for k in range(1, N_BUCKETS):
        out = jnp.where(dist >= BUCKET_START[k], tbl_ref[k, h], out)
    return jnp.where(dist >= 0, out, MASKED)


def _cmp_bias_kernel(tbl_ref, o_ref):
    h = pl.program_id(0)
    i = pl.program_id(1)
    rows, nc = o_ref.shape[1], o_ref.shape[2]
    t = i * rows + lax.broadcasted_iota(jnp.int32, (rows, nc), 0)
    j = lax.broadcasted_iota(jnp.int32, (rows, nc), 1)
    o_ref[0] = _bias_of_dist(t - (j * CMP_STRIDE + CMP_BLOCK - 1), tbl_ref, h) * LOG2E


def _near_bias_kernel(tbl_ref, o_ref):
    h = pl.program_id(0)
    r = lax.broadcasted_iota(jnp.int32, (TQ, KU + TQ), 0)
    c = lax.broadcasted_iota(jnp.int32, (TQ, KU + TQ), 1)
    o_ref[0] = (_bias_of_dist(r - c + KU, tbl_ref, h) - tbl_ref[N_BUCKETS - 1, h]) * LOG2E


def _bias_tiles(rel_bias, seq, nc_pad):
    tbl = rel_bias.astype(f32)
    smem = pl.BlockSpec(memory_space=pltpu.SMEM)
    rows = 512
    cmp_bias = pl.pallas_call(
        _cmp_bias_kernel,
        grid=(NSA_HEADS, seq // rows),
        in_specs=[smem],
        out_specs=pl.BlockSpec((1, rows, nc_pad), lambda h, i: (h, i, 0)),
        out_shape=jax.ShapeDtypeStruct((NSA_HEADS, seq, nc_pad), f32),
        compiler_params=_params(2),
        name="cmp_bias",
    )(tbl)
    near_bias = pl.pallas_call(
        _near_bias_kernel,
        grid=(NSA_HEADS,),
        in_specs=[smem],
        out_specs=pl.BlockSpec((1, TQ, KU + TQ), lambda h: (h, 0, 0)),
        out_shape=jax.ShapeDtypeStruct((NSA_HEADS, TQ, KU + TQ), f32),
        compiler_params=_params(1),
        name="near_bias",
    )(tbl)
    return cmp_bias, near_bias


_QA_W = NSA_HEADS * LANES
_KV_W = NSA_KV_HEADS * LANES
_CMP_W = NSA_KV_HEADS * HEAD_DIM
_SB_W = SB_HEADS * HEAD_DIM
_GATE_ROWS = NSA_GROUP * 3


def _layout(d_model):
    names = ["qa", "kc", "vc", "ksl", "vsl", "kwn", "vwn", "g", "qb", "kb", "vb", "ma", "mb"]
    widths = [NSA_HEADS * HEAD_DIM] + [_CMP_W] * 6 + [LANES] + [_SB_W] * 3 + [d_model, d_model]
    offs = np.concatenate([[0], np.cumsum(widths)])
    return {n: (int(offs[i]), int(offs[i + 1])) for i, n in enumerate(names)}, int(offs[-1])


def _pack_w_in(w_in, d_model):
    q_w = NSA_HEADS * HEAD_DIM
    kv_w = NSA_KV_HEADS * HEAD_DIM
    g_w = NSA_HEADS * 3
    sizes = [q_w] + [kv_w] * 6 + [g_w] + [_SB_W] * 3 + [d_model, d_model]
    offs = np.concatenate([[0], np.cumsum(sizes)])
    parts = [w_in[:, int(offs[i]):int(offs[i + 1])] for i in range(len(sizes))]
    parts[7] = jnp.pad(parts[7], ((0, 0), (0, LANES - g_w)))
    return jnp.concatenate(parts, axis=1).astype(bf16)


def _inproj_kernel(lay, x_ref, g1_ref, sc_ref, sh_ref, w_ref, gq_ref, gk_ref,
                   qa_ref, kc_ref, vc_ref, ks_ref, vs_ref, kw_ref, vw_ref, g_ref,
                   qb_ref, kb_ref, vb_ref, ma_ref, mb_ref):
    i = pl.program_id(1)
    x = x_ref[0]
    ms = jnp.mean(x * x, axis=-1, keepdims=True)
    u = (x * lax.rsqrt(ms + EPS) * g1_ref[...]) * (1.0 + sc_ref[0]) + sh_ref[0]
    ub = u.astype(bf16)

    narrow = ("kc", "vc", "ksl", "vsl", "kwn", "vwn", "g")
    narrow_lo = lay[narrow[0]][0]
    z_narrow = _dot(ub, w_ref[:, narrow_lo:lay[narrow[-1]][1]])

    def proj(name):
        lo, hi = lay[name]
        if name in narrow:
            return z_narrow[:, lo - narrow_lo:hi - narrow_lo]
        return _dot(ub, w_ref[:, lo:hi])

    rows = x.shape[0]
    lane = lax.broadcasted_iota(jnp.int32, (rows, LANES), 1)
    low = lane < HEAD_DIM

    def pair_norm(z, gain):
        sq = z * z
        ss_a = jnp.sum(jnp.where(low, sq, 0.0), axis=1, keepdims=True)
        ss_b = jnp.sum(jnp.where(low, 0.0, sq), axis=1, keepdims=True)
        inv = lax.rsqrt(jnp.where(low, ss_a, ss_b) * (1.0 / HEAD_DIM) + EPS)
        return z * inv * gain

    def spread(z, extra):
        return jnp.concatenate([jnp.where(low, z, extra),
                                jnp.where(low, pltpu.roll(z, HEAD_DIM, 1), extra)], axis=1)

    scale = HEAD_DIM ** -0.5 * LOG2E
    zq = proj("qa")
    qa_ref[0] = jnp.concatenate(
        [spread(pair_norm(zq[:, j * LANES:(j + 1) * LANES], gq_ref[...]) * scale, 0.0)
         for j in range(NSA_HEADS // 2)], axis=1).astype(bf16)
    kc_ref[0] = proj("kc")
    vc_ref[0] = proj("vc")

    tok_blk = (i * rows + lax.broadcasted_iota(jnp.int32, (rows, LANES), 0)) // SEL_BLOCK
    onehot = jnp.where(lane - HEAD_DIM == tok_blk, 1.0, 0.0)
    ones_col = jnp.where(lane == HEAD_DIM, 1.0, 0.0)

    ks_ref[0] = spread(pair_norm(proj("ksl"), gk_ref[1:2, :]), onehot).astype(bf16)
    vs_ref[0] = spread(proj("vsl"), ones_col).astype(bf16)
    kw_ref[0] = spread(pair_norm(proj("kwn"), gk_ref[2:3, :]), 0.0).astype(bf16)
    vw_ref[0] = spread(proj("vwn"), ones_col).astype(bf16)
    g_ref[0] = jax.nn.sigmoid(proj("g")).T
    qb_ref[0] = (proj("qb") * scale).astype(bf16)
    kb_ref[0] = proj("kb").astype(bf16)
    vb_ref[0] = proj("vb").astype(bf16)
    ma_ref[0] = jax.nn.sigmoid(proj("ma")).astype(bf16)
    mb_ref[0] = jax.nn.sigmoid(proj("mb")).astype(bf16)


def _inproj(x, g1, scale1, shift1, w_packed, gq, gk):
    bsz, seq, d = x.shape
    lay, width = _layout(d)
    assert w_packed.shape == (d, width)
    tok = lambda w: pl.BlockSpec((1, TM, w), lambda b, i: (b, i, 0))
    full = lambda a: pl.BlockSpec(a.shape, lambda b, i: (0,) * a.ndim,
                                  pipeline_mode=pl.Buffered(1))
    mod = pl.BlockSpec((1, 1, d), lambda b, i: (b, 0, 0))
    out_w = [(_QA_W, bf16), (_CMP_W, f32), (_CMP_W, f32), (_KV_W, bf16), (_KV_W, bf16),
             (_KV_W, bf16), (_KV_W, bf16), (LANES, f32), (_SB_W, bf16), (_SB_W, bf16),
             (_SB_W, bf16), (d, bf16), (d, bf16)]
    gates_at = 7
    out_specs = [tok(w) for w, _ in out_w]
    out_shape = [jax.ShapeDtypeStruct((bsz, seq, w), dt) for w, dt in out_w]
    out_specs[gates_at] = pl.BlockSpec((1, LANES, TM), lambda b, i: (b, 0, i))
    out_shape[gates_at] = jax.ShapeDtypeStruct((bsz, LANES, seq), f32)
    return pl.pallas_call(
        functools.partial(_inproj_kernel, lay),
        grid=(bsz, seq // TM),
        in_specs=[tok(d), full(g1), mod, mod, full(w_packed), full(gq), full(gk)],
        out_specs=out_specs,
        out_shape=out_shape,
        compiler_params=_params(2),
        name="inproj",
    )(x, g1, scale1, shift1, w_packed, gq, gk)


def _compress_kernel(xk_ref, xv_ref, pos_ref, w1k_ref, w2k_ref, w1v_ref, w2v_ref, gk_ref,
                     ko_ref, vo_ref):
    nch = xk_ref.shape[1] // CMP_STRIDE

    def mlp(x_ref, w1_ref, w2_ref):
        first = jnp.zeros((nch, LANES), f32)
        second = jnp.zeros((nch, LANES), f32)
        for l in range(CMP_STRIDE):
            xl = x_ref[0, pl.ds(l, nch, stride=CMP_STRIDE), :]
            lo = l + CMP_STRIDE
            first = first + _dot((xl + pos_ref[l:l + 1, :]).astype(bf16), w1_ref[l])
            second = second + _dot((xl + pos_ref[lo:lo + 1, :]).astype(bf16), w1_ref[lo])
        pre = first + pltpu.roll(second, nch - 1, 0)
        hid = pre * jax.nn.sigmoid(pre)
        return _dot(hid.astype(bf16), w2_ref[...])

    k = mlp(xk_ref, w1k_ref, w2k_ref)
    v = mlp(xv_ref, w1v_ref, w2v_ref)
    for h in range(NSA_KV_HEADS):
        kh = k[:, h * HEAD_DIM:(h + 1) * HEAD_DIM]
        ms = jnp.mean(kh * kh, axis=-1, keepdims=True)
        ko_ref[0, h] = (kh * lax.rsqrt(ms + EPS) * gk_ref[...]).astype(bf16)
        vh = v[:, h * HEAD_DIM:(h + 1) * HEAD_DIM]
        ones_col = jnp.where(lax.broadcasted_iota(jnp.int32, vh.shape, 1) == 0, 1.0, 0.0)
        vo_ref[0, h] = jnp.concatenate([vh, ones_col], axis=1).astype(bf16)


def _compress(xk, xv, pos, w1k, w2k, w1v, w2v, gk0):
    bsz, seq, width = xk.shape
    nch = seq // CMP_STRIDE
    assert width == NSA_KV_HEADS * HEAD_DIM == LANES

    def both_heads(w):
        z = jnp.zeros_like(w)
        return jnp.concatenate([jnp.concatenate([w, z], axis=2),
                                jnp.concatenate([z, w], axis=2)], axis=1).astype(bf16)

    w1 = lambda w: both_heads(w.reshape(CMP_BLOCK, HEAD_DIM, w.shape[1]))
    w2 = lambda w: both_heads(w[None])[0]
    args = (xk, xv, jnp.tile(pos, (1, NSA_KV_HEADS)), w1(w1k), w2(w2k), w1(w1v), w2(w2v), gk0)
    blk = pl.BlockSpec((1, seq, width), lambda b: (b, 0, 0))
    full = lambda a: pl.BlockSpec(a.shape, lambda b: (0,) * a.ndim)
    out = lambda w: pl.BlockSpec((1, NSA_KV_HEADS, nch, w), lambda b: (b, 0, 0, 0))
    shape = lambda w: jax.ShapeDtypeStruct((bsz, NSA_KV_HEADS, nch, w), bf16)
    return pl.pallas_call(
        _compress_kernel,
        grid=(bsz,),
        in_specs=[blk, blk] + [full(a) for a in args[2:]],
        out_specs=[out(HEAD_DIM), out(LANES)],
        out_shape=[shape(HEAD_DIM), shape(LANES)],
        compiler_params=_params(1),
        name="compress",
    )(*args)


def _flash_step(s, v, m_ref, acc_ref):
    m_prev = m_ref[...]
    m_new = jnp.maximum(m_prev, jnp.max(s, axis=1, keepdims=True))
    alpha = jnp.exp2(m_prev - m_new)
    p = jnp.exp2(s - jnp.concatenate([m_new] * (s.shape[1] // LANES), axis=1))
    acc_ref[...] = alpha * acc_ref[...] + _dot(p.astype(bf16), v)
    m_ref[...] = m_new


def _flash_init(m_ref, acc_ref):
    m_ref[...] = jnp.full(m_ref.shape, MASKED, f32)
    acc_ref[...] = jnp.zeros(acc_ref.shape, f32)


def _flash_out_t(acc_ref):
    acc_t = acc_ref[...].T
    return acc_t[:HEAD_DIM] / acc_t[HEAD_DIM:HEAD_DIM + 1]


def _nsa_kernel(n_top, q_ref, g_ref, kc_ref, vc_ref, ks_ref, vs_ref, kw_ref, vw_ref,
                bc_ref, bn_ref, ovt_ref, o_ref, qs_ref, qw_ref, oc_ref,
                m_ref, acc_ref, mw_ref, accw_ref):
    i = pl.program_id(1)
    kv_heads = range(NSA_KV_HEADS)
    rows = NSA_GROUP * TQ
    start = i * TQ
    assert WINDOW % TQ == 0 and TQ % KU == 0
    first_interior = WINDOW // TQ
    window_extra = (WINDOW - TQ) // KU - 1

    def attend(hk, q_rows_ref, k_ref, v_ref, off, width, bias, state):
        off = pl.multiple_of(off, KU)
        lanes = slice(hk * LANES, (hk + 1) * LANES)
        s = _dot_nt(q_rows_ref[hk], k_ref[0, pl.ds(off, width), lanes])
        if bias is not None:
            s = s + bias
        _flash_step(s, v_ref[0, pl.ds(off, width), lanes], state[0].at[hk], state[1].at[hk])

    def when(cond, guarded):
        return pl.when(cond) if guarded else (lambda fn: fn())

    def near_steps(q_rows_ref, k_ref, v_ref, state, guarded, extra=0):
        group = lambda hk: slice(hk * NSA_GROUP, (hk + 1) * NSA_GROUP)

        @when(i >= 1, guarded)
        def _():
            for hk in kv_heads:
                bias = bn_ref[group(hk)].reshape(rows, KU + TQ)
                if extra:
                    bias = jnp.concatenate([jnp.zeros((rows, extra * KU), f32), bias], axis=1)
                attend(hk, q_rows_ref, k_ref, v_ref, start - (1 + extra) * KU,
                       TQ + (1 + extra) * KU, bias, state)

        if guarded:
            @pl.when(i == 0)
            def _():
                for hk in kv_heads:
                    attend(hk, q_rows_ref, k_ref, v_ref, start, TQ,
                           bn_ref[group(hk), :, KU:].reshape(rows, TQ), state)

    def select(hk):
        heads = range(hk * NSA_GROUP, (hk + 1) * NSA_GROUP)
        qpad = jnp.concatenate([q_ref[0, :, g * LANES:(g + 1) * LANES] for g in heads], axis=0)
        qw_ref[hk] = qpad

        bc = bc_ref[hk * NSA_GROUP:(hk + 1) * NSA_GROUP].reshape(rows, bc_ref.shape[2])
        s_c = _dot_nt(qpad[:, :HEAD_DIM], kc_ref[0, hk]) + bc
        m_c = jnp.maximum(jnp.max(s_c, axis=1, keepdims=True), 0.1 * MASKED)
        e_c = jnp.exp2(s_c - m_c).astype(bf16)
        oc_ref[hk] = _dot(e_c, vc_ref[0, hk])

        nblk = ovt_ref.shape[0] - 8
        imp = jnp.zeros((nblk, TQ), f32)
        for g in range(NSA_GROUP):
            t = _dot_nt(ovt_ref[...], e_c[g * TQ:(g + 1) * TQ])
            imp = imp + t[:nblk] / jnp.maximum(t[nblk:nblk + 1], 1e-30)
        blk = lax.broadcasted_iota(jnp.int32, (nblk, TQ), 0)
        cur = (start + lax.broadcasted_iota(jnp.int32, (nblk, TQ), 1)) // SEL_BLOCK
        forced = (blk == 0) | (blk == cur) | (blk == cur - 1)
        imp = jnp.where(blk > cur, NEG_BLOCK, imp + jnp.where(forced, FORCED_BONUS, 0.0))
        sub = 8
        groups = [imp[lo:lo + sub] for lo in range(0, nblk, sub)]
        ranks = [jnp.zeros((sub, TQ), f32) for _ in groups]
        row = lax.broadcasted_iota(jnp.int32, (sub, TQ), 0)
        for b2 in range(nblk):
            other = imp[b2:b2 + 1, :]
            for gi, grp in enumerate(groups):
                lo = gi * sub
                if lo > b2:
                    ranks[gi] = jnp.where(other >= grp, ranks[gi] + 1.0, ranks[gi])
                elif lo + sub - 1 < b2:
                    ranks[gi] = jnp.where(other > grp, ranks[gi] + 1.0, ranks[gi])
                else:
                    ranks[gi] = ranks[gi] + jnp.where(row + lo > b2,
                                                      jnp.where(other >= grp, 1.0, 0.0),
                                                      jnp.where(other > grp, 1.0, 0.0))
        rank = jnp.concatenate(ranks, axis=0)
        usable = (rank < n_top) & (blk <= cur)
        sel_t = jnp.where(usable, 0.0, UNSELECTED)
        sel_pad = jnp.concatenate([jnp.zeros((LANES - nblk, TQ), f32), sel_t], axis=0).T
        sel_rows = jnp.concatenate([sel_pad.astype(bf16)] * NSA_GROUP, axis=0)
        qs_ref[hk] = qpad + sel_rows

    def head(guarded):
        for hk in kv_heads:
            select(hk)

        win = (mw_ref, accw_ref)
        _flash_init(*win)

        @when(i >= first_interior, guarded)
        def _():
            r = lax.broadcasted_iota(jnp.int32, (rows, TQ), 0) & (TQ - 1)
            c = lax.broadcasted_iota(jnp.int32, (rows, TQ), 1)
            for hk in kv_heads:
                attend(hk, qw_ref, kw_ref, vw_ref, start - WINDOW, TQ,
                       jnp.where(c > r, 0.0, MASKED), win)

        near_steps(qw_ref, kw_ref, vw_ref, win, guarded, window_extra)
        _flash_init(m_ref, acc_ref)

    def tail(guarded, extra=0):
        near_steps(qs_ref, ks_ref, vs_ref, (m_ref, acc_ref), guarded, extra)
        outs = []
        for hk in kv_heads:
            oc_t = oc_ref[hk].T
            o_c = oc_t[:HEAD_DIM] / jnp.maximum(oc_t[HEAD_DIM:HEAD_DIM + 1], 1e-30)
            o_s = _flash_out_t(acc_ref.at[hk])
            o_w = _flash_out_t(accw_ref.at[hk])
            gates = g_ref[0]
            for g in range(NSA_GROUP):
                sl = slice(g * TQ, (g + 1) * TQ)
                at = hk * _GATE_ROWS + 3 * g
                outs.append(gates[at:at + 1] * o_c[:, sl]
                            + gates[at + 1:at + 2] * o_s[:, sl]
                            + gates[at + 2:at + 3] * o_w[:, sl])
        o_ref[0] = jnp.concatenate(outs, axis=0).T.astype(bf16)

    interior = i >= first_interior
    pl.when(interior)(lambda: head(False))
    pl.when(jnp.logical_not(interior))(lambda: head(True))

    sel = (m_ref, acc_ref)
    n_far = jnp.maximum(i * (TQ // KU) - 1, 0)

    def far_steps(off, n_steps):
        for step in range(n_steps):
            for hk in kv_heads:
                attend(hk, qs_ref, ks_ref, vs_ref, off + step * (2 * KU), 2 * KU, None, sel)

    def far_trip(c, carry):
        far_steps(c * (8 * KU), 4)
        return carry

    lax.fori_loop(0, n_far // 8, far_trip, 0)
    rem_off = (n_far // 8) * (8 * KU)

    @pl.when((n_far & 4) != 0)
    def _():
        far_steps(rem_off, 2)

    edge = jnp.logical_not(interior)

    @pl.when(edge & ((n_far & 2) != 0))
    def _():
        far_steps(rem_off + (n_far & 4) * KU, 1)

    @pl.when(edge & ((n_far & 1) != 0))
    def _():
        for hk in kv_heads:
            attend(hk, qs_ref, ks_ref, vs_ref, rem_off + (n_far & 6) * KU, KU, None, sel)

    for extra in range(4):
        pl.when(interior & ((n_far & 3) == extra))(functools.partial(tail, False, extra))
    pl.when(edge)(lambda: tail(True))


def _nsa(qa, gates, kcmp, vcmp, ks, vs, kw, vw, cmp_bias, near_bias, ovt):
    bsz, seq, _ = qa.shape
    nc = kcmp.shape[2]
    n_top = min(SEL_TOPK, seq // SEL_BLOCK)
    rows = NSA_GROUP * TQ
    tok = lambda w: pl.BlockSpec((1, TQ, w), lambda b, i: (b, i, 0))
    kv = pl.BlockSpec((1, seq, NSA_KV_HEADS * LANES), lambda b, i: (b, 0, 0))
    cmp = lambda a: pl.BlockSpec((1,) + a.shape[1:], lambda b, i: (b, 0, 0, 0))
    per_kv = lambda width, dt: pltpu.VMEM((NSA_KV_HEADS, rows, width), dt)
    return pl.pallas_call(
        functools.partial(_nsa_kernel, n_top),
        grid=(bsz, seq // TQ),
        in_specs=[tok(NSA_HEADS * LANES),
                  pl.BlockSpec((1, LANES, TQ), lambda b, i: (b, 0, i)),
                  cmp(kcmp), cmp(vcmp), kv, kv, kv, kv,
                  pl.BlockSpec((NSA_HEADS, TQ, nc), lambda b, i: (0, i, 0)),
                  pl.BlockSpec((NSA_HEADS, TQ, KU + TQ), lambda b, i: (0, 0, 0)),
                  pl.BlockSpec(ovt.shape, lambda b, i: (0, 0))],
        out_specs=tok(NSA_HEADS * HEAD_DIM),
        out_shape=jax.ShapeDtypeStruct((bsz, seq, NSA_HEADS * HEAD_DIM), bf16),
        scratch_shapes=[per_kv(LANES, bf16), per_kv(LANES, bf16),
                        per_kv(LANES, f32), per_kv(LANES, f32), per_kv(LANES, f32),
                        per_kv(LANES, f32), per_kv(LANES, f32)],
        compiler_params=_params(2),
        name="nsa",
    )(qa, gates, kcmp, vcmp, ks, vs, kw, vw, cmp_bias, near_bias, ovt)


def _sb_kernel(q_ref, k_ref, v_ref, tri_ref, o_ref, carry_ref, acc_ref):
    i = pl.program_id(2)
    n_heads = carry_ref.shape[0]
    lane = lax.broadcasted_iota(jnp.int32, (TS, LANES), 1)
    q_heads = []
    for pair in range(n_heads // 2):
        q = q_ref[0, :, pair * LANES:(pair + 1) * LANES]
        zero = jnp.zeros_like(q)
        q_heads += [jnp.where(lane < HEAD_DIM, q, zero), jnp.where(lane >= HEAD_DIM, q, zero)]
    pair_lanes = lambda hh: slice((hh // 2) * LANES, (hh // 2 + 1) * LANES)
    r = lax.broadcasted_iota(jnp.int32, (TS, TS), 0)
    c = lax.broadcasted_iota(jnp.int32, (TS, TS), 1)
    before = c < r

    def chunks(jobs, first):
        offs = [pl.multiple_of(off, TS) for off, _ in jobs]
        stage = []
        for (_, diagonal), off in zip(jobs, offs):
            for hh in range(n_heads):
                z = _dot_nt(q_heads[hh], k_ref[0, pl.ds(off, TS), pair_lanes(hh)])
                soft = jnp.log2(1.0 + jnp.exp2(-jnp.abs(z)))
                log_keep = jnp.minimum(-z, 0.0) - soft
                log_sig = log_keep + z
                if diagonal:
                    log_keep = jnp.where(before, log_keep, 0.0)
                later = _dot(log_keep.astype(bf16), tri_ref[...])
                stage.append((log_sig + later, jnp.sum(log_keep, axis=1, keepdims=True)))
        for hh in range(n_heads):
            carry = None if first else carry_ref[hh]
            acc = None if first else acc_ref[hh]
            for j, ((_, diagonal), off) in enumerate(zip(jobs, offs)):
                base, total = stage[n_heads * j + hh]
                if carry is not None:
                    base = base + jnp.concatenate([carry] * (TS // LANES), axis=1)
                a = jnp.exp2(base)
                if diagonal:
                    a = jnp.where(before, a, 0.0)
                out = _dot(a.astype(bf16), v_ref[0, pl.ds(off, TS), pair_lanes(hh)])
                acc = out if acc is None else acc + out
                total = jnp.broadcast_to(total, (TS, LANES))
                carry = total if carry is None else carry + total
            carry_ref[hh] = carry
            acc_ref[hh] = acc

    @pl.when(i == 0)
    def _():
        chunks([(0, True)], True)

    @pl.when(i >= 1)
    def _():
        chunks([(i * TS, True), ((i - 1) * TS, False)], True)

    def any_live():
        return jnp.max(carry_ref[...]) > EXP2_UNDERFLOW

    def more(state):
        n, live = state
        return jnp.logical_and(n < i, live)

    def older(state):
        n, _ = state
        chunks([((i - 1 - n) * TS, False)], False)
        return n + 1, any_live()

    lax.while_loop(more, older, (jnp.int32(1), any_live()))
    o_ref[0] = jnp.concatenate(
        [jnp.where(lane < HEAD_DIM, acc_ref[2 * pair], acc_ref[2 * pair + 1])
         for pair in range(n_heads // 2)], axis=1).astype(bf16)


SB_STEP_HEADS = 8


def _sb(qb, kb, vb, tri):
    bsz, seq, width = qb.shape
    step_w = SB_STEP_HEADS * HEAD_DIM
    q_spec = pl.BlockSpec((1, TS, step_w), lambda b, p, i: (b, i, p))
    kv_spec = pl.BlockSpec((1, seq, step_w), lambda b, p, i: (b, 0, p))
    state = pltpu.VMEM((SB_STEP_HEADS, TS, LANES), f32)
    return pl.pallas_call(
        _sb_kernel,
        grid=(bsz, width // step_w, seq // TS),
        in_specs=[q_spec, kv_spec, kv_spec, pl.BlockSpec(tri.shape, lambda b, p, i: (0, 0))],
        out_specs=q_spec,
        out_shape=jax.ShapeDtypeStruct((bsz, seq, width), bf16),
        scratch_shapes=[state, state],
        compiler_params=_params(3),
        name="sb",
    )(qb, kb, vb, tri)


def _post_kernel(ya_ref, yb_ref, ma_ref, mb_ref, x_ref, gate1_ref, g2_ref, sc_ref, sh_ref,
                 gate_ref, wa_ref, wb_ref, wo_ref, w1_ref, w2_ref, o_ref):
    y_a = _dot(ya_ref[0], wa_ref[...])
    y_b = _dot(yb_ref[0], wb_ref[...])
    mixed = ma_ref[0].astype(f32) * y_a + mb_ref[0].astype(f32) * y_b
    hres = x_ref[0] + gate1_ref[0] * _dot(mixed.astype(bf16), wo_ref[...])
    d = hres.shape[1]
    ms = jnp.mean(hres * hres, axis=-1, keepdims=True)
    u = (hres * lax.rsqrt(ms + EPS) * g2_ref[...]) * (1.0 + sc_ref[0]) + sh_ref[0]
    ub = u.astype(bf16)
    ff = jnp.zeros(hres.shape, f32)
    for c in range(w1_ref.shape[1] // d):
        hid = jnp.maximum(_dot(ub, w1_ref[:, c * d:(c + 1) * d]), 0.0)
        ff = ff + _dot((hid * hid).astype(bf16), w2_ref[c * d:(c + 1) * d, :])
    o_ref[0] = hres + gate_ref[0] * ff


def _post(ya, yb, ma, mb, x, gate1, g2, scale2, shift2, gate2, wa, wb, wo, w1, w2):
    bsz, seq, d = x.shape
    tok = lambda w: pl.BlockSpec((1, TM, w), lambda b, i: (b, i, 0))
    mod = pl.BlockSpec((1, 1, d), lambda b, i: (b, 0, 0))
    const = lambda a: pl.BlockSpec(a.shape, lambda b, i: (0,) * a.ndim,
                                   pipeline_mode=pl.Buffered(1))
    return pl.pallas_call(
        _post_kernel,
        grid=(bsz, seq // TM),
        in_specs=[tok(ya.shape[2]), tok(yb.shape[2]), tok(d), tok(d), tok(d), mod,
                  const(g2), mod, mod, mod, const(wa), const(wb), const(wo), const(w1), const(w2)],
        out_specs=tok(d),
        out_shape=jax.ShapeDtypeStruct((bsz, seq, d), f32),
        compiler_params=_params(2),
        name="post",
    )(ya, yb, ma, mb, x, gate1, g2, scale2, shift2, gate2, wa, wb, wo, w1, w2)


def _overlap_t(nc_pad, nsel_pad, nc, nsel):
    c_start = np.arange(nc_pad) * CMP_STRIDE
    s_start = np.arange(nsel_pad) * SEL_BLOCK
    ov = (np.minimum(c_start[None, :] + CMP_BLOCK, s_start[:, None] + SEL_BLOCK)
          - np.maximum(c_start[None, :], s_start[:, None]))
    ov = np.clip(ov, 0, CMP_BLOCK).astype(np.float32) / CMP_BLOCK
    ov[nsel:, :] = 0.0
    ov[:, nc:] = 0.0
    return ov


def _layer(h, mod, rel_tiles, p):
    bsz, seq, d = h.shape
    shift1, scale1, gate1, shift2, scale2, gate2 = [
        mod[:, k * d:(k + 1) * d].reshape(bsz, 1, d) for k in range(6)]
    cmp_bias, near_bias = rel_tiles

    gq = jnp.tile(p["q_norm_g"], 2).reshape(1, LANES)
    gk = jnp.tile(p["k_norm_g"], (1, 2))

    (qa, kc, vc, ks, vs, kw, vw, gates, qb, kb, vb, ma, mb) = _inproj(
        h, p["norm1_g"].reshape(1, d), scale1, shift1, _pack_w_in(p["w_in"], d), gq, gk)

    nch = seq // CMP_STRIDE
    kcmp, vcmp = _compress(kc, vc, p["cmp_pos"], p["cmp_k_w1"], p["cmp_k_w2"],
                           p["cmp_v_w1"], p["cmp_v_w2"], p["k_norm_g"][0].reshape(1, HEAD_DIM))

    nc = (seq - CMP_BLOCK) // CMP_STRIDE + 1
    ovt = jnp.asarray(np.concatenate([_overlap_t(nch, HEAD_DIM, nc, seq // SEL_BLOCK),
                                      np.ones((8, nch), np.float32)], axis=0), bf16)
    y_nsa = _nsa(qa, gates, kcmp, vcmp, ks, vs, kw, vw, cmp_bias, near_bias, ovt)

    y_sb = _sb(qb, kb, vb, jnp.asarray(np.tril(np.ones((TS, TS)), -1), bf16))

    return _post(y_nsa, y_sb, ma, mb, h, gate1, p["norm2_g"].reshape(1, d), scale2, shift2, gate2,
                 p["w_up_nsa"].astype(bf16), p["w_up_sb"].astype(bf16), p["w_out"].astype(bf16),
                 p["mlp_w1"].astype(bf16), p["mlp_w2"].astype(bf16))


def kernel(x, c, rel_bias, ada_w, ada_b, norm1_g, norm2_g, w_in, cmp_pos, cmp_k_w1, cmp_k_w2,
           cmp_v_w1, cmp_v_w2, q_norm_g, k_norm_g, w_up_nsa, w_up_sb, w_out, mlp_w1, mlp_w2):
    bsz, seq, d = x.shape
    assert seq % TM == 0 and seq // SEL_BLOCK <= HEAD_DIM and seq >= WINDOW + TQ
    assert CMP_BLOCK == 2 * CMP_STRIDE and KU % SEL_BLOCK == 0
    tbl = rel_bias.astype(f32)
    rel_tiles = _bias_tiles(tbl, seq, seq // CMP_STRIDE)
    stacked = dict(norm1_g=norm1_g, norm2_g=norm2_g, w_in=w_in, cmp_pos=cmp_pos,
                   cmp_k_w1=cmp_k_w1, cmp_k_w2=cmp_k_w2, cmp_v_w1=cmp_v_w1, cmp_v_w2=cmp_v_w2,
                   q_norm_g=q_norm_g, k_norm_g=k_norm_g, w_up_nsa=w_up_nsa, w_up_sb=w_up_sb,
                   w_out=w_out, mlp_w1=mlp_w1, mlp_w2=mlp_w2)
    h = x
    for layer in range(ada_w.shape[0]):
        mod = _adaln(c, ada_w[layer], ada_b[layer])
        h = _layer(h, mod, rel_tiles, {k: v[layer] for k, v in stacked.items()})
    return h
```

```python
import functools
import math

import numpy as np
import jax
import jax.numpy as jnp
from jax import lax
from jax.experimental import pallas as pl
from jax.experimental.pallas import tpu as pltpu

f32 = jnp.float32
bf16 = jnp.bfloat16

HEAD_DIM = 64
NSA_HEADS = 8
NSA_KV_HEADS = 2
NSA_GROUP = NSA_HEADS // NSA_KV_HEADS
SB_HEADS = 8
CMP_BLOCK = 32
CMP_STRIDE = 16
SEL_BLOCK = 64
SEL_TOPK = 16
WINDOW = 512
N_BUCKETS = 32
MAX_DISTANCE = 128
EPS = 1e-6
FORCED_BONUS = 1e4
NEG_BLOCK = -1e9

LANES = 128
SUBLANES = 8
MASKED = -1e30
UNSELECTED = -1e9
LOG2E = math.log2(math.e)
EXP2_UNDERFLOW = -150.0
VMEM_LIMIT = 56 * 1024 * 1024

TQ = 256
KU = 128
TS = 256
TM = 512


def _bucket_thresholds():
    n = np.arange(0, 4 * MAX_DISTANCE)
    max_exact = N_BUCKETS // 2
    nf = np.maximum(n, 1).astype(np.float32)
    large = max_exact + (np.log(nf / max_exact) / math.log(MAX_DISTANCE / max_exact)
                         * (N_BUCKETS - max_exact)).astype(np.int32)
    large = np.minimum(large, N_BUCKETS - 1)
    b = np.where(n < max_exact, n, large)
    assert np.all(np.diff(b) >= 0) and b[-1] == N_BUCKETS - 1
    return [int(np.argmax(b >= k)) for k in range(N_BUCKETS)]


BUCKET_START = _bucket_thresholds()
assert BUCKET_START[-1] <= KU


def _dot(a, b):
    return jnp.dot(a, b, preferred_element_type=f32)


def _dot_nt(a, b):
    return lax.dot_general(a, b, (((1,), (1,)), ((), ())), preferred_element_type=f32)


def _split(a):
    hi = a.astype(bf16)
    lo = (a - hi.astype(f32)).astype(bf16)
    return hi, lo


def _params(n_grid):
    return pltpu.CompilerParams(dimension_semantics=("arbitrary",) * n_grid,
                                vmem_limit_bytes=VMEM_LIMIT)


def _adaln_kernel(c_ref, w_ref, b_ref, o_ref):
    c = c_ref[...]
    a = c * jax.nn.sigmoid(c)
    ah, al = _split(a)
    wh, wl = _split(w_ref[...])
    o_ref[...] = _dot(ah, wh) + _dot(ah, wl) + _dot(al, wh) + b_ref[...]


def _adaln(c, w, b):
    bsz, d = c.shape
    n = w.shape[1]
    return pl.pallas_call(
        _adaln_kernel,
        grid=(n // d,),
        in_specs=[pl.BlockSpec((bsz, d), lambda j: (0, 0)),
                  pl.BlockSpec((d, d), lambda j: (0, j)),
                  pl.BlockSpec((1, d), lambda j: (0, j))],
        out_specs=pl.BlockSpec((bsz, d), lambda j: (0, j)),
        out_shape=jax.ShapeDtypeStruct((bsz, n), f32),
        compiler_params=_params(1),
        name="adaln",
    )(c, w, b.reshape(1, n))


def _bias_of_dist(dist, tbl_ref, h):
    out = jnp.full(dist.shape, tbl_ref[0, h], f32)
    for k in range(1, N_BUCKETS):
        out = jnp.where(dist >= BUCKET_START[k], tbl_ref[k, h], out)
    return jnp.where(dist >= 0, out, MASKED)


def _cmp_bias_kernel(tbl_ref, o_ref):
    h = pl.program_id(0)
    i = pl.program_id(1)
    rows, nc = o_ref.shape[1], o_ref.shape[2]
    t = i * rows + lax.broadcasted_iota(jnp.int32, (rows, LANES), 0)
    for lo in range(0, nc, LANES):
        last_first = lo * CMP_STRIDE + CMP_BLOCK - 1
        last_final = (lo + LANES - 1) * CMP_STRIDE + CMP_BLOCK - 1
        d_max = i * rows + rows - 1 - last_first
        d_min = i * rows - last_final
        cols = slice(lo, lo + LANES)

        @pl.when(d_max < 0)
        def _(cols=cols):
            o_ref[0, :, cols] = jnp.full((rows, LANES), MASKED * LOG2E, f32)

        @pl.when(d_min >= BUCKET_START[-1])
        def _(cols=cols):
            o_ref[0, :, cols] = jnp.full((rows, LANES), tbl_ref[N_BUCKETS - 1, h] * LOG2E, f32)

        @pl.when((d_max >= 0) & (d_min < BUCKET_START[-1]))
        def _(cols=cols, lo=lo):
            j = lo + lax.broadcasted_iota(jnp.int32, (rows, LANES), 1)
            dist = t - (j * CMP_STRIDE + CMP_BLOCK - 1)
            o_ref[0, :, cols] = _bias_of_dist(dist, tbl_ref, h) * LOG2E


def _near_bias_kernel(tbl_ref, o_ref):
    h = pl.program_id(0)
    r = lax.broadcasted_iota(jnp.int32, (TQ, KU + TQ), 0)
    c = lax.broadcasted_iota(jnp.int32, (TQ, KU + TQ), 1)
    o_ref[0] = (_bias_of_dist(r - c + KU, tbl_ref, h) - tbl_ref[N_BUCKETS - 1, h]) * LOG2E


def _bias_tiles(rel_bias, seq, nc_pad):
    tbl = rel_bias.astype(f32)
    smem = pl.BlockSpec(memory_space=pltpu.SMEM)
    rows = 512
    cmp_bias = pl.pallas_call(
        _cmp_bias_kernel,
        grid=(NSA_HEADS, seq // rows),
        in_specs=[smem],
        out_specs=pl.BlockSpec((1, rows, nc_pad), lambda h, i: (h, i, 0)),
        out_shape=jax.ShapeDtypeStruct((NSA_HEADS, seq, nc_pad), f32),
        compiler_params=_params(2),
        name="cmp_bias",
    )(tbl)
    near_bias = pl.pallas_call(
        _near_bias_kernel,
        grid=(NSA_HEADS,),
        in_specs=[smem],
        out_specs=pl.BlockSpec((1, TQ, KU + TQ), lambda h: (h, 0, 0)),
        out_shape=jax.ShapeDtypeStruct((NSA_HEADS, TQ, KU + TQ), f32),
        compiler_params=_params(1),
        name="near_bias",
    )(tbl)
    return cmp_bias, near_bias


_QA_W = NSA_HEADS * LANES
_KV_W = NSA_KV_HEADS * LANES
_CMP_W = NSA_KV_HEADS * HEAD_DIM
_SB_W = SB_HEADS * HEAD_DIM
_GATE_ROWS = NSA_GROUP * 3


def _layout(d_model):
    names = ["qa", "kc", "vc", "ksl", "vsl", "kwn", "vwn", "g", "qb", "kb", "vb", "ma", "mb"]
    widths = [NSA_HEADS * HEAD_DIM] + [_CMP_W] * 6 + [LANES] + [_SB_W] * 3 + [d_model, d_model]
    offs = np.concatenate([[0], np.cumsum(widths)])
    return {n: (int(offs[i]), int(offs[i + 1])) for i, n in enumerate(names)}, int(offs[-1])


def _pack_w_in(w_in, d_model):
    q_w = NSA_HEADS * HEAD_DIM
    kv_w = NSA_KV_HEADS * HEAD_DIM
    g_w = NSA_HEADS * 3
    sizes = [q_w] + [kv_w] * 6 + [g_w] + [_SB_W] * 3 + [d_model, d_model]
    offs = np.concatenate([[0], np.cumsum(sizes)])
    parts = [w_in[:, int(offs[i]):int(offs[i + 1])] for i in range(len(sizes))]
    parts[7] = jnp.pad(parts[7], ((0, 0), (0, LANES - g_w)))
    return jnp.concatenate(parts, axis=1).astype(bf16)


def _inproj_kernel(lay, x_ref, g1_ref, sc_ref, sh_ref, w_ref, gq_ref, gk_ref,
                   qa_ref, kc_ref, vc_ref, ks_ref, vs_ref, kw_ref, vw_ref, g_ref,
                   qb_ref, kb_ref, vb_ref, ma_ref, mb_ref):
    i = pl.program_id(1)
    x = x_ref[0]
    ms = jnp.mean(x * x, axis=-1, keepdims=True)
    u = (x * lax.rsqrt(ms + EPS) * g1_ref[...]) * (1.0 + sc_ref[0]) + sh_ref[0]
    ub = u.astype(bf16)

    narrow = ("kc", "vc", "ksl", "vsl", "kwn", "vwn", "g")
    narrow_lo = lay[narrow[0]][0]
    z_narrow = _dot(ub, w_ref[:, narrow_lo:lay[narrow[-1]][1]])

    def proj(name):
        lo, hi = lay[name]
        if name in narrow:
            return z_narrow[:, lo - narrow_lo:hi - narrow_lo]
        return _dot(ub, w_ref[:, lo:hi])

    rows = x.shape[0]
    lane = lax.broadcasted_iota(jnp.int32, (rows, LANES), 1)
    low = lane < HEAD_DIM

    def pair_norm(z, gain):
        sq = z * z
        ss_a = jnp.sum(jnp.where(low, sq, 0.0), axis=1, keepdims=True)
        ss_b = jnp.sum(jnp.where(low, 0.0, sq), axis=1, keepdims=True)
        inv = lax.rsqrt(jnp.where(low, ss_a, ss_b) * (1.0 / HEAD_DIM) + EPS)
        return z * inv * gain

    def spread(z, extra):
        return jnp.concatenate([jnp.where(low, z, extra),
                                jnp.where(low, pltpu.roll(z, HEAD_DIM, 1), extra)], axis=1)

    scale = HEAD_DIM ** -0.5 * LOG2E
    zq = proj("qa")
    qa_ref[0] = jnp.concatenate(
        [spread(pair_norm(zq[:, j * LANES:(j + 1) * LANES], gq_ref[...]) * scale, 0.0)
         for j in range(NSA_HEADS // 2)], axis=1).astype(bf16)
    kc_ref[0] = proj("kc")
    vc_ref[0] = proj("vc")

    tok_blk = (i * rows + lax.broadcasted_iota(jnp.int32, (rows, LANES), 0)) // SEL_BLOCK
    onehot = jnp.where(lane - HEAD_DIM == tok_blk, 1.0, 0.0)
    ones_col = jnp.where(lane == HEAD_DIM, 1.0, 0.0)

    ks_ref[0] = spread(pair_norm(proj("ksl"), gk_ref[1:2, :]), onehot).astype(bf16)
    vs_ref[0] = spread(proj("vsl"), ones_col).astype(bf16)
    kw_ref[0] = spread(pair_norm(proj("kwn"), gk_ref[2:3, :]), 0.0).astype(bf16)
    vw_ref[0] = spread(proj("vwn"), ones_col).astype(bf16)
    g_ref[0] = jax.nn.sigmoid(proj("g")).T
    qb_ref[0] = (proj("qb") * scale).astype(bf16)
    kb_ref[0] = proj("kb").astype(bf16)
    vb_ref[0] = proj("vb").astype(bf16)
    ma_ref[0] = jax.nn.sigmoid(proj("ma")).astype(bf16)
    mb_ref[0] = jax.nn.sigmoid(proj("mb")).astype(bf16)


def _inproj(x, g1, scale1, shift1, w_packed, gq, gk):
    bsz, seq, d = x.shape
    lay, width = _layout(d)
    assert w_packed.shape == (d, width)
    tok = lambda w: pl.BlockSpec((1, TM, w), lambda b, i: (b, i, 0))
    full = lambda a: pl.BlockSpec(a.shape, lambda b, i: (0,) * a.ndim,
                                  pipeline_mode=pl.Buffered(1))
    mod = pl.BlockSpec((1, 1, d), lambda b, i: (b, 0, 0))
    out_w = [(_QA_W, bf16), (_CMP_W, f32), (_CMP_W, f32), (_KV_W, bf16), (_KV_W, bf16),
             (_KV_W, bf16), (_KV_W, bf16), (LANES, f32), (_SB_W, bf16), (_SB_W, bf16),
             (_SB_W, bf16), (d, bf16), (d, bf16)]
    gates_at = 7
    out_specs = [tok(w) for w, _ in out_w]
    out_shape = [jax.ShapeDtypeStruct((bsz, seq, w), dt) for w, dt in out_w]
    out_specs[gates_at] = pl.BlockSpec((1, LANES, TM), lambda b, i: (b, 0, i))
    out_shape[gates_at] = jax.ShapeDtypeStruct((bsz, LANES, seq), f32)
    return pl.pallas_call(
        functools.partial(_inproj_kernel, lay),
        grid=(bsz, seq // TM),
        in_specs=[tok(d), full(g1), mod, mod, full(w_packed), full(gq), full(gk)],
        out_specs=out_specs,
        out_shape=out_shape,
        compiler_params=_params(2),
        name="inproj",
    )(x, g1, scale1, shift1, w_packed, gq, gk)


def _compress_kernel(xk_ref, xv_ref, pos_ref, w1k_ref, w2k_ref, w1v_ref, w2v_ref, gk_ref,
                     ko_ref, vo_ref):
    nch = xk_ref.shape[1] // CMP_STRIDE

    def mlp(x_ref, w1_ref, w2_ref):
        first = jnp.zeros((nch, LANES), f32)
        second = jnp.zeros((nch, LANES), f32)
        for l in range(CMP_STRIDE):
            xl = x_ref[0, pl.ds(l, nch, stride=CMP_STRIDE), :]
            lo = l + CMP_STRIDE
            first = first + _dot((xl + pos_ref[l:l + 1, :]).astype(bf16), w1_ref[l])
            second = second + _dot((xl + pos_ref[lo:lo + 1, :]).astype(bf16), w1_ref[lo])
        pre = first + pltpu.roll(second, nch - 1, 0)
        hid = pre * jax.nn.sigmoid(pre)
        return _dot(hid.astype(bf16), w2_ref[...])

    k = mlp(xk_ref, w1k_ref, w2k_ref)
    v = mlp(xv_ref, w1v_ref, w2v_ref)
    for h in range(NSA_KV_HEADS):
        kh = k[:, h * HEAD_DIM:(h + 1) * HEAD_DIM]
        ms = jnp.mean(kh * kh, axis=-1, keepdims=True)
        ko_ref[0, h] = (kh * lax.rsqrt(ms + EPS) * gk_ref[...]).astype(bf16)
        vh = v[:, h * HEAD_DIM:(h + 1) * HEAD_DIM]
        ones_col = jnp.where(lax.broadcasted_iota(jnp.int32, vh.shape, 1) == 0, 1.0, 0.0)
        vo_ref[0, h] = jnp.concatenate([vh, ones_col], axis=1).astype(bf16)


def _compress(xk, xv, pos, w1k, w2k, w1v, w2v, gk0):
    bsz, seq, width = xk.shape
    nch = seq // CMP_STRIDE
    assert width == NSA_KV_HEADS * HEAD_DIM == LANES

    def both_heads(w):
        z = jnp.zeros_like(w)
        return jnp.concatenate([jnp.concatenate([w, z], axis=2),
                                jnp.concatenate([z, w], axis=2)], axis=1).astype(bf16)

    w1 = lambda w: both_heads(w.reshape(CMP_BLOCK, HEAD_DIM, w.shape[1]))
    w2 = lambda w: both_heads(w[None])[0]
    args = (xk, xv, jnp.tile(pos, (1, NSA_KV_HEADS)), w1(w1k), w2(w2k), w1(w1v), w2(w2v), gk0)
    blk = pl.BlockSpec((1, seq, width), lambda b: (b, 0, 0))
    full = lambda a: pl.BlockSpec(a.shape, lambda b: (0,) * a.ndim)
    out = lambda w: pl.BlockSpec((1, NSA_KV_HEADS, nch, w), lambda b: (b, 0, 0, 0))
    shape = lambda w: jax.ShapeDtypeStruct((bsz, NSA_KV_HEADS, nch, w), bf16)
    return pl.pallas_call(
        _compress_kernel,
        grid=(bsz,),
        in_specs=[blk, blk] + [full(a) for a in args[2:]],
        out_specs=[out(HEAD_DIM), out(LANES)],
        out_shape=[shape(HEAD_DIM), shape(LANES)],
        compiler_params=_params(1),
        name="compress",
    )(*args)


def _flash_step(s, v, m_ref, acc_ref):
    m_prev = m_ref[...]
    m_new = jnp.maximum(m_prev, jnp.max(s, axis=1, keepdims=True))
    alpha = jnp.exp2(m_prev - m_new)
    p = jnp.exp2(s - jnp.concatenate([m_new] * (s.shape[1] // LANES), axis=1))
    acc_ref[...] = alpha * acc_ref[...] + _dot(p.astype(bf16), v)
    m_ref[...] = m_new


def _flash_init(m_ref, acc_ref):
    m_ref[...] = jnp.full(m_ref.shape, MASKED, f32)
    acc_ref[...] = jnp.zeros(acc_ref.shape, f32)


def _flash_out_t(acc_ref):
    acc_t = acc_ref[...].T
    return acc_t[:HEAD_DIM] / acc_t[HEAD_DIM:HEAD_DIM + 1]


def _nsa_kernel(n_top, q_ref, g_ref, kc_ref, vc_ref, ks_ref, vs_ref, kw_ref, vw_ref,
                bc_ref, bn_ref, ovt_ref, o_ref, qs_ref, qw_ref, oc_ref,
                m_ref, acc_ref, mw_ref, accw_ref):
    i = pl.program_id(1)
    kv_heads = range(NSA_KV_HEADS)
    rows = NSA_GROUP * TQ
    start = i * TQ
    assert WINDOW % TQ == 0 and TQ % KU == 0
    first_interior = WINDOW // TQ
    window_extra = (WINDOW - TQ) // KU - 1

    def attend(hk, q_rows_ref, k_ref, v_ref, off, width, bias, state):
        off = pl.multiple_of(off, KU)
        lanes = slice(hk * LANES, (hk + 1) * LANES)
        s = _dot_nt(q_rows_ref[hk], k_ref[0, pl.ds(off, width), lanes])
        if bias is not None:
            s = s + bias
        _flash_step(s, v_ref[0, pl.ds(off, width), lanes], state[0].at[hk], state[1].at[hk])

    def when(cond, guarded):
        return pl.when(cond) if guarded else (lambda fn: fn())

    def near_steps(q_rows_ref, k_ref, v_ref, state, guarded, extra=0):
        group = lambda hk: slice(hk * NSA_GROUP, (hk + 1) * NSA_GROUP)

        @when(i >= 1, guarded)
        def _():
            for hk in kv_heads:
                bias = bn_ref[group(hk)].reshape(rows, KU + TQ)
                if extra:
                    bias = jnp.concatenate([jnp.zeros((rows, extra * KU), f32), bias], axis=1)
                attend(hk, q_rows_ref, k_ref, v_ref, start - (1 + extra) * KU,
                       TQ + (1 + extra) * KU, bias, state)

        if guarded:
            @pl.when(i == 0)
            def _():
                for hk in kv_heads:
                    attend(hk, q_rows_ref, k_ref, v_ref, start, TQ,
                           bn_ref[group(hk), :, KU:].reshape(rows, TQ), state)

    def select(hk):
        heads = range(hk * NSA_GROUP, (hk + 1) * NSA_GROUP)
        qpad = jnp.concatenate([q_ref[0, :, g * LANES:(g + 1) * LANES] for g in heads], axis=0)
        qw_ref[hk] = qpad

        bc = bc_ref[hk * NSA_GROUP:(hk + 1) * NSA_GROUP].reshape(rows, bc_ref.shape[2])
        s_c = _dot_nt(qpad[:, :HEAD_DIM], kc_ref[0, hk]) + bc
        m_c = jnp.maximum(jnp.max(s_c, axis=1, keepdims=True), 0.1 * MASKED)
        e_c = jnp.exp2(s_c - m_c).astype(bf16)
        oc_ref[hk] = _dot(e_c, vc_ref[0, hk])

        nblk = ovt_ref.shape[0] - SUBLANES
        imp = jnp.zeros((nblk, TQ), f32)
        for g in range(NSA_GROUP):
            t = _dot_nt(ovt_ref[...], e_c[g * TQ:(g + 1) * TQ])
            imp = imp + t[:nblk] / jnp.maximum(t[nblk:nblk + 1], 1e-30)
        blk = lax.broadcasted_iota(jnp.int32, (nblk, TQ), 0)
        cur = (start + lax.broadcasted_iota(jnp.int32, (nblk, TQ), 1)) // SEL_BLOCK
        forced = (blk == 0) | (blk == cur) | (blk == cur - 1)
        imp = jnp.where(blk > cur, NEG_BLOCK, imp + jnp.where(forced, FORCED_BONUS, 0.0))
        sub = SUBLANES
        groups = [imp[lo:lo + sub] for lo in range(0, nblk, sub)]
        ranks = [jnp.zeros((sub, TQ), f32) for _ in groups]
        row = lax.broadcasted_iota(jnp.int32, (sub, TQ), 0)
        for b2 in range(nblk):
            other = imp[b2:b2 + 1, :]
            for gi, grp in enumerate(groups):
                lo = gi * sub
                if lo > b2:
                    ranks[gi] = jnp.where(other >= grp, ranks[gi] + 1.0, ranks[gi])
                elif lo + sub - 1 < b2:
                    ranks[gi] = jnp.where(other > grp, ranks[gi] + 1.0, ranks[gi])
                else:
                    ranks[gi] = ranks[gi] + jnp.where(row + lo > b2,
                                                      jnp.where(other >= grp, 1.0, 0.0),
                                                      jnp.where(other > grp, 1.0, 0.0))
        rank = jnp.concatenate(ranks, axis=0)
        usable = (rank < n_top) & (blk <= cur)
        sel_t = jnp.where(usable, 0.0, UNSELECTED)
        sel_pad = jnp.concatenate([jnp.zeros((LANES - nblk, TQ), f32), sel_t], axis=0).T
        sel_rows = jnp.concatenate([sel_pad.astype(bf16)] * NSA_GROUP, axis=0)
        qs_ref[hk] = qpad + sel_rows

    def head(guarded):
        for hk in kv_heads:
            select(hk)

        win = (mw_ref, accw_ref)
        _flash_init(*win)

        @when(i >= first_interior, guarded)
        def _():
            r = lax.broadcasted_iota(jnp.int32, (rows, TQ), 0) & (TQ - 1)
            c = lax.broadcasted_iota(jnp.int32, (rows, TQ), 1)
            for hk in kv_heads:
                attend(hk, qw_ref, kw_ref, vw_ref, start - WINDOW, TQ,
                       jnp.where(c > r, 0.0, MASKED), win)

        near_steps(qw_ref, kw_ref, vw_ref, win, guarded, window_extra)
        _flash_init(m_ref, acc_ref)

    def tail(guarded, extra=0):
        near_steps(qs_ref, ks_ref, vs_ref, (m_ref, acc_ref), guarded, extra)
        outs = []
        for hk in kv_heads:
            oc_t = oc_ref[hk].T
            o_c = oc_t[:HEAD_DIM] / jnp.maximum(oc_t[HEAD_DIM:HEAD_DIM + 1], 1e-30)
            o_s = _flash_out_t(acc_ref.at[hk])
            o_w = _flash_out_t(accw_ref.at[hk])
            gates = g_ref[0]
            for g in range(NSA_GROUP):
                sl = slice(g * TQ, (g + 1) * TQ)
                at = hk * _GATE_ROWS + 3 * g
                outs.append(gates[at:at + 1] * o_c[:, sl]
                            + gates[at + 1:at + 2] * o_s[:, sl]
                            + gates[at + 2:at + 3] * o_w[:, sl])
        o_ref[0] = jnp.concatenate(outs, axis=0).T.astype(bf16)

    interior = i >= first_interior
    pl.when(interior)(lambda: head(False))
    pl.when(jnp.logical_not(interior))(lambda: head(True))

    sel = (m_ref, acc_ref)
    n_far = jnp.maximum(i * (TQ // KU) - 1, 0)

    def far_steps(off, n_steps):
        for step in range(n_steps):
            for hk in kv_heads:
                attend(hk, qs_ref, ks_ref, vs_ref, off + step * (2 * KU), 2 * KU, None, sel)

    def far_trip(c, carry):
        far_steps(c * (8 * KU), 4)
        return carry

    lax.fori_loop(0, n_far // 8, far_trip, 0)
    rem_off = (n_far // 8) * (8 * KU)

    @pl.when((n_far & 4) != 0)
    def _():
        far_steps(rem_off, 2)

    edge = jnp.logical_not(interior)

    @pl.when(edge & ((n_far & 2) != 0))
    def _():
        far_steps(rem_off + (n_far & 4) * KU, 1)

    @pl.when(edge & ((n_far & 1) != 0))
    def _():
        for hk in kv_heads:
            attend(hk, qs_ref, ks_ref, vs_ref, rem_off + (n_far & 6) * KU, KU, None, sel)

    for extra in range(4):
        pl.when(interior & ((n_far & 3) == extra))(functools.partial(tail, False, extra))
    pl.when(edge)(lambda: tail(True))


def _nsa(qa, gates, kcmp, vcmp, ks, vs, kw, vw, cmp_bias, near_bias, ovt):
    bsz, seq, _ = qa.shape
    nc = kcmp.shape[2]
    n_top = min(SEL_TOPK, seq // SEL_BLOCK)
    rows = NSA_GROUP * TQ
    tok = lambda w: pl.BlockSpec((1, TQ, w), lambda b, i: (b, i, 0))
    kv = pl.BlockSpec((1, seq, NSA_KV_HEADS * LANES), lambda b, i: (b, 0, 0))
    cmp = lambda a: pl.BlockSpec((1,) + a.shape[1:], lambda b, i: (b, 0, 0, 0))
    per_kv = lambda width, dt: pltpu.VMEM((NSA_KV_HEADS, rows, width), dt)
    return pl.pallas_call(
        functools.partial(_nsa_kernel, n_top),
        grid=(bsz, seq // TQ),
        in_specs=[tok(NSA_HEADS * LANES),
                  pl.BlockSpec((1, LANES, TQ), lambda b, i: (b, 0, i)),
                  cmp(kcmp), cmp(vcmp), kv, kv, kv, kv,
                  pl.BlockSpec((NSA_HEADS, TQ, nc), lambda b, i: (0, i, 0)),
                  pl.BlockSpec((NSA_HEADS, TQ, KU + TQ), lambda b, i: (0, 0, 0)),
                  pl.BlockSpec(ovt.shape, lambda b, i: (0, 0))],
        out_specs=tok(NSA_HEADS * HEAD_DIM),
        out_shape=jax.ShapeDtypeStruct((bsz, seq, NSA_HEADS * HEAD_DIM), bf16),
        scratch_shapes=[per_kv(LANES, bf16), per_kv(LANES, bf16),
                        per_kv(LANES, f32), per_kv(LANES, f32), per_kv(LANES, f32),
                        per_kv(LANES, f32), per_kv(LANES, f32)],
        compiler_params=_params(2),
        name="nsa",
    )(qa, gates, kcmp, vcmp, ks, vs, kw, vw, cmp_bias, near_bias, ovt)


def _sb_kernel(q_ref, k_ref, v_ref, tri_ref, o_ref, carry_ref, acc_ref):
    i = pl.program_id(2)
    n_heads = carry_ref.shape[0]
    lane = lax.broadcasted_iota(jnp.int32, (TS, LANES), 1)
    q_heads = []
    for pair in range(n_heads // 2):
        q = q_ref[0, :, pair * LANES:(pair + 1) * LANES]
        zero = jnp.zeros_like(q)
        q_heads += [jnp.where(lane < HEAD_DIM, q, zero), jnp.where(lane >= HEAD_DIM, q, zero)]
    pair_lanes = lambda hh: slice((hh // 2) * LANES, (hh // 2 + 1) * LANES)
    r = lax.broadcasted_iota(jnp.int32, (TS, TS), 0)
    c = lax.broadcasted_iota(jnp.int32, (TS, TS), 1)
    before = c < r

    def chunks(jobs, first):
        offs = [pl.multiple_of(off, TS) for off, _ in jobs]
        stage = []
        for (_, diagonal), off in zip(jobs, offs):
            for hh in range(n_heads):
                z = _dot_nt(q_heads[hh], k_ref[0, pl.ds(off, TS), pair_lanes(hh)])
                soft = jnp.log2(1.0 + jnp.exp2(-jnp.abs(z)))
                log_keep = jnp.minimum(-z, 0.0) - soft
                log_sig = log_keep + z
                if diagonal:
                    log_keep = jnp.where(before, log_keep, 0.0)
                later = _dot(log_keep.astype(bf16), tri_ref[...])
                stage.append((log_sig + later, jnp.sum(log_keep, axis=1, keepdims=True)))
        for hh in range(n_heads):
            carry = None if first else carry_ref[hh]
            acc = None if first else acc_ref[hh]
            for j, ((_, diagonal), off) in enumerate(zip(jobs, offs)):
                base, total = stage[n_heads * j + hh]
                if carry is not None:
                    base = base + jnp.concatenate([carry] * (TS // LANES), axis=1)
                a = jnp.exp2(base)
                if diagonal:
                    a = jnp.where(before, a, 0.0)
                out = _dot(a.astype(bf16), v_ref[0, pl.ds(off, TS), pair_lanes(hh)])
                acc = out if acc is None else acc + out
                total = jnp.broadcast_to(total, (TS, LANES))
                carry = total if carry is None else carry + total
            carry_ref[hh] = carry
            acc_ref[hh] = acc

    @pl.when(i == 0)
    def _():
        chunks([(0, True)], True)

    @pl.when(i >= 1)
    def _():
        chunks([(i * TS, True), ((i - 1) * TS, False)], True)

    def any_live():
        return jnp.max(carry_ref[...]) > EXP2_UNDERFLOW

    def more(state):
        n, live = state
        return jnp.logical_and(n < i, live)

    def older(state):
        n, _ = state
        chunks([((i - 1 - n) * TS, False)], False)
        return n + 1, any_live()

    lax.while_loop(more, older, (jnp.int32(1), any_live()))
    o_ref[0] = jnp.concatenate(
        [jnp.where(lane < HEAD_DIM, acc_ref[2 * pair], acc_ref[2 * pair + 1])
         for pair in range(n_heads // 2)], axis=1).astype(bf16)


SB_STEP_HEADS = 8


def _sb(qb, kb, vb, tri):
    bsz, seq, width = qb.shape
    step_w = SB_STEP_HEADS * HEAD_DIM
    q_spec = pl.BlockSpec((1, TS, step_w), lambda b, p, i: (b, i, p))
    kv_spec = pl.BlockSpec((1, seq, step_w), lambda b, p, i: (b, 0, p))
    state = pltpu.VMEM((SB_STEP_HEADS, TS, LANES), f32)
    return pl.pallas_call(
        _sb_kernel,
        grid=(bsz, width // step_w, seq // TS),
        in_specs=[q_spec, kv_spec, kv_spec, pl.BlockSpec(tri.shape, lambda b, p, i: (0, 0))],
        out_specs=q_spec,
        out_shape=jax.ShapeDtypeStruct((bsz, seq, width), bf16),
        scratch_shapes=[state, state],
        compiler_params=_params(3),
        name="sb",
    )(qb, kb, vb, tri)


def _post_kernel(ya_ref, yb_ref, ma_ref, mb_ref, x_ref, gate1_ref, g2_ref, sc_ref, sh_ref,
                 gate_ref, wa_ref, wb_ref, wo_ref, w1_ref, w2_ref, o_ref):
    y_a = _dot(ya_ref[0], wa_ref[...])
    y_b = _dot(yb_ref[0], wb_ref[...])
    mixed = ma_ref[0].astype(f32) * y_a + mb_ref[0].astype(f32) * y_b
    hres = x_ref[0] + gate1_ref[0] * _dot(mixed.astype(bf16), wo_ref[...])
    d = hres.shape[1]
    ms = jnp.mean(hres * hres, axis=-1, keepdims=True)
    u = (hres * lax.rsqrt(ms + EPS) * g2_ref[...]) * (1.0 + sc_ref[0]) + sh_ref[0]
    ub = u.astype(bf16)
    ff = jnp.zeros(hres.shape, f32)
    for c in range(w1_ref.shape[1] // d):
        hid = jnp.maximum(_dot(ub, w1_ref[:, c * d:(c + 1) * d]), 0.0)
        ff = ff + _dot((hid * hid).astype(bf16), w2_ref[c * d:(c + 1) * d, :])
    o_ref[0] = hres + gate_ref[0] * ff


def _post(ya, yb, ma, mb, x, gate1, g2, scale2, shift2, gate2, wa, wb, wo, w1, w2):
    bsz, seq, d = x.shape
    tok = lambda w: pl.BlockSpec((1, TM, w), lambda b, i: (b, i, 0))
    mod = pl.BlockSpec((1, 1, d), lambda b, i: (b, 0, 0))
    const = lambda a: pl.BlockSpec(a.shape, lambda b, i: (0,) * a.ndim,
                                   pipeline_mode=pl.Buffered(1))
    return pl.pallas_call(
        _post_kernel,
        grid=(bsz, seq // TM),
        in_specs=[tok(ya.shape[2]), tok(yb.shape[2]), tok(d), tok(d), tok(d), mod,
                  const(g2), mod, mod, mod, const(wa), const(wb), const(wo), const(w1), const(w2)],
        out_specs=tok(d),
        out_shape=jax.ShapeDtypeStruct((bsz, seq, d), f32),
        compiler_params=_params(2),
        name="post",
    )(ya, yb, ma, mb, x, gate1, g2, scale2, shift2, gate2, wa, wb, wo, w1, w2)


def _overlap_t(nc_pad, nsel_pad, nc, nsel):
    c_start = np.arange(nc_pad) * CMP_STRIDE
    s_start = np.arange(nsel_pad) * SEL_BLOCK
    ov = (np.minimum(c_start[None, :] + CMP_BLOCK, s_start[:, None] + SEL_BLOCK)
          - np.maximum(c_start[None, :], s_start[:, None]))
    ov = np.clip(ov, 0, CMP_BLOCK).astype(np.float32) / CMP_BLOCK
    ov[nsel:, :] = 0.0
    ov[:, nc:] = 0.0
    return ov


def _layer(h, mod, rel_tiles, p):
    bsz, seq, d = h.shape
    shift1, scale1, gate1, shift2, scale2, gate2 = [
        mod[:, k * d:(k + 1) * d].reshape(bsz, 1, d) for k in range(6)]
    cmp_bias, near_bias = rel_tiles

    gq = jnp.tile(p["q_norm_g"], 2).reshape(1, LANES)
    gk = jnp.tile(p["k_norm_g"], (1, 2))

    (qa, kc, vc, ks, vs, kw, vw, gates, qb, kb, vb, ma, mb) = _inproj(
        h, p["norm1_g"].reshape(1, d), scale1, shift1, _pack_w_in(p["w_in"], d), gq, gk)

    nch = seq // CMP_STRIDE
    kcmp, vcmp = _compress(kc, vc, p["cmp_pos"], p["cmp_k_w1"], p["cmp_k_w2"],
                           p["cmp_v_w1"], p["cmp_v_w2"], p["k_norm_g"][0].reshape(1, HEAD_DIM))

    nc = (seq - CMP_BLOCK) // CMP_STRIDE + 1
    ovt = jnp.asarray(np.concatenate([_overlap_t(nch, HEAD_DIM, nc, seq // SEL_BLOCK),
                                      np.ones((SUBLANES, nch), np.float32)], axis=0), bf16)
    y_nsa = _nsa(qa, gates, kcmp, vcmp, ks, vs, kw, vw, cmp_bias, near_bias, ovt)

    y_sb = _sb(qb, kb, vb, jnp.asarray(np.tril(np.ones((TS, TS)), -1), bf16))

    return _post(y_nsa, y_sb, ma, mb, h, gate1, p["norm2_g"].reshape(1, d), scale2, shift2, gate2,
                 p["w_up_nsa"].astype(bf16), p["w_up_sb"].astype(bf16), p["w_out"].astype(bf16),
                 p["mlp_w1"].astype(bf16), p["mlp_w2"].astype(bf16))


def kernel(x, c, rel_bias, ada_w, ada_b, norm1_g, norm2_g, w_in, cmp_pos, cmp_k_w1, cmp_k_w2,
           cmp_v_w1, cmp_v_w2, q_norm_g, k_norm_g, w_up_nsa, w_up_sb, w_out, mlp_w1, mlp_w2):
    bsz, seq, d = x.shape
    assert seq % TM == 0 and seq // SEL_BLOCK <= HEAD_DIM and seq >= WINDOW + TQ
    assert CMP_BLOCK == 2 * CMP_STRIDE and KU % SEL_BLOCK == 0
    tbl = rel_bias.astype(f32)
    rel_tiles = _bias_tiles(tbl, seq, seq // CMP_STRIDE)
    stacked = dict(norm1_g=norm1_g, norm2_g=norm2_g, w_in=w_in, cmp_pos=cmp_pos,
                   cmp_k_w1=cmp_k_w1, cmp_k_w2=cmp_k_w2, cmp_v_w1=cmp_v_w1, cmp_v_w2=cmp_v_w2,
                   q_norm_g=q_norm_g, k_norm_g=k_norm_g, w_up_nsa=w_up_nsa, w_up_sb=w_up_sb,
                   w_out=w_out, mlp_w1=mlp_w1, mlp_w2=mlp_w2)
    h = x
    for layer in range(ada_w.shape[0]):
        mod = _adaln(c, ada_w[layer], ada_b[layer])
        h = _layer(h, mod, rel_tiles, {k: v[layer] for k, v in stacked.items()})
    return h
```

```python
import functools
import math

import numpy as np
import jax
import jax.numpy as jnp
from jax import lax
from jax.experimental import pallas as pl
from jax.experimental.pallas import tpu as pltpu

f32 = jnp.float32
bf16 = jnp.bfloat16

HEAD_DIM = 64
NSA_HEADS = 8
NSA_KV_HEADS = 2
NSA_GROUP = NSA_HEADS // NSA_KV_HEADS
SB_HEADS = 8
CMP_BLOCK = 32
CMP_STRIDE = 16
SEL_BLOCK = 64
SEL_TOPK = 16
WINDOW = 512
N_BUCKETS = 32
MAX_DISTANCE = 128
EPS = 1e-6
FORCED_BONUS = 1e4
NEG_BLOCK = -1e9

LANES = 128
SUBLANES = 8
MASKED = -1e30
UNSELECTED = -1e9
LOG2E = math.log2(math.e)
EXP2_UNDERFLOW = -150.0
VMEM_LIMIT = 56 * 1024 * 1024

TQ = 256
KU = 128
TS = 256
TM = 512


def _bucket_thresholds():
    n = np.arange(0, 4 * MAX_DISTANCE)
    max_exact = N_BUCKETS // 2
    nf = np.maximum(n, 1).astype(np.float32)
    large = max_exact + (np.log(nf / max_exact) / math.log(MAX_DISTANCE / max_exact)
                         * (N_BUCKETS - max_exact)).astype(np.int32)
    large = np.minimum(large, N_BUCKETS - 1)
    b = np.where(n < max_exact, n, large)
    assert np.all(np.diff(b) >= 0) and b[-1] == N_BUCKETS - 1
    return [int(np.argmax(b >= k)) for k in range(N_BUCKETS)]


BUCKET_START = _bucket_thresholds()
assert BUCKET_START[-1] <= KU


def _dot(a, b):
    return jnp.dot(a, b, preferred_element_type=f32)


def _dot_nt(a, b):
    return lax.dot_general(a, b, (((1,), (1,)), ((), ())), preferred_element_type=f32)


def _split(a):
    hi = a.astype(bf16)
    lo = (a - hi.astype(f32)).astype(bf16)
    return hi, lo


def _params(n_grid):
    return pltpu.CompilerParams(dimension_semantics=("arbitrary",) * n_grid,
                                vmem_limit_bytes=VMEM_LIMIT)


def _adaln_kernel(c_ref, w_ref, b_ref, o_ref):
    c = c_ref[...]
    a = c * jax.nn.sigmoid(c)
    ah, al = _split(a)
    wh, wl = _split(w_ref[...])
    o_ref[...] = _dot(ah, wh) + _dot(ah, wl) + _dot(al, wh) + b_ref[...]


def _adaln(c, w, b):
    bsz, d = c.shape
    n = w.shape[1]
    return pl.pallas_call(
        _adaln_kernel,
        grid=(n // d,),
        in_specs=[pl.BlockSpec((bsz, d), lambda j: (0, 0)),
                  pl.BlockSpec((d, d), lambda j: (0, j)),
                  pl.BlockSpec((1, d), lambda j: (0, j))],
        out_specs=pl.BlockSpec((bsz, d), lambda j: (0, j)),
        out_shape=jax.ShapeDtypeStruct((bsz, n), f32),
        compiler_params=_params(1),
        name="adaln",
    )(c, w, b.reshape(1, n))


def _bias_of_dist(dist, tbl_ref, h):
    out = jnp.full(dist.shape, tbl_ref[0, h], f32)
    for k in range(1, N_BUCKETS):
        out = jnp.where(dist >= BUCKET_START[k], tbl_ref[k, h], out)
    return jnp.where(dist >= 0, out, MASKED)


def _cmp_bias_kernel(tbl_ref, o_ref):
    h = pl.program_id(0)
    i = pl.program_id(1)
    rows, nc = o_ref.shape[1], o_ref.shape[2]
    t = i * rows + lax.broadcasted_iota(jnp.int32, (rows, LANES), 0)
    for lo in range(0, nc, LANES):
        last_first = lo * CMP_STRIDE + CMP_BLOCK - 1
        last_final = (lo + LANES - 1) * CMP_STRIDE + CMP_BLOCK - 1
        d_max = i * rows + rows - 1 - last_first
        d_min = i * rows - last_final
        cols = slice(lo, lo + LANES)

        @pl.when(d_max < 0)
        def _(cols=cols):
            o_ref[0, :, cols] = jnp.full((rows, LANES), MASKED * LOG2E, f32)

        @pl.when(d_min >= BUCKET_START[-1])
        def _(cols=cols):
            o_ref[0, :, cols] = jnp.full((rows, LANES), tbl_ref[N_BUCKETS - 1, h] * LOG2E, f32)

        @pl.when((d_max >= 0) & (d_min < BUCKET_START[-1]))
        def _(cols=cols, lo=lo):
            j = lo + lax.broadcasted_iota(jnp.int32, (rows, LANES), 1)
            dist = t - (j * CMP_STRIDE + CMP_BLOCK - 1)
            o_ref[0, :, cols] = _bias_of_dist(dist, tbl_ref, h) * LOG2E


def _near_bias_kernel(tbl_ref, o_ref):
    h = pl.program_id(0)
    r = lax.broadcasted_iota(jnp.int32, (TQ, KU + TQ), 0)
    c = lax.broadcasted_iota(jnp.int32, (TQ, KU + TQ), 1)
    o_ref[0] = (_bias_of_dist(r - c + KU, tbl_ref, h) - tbl_ref[N_BUCKETS - 1, h]) * LOG2E


def _bias_tiles(rel_bias, seq, nc_pad):
    tbl = rel_bias.astype(f32)
    smem = pl.BlockSpec(memory_space=pltpu.SMEM)
    rows = 512
    cmp_bias = pl.pallas_call(
        _cmp_bias_kernel,
        grid=(NSA_HEADS, seq // rows),
        in_specs=[smem],
        out_specs=pl.BlockSpec((1, rows, nc_pad), lambda h, i: (h, i, 0)),
        out_shape=jax.ShapeDtypeStruct((NSA_HEADS, seq, nc_pad), f32),
        compiler_params=_params(2),
        name="cmp_bias",
    )(tbl)
    near_bias = pl.pallas_call(
        _near_bias_kernel,
        grid=(NSA_HEADS,),
        in_specs=[smem],
        out_specs=pl.BlockSpec((1, TQ, KU + TQ), lambda h: (h, 0, 0)),
        out_shape=jax.ShapeDtypeStruct((NSA_HEADS, TQ, KU + TQ), f32),
        compiler_params=_params(1),
        name="near_bias",
    )(tbl)
    return cmp_bias, near_bias


_QA_W = NSA_HEADS * LANES
_KV_W = NSA_KV_HEADS * LANES
_CMP_W = NSA_KV_HEADS * HEAD_DIM
_SB_W = SB_HEADS * HEAD_DIM
_GATE_ROWS = NSA_GROUP * 3


def _layout(d_model):
    names = ["qa", "kc", "vc", "ksl", "vsl", "kwn", "vwn", "g", "qb", "kb", "vb", "ma", "mb"]
    widths = [NSA_HEADS * HEAD_DIM] + [_CMP_W] * 6 + [LANES] + [_SB_W] * 3 + [d_model, d_model]
    offs = np.concatenate([[0], np.cumsum(widths)])
    return {n: (int(offs[i]), int(offs[i + 1])) for i, n in enumerate(names)}, int(offs[-1])


def _pack_w_in(w_in, d_model):
    q_w = NSA_HEADS * HEAD_DIM
    kv_w = NSA_KV_HEADS * HEAD_DIM
    g_w = NSA_HEADS * 3
    sizes = [q_w] + [kv_w] * 6 + [g_w] + [_SB_W] * 3 + [d_model, d_model]
    offs = np.concatenate([[0], np.cumsum(sizes)])
    parts = [w_in[:, int(offs[i]):int(offs[i + 1])] for i in range(len(sizes))]
    parts[7] = jnp.pad(parts[7], ((0, 0), (0, LANES - g_w)))
    return jnp.concatenate(parts, axis=1).astype(bf16)


def _inproj_kernel(lay, x_ref, g1_ref, sc_ref, sh_ref, w_ref, gq_ref, gk_ref,
                   qa_ref, kc_ref, vc_ref, ks_ref, vs_ref, kw_ref, vw_ref, g_ref,
                   qb_ref, kb_ref, vb_ref, ma_ref, mb_ref):
    i = pl.program_id(1)
    x = x_ref[0]
    ms = jnp.mean(x * x, axis=-1, keepdims=True)
    u = (x * lax.rsqrt(ms + EPS) * g1_ref[...]) * (1.0 + sc_ref[0]) + sh_ref[0]
    ub = u.astype(bf16)

    narrow = ("kc", "vc", "ksl", "vsl", "kwn", "vwn", "g")
    narrow_lo = lay[narrow[0]][0]
    z_narrow = _dot(ub, w_ref[:, narrow_lo:lay[narrow[-1]][1]])

    def proj(name):
        lo, hi = lay[name]
        if name in narrow:
            return z_narrow[:, lo - narrow_lo:hi - narrow_lo]
        return _dot(ub, w_ref[:, lo:hi])

    rows = x.shape[0]
    lane = lax.broadcasted_iota(jnp.int32, (rows, LANES), 1)
    low = lane < HEAD_DIM

    def pair_norm(z, gain):
        sq = z * z
        ss_a = jnp.sum(jnp.where(low, sq, 0.0), axis=1, keepdims=True)
        ss_b = jnp.sum(jnp.where(low, 0.0, sq), axis=1, keepdims=True)
        inv = lax.rsqrt(jnp.where(low, ss_a, ss_b) * (1.0 / HEAD_DIM) + EPS)
        return z * inv * gain

    def spread(z, extra):
        return jnp.concatenate([jnp.where(low, z, extra),
                                jnp.where(low, pltpu.roll(z, HEAD_DIM, 1), extra)], axis=1)

    scale = HEAD_DIM ** -0.5 * LOG2E
    zq = proj("qa")
    qa_ref[0] = jnp.concatenate(
        [spread(pair_norm(zq[:, j * LANES:(j + 1) * LANES], gq_ref[...]) * scale, 0.0)
         for j in range(NSA_HEADS // 2)], axis=1).astype(bf16)
    kc_ref[0] = proj("kc")
    vc_ref[0] = proj("vc")

    tok_blk = (i * rows + lax.broadcasted_iota(jnp.int32, (rows, LANES), 0)) // SEL_BLOCK
    onehot = jnp.where(lane - HEAD_DIM == tok_blk, 1.0, 0.0)
    ones_col = jnp.where(lane == HEAD_DIM, 1.0, 0.0)

    ks_ref[0] = spread(pair_norm(proj("ksl"), gk_ref[1:2, :]), onehot).astype(bf16)
    vs_ref[0] = spread(proj("vsl"), ones_col).astype(bf16)
    kw_ref[0] = spread(pair_norm(proj("kwn"), gk_ref[2:3, :]), 0.0).astype(bf16)
    vw_ref[0] = spread(proj("vwn"), ones_col).astype(bf16)
    g_ref[0] = jax.nn.sigmoid(proj("g")).T
    qb_ref[0] = (proj("qb") * scale).astype(bf16)
    kb_ref[0] = proj("kb").astype(bf16)
    vb_ref[0] = proj("vb").astype(bf16)
    ma_ref[0] = jax.nn.sigmoid(proj("ma")).astype(bf16)
    mb_ref[0] = jax.nn.sigmoid(proj("mb")).astype(bf16)


def _inproj(x, g1, scale1, shift1, w_packed, gq, gk):
    bsz, seq, d = x.shape
    lay, width = _layout(d)
    assert w_packed.shape == (d, width)
    tok = lambda w: pl.BlockSpec((1, TM, w), lambda b, i: (b, i, 0))
    full = lambda a: pl.BlockSpec(a.shape, lambda b, i: (0,) * a.ndim,
                                  pipeline_mode=pl.Buffered(1))
    mod = pl.BlockSpec((1, 1, d), lambda b, i: (b, 0, 0))
    out_w = [(_QA_W, bf16), (_CMP_W, f32), (_CMP_W, f32), (_KV_W, bf16), (_KV_W, bf16),
             (_KV_W, bf16), (_KV_W, bf16), (LANES, f32), (_SB_W, bf16), (_SB_W, bf16),
             (_SB_W, bf16), (d, bf16), (d, bf16)]
    gates_at = 7
    out_specs = [tok(w) for w, _ in out_w]
    out_shape = [jax.ShapeDtypeStruct((bsz, seq, w), dt) for w, dt in out_w]
    out_specs[gates_at] = pl.BlockSpec((1, LANES, TM), lambda b, i: (b, 0, i))
    out_shape[gates_at] = jax.ShapeDtypeStruct((bsz, LANES, seq), f32)
    return pl.pallas_call(
        functools.partial(_inproj_kernel, lay),
        grid=(bsz, seq // TM),
        in_specs=[tok(d), full(g1), mod, mod, full(w_packed), full(gq), full(gk)],
        out_specs=out_specs,
        out_shape=out_shape,
        compiler_params=_params(2),
        name="inproj",
    )(x, g1, scale1, shift1, w_packed, gq, gk)


def _compress_kernel(xk_ref, xv_ref, pos_ref, w1k_ref, w2k_ref, w1v_ref, w2v_ref, gk_ref,
                     ko_ref, vo_ref):
    nch = xk_ref.shape[1] // CMP_STRIDE

    def mlp(x_ref, w1_ref, w2_ref):
        first = jnp.zeros((nch, LANES), f32)
        second = jnp.zeros((nch, LANES), f32)
        for l in range(CMP_STRIDE):
            xl = x_ref[0, pl.ds(l, nch, stride=CMP_STRIDE), :]
            lo = l + CMP_STRIDE
            first = first + _dot((xl + pos_ref[l:l + 1, :]).astype(bf16), w1_ref[l])
            second = second + _dot((xl + pos_ref[lo:lo + 1, :]).astype(bf16), w1_ref[lo])
        pre = first + pltpu.roll(second, nch - 1, 0)
        hid = pre * jax.nn.sigmoid(pre)
        return _dot(hid.astype(bf16), w2_ref[...])

    k = mlp(xk_ref, w1k_ref, w2k_ref)
    v = mlp(xv_ref, w1v_ref, w2v_ref)
    for h in range(NSA_KV_HEADS):
        kh = k[:, h * HEAD_DIM:(h + 1) * HEAD_DIM]
        ms = jnp.mean(kh * kh, axis=-1, keepdims=True)
        ko_ref[0, h] = (kh * lax.rsqrt(ms + EPS) * gk_ref[...]).astype(bf16)
        vh = v[:, h * HEAD_DIM:(h + 1) * HEAD_DIM]
        ones_col = jnp.where(lax.broadcasted_iota(jnp.int32, vh.shape, 1) == 0, 1.0, 0.0)
        vo_ref[0, h] = jnp.concatenate([vh, ones_col], axis=1).astype(bf16)


def _compress(xk, xv, pos, w1k, w2k, w1v, w2v, gk0):
    bsz, seq, width = xk.shape
    nch = seq // CMP_STRIDE
    assert width == NSA_KV_HEADS * HEAD_DIM == LANES

    def both_heads(w):
        z = jnp.zeros_like(w)
        return jnp.concatenate([jnp.concatenate([w, z], axis=2),
                                jnp.concatenate([z, w], axis=2)], axis=1).astype(bf16)

    w1 = lambda w: both_heads(w.reshape(CMP_BLOCK, HEAD_DIM, w.shape[1]))
    w2 = lambda w: both_heads(w[None])[0]
    args = (xk, xv, jnp.tile(pos, (1, NSA_KV_HEADS)), w1(w1k), w2(w2k), w1(w1v), w2(w2v), gk0)
    blk = pl.BlockSpec((1, seq, width), lambda b: (b, 0, 0))
    full = lambda a: pl.BlockSpec(a.shape, lambda b: (0,) * a.ndim)
    out = lambda w: pl.BlockSpec((1, NSA_KV_HEADS, nch, w), lambda b: (b, 0, 0, 0))
    shape = lambda w: jax.ShapeDtypeStruct((bsz, NSA_KV_HEADS, nch, w), bf16)
    return pl.pallas_call(
        _compress_kernel,
        grid=(bsz,),
        in_specs=[blk, blk] + [full(a) for a in args[2:]],
        out_specs=[out(HEAD_DIM), out(LANES)],
        out_shape=[shape(HEAD_DIM), shape(LANES)],
        compiler_params=_params(1),
        name="compress",
    )(*args)


def _flash_step(s, v, m_ref, acc_ref, first=False):
    row_max = jnp.max(s, axis=1, keepdims=True)
    if first:
        m_new = jnp.broadcast_to(row_max, m_ref.shape)
    else:
        m_prev = m_ref[...]
        m_new = jnp.maximum(m_prev, row_max)
    p = jnp.exp2(s - jnp.concatenate([m_new] * (s.shape[1] // LANES), axis=1))
    pv = _dot(p.astype(bf16), v)
    acc_ref[...] = pv if first else jnp.exp2(m_prev - m_new) * acc_ref[...] + pv
    m_ref[...] = m_new


def _flash_init(m_ref, acc_ref):
    m_ref[...] = jnp.full(m_ref.shape, MASKED, f32)
    acc_ref[...] = jnp.zeros(acc_ref.shape, f32)


def _flash_out_t(acc_ref):
    acc_t = acc_ref[...].T
    return acc_t[:HEAD_DIM] / acc_t[HEAD_DIM:HEAD_DIM + 1]


def _nsa_kernel(n_top, q_ref, g_ref, kc_ref, vc_ref, ks_ref, vs_ref, kw_ref, vw_ref,
                bc_ref, bn_ref, ovt_ref, o_ref, qs_ref, qw_ref, oc_ref,
                m_ref, acc_ref, mw_ref, accw_ref):
    i = pl.program_id(1)
    kv_heads = range(NSA_KV_HEADS)
    rows = NSA_GROUP * TQ
    start = i * TQ
    assert WINDOW % TQ == 0 and TQ % KU == 0
    first_interior = WINDOW // TQ
    window_extra = (WINDOW - TQ) // KU - 1

    def attend(hk, q_rows_ref, k_ref, v_ref, off, width, bias, state, first=False):
        off = pl.multiple_of(off, KU)
        lanes = slice(hk * LANES, (hk + 1) * LANES)
        s = _dot_nt(q_rows_ref[hk], k_ref[0, pl.ds(off, width), lanes])
        if bias is not None:
            s = s + bias
        _flash_step(s, v_ref[0, pl.ds(off, width), lanes], state[0].at[hk], state[1].at[hk],
                    first)

    def when(cond, guarded):
        return pl.when(cond) if guarded else (lambda fn: fn())

    def near_steps(q_rows_ref, k_ref, v_ref, state, guarded, extra=0):
        group = lambda hk: slice(hk * NSA_GROUP, (hk + 1) * NSA_GROUP)

        @when(i >= 1, guarded)
        def _():
            for hk in kv_heads:
                bias = bn_ref[group(hk)].reshape(rows, KU + TQ)
                if extra:
                    bias = jnp.concatenate([jnp.zeros((rows, extra * KU), f32), bias], axis=1)
                attend(hk, q_rows_ref, k_ref, v_ref, start - (1 + extra) * KU,
                       TQ + (1 + extra) * KU, bias, state)

        if guarded:
            @pl.when(i == 0)
            def _():
                for hk in kv_heads:
                    attend(hk, q_rows_ref, k_ref, v_ref, start, TQ,
                           bn_ref[group(hk), :, KU:].reshape(rows, TQ), state)

    def select(hk):
        heads = range(hk * NSA_GROUP, (hk + 1) * NSA_GROUP)
        qpad = jnp.concatenate([q_ref[0, :, g * LANES:(g + 1) * LANES] for g in heads], axis=0)
        qw_ref[hk] = qpad

        bc = bc_ref[hk * NSA_GROUP:(hk + 1) * NSA_GROUP].reshape(rows, bc_ref.shape[2])
        s_c = _dot_nt(qpad[:, :HEAD_DIM], kc_ref[0, hk]) + bc
        m_c = jnp.maximum(jnp.max(s_c, axis=1, keepdims=True), 0.1 * MASKED)
        e_c = jnp.exp2(s_c - m_c).astype(bf16)
        oc_ref[hk] = _dot(e_c, vc_ref[0, hk])

        nblk = ovt_ref.shape[0] - SUBLANES
        imp = jnp.zeros((nblk, TQ), f32)
        for g in range(NSA_GROUP):
            t = _dot_nt(ovt_ref[...], e_c[g * TQ:(g + 1) * TQ])
            imp = imp + t[:nblk] / jnp.maximum(t[nblk:nblk + 1], 1e-30)
        blk = lax.broadcasted_iota(jnp.int32, (nblk, TQ), 0)
        cur = (start + lax.broadcasted_iota(jnp.int32, (nblk, TQ), 1)) // SEL_BLOCK
        forced = (blk == 0) | (blk == cur) | (blk == cur - 1)
        imp = jnp.where(blk > cur, NEG_BLOCK, imp + jnp.where(forced, FORCED_BONUS, 0.0))
        sub = SUBLANES
        groups = [imp[lo:lo + sub] for lo in range(0, nblk, sub)]
        ranks = [jnp.zeros((sub, TQ), f32) for _ in groups]
        row = lax.broadcasted_iota(jnp.int32, (sub, TQ), 0)
        for b2 in range(nblk):
            other = imp[b2:b2 + 1, :]
            for gi, grp in enumerate(groups):
                lo = gi * sub
                if lo > b2:
                    ranks[gi] = jnp.where(other >= grp, ranks[gi] + 1.0, ranks[gi])
                elif lo + sub - 1 < b2:
                    ranks[gi] = jnp.where(other > grp, ranks[gi] + 1.0, ranks[gi])
                else:
                    ranks[gi] = ranks[gi] + jnp.where(row + lo > b2,
                                                      jnp.where(other >= grp, 1.0, 0.0),
                                                      jnp.where(other > grp, 1.0, 0.0))
        rank = jnp.concatenate(ranks, axis=0)
        usable = (rank < n_top) & (blk <= cur)
        sel_t = jnp.where(usable, 0.0, UNSELECTED)
        sel_pad = jnp.concatenate([jnp.zeros((LANES - nblk, TQ), f32), sel_t], axis=0).T
        sel_rows = jnp.concatenate([sel_pad.astype(bf16)] * NSA_GROUP, axis=0)
        qs_ref[hk] = qpad + sel_rows

    def head(guarded):
        for hk in kv_heads:
            select(hk)

        win = (mw_ref, accw_ref)
        if guarded:
            _flash_init(*win)

        @when(i >= first_interior, guarded)
        def _():
            r = lax.broadcasted_iota(jnp.int32, (rows, TQ), 0) & (TQ - 1)
            c = lax.broadcasted_iota(jnp.int32, (rows, TQ), 1)
            for hk in kv_heads:
                attend(hk, qw_ref, kw_ref, vw_ref, start - WINDOW, TQ,
                       jnp.where(c > r, 0.0, MASKED), win, first=not guarded)

        near_steps(qw_ref, kw_ref, vw_ref, win, guarded, window_extra)
        _flash_init(m_ref, acc_ref)

    def tail(guarded, extra=0):
        near_steps(qs_ref, ks_ref, vs_ref, (m_ref, acc_ref), guarded, extra)
        outs = []
        for hk in kv_heads:
            oc_t = oc_ref[hk].T
            o_c = oc_t[:HEAD_DIM] / jnp.maximum(oc_t[HEAD_DIM:HEAD_DIM + 1], 1e-30)
            o_s = _flash_out_t(acc_ref.at[hk])
            o_w = _flash_out_t(accw_ref.at[hk])
            gates = g_ref[0]
            for g in range(NSA_GROUP):
                sl = slice(g * TQ, (g + 1) * TQ)
                at = hk * _GATE_ROWS + 3 * g
                outs.append(gates[at:at + 1] * o_c[:, sl]
                            + gates[at + 1:at + 2] * o_s[:, sl]
                            + gates[at + 2:at + 3] * o_w[:, sl])
        o_ref[0] = jnp.concatenate(outs, axis=0).T.astype(bf16)

    interior = i >= first_interior
    pl.when(interior)(lambda: head(False))
    pl.when(jnp.logical_not(interior))(lambda: head(True))

    sel = (m_ref, acc_ref)
    n_far = jnp.maximum(i * (TQ // KU) - 1, 0)

    def far_steps(off, n_steps):
        for step in range(n_steps):
            for hk in kv_heads:
                attend(hk, qs_ref, ks_ref, vs_ref, off + step * (2 * KU), 2 * KU, None, sel)

    def far_trip(c, carry):
        far_steps(c * (8 * KU), 4)
        return carry

    lax.fori_loop(0, n_far // 8, far_trip, 0)
    rem_off = (n_far // 8) * (8 * KU)

    @pl.when((n_far & 4) != 0)
    def _():
        far_steps(rem_off, 2)

    edge = jnp.logical_not(interior)

    @pl.when(edge & ((n_far & 2) != 0))
    def _():
        far_steps(rem_off + (n_far & 4) * KU, 1)

    @pl.when(edge & ((n_far & 1) != 0))
    def _():
        for hk in kv_heads:
            attend(hk, qs_ref, ks_ref, vs_ref, rem_off + (n_far & 6) * KU, KU, None, sel)

    for extra in range(4):
        pl.when(interior & ((n_far & 3) == extra))(functools.partial(tail, False, extra))
    pl.when(edge)(lambda: tail(True))


def _nsa(qa, gates, kcmp, vcmp, ks, vs, kw, vw, cmp_bias, near_bias, ovt):
    bsz, seq, _ = qa.shape
    nc = kcmp.shape[2]
    n_top = min(SEL_TOPK, seq // SEL_BLOCK)
    rows = NSA_GROUP * TQ
    tok = lambda w: pl.BlockSpec((1, TQ, w), lambda b, i: (b, i, 0))
    kv = pl.BlockSpec((1, seq, NSA_KV_HEADS * LANES), lambda b, i: (b, 0, 0))
    cmp = lambda a: pl.BlockSpec((1,) + a.shape[1:], lambda b, i: (b, 0, 0, 0))
    per_kv = lambda width, dt: pltpu.VMEM((NSA_KV_HEADS, rows, width), dt)
    return pl.pallas_call(
        functools.partial(_nsa_kernel, n_top),
        grid=(bsz, seq // TQ),
        in_specs=[tok(NSA_HEADS * LANES),
                  pl.BlockSpec((1, LANES, TQ), lambda b, i: (b, 0, i)),
                  cmp(kcmp), cmp(vcmp), kv, kv, kv, kv,
                  pl.BlockSpec((NSA_HEADS, TQ, nc), lambda b, i: (0, i, 0)),
                  pl.BlockSpec((NSA_HEADS, TQ, KU + TQ), lambda b, i: (0, 0, 0)),
                  pl.BlockSpec(ovt.shape, lambda b, i: (0, 0))],
        out_specs=tok(NSA_HEADS * HEAD_DIM),
        out_shape=jax.ShapeDtypeStruct((bsz, seq, NSA_HEADS * HEAD_DIM), bf16),
        scratch_shapes=[per_kv(LANES, bf16), per_kv(LANES, bf16),
                        per_kv(LANES, f32), per_kv(LANES, f32), per_kv(LANES, f32),
                        per_kv(LANES, f32), per_kv(LANES, f32)],
        compiler_params=_params(2),
        name="nsa",
    )(qa, gates, kcmp, vcmp, ks, vs, kw, vw, cmp_bias, near_bias, ovt)


def _sb_kernel(q_ref, k_ref, v_ref, tri_ref, o_ref, carry_ref, acc_ref):
    i = pl.program_id(2)
    n_heads = carry_ref.shape[0]
    lane = lax.broadcasted_iota(jnp.int32, (TS, LANES), 1)
    q_heads = []
    for pair in range(n_heads // 2):
        q = q_ref[0, :, pair * LANES:(pair + 1) * LANES]
        zero = jnp.zeros_like(q)
        q_heads += [jnp.where(lane < HEAD_DIM, q, zero), jnp.where(lane >= HEAD_DIM, q, zero)]
    pair_lanes = lambda hh: slice((hh // 2) * LANES, (hh // 2 + 1) * LANES)
    r = lax.broadcasted_iota(jnp.int32, (TS, TS), 0)
    c = lax.broadcasted_iota(jnp.int32, (TS, TS), 1)
    before = c < r

    def chunks(jobs, first):
        offs = [pl.multiple_of(off, TS) for off, _ in jobs]
        stage = []
        for (_, diagonal), off in zip(jobs, offs):
            for hh in range(n_heads):
                z = _dot_nt(q_heads[hh], k_ref[0, pl.ds(off, TS), pair_lanes(hh)])
                soft = jnp.log2(1.0 + jnp.exp2(-jnp.abs(z)))
                log_keep = jnp.minimum(-z, 0.0) - soft
                log_sig = log_keep + z
                if diagonal:
                    log_keep = jnp.where(before, log_keep, 0.0)
                later = _dot(log_keep.astype(bf16), tri_ref[...])
                stage.append((log_sig + later, jnp.sum(log_keep, axis=1, keepdims=True)))
        for hh in range(n_heads):
            carry = None if first else carry_ref[hh]
            acc = None if first else acc_ref[hh]
            for j, ((_, diagonal), off) in enumerate(zip(jobs, offs)):
                base, total = stage[n_heads * j + hh]
                if carry is not None:
                    base = base + jnp.concatenate([carry] * (TS // LANES), axis=1)
                a = jnp.exp2(base)
                if diagonal:
                    a = jnp.where(before, a, 0.0)
                out = _dot(a.astype(bf16), v_ref[0, pl.ds(off, TS), pair_lanes(hh)])
                acc = out if acc is None else acc + out
                total = jnp.broadcast_to(total, (TS, LANES))
                carry = total if carry is None else carry + total
            carry_ref[hh] = carry
            acc_ref[hh] = acc

    @pl.when(i == 0)
    def _():
        chunks([(0, True)], True)

    @pl.when(i >= 1)
    def _():
        chunks([(i * TS, True), ((i - 1) * TS, False)], True)

    def any_live():
        return jnp.max(carry_ref[...]) > EXP2_UNDERFLOW

    def more(state):
        n, live = state
        return jnp.logical_and(n < i, live)

    def older(state):
        n, _ = state
        chunks([((i - 1 - n) * TS, False)], False)
        return n + 1, any_live()

    lax.while_loop(more, older, (jnp.int32(1), any_live()))
    o_ref[0] = jnp.concatenate(
        [jnp.where(lane < HEAD_DIM, acc_ref[2 * pair], acc_ref[2 * pair + 1])
         for pair in range(n_heads // 2)], axis=1).astype(bf16)


SB_STEP_HEADS = 8


def _sb(qb, kb, vb, tri):
    bsz, seq, width = qb.shape
    step_w = SB_STEP_HEADS * HEAD_DIM
    q_spec = pl.BlockSpec((1, TS, step_w), lambda b, p, i: (b, i, p))
    kv_spec = pl.BlockSpec((1, seq, step_w), lambda b, p, i: (b, 0, p))
    state = pltpu.VMEM((SB_STEP_HEADS, TS, LANES), f32)
    return pl.pallas_call(
        _sb_kernel,
        grid=(bsz, width // step_w, seq // TS),
        in_specs=[q_spec, kv_spec, kv_spec, pl.BlockSpec(tri.shape, lambda b, p, i: (0, 0))],
        out_specs=q_spec,
        out_shape=jax.ShapeDtypeStruct((bsz, seq, width), bf16),
        scratch_shapes=[state, state],
        compiler_params=_params(3),
        name="sb",
    )(qb, kb, vb, tri)


def _post_kernel(ya_ref, yb_ref, ma_ref, mb_ref, x_ref, gate1_ref, g2_ref, sc_ref, sh_ref,
                 gate_ref, wa_ref, wb_ref, wo_ref, w1_ref, w2_ref, o_ref):
    y_a = _dot(ya_ref[0], wa_ref[...])
    y_b = _dot(yb_ref[0], wb_ref[...])
    mixed = ma_ref[0].astype(f32) * y_a + mb_ref[0].astype(f32) * y_b
    hres = x_ref[0] + gate1_ref[0] * _dot(mixed.astype(bf16), wo_ref[...])
    d = hres.shape[1]
    ms = jnp.mean(hres * hres, axis=-1, keepdims=True)
    u = (hres * lax.rsqrt(ms + EPS) * g2_ref[...]) * (1.0 + sc_ref[0]) + sh_ref[0]
    ub = u.astype(bf16)
    ff = jnp.zeros(hres.shape, f32)
    for c in range(w1_ref.shape[1] // d):
        hid = jnp.maximum(_dot(ub, w1_ref[:, c * d:(c + 1) * d]), 0.0)
        ff = ff + _dot((hid * hid).astype(bf16), w2_ref[c * d:(c + 1) * d, :])
    o_ref[0] = hres + gate_ref[0] * ff


def _post(ya, yb, ma, mb, x, gate1, g2, scale2, shift2, gate2, wa, wb, wo, w1, w2):
    bsz, seq, d = x.shape
    tok = lambda w: pl.BlockSpec((1, TM, w), lambda b, i: (b, i, 0))
    mod = pl.BlockSpec((1, 1, d), lambda b, i: (b, 0, 0))
    const = lambda a: pl.BlockSpec(a.shape, lambda b, i: (0,) * a.ndim,
                                   pipeline_mode=pl.Buffered(1))
    return pl.pallas_call(
        _post_kernel,
        grid=(bsz, seq // TM),
        in_specs=[tok(ya.shape[2]), tok(yb.shape[2]), tok(d), tok(d), tok(d), mod,
                  const(g2), mod, mod, mod, const(wa), const(wb), const(wo), const(w1), const(w2)],
        out_specs=tok(d),
        out_shape=jax.ShapeDtypeStruct((bsz, seq, d), f32),
        compiler_params=_params(2),
        name="post",
    )(ya, yb, ma, mb, x, gate1, g2, scale2, shift2, gate2, wa, wb, wo, w1, w2)


def _overlap_t(nc_pad, nsel_pad, nc, nsel):
    c_start = np.arange(nc_pad) * CMP_STRIDE
    s_start = np.arange(nsel_pad) * SEL_BLOCK
    ov = (np.minimum(c_start[None, :] + CMP_BLOCK, s_start[:, None] + SEL_BLOCK)
          - np.maximum(c_start[None, :], s_start[:, None]))
    ov = np.clip(ov, 0, CMP_BLOCK).astype(np.float32) / CMP_BLOCK
    ov[nsel:, :] = 0.0
    ov[:, nc:] = 0.0
    return ov


def _layer(h, mod, rel_tiles, p):
    bsz, seq, d = h.shape
    shift1, scale1, gate1, shift2, scale2, gate2 = [
        mod[:, k * d:(k + 1) * d].reshape(bsz, 1, d) for k in range(6)]
    cmp_bias, near_bias = rel_tiles

    gq = jnp.tile(p["q_norm_g"], 2).reshape(1, LANES)
    gk = jnp.tile(p["k_norm_g"], (1, 2))

    (qa, kc, vc, ks, vs, kw, vw, gates, qb, kb, vb, ma, mb) = _inproj(
        h, p["norm1_g"].reshape(1, d), scale1, shift1, _pack_w_in(p["w_in"], d), gq, gk)

    nch = seq // CMP_STRIDE
    kcmp, vcmp = _compress(kc, vc, p["cmp_pos"], p["cmp_k_w1"], p["cmp_k_w2"],
                           p["cmp_v_w1"], p["cmp_v_w2"], p["k_norm_g"][0].reshape(1, HEAD_DIM))

    nc = (seq - CMP_BLOCK) // CMP_STRIDE + 1
    ovt = jnp.asarray(np.concatenate([_overlap_t(nch, HEAD_DIM, nc, seq // SEL_BLOCK),
                                      np.ones((SUBLANES, nch), np.float32)], axis=0), bf16)
    y_nsa = _nsa(qa, gates, kcmp, vcmp, ks, vs, kw, vw, cmp_bias, near_bias, ovt)

    y_sb = _sb(qb, kb, vb, jnp.asarray(np.tril(np.ones((TS, TS)), -1), bf16))

    return _post(y_nsa, y_sb, ma, mb, h, gate1, p["norm2_g"].reshape(1, d), scale2, shift2, gate2,
                 p["w_up_nsa"].astype(bf16), p["w_up_sb"].astype(bf16), p["w_out"].astype(bf16),
                 p["mlp_w1"].astype(bf16), p["mlp_w2"].astype(bf16))


def kernel(x, c, rel_bias, ada_w, ada_b, norm1_g, norm2_g, w_in, cmp_pos, cmp_k_w1, cmp_k_w2,
           cmp_v_w1, cmp_v_w2, q_norm_g, k_norm_g, w_up_nsa, w_up_sb, w_out, mlp_w1, mlp_w2):
    bsz, seq, d = x.shape
    assert seq % TM == 0 and seq // SEL_BLOCK <= HEAD_DIM and seq >= WINDOW + TQ
    assert CMP_BLOCK == 2 * CMP_STRIDE and KU % SEL_BLOCK == 0
    tbl = rel_bias.astype(f32)
    rel_tiles = _bias_tiles(tbl, seq, seq // CMP_STRIDE)
    stacked = dict(norm1_g=norm1_g, norm2_g=norm2_g, w_in=w_in, cmp_pos=cmp_pos,
                   cmp_k_w1=cmp_k_w1, cmp_k_w2=cmp_k_w2, cmp_v_w1=cmp_v_w1, cmp_v_w2=cmp_v_w2,
                   q_norm_g=q_norm_g, k_norm_g=k_norm_g, w_up_nsa=w_up_nsa, w_up_sb=w_up_sb,
                   w_out=w_out, mlp_w1=mlp_w1, mlp_w2=mlp_w2)
    h = x
    for layer in range(ada_w.shape[0]):
        mod = _adaln(c, ada_w[layer], ada_b[layer])
        h = _layer(h, mod, rel_tiles, {k: v[layer] for k, v in stacked.items()})
    return h
```

```python
import functools
import math

import numpy as np
import jax
import jax.numpy as jnp
from jax import lax
from jax.experimental import pallas as pl
from jax.experimental.pallas import tpu as pltpu

f32 = jnp.float32
bf16 = jnp.bfloat16

HEAD_DIM = 64
NSA_HEADS = 8
NSA_KV_HEADS = 2
NSA_GROUP = NSA_HEADS // NSA_KV_HEADS
SB_HEADS = 8
CMP_BLOCK = 32
CMP_STRIDE = 16
SEL_BLOCK = 64
SEL_TOPK = 16
WINDOW = 512
N_BUCKETS = 32
MAX_DISTANCE = 128
EPS = 1e-6
FORCED_BONUS = 1e4
NEG_BLOCK = -1e9

LANES = 128
SUBLANES = 8
MASKED = -1e30
UNSELECTED = -1e9
LOG2E = math.log2(math.e)
EXP2_UNDERFLOW = -150.0
VMEM_LIMIT = 56 * 1024 * 1024

TQ = 256
KU = 128
TS = 256
TM = 512


def _bucket_thresholds():
    n = np.arange(0, 4 * MAX_DISTANCE)
    max_exact = N_BUCKETS // 2
    nf = np.maximum(n, 1).astype(np.float32)
    large = max_exact + (np.log(nf / max_exact) / math.log(MAX_DISTANCE / max_exact)
                         * (N_BUCKETS - max_exact)).astype(np.int32)
    large = np.minimum(large, N_BUCKETS - 1)
    b = np.where(n < max_exact, n, large)
    assert np.all(np.diff(b) >= 0) and b[-1] == N_BUCKETS - 1
    return [int(np.argmax(b >= k)) for k in range(N_BUCKETS)]


BUCKET_START = _bucket_thresholds()
assert BUCKET_START[-1] <= KU


def _dot(a, b):
    return jnp.dot(a, b, preferred_element_type=f32)


def _dot_nt(a, b):
    return lax.dot_general(a, b, (((1,), (1,)), ((), ())), preferred_element_type=f32)


def _split(a):
    hi = a.astype(bf16)
    lo = (a - hi.astype(f32)).astype(bf16)
    return hi, lo


def _params(n_grid):
    return pltpu.CompilerParams(dimension_semantics=("arbitrary",) * n_grid,
                                vmem_limit_bytes=VMEM_LIMIT)


def _adaln_kernel(c_ref, w_ref, b_ref, o_ref):
    c = c_ref[...]
    a = c * jax.nn.sigmoid(c)
    ah, al = _split(a)
    wh, wl = _split(w_ref[...])
    o_ref[...] = _dot(ah, wh) + _dot(ah, wl) + _dot(al, wh) + b_ref[...]


def _adaln(c, w, b):
    bsz, d = c.shape
    n = w.shape[1]
    return pl.pallas_call(
        _adaln_kernel,
        grid=(n // d,),
        in_specs=[pl.BlockSpec((bsz, d), lambda j: (0, 0)),
                  pl.BlockSpec((d, d), lambda j: (0, j)),
                  pl.BlockSpec((1, d), lambda j: (0, j))],
        out_specs=pl.BlockSpec((bsz, d), lambda j: (0, j)),
        out_shape=jax.ShapeDtypeStruct((bsz, n), f32),
        compiler_params=_params(1),
        name="adaln",
    )(c, w, b.reshape(1, n))


def _bias_of_dist(dist, tbl_ref, h):
    out = jnp.full(dist.shape, tbl_ref[0, h], f32)
    for k in range(1, N_BUCKETS):
        out = jnp.where(dist >= BUCKET_START[k], tbl_ref[k, h], out)
    return jnp.where(dist >= 0, out, MASKED)


def _cmp_bias_kernel(tbl_ref, o_ref):
    h = pl.program_id(0)
    i = pl.program_id(1)
    rows, nc = o_ref.shape[1], o_ref.shape[2]
    t = i * rows + lax.broadcasted_iota(jnp.int32, (rows, LANES), 0)
    for lo in range(0, nc, LANES):
        last_first = lo * CMP_STRIDE + CMP_BLOCK - 1
        last_final = (lo + LANES - 1) * CMP_STRIDE + CMP_BLOCK - 1
        d_max = i * rows + rows - 1 - last_first
        d_min = i * rows - last_final
        cols = slice(lo, lo + LANES)

        @pl.when(d_max < 0)
        def _(cols=cols):
            o_ref[0, :, cols] = jnp.full((rows, LANES), MASKED * LOG2E, f32)

        @pl.when(d_min >= BUCKET_START[-1])
        def _(cols=cols):
            o_ref[0, :, cols] = jnp.full((rows, LANES), tbl_ref[N_BUCKETS - 1, h] * LOG2E, f32)

        @pl.when((d_max >= 0) & (d_min < BUCKET_START[-1]))
        def _(cols=cols, lo=lo):
            j = lo + lax.broadcasted_iota(jnp.int32, (rows, LANES), 1)
            dist = t - (j * CMP_STRIDE + CMP_BLOCK - 1)
            o_ref[0, :, cols] = _bias_of_dist(dist, tbl_ref, h) * LOG2E


def _near_bias_kernel(tbl_ref, o_ref):
    h = pl.program_id(0)
    r = lax.broadcasted_iota(jnp.int32, (TQ, KU + TQ), 0)
    c = lax.broadcasted_iota(jnp.int32, (TQ, KU + TQ), 1)
    o_ref[0] = (_bias_of_dist(r - c + KU, tbl_ref, h) - tbl_ref[N_BUCKETS - 1, h]) * LOG2E


def _bias_tiles(rel_bias, seq, nc_pad):
    tbl = rel_bias.astype(f32)
    smem = pl.BlockSpec(memory_space=pltpu.SMEM)
    rows = 512
    cmp_bias = pl.pallas_call(
        _cmp_bias_kernel,
        grid=(NSA_HEADS, seq // rows),
        in_specs=[smem],
        out_specs=pl.BlockSpec((1, rows, nc_pad), lambda h, i: (h, i, 0)),
        out_shape=jax.ShapeDtypeStruct((NSA_HEADS, seq, nc_pad), f32),
        compiler_params=_params(2),
        name="cmp_bias",
    )(tbl)
    near_bias = pl.pallas_call(
        _near_bias_kernel,
        grid=(NSA_HEADS,),
        in_specs=[smem],
        out_specs=pl.BlockSpec((1, TQ, KU + TQ), lambda h: (h, 0, 0)),
        out_shape=jax.ShapeDtypeStruct((NSA_HEADS, TQ, KU + TQ), f32),
        compiler_params=_params(1),
        name="near_bias",
    )(tbl)
    return cmp_bias, near_bias


_QA_W = NSA_HEADS * LANES
_KV_W = NSA_KV_HEADS * LANES
_CMP_W = NSA_KV_HEADS * HEAD_DIM
_SB_W = SB_HEADS * HEAD_DIM
_GATE_ROWS = NSA_GROUP * 3


def _layout(d_model):
    names = ["qa", "kc", "vc", "ksl", "vsl", "kwn", "vwn", "g", "qb", "kb", "vb", "ma", "mb"]
    widths = [NSA_HEADS * HEAD_DIM] + [_CMP_W] * 6 + [LANES] + [_SB_W] * 3 + [d_model, d_model]
    offs = np.concatenate([[0], np.cumsum(widths)])
    return {n: (int(offs[i]), int(offs[i + 1])) for i, n in enumerate(names)}, int(offs[-1])


def _pack_w_in(w_in, d_model):
    q_w = NSA_HEADS * HEAD_DIM
    kv_w = NSA_KV_HEADS * HEAD_DIM
    g_w = NSA_HEADS * 3
    sizes = [q_w] + [kv_w] * 6 + [g_w] + [_SB_W] * 3 + [d_model, d_model]
    offs = np.concatenate([[0], np.cumsum(sizes)])
    parts = [w_in[:, int(offs[i]):int(offs[i + 1])] for i in range(len(sizes))]
    parts[7] = jnp.pad(parts[7], ((0, 0), (0, LANES - g_w)))
    return jnp.concatenate(parts, axis=1).astype(bf16)


def _inproj_kernel(lay, x_ref, g1_ref, sc_ref, sh_ref, w_ref, gq_ref, gk_ref,
                   qa_ref, kc_ref, vc_ref, ks_ref, vs_ref, kw_ref, vw_ref, g_ref,
                   qb_ref, kb_ref, vb_ref, ma_ref, mb_ref):
    i = pl.program_id(1)
    x = x_ref[0]
    ms = jnp.mean(x * x, axis=-1, keepdims=True)
    u = (x * lax.rsqrt(ms + EPS) * g1_ref[...]) * (1.0 + sc_ref[0]) + sh_ref[0]
    ub = u.astype(bf16)

    narrow = ("kc", "vc", "ksl", "vsl", "kwn", "vwn", "g")
    narrow_lo = lay[narrow[0]][0]
    z_narrow = _dot(ub, w_ref[:, narrow_lo:lay[narrow[-1]][1]])

    def proj(name):
        lo, hi = lay[name]
        if name in narrow:
            return z_narrow[:, lo - narrow_lo:hi - narrow_lo]
        return _dot(ub, w_ref[:, lo:hi])

    rows = x.shape[0]
    lane = lax.broadcasted_iota(jnp.int32, (rows, LANES), 1)
    low = lane < HEAD_DIM

    def pair_norm(z, gain):
        sq = z * z
        ss_a = jnp.sum(jnp.where(low, sq, 0.0), axis=1, keepdims=True)
        ss_b = jnp.sum(jnp.where(low, 0.0, sq), axis=1, keepdims=True)
        inv = lax.rsqrt(jnp.where(low, ss_a, ss_b) * (1.0 / HEAD_DIM) + EPS)
        return z * inv * gain

    def spread(z, extra):
        return jnp.concatenate([jnp.where(low, z, extra),
                                jnp.where(low, pltpu.roll(z, HEAD_DIM, 1), extra)], axis=1)

    scale = HEAD_DIM ** -0.5 * LOG2E
    zq = proj("qa")
    qa_ref[0] = jnp.concatenate(
        [spread(pair_norm(zq[:, j * LANES:(j + 1) * LANES], gq_ref[...]) * scale, 0.0)
         for j in range(NSA_HEADS // 2)], axis=1).astype(bf16)
    kc_ref[0] = proj("kc")
    vc_ref[0] = proj("vc")

    tok_blk = (i * rows + lax.broadcasted_iota(jnp.int32, (rows, LANES), 0)) // SEL_BLOCK
    onehot = jnp.where(lane - HEAD_DIM == tok_blk, 1.0, 0.0)
    ones_col = jnp.where(lane == HEAD_DIM, 1.0, 0.0)

    ks_ref[0] = spread(pair_norm(proj("ksl"), gk_ref[1:2, :]), onehot).astype(bf16)
    vs_ref[0] = spread(proj("vsl"), ones_col).astype(bf16)
    kw_ref[0] = spread(pair_norm(proj("kwn"), gk_ref[2:3, :]), 0.0).astype(bf16)
    vw_ref[0] = spread(proj("vwn"), ones_col).astype(bf16)
    g_ref[0] = jax.nn.sigmoid(proj("g")).T
    qb_ref[0] = (proj("qb") * scale).astype(bf16)
    kb_ref[0] = proj("kb").astype(bf16)
    vb_ref[0] = proj("vb").astype(bf16)
    ma_ref[0] = jax.nn.sigmoid(proj("ma")).astype(bf16)
    mb_ref[0] = jax.nn.sigmoid(proj("mb")).astype(bf16)


def _inproj(x, g1, scale1, shift1, w_packed, gq, gk):
    bsz, seq, d = x.shape
    lay, width = _layout(d)
    assert w_packed.shape == (d, width)
    tok = lambda w: pl.BlockSpec((1, TM, w), lambda b, i: (b, i, 0))
    full = lambda a: pl.BlockSpec(a.shape, lambda b, i: (0,) * a.ndim,
                                  pipeline_mode=pl.Buffered(1))
    mod = pl.BlockSpec((1, 1, d), lambda b, i: (b, 0, 0))
    out_w = [(_QA_W, bf16), (_CMP_W, f32), (_CMP_W, f32), (_KV_W, bf16), (_KV_W, bf16),
             (_KV_W, bf16), (_KV_W, bf16), (LANES, f32), (_SB_W, bf16), (_SB_W, bf16),
             (_SB_W, bf16), (d, bf16), (d, bf16)]
    gates_at = 7
    out_specs = [tok(w) for w, _ in out_w]
    out_shape = [jax.ShapeDtypeStruct((bsz, seq, w), dt) for w, dt in out_w]
    out_specs[gates_at] = pl.BlockSpec((1, LANES, TM), lambda b, i: (b, 0, i))
    out_shape[gates_at] = jax.ShapeDtypeStruct((bsz, LANES, seq), f32)
    return pl.pallas_call(
        functools.partial(_inproj_kernel, lay),
        grid=(bsz, seq // TM),
        in_specs=[tok(d), full(g1), mod, mod, full(w_packed), full(gq), full(gk)],
        out_specs=out_specs,
        out_shape=out_shape,
        compiler_params=_params(2),
        name="inproj",
    )(x, g1, scale1, shift1, w_packed, gq, gk)


def _compress_kernel(xk_ref, xv_ref, pos_ref, w1k_ref, w2k_ref, w1v_ref, w2v_ref, gk_ref,
                     ko_ref, vo_ref):
    nch = xk_ref.shape[1] // CMP_STRIDE

    def mlp(x_ref, w1_ref, w2_ref):
        first = jnp.zeros((nch, LANES), f32)
        second = jnp.zeros((nch, LANES), f32)
        for l in range(CMP_STRIDE):
            xl = x_ref[0, pl.ds(l, nch, stride=CMP_STRIDE), :]
            lo = l + CMP_STRIDE
            first = first + _dot((xl + pos_ref[l:l + 1, :]).astype(bf16), w1_ref[l])
            second = second + _dot((xl + pos_ref[lo:lo + 1, :]).astype(bf16), w1_ref[lo])
        pre = first + pltpu.roll(second, nch - 1, 0)
        hid = pre * jax.nn.sigmoid(pre)
        return _dot(hid.astype(bf16), w2_ref[...])

    k = mlp(xk_ref, w1k_ref, w2k_ref)
    v = mlp(xv_ref, w1v_ref, w2v_ref)
    for h in range(NSA_KV_HEADS):
        kh = k[:, h * HEAD_DIM:(h + 1) * HEAD_DIM]
        ms = jnp.mean(kh * kh, axis=-1, keepdims=True)
        ko_ref[0, h] = (kh * lax.rsqrt(ms + EPS) * gk_ref[...]).astype(bf16)
        vh = v[:, h * HEAD_DIM:(h + 1) * HEAD_DIM]
        ones_col = jnp.where(lax.broadcasted_iota(jnp.int32, vh.shape, 1) == 0, 1.0, 0.0)
        vo_ref[0, h] = jnp.concatenate([vh, ones_col], axis=1).astype(bf16)


def _compress(xk, xv, pos, w1k, w2k, w1v, w2v, gk0):
    bsz, seq, width = xk.shape
    nch = seq // CMP_STRIDE
    assert width == NSA_KV_HEADS * HEAD_DIM == LANES

    def both_heads(w):
        z = jnp.zeros_like(w)
        return jnp.concatenate([jnp.concatenate([w, z], axis=2),
                                jnp.concatenate([z, w], axis=2)], axis=1).astype(bf16)

    w1 = lambda w: both_heads(w.reshape(CMP_BLOCK, HEAD_DIM, w.shape[1]))
    w2 = lambda w: both_heads(w[None])[0]
    args = (xk, xv, jnp.tile(pos, (1, NSA_KV_HEADS)), w1(w1k), w2(w2k), w1(w1v), w2(w2v), gk0)
    blk = pl.BlockSpec((1, seq, width), lambda b: (b, 0, 0))
    full = lambda a: pl.BlockSpec(a.shape, lambda b: (0,) * a.ndim)
    out = lambda w: pl.BlockSpec((1, NSA_KV_HEADS, nch, w), lambda b: (b, 0, 0, 0))
    shape = lambda w: jax.ShapeDtypeStruct((bsz, NSA_KV_HEADS, nch, w), bf16)
    return pl.pallas_call(
        _compress_kernel,
        grid=(bsz,),
        in_specs=[blk, blk] + [full(a) for a in args[2:]],
        out_specs=[out(HEAD_DIM), out(LANES)],
        out_shape=[shape(HEAD_DIM), shape(LANES)],
        compiler_params=_params(1),
        name="compress",
    )(*args)


def _flash_step(s, v, m_ref, acc_ref, first=False):
    row_max = jnp.max(s, axis=1, keepdims=True)
    if first:
        m_new = jnp.broadcast_to(row_max, m_ref.shape)
    else:
        m_prev = m_ref[...]
        m_new = jnp.maximum(m_prev, row_max)
    p = jnp.exp2(s - jnp.concatenate([m_new] * (s.shape[1] // LANES), axis=1))
    pv = _dot(p.astype(bf16), v)
    acc_ref[...] = pv if first else jnp.exp2(m_prev - m_new) * acc_ref[...] + pv
    m_ref[...] = m_new


def _flash_init(m_ref, acc_ref):
    m_ref[...] = jnp.full(m_ref.shape, MASKED, f32)
    acc_ref[...] = jnp.zeros(acc_ref.shape, f32)


def _flash_out_t(acc_ref):
    acc_t = acc_ref[...].T
    return acc_t[:HEAD_DIM] / acc_t[HEAD_DIM:HEAD_DIM + 1]


def _nsa_kernel(n_top, n_tiles, q_ref, g_ref, kc_ref, vc_ref, ks_ref, vs_ref, kw_ref, vw_ref,
                bc_ref, bn_ref, ovt_ref, o_ref, qs_ref, qw_ref, oc_ref,
                m_ref, acc_ref, mw_ref, accw_ref):
    i = pl.program_id(1)
    kv_heads = range(NSA_KV_HEADS)
    rows = NSA_GROUP * TQ
    start = i * TQ
    assert WINDOW % TQ == 0 and TQ % KU == 0
    first_interior = WINDOW // TQ
    window_extra = (WINDOW - TQ) // KU - 1

    def attend(hk, q_rows_ref, k_ref, v_ref, off, width, bias, state, first=False):
        off = pl.multiple_of(off, KU)
        lanes = slice(hk * LANES, (hk + 1) * LANES)
        s = _dot_nt(q_rows_ref[hk], k_ref[0, pl.ds(off, width), lanes])
        if bias is not None:
            s = s + bias
        _flash_step(s, v_ref[0, pl.ds(off, width), lanes], state[0].at[hk], state[1].at[hk],
                    first)

    def when(cond, guarded):
        return pl.when(cond) if guarded else (lambda fn: fn())

    def near_steps(q_rows_ref, k_ref, v_ref, state, guarded, extra=0):
        group = lambda hk: slice(hk * NSA_GROUP, (hk + 1) * NSA_GROUP)

        @when(i >= 1, guarded)
        def _():
            for hk in kv_heads:
                bias = bn_ref[group(hk)].reshape(rows, KU + TQ)
                if extra:
                    bias = jnp.concatenate([jnp.zeros((rows, extra * KU), f32), bias], axis=1)
                attend(hk, q_rows_ref, k_ref, v_ref, start - (1 + extra) * KU,
                       TQ + (1 + extra) * KU, bias, state)

        if guarded:
            @pl.when(i == 0)
            def _():
                for hk in kv_heads:
                    attend(hk, q_rows_ref, k_ref, v_ref, start, TQ,
                           bn_ref[group(hk), :, KU:].reshape(rows, TQ), state)

    def select(hk, live):
        heads = range(hk * NSA_GROUP, (hk + 1) * NSA_GROUP)
        qpad = jnp.concatenate([q_ref[0, :, g * LANES:(g + 1) * LANES] for g in heads], axis=0)
        qw_ref[hk] = qpad

        bc = bc_ref[hk * NSA_GROUP:(hk + 1) * NSA_GROUP].reshape(rows, bc_ref.shape[2])
        s_c = _dot_nt(qpad[:, :HEAD_DIM], kc_ref[0, hk]) + bc
        m_c = jnp.maximum(jnp.max(s_c, axis=1, keepdims=True), 0.1 * MASKED)
        e_c = jnp.exp2(s_c - m_c).astype(bf16)
        oc_ref[hk] = _dot(e_c, vc_ref[0, hk])

        nblk = ovt_ref.shape[0] - SUBLANES
        imp = jnp.zeros((nblk, TQ), f32)
        for g in range(NSA_GROUP):
            t = _dot_nt(ovt_ref[...], e_c[g * TQ:(g + 1) * TQ])
            imp = imp + t[:nblk] / jnp.maximum(t[nblk:nblk + 1], 1e-30)
        blk = lax.broadcasted_iota(jnp.int32, (nblk, TQ), 0)
        cur = (start + lax.broadcasted_iota(jnp.int32, (nblk, TQ), 1)) // SEL_BLOCK
        forced = (blk == 0) | (blk == cur) | (blk == cur - 1)
        imp = jnp.where(blk > cur, NEG_BLOCK, imp + jnp.where(forced, FORCED_BONUS, 0.0))
        sub = SUBLANES
        groups = [imp[lo:lo + sub] for lo in range(0, live, sub)]
        ranks = [jnp.zeros((sub, TQ), f32) for _ in groups]
        row = lax.broadcasted_iota(jnp.int32, (sub, TQ), 0)
        for b2 in range(live):
            other = imp[b2:b2 + 1, :]
            for gi, grp in enumerate(groups):
                lo = gi * sub
                if lo > b2:
                    ranks[gi] = jnp.where(other >= grp, ranks[gi] + 1.0, ranks[gi])
                elif lo + sub - 1 < b2:
                    ranks[gi] = jnp.where(other > grp, ranks[gi] + 1.0, ranks[gi])
                else:
                    ranks[gi] = ranks[gi] + jnp.where(row + lo > b2,
                                                      jnp.where(other >= grp, 1.0, 0.0),
                                                      jnp.where(other > grp, 1.0, 0.0))
        rank = jnp.concatenate(ranks + [jnp.zeros((nblk - live, TQ), f32)] * (live < nblk), axis=0)
        usable = (rank < n_top) & (blk <= cur)
        sel_t = jnp.where(usable, 0.0, UNSELECTED)
        sel_pad = jnp.concatenate([jnp.zeros((LANES - nblk, TQ), f32), sel_t], axis=0).T
        sel_rows = jnp.concatenate([sel_pad.astype(bf16)] * NSA_GROUP, axis=0)
        qs_ref[hk] = qpad + sel_rows

    def head(guarded, live):
        for hk in kv_heads:
            select(hk, live)

        win = (mw_ref, accw_ref)
        if guarded:
            _flash_init(*win)

        @when(i >= first_interior, guarded)
        def _():
            r = lax.broadcasted_iota(jnp.int32, (rows, TQ), 0) & (TQ - 1)
            c = lax.broadcasted_iota(jnp.int32, (rows, TQ), 1)
            for hk in kv_heads:
                attend(hk, qw_ref, kw_ref, vw_ref, start - WINDOW, TQ,
                       jnp.where(c > r, 0.0, MASKED), win, first=not guarded)

        near_steps(qw_ref, kw_ref, vw_ref, win, guarded, window_extra)
        _flash_init(m_ref, acc_ref)

    def tail(guarded, extra=0):
        near_steps(qs_ref, ks_ref, vs_ref, (m_ref, acc_ref), guarded, extra)
        outs = []
        for hk in kv_heads:
            oc_t = oc_ref[hk].T
            o_c = oc_t[:HEAD_DIM] / jnp.maximum(oc_t[HEAD_DIM:HEAD_DIM + 1], 1e-30)
            o_s = _flash_out_t(acc_ref.at[hk])
            o_w = _flash_out_t(accw_ref.at[hk])
            gates = g_ref[0]
            for g in range(NSA_GROUP):
                sl = slice(g * TQ, (g + 1) * TQ)
                at = hk * _GATE_ROWS + 3 * g
                outs.append(gates[at:at + 1] * o_c[:, sl]
                            + gates[at + 1:at + 2] * o_s[:, sl]
                            + gates[at + 2:at + 3] * o_w[:, sl])
        o_ref[0] = jnp.concatenate(outs, axis=0).T.astype(bf16)

    interior = i >= first_interior
    blocks_upto = lambda hi: min(-(-hi * TQ // (SEL_BLOCK * SUBLANES)) * SUBLANES, LANES - HEAD_DIM)
    pl.when(jnp.logical_not(interior))(lambda: head(True, blocks_upto(first_interior)))
    lo = first_interior
    while lo < n_tiles:
        hi = min(2 * lo, n_tiles)
        pl.when((i >= lo) & (i < hi))(functools.partial(head, False, blocks_upto(hi)))
        lo = hi

    sel = (m_ref, acc_ref)
    n_far = jnp.maximum(i * (TQ // KU) - 1, 0)

    def far_steps(off, n_steps):
        for step in range(n_steps):
            for hk in kv_heads:
                attend(hk, qs_ref, ks_ref, vs_ref, off + step * (2 * KU), 2 * KU, None, sel)

    def far_trip(c, carry):
        far_steps(c * (8 * KU), 4)
        return carry

    lax.fori_loop(0, n_far // 8, far_trip, 0)
    rem_off = (n_far // 8) * (8 * KU)

    @pl.when((n_far & 4) != 0)
    def _():
        far_steps(rem_off, 2)

    edge = jnp.logical_not(interior)

    @pl.when(edge & ((n_far & 2) != 0))
    def _():
        far_steps(rem_off + (n_far & 4) * KU, 1)

    @pl.when(edge & ((n_far & 1) != 0))
    def _():
        for hk in kv_heads:
            attend(hk, qs_ref, ks_ref, vs_ref, rem_off + (n_far & 6) * KU, KU, None, sel)

    for extra in range(4):
        pl.when(interior & ((n_far & 3) == extra))(functools.partial(tail, False, extra))
    pl.when(edge)(lambda: tail(True))


def _nsa(qa, gates, kcmp, vcmp, ks, vs, kw, vw, cmp_bias, near_bias, ovt):
    bsz, seq, _ = qa.shape
    nc = kcmp.shape[2]
    n_top = min(SEL_TOPK, seq // SEL_BLOCK)
    rows = NSA_GROUP * TQ
    tok = lambda w: pl.BlockSpec((1, TQ, w), lambda b, i: (b, i, 0))
    kv = pl.BlockSpec((1, seq, NSA_KV_HEADS * LANES), lambda b, i: (b, 0, 0))
    cmp = lambda a: pl.BlockSpec((1,) + a.shape[1:], lambda b, i: (b, 0, 0, 0))
    per_kv = lambda width, dt: pltpu.VMEM((NSA_KV_HEADS, rows, width), dt)
    return pl.pallas_call(
        functools.partial(_nsa_kernel, n_top, seq // TQ),
        grid=(bsz, seq // TQ),
        in_specs=[tok(NSA_HEADS * LANES),
                  pl.BlockSpec((1, LANES, TQ), lambda b, i: (b, 0, i)),
                  cmp(kcmp), cmp(vcmp), kv, kv, kv, kv,
                  pl.BlockSpec((NSA_HEADS, TQ, nc), lambda b, i: (0, i, 0)),
                  pl.BlockSpec((NSA_HEADS, TQ, KU + TQ), lambda b, i: (0, 0, 0)),
                  pl.BlockSpec(ovt.shape, lambda b, i: (0, 0))],
        out_specs=tok(NSA_HEADS * HEAD_DIM),
        out_shape=jax.ShapeDtypeStruct((bsz, seq, NSA_HEADS * HEAD_DIM), bf16),
        scratch_shapes=[per_kv(LANES, bf16), per_kv(LANES, bf16),
                        per_kv(LANES, f32), per_kv(LANES, f32), per_kv(LANES, f32),
                        per_kv(LANES, f32), per_kv(LANES, f32)],
        compiler_params=_params(2),
        name="nsa",
    )(qa, gates, kcmp, vcmp, ks, vs, kw, vw, cmp_bias, near_bias, ovt)


def _sb_kernel(q_ref, k_ref, v_ref, tri_ref, o_ref, carry_ref, acc_ref):
    i = pl.program_id(2)
    n_heads = carry_ref.shape[0]
    lane = lax.broadcasted_iota(jnp.int32, (TS, LANES), 1)
    q_heads = []
    for pair in range(n_heads // 2):
        q = q_ref[0, :, pair * LANES:(pair + 1) * LANES]
        zero = jnp.zeros_like(q)
        q_heads += [jnp.where(lane < HEAD_DIM, q, zero), jnp.where(lane >= HEAD_DIM, q, zero)]
    pair_lanes = lambda hh: slice((hh // 2) * LANES, (hh // 2 + 1) * LANES)
    r = lax.broadcasted_iota(jnp.int32, (TS, TS), 0)
    c = lax.broadcasted_iota(jnp.int32, (TS, TS), 1)
    before = c < r

    def chunks(jobs, first):
        offs = [pl.multiple_of(off, TS) for off, _ in jobs]
        stage = []
        for (_, diagonal), off in zip(jobs, offs):
            for hh in range(n_heads):
                z = _dot_nt(q_heads[hh], k_ref[0, pl.ds(off, TS), pair_lanes(hh)])
                soft = jnp.log2(1.0 + jnp.exp2(-jnp.abs(z)))
                log_keep = jnp.minimum(-z, 0.0) - soft
                log_sig = log_keep + z
                if diagonal:
                    log_keep = jnp.where(before, log_keep, 0.0)
                later = _dot(log_keep.astype(bf16), tri_ref[...])
                stage.append((log_sig + later, jnp.sum(log_keep, axis=1, keepdims=True)))
        for hh in range(n_heads):
            carry = None if first else carry_ref[hh]
            acc = None if first else acc_ref[hh]
            for j, ((_, diagonal), off) in enumerate(zip(jobs, offs)):
                base, total = stage[n_heads * j + hh]
                if carry is not None:
                    base = base + jnp.concatenate([carry] * (TS // LANES), axis=1)
                a = jnp.exp2(base)
                if diagonal:
                    a = jnp.where(before, a, 0.0)
                out = _dot(a.astype(bf16), v_ref[0, pl.ds(off, TS), pair_lanes(hh)])
                acc = out if acc is None else acc + out
                total = jnp.broadcast_to(total, (TS, LANES))
                carry = total if carry is None else carry + total
            carry_ref[hh] = carry
            acc_ref[hh] = acc

    @pl.when(i == 0)
    def _():
        chunks([(0, True)], True)

    @pl.when(i >= 1)
    def _():
        chunks([(i * TS, True), ((i - 1) * TS, False)], True)

    def any_live():
        return jnp.max(carry_ref[...]) > EXP2_UNDERFLOW

    def more(state):
        n, live = state
        return jnp.logical_and(n < i, live)

    def older(state):
        n, _ = state
        chunks([((i - 1 - n) * TS, False)], False)
        return n + 1, any_live()

    lax.while_loop(more, older, (jnp.int32(1), any_live()))
    o_ref[0] = jnp.concatenate(
        [jnp.where(lane < HEAD_DIM, acc_ref[2 * pair], acc_ref[2 * pair + 1])
         for pair in range(n_heads // 2)], axis=1).astype(bf16)


SB_STEP_HEADS = 8


def _sb(qb, kb, vb, tri):
    bsz, seq, width = qb.shape
    step_w = SB_STEP_HEADS * HEAD_DIM
    q_spec = pl.BlockSpec((1, TS, step_w), lambda b, p, i: (b, i, p))
    kv_spec = pl.BlockSpec((1, seq, step_w), lambda b, p, i: (b, 0, p))
    state = pltpu.VMEM((SB_STEP_HEADS, TS, LANES), f32)
    return pl.pallas_call(
        _sb_kernel,
        grid=(bsz, width // step_w, seq // TS),
        in_specs=[q_spec, kv_spec, kv_spec, pl.BlockSpec(tri.shape, lambda b, p, i: (0, 0))],
        out_specs=q_spec,
        out_shape=jax.ShapeDtypeStruct((bsz, seq, width), bf16),
        scratch_shapes=[state, state],
        compiler_params=_params(3),
        name="sb",
    )(qb, kb, vb, tri)


def _post_kernel(ya_ref, yb_ref, ma_ref, mb_ref, x_ref, gate1_ref, g2_ref, sc_ref, sh_ref,
                 gate_ref, wa_ref, wb_ref, wo_ref, w1_ref, w2_ref, o_ref):
    y_a = _dot(ya_ref[0], wa_ref[...])
    y_b = _dot(yb_ref[0], wb_ref[...])
    mixed = ma_ref[0].astype(f32) * y_a + mb_ref[0].astype(f32) * y_b
    hres = x_ref[0] + gate1_ref[0] * _dot(mixed.astype(bf16), wo_ref[...])
    d = hres.shape[1]
    ms = jnp.mean(hres * hres, axis=-1, keepdims=True)
    u = (hres * lax.rsqrt(ms + EPS) * g2_ref[...]) * (1.0 + sc_ref[0]) + sh_ref[0]
    ub = u.astype(bf16)
    ff = jnp.zeros(hres.shape, f32)
    for c in range(w1_ref.shape[1] // d):
        hid = jnp.maximum(_dot(ub, w1_ref[:, c * d:(c + 1) * d]), 0.0)
        ff = ff + _dot((hid * hid).astype(bf16), w2_ref[c * d:(c + 1) * d, :])
    o_ref[0] = hres + gate_ref[0] * ff


def _post(ya, yb, ma, mb, x, gate1, g2, scale2, shift2, gate2, wa, wb, wo, w1, w2):
    bsz, seq, d = x.shape
    tok = lambda w: pl.BlockSpec((1, TM, w), lambda b, i: (b, i, 0))
    mod = pl.BlockSpec((1, 1, d), lambda b, i: (b, 0, 0))
    const = lambda a: pl.BlockSpec(a.shape, lambda b, i: (0,) * a.ndim,
                                   pipeline_mode=pl.Buffered(1))
    return pl.pallas_call(
        _post_kernel,
        grid=(bsz, seq // TM),
        in_specs=[tok(ya.shape[2]), tok(yb.shape[2]), tok(d), tok(d), tok(d), mod,
                  const(g2), mod, mod, mod, const(wa), const(wb), const(wo), const(w1), const(w2)],
        out_specs=tok(d),
        out_shape=jax.ShapeDtypeStruct((bsz, seq, d), f32),
        compiler_params=_params(2),
        name="post",
    )(ya, yb, ma, mb, x, gate1, g2, scale2, shift2, gate2, wa, wb, wo, w1, w2)


def _overlap_t(nc_pad, nsel_pad, nc, nsel):
    c_start = np.arange(nc_pad) * CMP_STRIDE
    s_start = np.arange(nsel_pad) * SEL_BLOCK
    ov = (np.minimum(c_start[None, :] + CMP_BLOCK, s_start[:, None] + SEL_BLOCK)
          - np.maximum(c_start[None, :], s_start[:, None]))
    ov = np.clip(ov, 0, CMP_BLOCK).astype(np.float32) / CMP_BLOCK
    ov[nsel:, :] = 0.0
    ov[:, nc:] = 0.0
    return ov


def _layer(h, mod, rel_tiles, p):
    bsz, seq, d = h.shape
    shift1, scale1, gate1, shift2, scale2, gate2 = [
        mod[:, k * d:(k + 1) * d].reshape(bsz, 1, d) for k in range(6)]
    cmp_bias, near_bias = rel_tiles

    gq = jnp.tile(p["q_norm_g"], 2).reshape(1, LANES)
    gk = jnp.tile(p["k_norm_g"], (1, 2))

    (qa, kc, vc, ks, vs, kw, vw, gates, qb, kb, vb, ma, mb) = _inproj(
        h, p["norm1_g"].reshape(1, d), scale1, shift1, _pack_w_in(p["w_in"], d), gq, gk)

    nch = seq // CMP_STRIDE
    kcmp, vcmp = _compress(kc, vc, p["cmp_pos"], p["cmp_k_w1"], p["cmp_k_w2"],
                           p["cmp_v_w1"], p["cmp_v_w2"], p["k_norm_g"][0].reshape(1, HEAD_DIM))

    nc = (seq - CMP_BLOCK) // CMP_STRIDE + 1
    ovt = jnp.asarray(np.concatenate([_overlap_t(nch, HEAD_DIM, nc, seq // SEL_BLOCK),
                                      np.ones((SUBLANES, nch), np.float32)], axis=0), bf16)
    y_nsa = _nsa(qa, gates, kcmp, vcmp, ks, vs, kw, vw, cmp_bias, near_bias, ovt)

    y_sb = _sb(qb, kb, vb, jnp.asarray(np.tril(np.ones((TS, TS)), -1), bf16))

    return _post(y_nsa, y_sb, ma, mb, h, gate1, p["norm2_g"].reshape(1, d), scale2, shift2, gate2,
                 p["w_up_nsa"].astype(bf16), p["w_up_sb"].astype(bf16), p["w_out"].astype(bf16),
                 p["mlp_w1"].astype(bf16), p["mlp_w2"].astype(bf16))


def kernel(x, c, rel_bias, ada_w, ada_b, norm1_g, norm2_g, w_in, cmp_pos, cmp_k_w1, cmp_k_w2,
           cmp_v_w1, cmp_v_w2, q_norm_g, k_norm_g, w_up_nsa, w_up_sb, w_out, mlp_w1, mlp_w2):
    bsz, seq, d = x.shape
    assert seq % TM == 0 and seq // SEL_BLOCK <= HEAD_DIM and seq >= WINDOW + TQ
    assert CMP_BLOCK == 2 * CMP_STRIDE and KU % SEL_BLOCK == 0
    tbl = rel_bias.astype(f32)
    rel_tiles = _bias_tiles(tbl, seq, seq // CMP_STRIDE)
    stacked = dict(norm1_g=norm1_g, norm2_g=norm2_g, w_in=w_in, cmp_pos=cmp_pos,
                   cmp_k_w1=cmp_k_w1, cmp_k_w2=cmp_k_w2, cmp_v_w1=cmp_v_w1, cmp_v_w2=cmp_v_w2,
                   q_norm_g=q_norm_g, k_norm_g=k_norm_g, w_up_nsa=w_up_nsa, w_up_sb=w_up_sb,
                   w_out=w_out, mlp_w1=mlp_w1, mlp_w2=mlp_w2)
    h = x
    for layer in range(ada_w.shape[0]):
        mod = _adaln(c, ada_w[layer], ada_b[layer])
        h = _layer(h, mod, rel_tiles, {k: v[layer] for k, v in stacked.items()})
    return h
```

```python
import functools
import math

import numpy as np
import jax
import jax.numpy as jnp
from jax import lax
from jax.experimental import pallas as pl
from jax.experimental.pallas import tpu as pltpu

f32 = jnp.float32
bf16 = jnp.bfloat16

HEAD_DIM = 64
NSA_HEADS = 8
NSA_KV_HEADS = 2
NSA_GROUP = NSA_HEADS // NSA_KV_HEADS
SB_HEADS = 8
CMP_BLOCK = 32
CMP_STRIDE = 16
SEL_BLOCK = 64
SEL_TOPK = 16
WINDOW = 512
N_BUCKETS = 32
MAX_DISTANCE = 128
EPS = 1e-6
FORCED_BONUS = 1e4
NEG_BLOCK = -1e9

LANES = 128
SUBLANES = 8
MASKED = -1e30
UNSELECTED = -1e9
LOG2E = math.log2(math.e)
EXP2_UNDERFLOW = -150.0
VMEM_LIMIT = 56 * 1024 * 1024

TQ = 256
KU = 128
TS = 256
TM = 512


def _bucket_thresholds():
    n = np.arange(0, 4 * MAX_DISTANCE)
    max_exact = N_BUCKETS // 2
    nf = np.maximum(n, 1).astype(np.float32)
    large = max_exact + (np.log(nf / max_exact) / math.log(MAX_DISTANCE / max_exact)
                         * (N_BUCKETS - max_exact)).astype(np.int32)
    large = np.minimum(large, N_BUCKETS - 1)
    b = np.where(n < max_exact, n, large)
    assert np.all(np.diff(b) >= 0) and b[-1] == N_BUCKETS - 1
    return [int(np.argmax(b >= k)) for k in range(N_BUCKETS)]


BUCKET_START = _bucket_thresholds()
assert BUCKET_START[-1] <= KU


def _dot(a, b):
    return jnp.dot(a, b, preferred_element_type=f32)


def _dot_nt(a, b):
    return lax.dot_general(a, b, (((1,), (1,)), ((), ())), preferred_element_type=f32)


def _split(a):
    hi = a.astype(bf16)
    lo = (a - hi.astype(f32)).astype(bf16)
    return hi, lo


def _params(n_grid):
    return pltpu.CompilerParams(dimension_semantics=("arbitrary",) * n_grid,
                                vmem_limit_bytes=VMEM_LIMIT)


def _adaln_kernel(c_ref, w_ref, b_ref, o_ref):
    c = c_ref[...]
    a = c * jax.nn.sigmoid(c)
    ah, al = _split(a)
    wh, wl = _split(w_ref[...])
    o_ref[...] = _dot(ah, wh) + _dot(ah, wl) + _dot(al, wh) + b_ref[...]


def _adaln(c, w, b):
    bsz, d = c.shape
    n = w.shape[1]
    return pl.pallas_call(
        _adaln_kernel,
        grid=(n // d,),
        in_specs=[pl.BlockSpec((bsz, d), lambda j: (0, 0)),
                  pl.BlockSpec((d, d), lambda j: (0, j)),
                  pl.BlockSpec((1, d), lambda j: (0, j))],
        out_specs=pl.BlockSpec((bsz, d), lambda j: (0, j)),
        out_shape=jax.ShapeDtypeStruct((bsz, n), f32),
        compiler_params=_params(1),
        name="adaln",
    )(c, w, b.reshape(1, n))


def _bias_of_dist(dist, tbl_ref, h):
    out = jnp.full(dist.shape, tbl_ref[0, h], f32)
    for k in range(1, N_BUCKETS):
        out = jnp.where(dist >= BUCKET_START[k], tbl_ref[k, h], out)
    return jnp.where(dist >= 0, out, MASKED)


def _cmp_bias_kernel(tbl_ref, o_ref):
    h = pl.program_id(0)
    i = pl.program_id(1)
    rows, nc = o_ref.shape[1], o_ref.shape[2]
    t = i * rows + lax.broadcasted_iota(jnp.int32, (rows, LANES), 0)
    for lo in range(0, nc, LANES):
        last_first = lo * CMP_STRIDE + CMP_BLOCK - 1
        last_final = (lo + LANES - 1) * CMP_STRIDE + CMP_BLOCK - 1
        d_max = i * rows + rows - 1 - last_first
        d_min = i * rows - last_final
        cols = slice(lo, lo + LANES)

        @pl.when(d_max < 0)
        def _(cols=cols):
            o_ref[0, :, cols] = jnp.full((rows, LANES), MASKED * LOG2E, f32)

        @pl.when(d_min >= BUCKET_START[-1])
        def _(cols=cols):
            o_ref[0, :, cols] = jnp.full((rows, LANES), tbl_ref[N_BUCKETS - 1, h] * LOG2E, f32)

        @pl.when((d_max >= 0) & (d_min < BUCKET_START[-1]))
        def _(cols=cols, lo=lo):
            j = lo + lax.broadcasted_iota(jnp.int32, (rows, LANES), 1)
            dist = t - (j * CMP_STRIDE + CMP_BLOCK - 1)
            o_ref[0, :, cols] = _bias_of_dist(dist, tbl_ref, h) * LOG2E


def _near_bias_kernel(tbl_ref, o_ref):
    h = pl.program_id(0)
    r = lax.broadcasted_iota(jnp.int32, (TQ, KU + TQ), 0)
    c = lax.broadcasted_iota(jnp.int32, (TQ, KU + TQ), 1)
    o_ref[0] = (_bias_of_dist(r - c + KU, tbl_ref, h) - tbl_ref[N_BUCKETS - 1, h]) * LOG2E


def _bias_tiles(rel_bias, seq, nc_pad):
    tbl = rel_bias.astype(f32)
    smem = pl.BlockSpec(memory_space=pltpu.SMEM)
    rows = 512
    cmp_bias = pl.pallas_call(
        _cmp_bias_kernel,
        grid=(NSA_HEADS, seq // rows),
        in_specs=[smem],
        out_specs=pl.BlockSpec((1, rows, nc_pad), lambda h, i: (h, i, 0)),
        out_shape=jax.ShapeDtypeStruct((NSA_HEADS, seq, nc_pad), f32),
        compiler_params=_params(2),
        name="cmp_bias",
    )(tbl)
    near_bias = pl.pallas_call(
        _near_bias_kernel,
        grid=(NSA_HEADS,),
        in_specs=[smem],
        out_specs=pl.BlockSpec((1, TQ, KU + TQ), lambda h: (h, 0, 0)),
        out_shape=jax.ShapeDtypeStruct((NSA_HEADS, TQ, KU + TQ), f32),
        compiler_params=_params(1),
        name="near_bias",
    )(tbl)
    return cmp_bias, near_bias


_QA_W = NSA_HEADS * LANES
_KV_W = NSA_KV_HEADS * LANES
_CMP_W = NSA_KV_HEADS * HEAD_DIM
_SB_W = SB_HEADS * HEAD_DIM
_GATE_ROWS = NSA_GROUP * 3


def _layout(d_model):
    names = ["qa", "kc", "vc", "ksl", "vsl", "kwn", "vwn", "g", "qb", "kb", "vb", "ma", "mb"]
    widths = [NSA_HEADS * HEAD_DIM] + [_CMP_W] * 6 + [LANES] + [_SB_W] * 3 + [d_model, d_model]
    offs = np.concatenate([[0], np.cumsum(widths)])
    return {n: (int(offs[i]), int(offs[i + 1])) for i, n in enumerate(names)}, int(offs[-1])


def _pack_w_in(w_in, d_model):
    q_w = NSA_HEADS * HEAD_DIM
    kv_w = NSA_KV_HEADS * HEAD_DIM
    g_w = NSA_HEADS * 3
    sizes = [q_w] + [kv_w] * 6 + [g_w] + [_SB_W] * 3 + [d_model, d_model]
    offs = np.concatenate([[0], np.cumsum(sizes)])
    parts = [w_in[:, int(offs[i]):int(offs[i + 1])] for i in range(len(sizes))]
    parts[7] = jnp.pad(parts[7], ((0, 0), (0, LANES - g_w)))
    return jnp.concatenate(parts, axis=1).astype(bf16)


def _inproj_kernel(lay, x_ref, g1_ref, sc_ref, sh_ref, w_ref, gq_ref, gk_ref,
                   qa_ref, kc_ref, vc_ref, ks_ref, vs_ref, kw_ref, vw_ref, g_ref,
                   qb_ref, kb_ref, vb_ref, ma_ref, mb_ref):
    i = pl.program_id(1)
    x = x_ref[0]
    ms = jnp.mean(x * x, axis=-1, keepdims=True)
    u = (x * lax.rsqrt(ms + EPS) * g1_ref[...]) * (1.0 + sc_ref[0]) + sh_ref[0]
    ub = u.astype(bf16)

    narrow = ("kc", "vc", "ksl", "vsl", "kwn", "vwn", "g")
    narrow_lo = lay[narrow[0]][0]
    z_narrow = _dot(ub, w_ref[:, narrow_lo:lay[narrow[-1]][1]])

    def proj(name):
        lo, hi = lay[name]
        if name in narrow:
            return z_narrow[:, lo - narrow_lo:hi - narrow_lo]
        return _dot(ub, w_ref[:, lo:hi])

    rows = x.shape[0]
    lane = lax.broadcasted_iota(jnp.int32, (rows, LANES), 1)
    low = lane < HEAD_DIM

    def pair_norm(z, gain):
        sq = z * z
        ss_a = jnp.sum(jnp.where(low, sq, 0.0), axis=1, keepdims=True)
        ss_b = jnp.sum(jnp.where(low, 0.0, sq), axis=1, keepdims=True)
        inv = lax.rsqrt(jnp.where(low, ss_a, ss_b) * (1.0 / HEAD_DIM) + EPS)
        return z * inv * gain

    def spread(z, extra):
        return jnp.concatenate([jnp.where(low, z, extra),
                                jnp.where(low, pltpu.roll(z, HEAD_DIM, 1), extra)], axis=1)

    scale = HEAD_DIM ** -0.5 * LOG2E
    zq = proj("qa")
    qa_ref[0] = jnp.concatenate(
        [spread(pair_norm(zq[:, j * LANES:(j + 1) * LANES], gq_ref[...]) * scale, 0.0)
         for j in range(NSA_HEADS // 2)], axis=1).astype(bf16)
    kc_ref[0] = proj("kc")
    vc_ref[0] = proj("vc")

    tok_blk = (i * rows + lax.broadcasted_iota(jnp.int32, (rows, LANES), 0)) // SEL_BLOCK
    onehot = jnp.where(lane - HEAD_DIM == tok_blk, 1.0, 0.0)
    ones_col = jnp.where(lane == HEAD_DIM, 1.0, 0.0)

    ks_ref[0] = spread(pair_norm(proj("ksl"), gk_ref[1:2, :]), onehot).astype(bf16)
    vs_ref[0] = spread(proj("vsl"), ones_col).astype(bf16)
    kw_ref[0] = spread(pair_norm(proj("kwn"), gk_ref[2:3, :]), 0.0).astype(bf16)
    vw_ref[0] = spread(proj("vwn"), ones_col).astype(bf16)
    g_ref[0] = jax.nn.sigmoid(proj("g")).T
    qb_ref[0] = (proj("qb") * scale).astype(bf16)
    kb_ref[0] = proj("kb").astype(bf16)
    vb_ref[0] = proj("vb").astype(bf16)
    ma_ref[0] = jax.nn.sigmoid(proj("ma")).astype(bf16)
    mb_ref[0] = jax.nn.sigmoid(proj("mb")).astype(bf16)


def _inproj(x, g1, scale1, shift1, w_packed, gq, gk):
    bsz, seq, d = x.shape
    lay, width = _layout(d)
    assert w_packed.shape == (d, width)
    tok = lambda w: pl.BlockSpec((1, TM, w), lambda b, i: (b, i, 0))
    full = lambda a: pl.BlockSpec(a.shape, lambda b, i: (0,) * a.ndim,
                                  pipeline_mode=pl.Buffered(1))
    mod = pl.BlockSpec((1, 1, d), lambda b, i: (b, 0, 0))
    out_w = [(_QA_W, bf16), (_CMP_W, f32), (_CMP_W, f32), (_KV_W, bf16), (_KV_W, bf16),
             (_KV_W, bf16), (_KV_W, bf16), (LANES, f32), (_SB_W, bf16), (_SB_W, bf16),
             (_SB_W, bf16), (d, bf16), (d, bf16)]
    gates_at = 7
    out_specs = [tok(w) for w, _ in out_w]
    out_shape = [jax.ShapeDtypeStruct((bsz, seq, w), dt) for w, dt in out_w]
    out_specs[gates_at] = pl.BlockSpec((1, LANES, TM), lambda b, i: (b, 0, i))
    out_shape[gates_at] = jax.ShapeDtypeStruct((bsz, LANES, seq), f32)
    return pl.pallas_call(
        functools.partial(_inproj_kernel, lay),
        grid=(bsz, seq // TM),
        in_specs=[tok(d), full(g1), mod, mod, full(w_packed), full(gq), full(gk)],
        out_specs=out_specs,
        out_shape=out_shape,
        compiler_params=_params(2),
        name="inproj",
    )(x, g1, scale1, shift1, w_packed, gq, gk)


def _compress_kernel(xk_ref, xv_ref, pos_ref, w1k_ref, w2k_ref, w1v_ref, w2v_ref, gk_ref,
                     ko_ref, vo_ref):
    nch = xk_ref.shape[1] // CMP_STRIDE

    def mlp(x_ref, w1_ref, w2_ref):
        first = jnp.zeros((nch, LANES), f32)
        second = jnp.zeros((nch, LANES), f32)
        for l in range(CMP_STRIDE):
            xl = x_ref[0, pl.ds(l, nch, stride=CMP_STRIDE), :]
            lo = l + CMP_STRIDE
            first = first + _dot((xl + pos_ref[l:l + 1, :]).astype(bf16), w1_ref[l])
            second = second + _dot((xl + pos_ref[lo:lo + 1, :]).astype(bf16), w1_ref[lo])
        pre = first + pltpu.roll(second, nch - 1, 0)
        hid = pre * jax.nn.sigmoid(pre)
        return _dot(hid.astype(bf16), w2_ref[...])

    k = mlp(xk_ref, w1k_ref, w2k_ref)
    v = mlp(xv_ref, w1v_ref, w2v_ref)
    for h in range(NSA_KV_HEADS):
        kh = k[:, h * HEAD_DIM:(h + 1) * HEAD_DIM]
        ms = jnp.mean(kh * kh, axis=-1, keepdims=True)
        ko_ref[0, h] = (kh * lax.rsqrt(ms + EPS) * gk_ref[...]).astype(bf16)
        vh = v[:, h * HEAD_DIM:(h + 1) * HEAD_DIM]
        ones_col = jnp.where(lax.broadcasted_iota(jnp.int32, vh.shape, 1) == 0, 1.0, 0.0)
        vo_ref[0, h] = jnp.concatenate([vh, ones_col], axis=1).astype(bf16)


def _compress(xk, xv, pos, w1k, w2k, w1v, w2v, gk0):
    bsz, seq, width = xk.shape
    nch = seq // CMP_STRIDE
    assert width == NSA_KV_HEADS * HEAD_DIM == LANES

    def both_heads(w):
        z = jnp.zeros_like(w)
        return jnp.concatenate([jnp.concatenate([w, z], axis=2),
                                jnp.concatenate([z, w], axis=2)], axis=1).astype(bf16)

    w1 = lambda w: both_heads(w.reshape(CMP_BLOCK, HEAD_DIM, w.shape[1]))
    w2 = lambda w: both_heads(w[None])[0]
    args = (xk, xv, jnp.tile(pos, (1, NSA_KV_HEADS)), w1(w1k), w2(w2k), w1(w1v), w2(w2v), gk0)
    blk = pl.BlockSpec((1, seq, width), lambda b: (b, 0, 0))
    full = lambda a: pl.BlockSpec(a.shape, lambda b: (0,) * a.ndim)
    out = lambda w: pl.BlockSpec((1, NSA_KV_HEADS, nch, w), lambda b: (b, 0, 0, 0))
    shape = lambda w: jax.ShapeDtypeStruct((bsz, NSA_KV_HEADS, nch, w), bf16)
    return pl.pallas_call(
        _compress_kernel,
        grid=(bsz,),
        in_specs=[blk, blk] + [full(a) for a in args[2:]],
        out_specs=[out(HEAD_DIM), out(LANES)],
        out_shape=[shape(HEAD_DIM), shape(LANES)],
        compiler_params=_params(1),
        name="compress",
    )(*args)


def _flash_step(s, v, m_ref, acc_ref, first=False):
    row_max = jnp.max(s, axis=1, keepdims=True)
    if first:
        m_new = jnp.broadcast_to(row_max, m_ref.shape)
    else:
        m_prev = m_ref[...]
        m_new = jnp.maximum(m_prev, row_max)
    p = jnp.exp2(s - jnp.concatenate([m_new] * (s.shape[1] // LANES), axis=1))
    pv = _dot(p.astype(bf16), v)
    acc_ref[...] = pv if first else jnp.exp2(m_prev - m_new) * acc_ref[...] + pv
    m_ref[...] = m_new


def _flash_init(m_ref, acc_ref):
    m_ref[...] = jnp.full(m_ref.shape, MASKED, f32)
    acc_ref[...] = jnp.zeros(acc_ref.shape, f32)


def _flash_out_t(acc_ref):
    acc_t = acc_ref[...].T
    return acc_t[:HEAD_DIM] / acc_t[HEAD_DIM:HEAD_DIM + 1]


def _nsa_kernel(n_top, n_tiles, q_ref, g_ref, kc_ref, vc_ref, ks_ref, vs_ref, kw_ref, vw_ref,
                bc_ref, bn_ref, ovt_ref, o_ref, qs_ref, qw_ref, oc_ref,
                m_ref, acc_ref, mw_ref, accw_ref):
    i = pl.program_id(1)
    kv_heads = range(NSA_KV_HEADS)
    rows = NSA_GROUP * TQ
    start = i * TQ
    assert WINDOW % TQ == 0 and TQ % KU == 0
    first_interior = WINDOW // TQ
    window_extra = (WINDOW - TQ) // KU - 1

    def attend(hk, q_rows_ref, k_ref, v_ref, off, width, bias, state, first=False):
        off = pl.multiple_of(off, KU)
        lanes = slice(hk * LANES, (hk + 1) * LANES)
        s = _dot_nt(q_rows_ref[hk], k_ref[0, pl.ds(off, width), lanes])
        if bias is not None:
            s = s + bias
        _flash_step(s, v_ref[0, pl.ds(off, width), lanes], state[0].at[hk], state[1].at[hk],
                    first)

    def when(cond, guarded):
        return pl.when(cond) if guarded else (lambda fn: fn())

    def near_steps(q_rows_ref, k_ref, v_ref, state, guarded, extra=0):
        group = lambda hk: slice(hk * NSA_GROUP, (hk + 1) * NSA_GROUP)

        @when(i >= 1, guarded)
        def _():
            for hk in kv_heads:
                bias = bn_ref[group(hk)].reshape(rows, KU + TQ)
                if extra:
                    bias = jnp.concatenate([jnp.zeros((rows, extra * KU), f32), bias], axis=1)
                attend(hk, q_rows_ref, k_ref, v_ref, start - (1 + extra) * KU,
                       TQ + (1 + extra) * KU, bias, state)

        if guarded:
            @pl.when(i == 0)
            def _():
                for hk in kv_heads:
                    attend(hk, q_rows_ref, k_ref, v_ref, start, TQ,
                           bn_ref[group(hk), :, KU:].reshape(rows, TQ), state)

    def select(hk, live):
        heads = range(hk * NSA_GROUP, (hk + 1) * NSA_GROUP)
        qpad = jnp.concatenate([q_ref[0, :, g * LANES:(g + 1) * LANES] for g in heads], axis=0)
        qw_ref[hk] = qpad

        n_cmp = min(bc_ref.shape[2], -(-live * SEL_BLOCK // (CMP_STRIDE * LANES)) * LANES)
        bc = bc_ref[hk * NSA_GROUP:(hk + 1) * NSA_GROUP, :, :n_cmp].reshape(rows, n_cmp)
        s_c = _dot_nt(qpad[:, :HEAD_DIM], kc_ref[0, hk, :n_cmp]) + bc
        m_c = jnp.maximum(jnp.max(s_c, axis=1, keepdims=True), 0.1 * MASKED)
        e_c = jnp.exp2(s_c - m_c).astype(bf16)
        oc_ref[hk] = _dot(e_c, vc_ref[0, hk, :n_cmp])

        nblk = ovt_ref.shape[0] - SUBLANES
        imp = jnp.zeros((nblk, TQ), f32)
        for g in range(NSA_GROUP):
            t = _dot_nt(ovt_ref[:, :n_cmp], e_c[g * TQ:(g + 1) * TQ])
            imp = imp + t[:nblk] / jnp.maximum(t[nblk:nblk + 1], 1e-30)
        blk = lax.broadcasted_iota(jnp.int32, (nblk, TQ), 0)
        cur = (start + lax.broadcasted_iota(jnp.int32, (nblk, TQ), 1)) // SEL_BLOCK
        forced = (blk == 0) | (blk == cur) | (blk == cur - 1)
        imp = jnp.where(blk > cur, NEG_BLOCK, imp + jnp.where(forced, FORCED_BONUS, 0.0))
        sub = SUBLANES
        groups = [imp[lo:lo + sub] for lo in range(0, live, sub)]
        ranks = [jnp.zeros((sub, TQ), f32) for _ in groups]
        row = lax.broadcasted_iota(jnp.int32, (sub, TQ), 0)
        for b2 in range(live):
            other = imp[b2:b2 + 1, :]
            for gi, grp in enumerate(groups):
                lo = gi * sub
                if lo > b2:
                    ranks[gi] = jnp.where(other >= grp, ranks[gi] + 1.0, ranks[gi])
                elif lo + sub - 1 < b2:
                    ranks[gi] = jnp.where(other > grp, ranks[gi] + 1.0, ranks[gi])
                else:
                    ranks[gi] = ranks[gi] + jnp.where(row + lo > b2,
                                                      jnp.where(other >= grp, 1.0, 0.0),
                                                      jnp.where(other > grp, 1.0, 0.0))
        rank = jnp.concatenate(ranks + [jnp.zeros((nblk - live, TQ), f32)] * (live < nblk), axis=0)
        usable = (rank < n_top) & (blk <= cur)
        sel_t = jnp.where(usable, 0.0, UNSELECTED)
        sel_pad = jnp.concatenate([jnp.zeros((LANES - nblk, TQ), f32), sel_t], axis=0).T
        sel_rows = jnp.concatenate([sel_pad.astype(bf16)] * NSA_GROUP, axis=0)
        qs_ref[hk] = qpad + sel_rows

    def head(guarded, live):
        for hk in kv_heads:
            select(hk, live)

        win = (mw_ref, accw_ref)
        if guarded:
            _flash_init(*win)

        @when(i >= first_interior, guarded)
        def _():
            r = lax.broadcasted_iota(jnp.int32, (rows, TQ), 0) & (TQ - 1)
            c = lax.broadcasted_iota(jnp.int32, (rows, TQ), 1)
            for hk in kv_heads:
                attend(hk, qw_ref, kw_ref, vw_ref, start - WINDOW, TQ,
                       jnp.where(c > r, 0.0, MASKED), win, first=not guarded)

        near_steps(qw_ref, kw_ref, vw_ref, win, guarded, window_extra)
        _flash_init(m_ref, acc_ref)

    def tail(guarded, extra=0):
        near_steps(qs_ref, ks_ref, vs_ref, (m_ref, acc_ref), guarded, extra)
        outs = []
        for hk in kv_heads:
            oc_t = oc_ref[hk].T
            o_c = oc_t[:HEAD_DIM] / jnp.maximum(oc_t[HEAD_DIM:HEAD_DIM + 1], 1e-30)
            o_s = _flash_out_t(acc_ref.at[hk])
            o_w = _flash_out_t(accw_ref.at[hk])
            gates = g_ref[0]
            for g in range(NSA_GROUP):
                sl = slice(g * TQ, (g + 1) * TQ)
                at = hk * _GATE_ROWS + 3 * g
                outs.append(gates[at:at + 1] * o_c[:, sl]
                            + gates[at + 1:at + 2] * o_s[:, sl]
                            + gates[at + 2:at + 3] * o_w[:, sl])
        o_ref[0] = jnp.concatenate(outs, axis=0).T.astype(bf16)

    interior = i >= first_interior
    blocks_upto = lambda hi: min(-(-hi * TQ // (SEL_BLOCK * SUBLANES)) * SUBLANES, LANES - HEAD_DIM)
    pl.when(jnp.logical_not(interior))(lambda: head(True, blocks_upto(first_interior)))
    lo = first_interior
    while lo < n_tiles:
        hi = min(2 * lo, n_tiles)
        pl.when((i >= lo) & (i < hi))(functools.partial(head, False, blocks_upto(hi)))
        lo = hi

    sel = (m_ref, acc_ref)
    n_far = jnp.maximum(i * (TQ // KU) - 1, 0)

    def far_steps(off, n_steps):
        for step in range(n_steps):
            for hk in kv_heads:
                attend(hk, qs_ref, ks_ref, vs_ref, off + step * (2 * KU), 2 * KU, None, sel)

    def far_trip(c, carry):
        far_steps(c * (8 * KU), 4)
        return carry

    lax.fori_loop(0, n_far // 8, far_trip, 0)
    rem_off = (n_far // 8) * (8 * KU)

    @pl.when((n_far & 4) != 0)
    def _():
        far_steps(rem_off, 2)

    edge = jnp.logical_not(interior)

    @pl.when(edge & ((n_far & 2) != 0))
    def _():
        far_steps(rem_off + (n_far & 4) * KU, 1)

    @pl.when(edge & ((n_far & 1) != 0))
    def _():
        for hk in kv_heads:
            attend(hk, qs_ref, ks_ref, vs_ref, rem_off + (n_far & 6) * KU, KU, None, sel)

    for extra in range(4):
        pl.when(interior & ((n_far & 3) == extra))(functools.partial(tail, False, extra))
    pl.when(edge)(lambda: tail(True))


def _nsa(qa, gates, kcmp, vcmp, ks, vs, kw, vw, cmp_bias, near_bias, ovt):
    bsz, seq, _ = qa.shape
    nc = kcmp.shape[2]
    n_top = min(SEL_TOPK, seq // SEL_BLOCK)
    rows = NSA_GROUP * TQ
    tok = lambda w: pl.BlockSpec((1, TQ, w), lambda b, i: (b, i, 0))
    kv = pl.BlockSpec((1, seq, NSA_KV_HEADS * LANES), lambda b, i: (b, 0, 0))
    cmp = lambda a: pl.BlockSpec((1,) + a.shape[1:], lambda b, i: (b, 0, 0, 0))
    per_kv = lambda width, dt: pltpu.VMEM((NSA_KV_HEADS, rows, width), dt)
    return pl.pallas_call(
        functools.partial(_nsa_kernel, n_top, seq // TQ),
        grid=(bsz, seq // TQ),
        in_specs=[tok(NSA_HEADS * LANES),
                  pl.BlockSpec((1, LANES, TQ), lambda b, i: (b, 0, i)),
                  cmp(kcmp), cmp(vcmp), kv, kv, kv, kv,
                  pl.BlockSpec((NSA_HEADS, TQ, nc), lambda b, i: (0, i, 0)),
                  pl.BlockSpec((NSA_HEADS, TQ, KU + TQ), lambda b, i: (0, 0, 0)),
                  pl.BlockSpec(ovt.shape, lambda b, i: (0, 0))],
        out_specs=tok(NSA_HEADS * HEAD_DIM),
        out_shape=jax.ShapeDtypeStruct((bsz, seq, NSA_HEADS * HEAD_DIM), bf16),
        scratch_shapes=[per_kv(LANES, bf16), per_kv(LANES, bf16),
                        per_kv(LANES, f32), per_kv(LANES, f32), per_kv(LANES, f32),
                        per_kv(LANES, f32), per_kv(LANES, f32)],
        compiler_params=_params(2),
        name="nsa",
    )(qa, gates, kcmp, vcmp, ks, vs, kw, vw, cmp_bias, near_bias, ovt)


def _sb_kernel(q_ref, k_ref, v_ref, tri_ref, o_ref, carry_ref, acc_ref):
    i = pl.program_id(2)
    n_heads = carry_ref.shape[0]
    lane = lax.broadcasted_iota(jnp.int32, (TS, LANES), 1)
    q_heads = []
    for pair in range(n_heads // 2):
        q = q_ref[0, :, pair * LANES:(pair + 1) * LANES]
        zero = jnp.zeros_like(q)
        q_heads += [jnp.where(lane < HEAD_DIM, q, zero), jnp.where(lane >= HEAD_DIM, q, zero)]
    pair_lanes = lambda hh: slice((hh // 2) * LANES, (hh // 2 + 1) * LANES)
    r = lax.broadcasted_iota(jnp.int32, (TS, TS), 0)
    c = lax.broadcasted_iota(jnp.int32, (TS, TS), 1)
    before = c < r

    def chunks(jobs, first):
        offs = [pl.multiple_of(off, TS) for off, _ in jobs]
        stage = []
        for (_, diagonal), off in zip(jobs, offs):
            for hh in range(n_heads):
                z = _dot_nt(q_heads[hh], k_ref[0, pl.ds(off, TS), pair_lanes(hh)])
                soft = jnp.log2(1.0 + jnp.exp2(-jnp.abs(z)))
                log_keep = jnp.minimum(-z, 0.0) - soft
                log_sig = log_keep + z
                if diagonal:
                    log_keep = jnp.where(before, log_keep, 0.0)
                later = _dot(log_keep.astype(bf16), tri_ref[...])
                stage.append((log_sig + later, jnp.sum(log_keep, axis=1, keepdims=True)))
        for hh in range(n_heads):
            carry = None if first else carry_ref[hh]
            acc = None if first else acc_ref[hh]
            for j, ((_, diagonal), off) in enumerate(zip(jobs, offs)):
                base, total = stage[n_heads * j + hh]
                if carry is not None:
                    base = base + jnp.concatenate([carry] * (TS // LANES), axis=1)
                a = jnp.exp2(base)
                if diagonal:
                    a = jnp.where(before, a, 0.0)
                out = _dot(a.astype(bf16), v_ref[0, pl.ds(off, TS), pair_lanes(hh)])
                acc = out if acc is None else acc + out
                total = jnp.broadcast_to(total, (TS, LANES))
                carry = total if carry is None else carry + total
            carry_ref[hh] = carry
            acc_ref[hh] = acc

    @pl.when(i == 0)
    def _():
        chunks([(0, True)], True)

    @pl.when(i >= 1)
    def _():
        chunks([(i * TS, True), ((i - 1) * TS, False)], True)

    def any_live():
        return jnp.max(carry_ref[...]) > EXP2_UNDERFLOW

    def more(state):
        n, live = state
        return jnp.logical_and(n < i, live)

    def older(state):
        n, _ = state
        chunks([((i - 1 - n) * TS, False)], False)
        return n + 1, any_live()

    lax.while_loop(more, older, (jnp.int32(1), any_live()))
    o_ref[0] = jnp.concatenate(
        [jnp.where(lane < HEAD_DIM, acc_ref[2 * pair], acc_ref[2 * pair + 1])
         for pair in range(n_heads // 2)], axis=1).astype(bf16)


SB_STEP_HEADS = 8


def _sb(qb, kb, vb, tri):
    bsz, seq, width = qb.shape
    step_w = SB_STEP_HEADS * HEAD_DIM
    q_spec = pl.BlockSpec((1, TS, step_w), lambda b, p, i: (b, i, p))
    kv_spec = pl.BlockSpec((1, seq, step_w), lambda b, p, i: (b, 0, p))
    state = pltpu.VMEM((SB_STEP_HEADS, TS, LANES), f32)
    return pl.pallas_call(
        _sb_kernel,
        grid=(bsz, width // step_w, seq // TS),
        in_specs=[q_spec, kv_spec, kv_spec, pl.BlockSpec(tri.shape, lambda b, p, i: (0, 0))],
        out_specs=q_spec,
        out_shape=jax.ShapeDtypeStruct((bsz, seq, width), bf16),
        scratch_shapes=[state, state],
        compiler_params=_params(3),
        name="sb",
    )(qb, kb, vb, tri)


def _post_kernel(ya_ref, yb_ref, ma_ref, mb_ref, x_ref, gate1_ref, g2_ref, sc_ref, sh_ref,
                 gate_ref, wa_ref, wb_ref, wo_ref, w1_ref, w2_ref, o_ref):
    y_a = _dot(ya_ref[0], wa_ref[...])
    y_b = _dot(yb_ref[0], wb_ref[...])
    mixed = ma_ref[0].astype(f32) * y_a + mb_ref[0].astype(f32) * y_b
    hres = x_ref[0] + gate1_ref[0] * _dot(mixed.astype(bf16), wo_ref[...])
    d = hres.shape[1]
    ms = jnp.mean(hres * hres, axis=-1, keepdims=True)
    u = (hres * lax.rsqrt(ms + EPS) * g2_ref[...]) * (1.0 + sc_ref[0]) + sh_ref[0]
    ub = u.astype(bf16)
    ff = jnp.zeros(hres.shape, f32)
    for c in range(w1_ref.shape[1] // d):
        hid = jnp.maximum(_dot(ub, w1_ref[:, c * d:(c + 1) * d]), 0.0)
        ff = ff + _dot((hid * hid).astype(bf16), w2_ref[c * d:(c + 1) * d, :])
    o_ref[0] = hres + gate_ref[0] * ff


def _post(ya, yb, ma, mb, x, gate1, g2, scale2, shift2, gate2, wa, wb, wo, w1, w2):
    bsz, seq, d = x.shape
    tok = lambda w: pl.BlockSpec((1, TM, w), lambda b, i: (b, i, 0))
    mod = pl.BlockSpec((1, 1, d), lambda b, i: (b, 0, 0))
    const = lambda a: pl.BlockSpec(a.shape, lambda b, i: (0,) * a.ndim,
                                   pipeline_mode=pl.Buffered(1))
    return pl.pallas_call(
        _post_kernel,
        grid=(bsz, seq // TM),
        in_specs=[tok(ya.shape[2]), tok(yb.shape[2]), tok(d), tok(d), tok(d), mod,
                  const(g2), mod, mod, mod, const(wa), const(wb), const(wo), const(w1), const(w2)],
        out_specs=tok(d),
        out_shape=jax.ShapeDtypeStruct((bsz, seq, d), f32),
        compiler_params=_params(2),
        name="post",
    )(ya, yb, ma, mb, x, gate1, g2, scale2, shift2, gate2, wa, wb, wo, w1, w2)


def _overlap_t(nc_pad, nsel_pad, nc, nsel):
    c_start = np.arange(nc_pad) * CMP_STRIDE
    s_start = np.arange(nsel_pad) * SEL_BLOCK
    ov = (np.minimum(c_start[None, :] + CMP_BLOCK, s_start[:, None] + SEL_BLOCK)
          - np.maximum(c_start[None, :], s_start[:, None]))
    ov = np.clip(ov, 0, CMP_BLOCK).astype(np.float32) / CMP_BLOCK
    ov[nsel:, :] = 0.0
    ov[:, nc:] = 0.0
    return ov


def _layer(h, mod, rel_tiles, p):
    bsz, seq, d = h.shape
    shift1, scale1, gate1, shift2, scale2, gate2 = [
        mod[:, k * d:(k + 1) * d].reshape(bsz, 1, d) for k in range(6)]
    cmp_bias, near_bias = rel_tiles

    gq = jnp.tile(p["q_norm_g"], 2).reshape(1, LANES)
    gk = jnp.tile(p["k_norm_g"], (1, 2))

    (qa, kc, vc, ks, vs, kw, vw, gates, qb, kb, vb, ma, mb) = _inproj(
        h, p["norm1_g"].reshape(1, d), scale1, shift1, _pack_w_in(p["w_in"], d), gq, gk)

    nch = seq // CMP_STRIDE
    kcmp, vcmp = _compress(kc, vc, p["cmp_pos"], p["cmp_k_w1"], p["cmp_k_w2"],
                           p["cmp_v_w1"], p["cmp_v_w2"], p["k_norm_g"][0].reshape(1, HEAD_DIM))

    nc = (seq - CMP_BLOCK) // CMP_STRIDE + 1
    ovt = jnp.asarray(np.concatenate([_overlap_t(nch, HEAD_DIM, nc, seq // SEL_BLOCK),
                                      np.ones((SUBLANES, nch), np.float32)], axis=0), bf16)
    y_nsa = _nsa(qa, gates, kcmp, vcmp, ks, vs, kw, vw, cmp_bias, near_bias, ovt)

    y_sb = _sb(qb, kb, vb, jnp.asarray(np.tril(np.ones((TS, TS)), -1), bf16))

    return _post(y_nsa, y_sb, ma, mb, h, gate1, p["norm2_g"].reshape(1, d), scale2, shift2, gate2,
                 p["w_up_nsa"].astype(bf16), p["w_up_sb"].astype(bf16), p["w_out"].astype(bf16),
                 p["mlp_w1"].astype(bf16), p["mlp_w2"].astype(bf16))


def kernel(x, c, rel_bias, ada_w, ada_b, norm1_g, norm2_g, w_in, cmp_pos, cmp_k_w1, cmp_k_w2,
           cmp_v_w1, cmp_v_w2, q_norm_g, k_norm_g, w_up_nsa, w_up_sb, w_out, mlp_w1, mlp_w2):
    bsz, seq, d = x.shape
    assert seq % TM == 0 and seq // SEL_BLOCK <= HEAD_DIM and seq >= WINDOW + TQ
    assert CMP_BLOCK == 2 * CMP_STRIDE and KU % SEL_BLOCK == 0
    tbl = rel_bias.astype(f32)
    rel_tiles = _bias_tiles(tbl, seq, seq // CMP_STRIDE)
    stacked = dict(norm1_g=norm1_g, norm2_g=norm2_g, w_in=w_in, cmp_pos=cmp_pos,
                   cmp_k_w1=cmp_k_w1, cmp_k_w2=cmp_k_w2, cmp_v_w1=cmp_v_w1, cmp_v_w2=cmp_v_w2,
                   q_norm_g=q_norm_g, k_norm_g=k_norm_g, w_up_nsa=w_up_nsa, w_up_sb=w_up_sb,
                   w_out=w_out, mlp_w1=mlp_w1, mlp_w2=mlp_w2)
    h = x
    for layer in range(ada_w.shape[0]):
        mod = _adaln(c, ada_w[layer], ada_b[layer])
        h = _layer(h, mod, rel_tiles, {k: v[layer] for k, v in stacked.items()})
    return h
```

```python
import functools
import math

import numpy as np
import jax
import jax.numpy as jnp
from jax import lax
from jax.experimental import pallas as pl
from jax.experimental.pallas import tpu as pltpu

f32 = jnp.float32
bf16 = jnp.bfloat16

HEAD_DIM = 64
NSA_HEADS = 8
NSA_KV_HEADS = 2
NSA_GROUP = NSA_HEADS // NSA_KV_HEADS
SB_HEADS = 8
CMP_BLOCK = 32
CMP_STRIDE = 16
SEL_BLOCK = 64
SEL_TOPK = 16
WINDOW = 512
N_BUCKETS = 32
MAX_DISTANCE = 128
EPS = 1e-6
FORCED_BONUS = 1e4
NEG_BLOCK = -1e9

LANES = 128
SUBLANES = 8
MASKED = -1e30
UNSELECTED = -1e9
LOG2E = math.log2(math.e)
EXP2_UNDERFLOW = -150.0
VMEM_LIMIT = 56 * 1024 * 1024

TQ = 256
KU = 128
TS = 256
TM = 512


def _bucket_thresholds():
    n = np.arange(0, 4 * MAX_DISTANCE)
    max_exact = N_BUCKETS // 2
    nf = np.maximum(n, 1).astype(np.float32)
    large = max_exact + (np.log(nf / max_exact) / math.log(MAX_DISTANCE / max_exact)
                         * (N_BUCKETS - max_exact)).astype(np.int32)
    large = np.minimum(large, N_BUCKETS - 1)
    b = np.where(n < max_exact, n, large)
    assert np.all(np.diff(b) >= 0) and b[-1] == N_BUCKETS - 1
    return [int(np.argmax(b >= k)) for k in range(N_BUCKETS)]


BUCKET_START = _bucket_thresholds()
assert BUCKET_START[-1] <= KU


def _dot(a, b):
    return jnp.dot(a, b, preferred_element_type=f32)


def _dot_nt(a, b):
    return lax.dot_general(a, b, (((1,), (1,)), ((), ())), preferred_element_type=f32)


def _split(a):
    hi = a.astype(bf16)
    lo = (a - hi.astype(f32)).astype(bf16)
    return hi, lo


def _params(n_grid):
    return pltpu.CompilerParams(dimension_semantics=("arbitrary",) * n_grid,
                                vmem_limit_bytes=VMEM_LIMIT)


def _adaln_kernel(c_ref, w_ref, b_ref, o_ref):
    c = c_ref[...]
    a = c * jax.nn.sigmoid(c)
    ah, al = _split(a)
    wh, wl = _split(w_ref[...])
    o_ref[...] = _dot(ah, wh) + _dot(ah, wl) + _dot(al, wh) + b_ref[...]


def _adaln(c, w, b):
    bsz, d = c.shape
    n = w.shape[1]
    return pl.pallas_call(
        _adaln_kernel,
        grid=(n // d,),
        in_specs=[pl.BlockSpec((bsz, d), lambda j: (0, 0)),
                  pl.BlockSpec((d, d), lambda j: (0, j)),
                  pl.BlockSpec((1, d), lambda j: (0, j))],
        out_specs=pl.BlockSpec((bsz, d), lambda j: (0, j)),
        out_shape=jax.ShapeDtypeStruct((bsz, n), f32),
        compiler_params=_params(1),
        name="adaln",
    )(c, w, b.reshape(1, n))


def _bias_of_dist(dist, tbl_ref, h):
    out = jnp.full(dist.shape, tbl_ref[0, h], f32)
    for k in range(1, N_BUCKETS):
        out = jnp.where(dist >= BUCKET_START[k], tbl_ref[k, h], out)
    return jnp.where(dist >= 0, out, MASKED)


def _cmp_bias_kernel(tbl_ref, o_ref):
    h = pl.program_id(0)
    i = pl.program_id(1)
    rows, nc = o_ref.shape[1], o_ref.shape[2]
    t = i * rows + lax.broadcasted_iota(jnp.int32, (rows, LANES), 0)
    for lo in range(0, nc, LANES):
        last_first = lo * CMP_STRIDE + CMP_BLOCK - 1
        last_final = (lo + LANES - 1) * CMP_STRIDE + CMP_BLOCK - 1
        d_max = i * rows + rows - 1 - last_first
        d_min = i * rows - last_final
        cols = slice(lo, lo + LANES)

        @pl.when(d_max < 0)
        def _(cols=cols):
            o_ref[0, :, cols] = jnp.full((rows, LANES), MASKED * LOG2E, f32)

        @pl.when(d_min >= BUCKET_START[-1])
        def _(cols=cols):
            o_ref[0, :, cols] = jnp.full((rows, LANES), tbl_ref[N_BUCKETS - 1, h] * LOG2E, f32)

        @pl.when((d_max >= 0) & (d_min < BUCKET_START[-1]))
        def _(cols=cols, lo=lo):
            j = lo + lax.broadcasted_iota(jnp.int32, (rows, LANES), 1)
            dist = t - (j * CMP_STRIDE + CMP_BLOCK - 1)
            o_ref[0, :, cols] = _bias_of_dist(dist, tbl_ref, h) * LOG2E


def _near_bias_kernel(tbl_ref, o_ref):
    h = pl.program_id(0)
    r = lax.broadcasted_iota(jnp.int32, (TQ, KU + TQ), 0)
    c = lax.broadcasted_iota(jnp.int32, (TQ, KU + TQ), 1)
    o_ref[0] = (_bias_of_dist(r - c + KU, tbl_ref, h) - tbl_ref[N_BUCKETS - 1, h]) * LOG2E


def _bias_tiles(rel_bias, seq, nc_pad):
    tbl = rel_bias.astype(f32)
    smem = pl.BlockSpec(memory_space=pltpu.SMEM)
    rows = 512
    cmp_bias = pl.pallas_call(
        _cmp_bias_kernel,
        grid=(NSA_HEADS, seq // rows),
        in_specs=[smem],
        out_specs=pl.BlockSpec((1, rows, nc_pad), lambda h, i: (h, i, 0)),
        out_shape=jax.ShapeDtypeStruct((NSA_HEADS, seq, nc_pad), f32),
        compiler_params=_params(2),
        name="cmp_bias",
    )(tbl)
    near_bias = pl.pallas_call(
        _near_bias_kernel,
        grid=(NSA_HEADS,),
        in_specs=[smem],
        out_specs=pl.BlockSpec((1, TQ, KU + TQ), lambda h: (h, 0, 0)),
        out_shape=jax.ShapeDtypeStruct((NSA_HEADS, TQ, KU + TQ), f32),
        compiler_params=_params(1),
        name="near_bias",
    )(tbl)
    return cmp_bias, near_bias


_QA_W = NSA_HEADS * LANES
_KV_W = NSA_KV_HEADS * LANES
_CMP_W = NSA_KV_HEADS * HEAD_DIM
_SB_W = SB_HEADS * HEAD_DIM
_GATE_ROWS = NSA_GROUP * 3


def _layout(d_model):
    names = ["qa", "kc", "vc", "ksl", "vsl", "kwn", "vwn", "g", "qb", "kb", "vb", "ma", "mb"]
    widths = [NSA_HEADS * HEAD_DIM] + [_CMP_W] * 6 + [LANES] + [_SB_W] * 3 + [d_model, d_model]
    offs = np.concatenate([[0], np.cumsum(widths)])
    return {n: (int(offs[i]), int(offs[i + 1])) for i, n in enumerate(names)}, int(offs[-1])


def _pack_w_in(w_in, d_model):
    q_w = NSA_HEADS * HEAD_DIM
    kv_w = NSA_KV_HEADS * HEAD_DIM
    g_w = NSA_HEADS * 3
    sizes = [q_w] + [kv_w] * 6 + [g_w] + [_SB_W] * 3 + [d_model, d_model]
    offs = np.concatenate([[0], np.cumsum(sizes)])
    parts = [w_in[:, int(offs[i]):int(offs[i + 1])] for i in range(len(sizes))]
    parts[7] = jnp.pad(parts[7], ((0, 0), (0, LANES - g_w)))
    return jnp.concatenate(parts, axis=1).astype(bf16)


def _inproj_kernel(lay, x_ref, g1_ref, sc_ref, sh_ref, w_ref, gq_ref, gk_ref,
                   qa_ref, kc_ref, vc_ref, ks_ref, vs_ref, kw_ref, vw_ref, g_ref,
                   qb_ref, kb_ref, vb_ref, ma_ref, mb_ref):
    i = pl.program_id(1)
    x = x_ref[0]
    ms = jnp.mean(x * x, axis=-1, keepdims=True)
    u = (x * lax.rsqrt(ms + EPS) * g1_ref[...]) * (1.0 + sc_ref[0]) + sh_ref[0]
    ub = u.astype(bf16)

    narrow = ("kc", "vc", "ksl", "vsl", "kwn", "vwn", "g")
    narrow_lo = lay[narrow[0]][0]
    z_narrow = _dot(ub, w_ref[:, narrow_lo:lay[narrow[-1]][1]])

    def proj(name):
        lo, hi = lay[name]
        if name in narrow:
            return z_narrow[:, lo - narrow_lo:hi - narrow_lo]
        return _dot(ub, w_ref[:, lo:hi])

    rows = x.shape[0]
    lane = lax.broadcasted_iota(jnp.int32, (rows, LANES), 1)
    low = lane < HEAD_DIM

    def pair_norm(z, gain):
        sq = z * z
        ss_a = jnp.sum(jnp.where(low, sq, 0.0), axis=1, keepdims=True)
        ss_b = jnp.sum(jnp.where(low, 0.0, sq), axis=1, keepdims=True)
        inv = lax.rsqrt(jnp.where(low, ss_a, ss_b) * (1.0 / HEAD_DIM) + EPS)
        return z * inv * gain

    def spread(z, extra):
        return jnp.concatenate([jnp.where(low, z, extra),
                                jnp.where(low, pltpu.roll(z, HEAD_DIM, 1), extra)], axis=1)

    scale = HEAD_DIM ** -0.5 * LOG2E
    zq = proj("qa")
    qa_ref[0] = jnp.concatenate(
        [spread(pair_norm(zq[:, j * LANES:(j + 1) * LANES], gq_ref[...]) * scale, 0.0)
         for j in range(NSA_HEADS // 2)], axis=1).astype(bf16)
    kc_ref[0] = proj("kc")
    vc_ref[0] = proj("vc")

    tok_blk = (i * rows + lax.broadcasted_iota(jnp.int32, (rows, LANES), 0)) // SEL_BLOCK
    onehot = jnp.where(lane - HEAD_DIM == tok_blk, 1.0, 0.0)
    ones_col = jnp.where(lane == HEAD_DIM, 1.0, 0.0)

    ks_ref[0] = spread(pair_norm(proj("ksl"), gk_ref[1:2, :]), onehot).astype(bf16)
    vs_ref[0] = spread(proj("vsl"), ones_col).astype(bf16)
    kw_ref[0] = spread(pair_norm(proj("kwn"), gk_ref[2:3, :]), 0.0).astype(bf16)
    vw_ref[0] = spread(proj("vwn"), ones_col).astype(bf16)
    g_ref[0] = jax.nn.sigmoid(proj("g")).T
    qb_ref[0] = (proj("qb") * scale).astype(bf16)
    kb_ref[0] = proj("kb").astype(bf16)
    vb_ref[0] = proj("vb").astype(bf16)
    ma_ref[0] = jax.nn.sigmoid(proj("ma")).astype(bf16)
    mb_ref[0] = jax.nn.sigmoid(proj("mb")).astype(bf16)


def _inproj(x, g1, scale1, shift1, w_packed, gq, gk):
    bsz, seq, d = x.shape
    lay, width = _layout(d)
    assert w_packed.shape == (d, width)
    tok = lambda w: pl.BlockSpec((1, TM, w), lambda b, i: (b, i, 0))
    full = lambda a: pl.BlockSpec(a.shape, lambda b, i: (0,) * a.ndim,
                                  pipeline_mode=pl.Buffered(1))
    mod = pl.BlockSpec((1, 1, d), lambda b, i: (b, 0, 0))
    out_w = [(_QA_W, bf16), (_CMP_W, f32), (_CMP_W, f32), (_KV_W, bf16), (_KV_W, bf16),
             (_KV_W, bf16), (_KV_W, bf16), (LANES, f32), (_SB_W, bf16), (_SB_W, bf16),
             (_SB_W, bf16), (d, bf16), (d, bf16)]
    gates_at = 7
    out_specs = [tok(w) for w, _ in out_w]
    out_shape = [jax.ShapeDtypeStruct((bsz, seq, w), dt) for w, dt in out_w]
    out_specs[gates_at] = pl.BlockSpec((1, LANES, TM), lambda b, i: (b, 0, i))
    out_shape[gates_at] = jax.ShapeDtypeStruct((bsz, LANES, seq), f32)
    return pl.pallas_call(
        functools.partial(_inproj_kernel, lay),
        grid=(bsz, seq // TM),
        in_specs=[tok(d), full(g1), mod, mod, full(w_packed), full(gq), full(gk)],
        out_specs=out_specs,
        out_shape=out_shape,
        compiler_params=_params(2),
        name="inproj",
    )(x, g1, scale1, shift1, w_packed, gq, gk)


def _compress_kernel(xk_ref, xv_ref, pos_ref, w1k_ref, w2k_ref, w1v_ref, w2v_ref, gk_ref,
                     ko_ref, vo_ref):
    nch = xk_ref.shape[1] // CMP_STRIDE

    def mlp(x_ref, w1_ref, w2_ref):
        first = jnp.zeros((nch, LANES), f32)
        second = jnp.zeros((nch, LANES), f32)
        for l in range(CMP_STRIDE):
            xl = x_ref[0, pl.ds(l, nch, stride=CMP_STRIDE), :]
            lo = l + CMP_STRIDE
            first = first + _dot((xl + pos_ref[l:l + 1, :]).astype(bf16), w1_ref[l])
            second = second + _dot((xl + pos_ref[lo:lo + 1, :]).astype(bf16), w1_ref[lo])
        pre = first + pltpu.roll(second, nch - 1, 0)
        hid = pre * jax.nn.sigmoid(pre)
        return _dot(hid.astype(bf16), w2_ref[...])

    k = mlp(xk_ref, w1k_ref, w2k_ref)
    v = mlp(xv_ref, w1v_ref, w2v_ref)
    for h in range(NSA_KV_HEADS):
        kh = k[:, h * HEAD_DIM:(h + 1) * HEAD_DIM]
        ms = jnp.mean(kh * kh, axis=-1, keepdims=True)
        ko_ref[0, h] = (kh * lax.rsqrt(ms + EPS) * gk_ref[...]).astype(bf16)
        vh = v[:, h * HEAD_DIM:(h + 1) * HEAD_DIM]
        ones_col = jnp.where(lax.broadcasted_iota(jnp.int32, vh.shape, 1) == 0, 1.0, 0.0)
        vo_ref[0, h] = jnp.concatenate([vh, ones_col], axis=1).astype(bf16)


def _compress(xk, xv, pos, w1k, w2k, w1v, w2v, gk0):
    bsz, seq, width = xk.shape
    nch = seq // CMP_STRIDE
    assert width == NSA_KV_HEADS * HEAD_DIM == LANES

    def both_heads(w):
        z = jnp.zeros_like(w)
        return jnp.concatenate([jnp.concatenate([w, z], axis=2),
                                jnp.concatenate([z, w], axis=2)], axis=1).astype(bf16)

    w1 = lambda w: both_heads(w.reshape(CMP_BLOCK, HEAD_DIM, w.shape[1]))
    w2 = lambda w: both_heads(w[None])[0]
    args = (xk, xv, jnp.tile(pos, (1, NSA_KV_HEADS)), w1(w1k), w2(w2k), w1(w1v), w2(w2v), gk0)
    blk = pl.BlockSpec((1, seq, width), lambda b: (b, 0, 0))
    full = lambda a: pl.BlockSpec(a.shape, lambda b: (0,) * a.ndim)
    out = lambda w: pl.BlockSpec((1, NSA_KV_HEADS, nch, w), lambda b: (b, 0, 0, 0))
    shape = lambda w: jax.ShapeDtypeStruct((bsz, NSA_KV_HEADS, nch, w), bf16)
    return pl.pallas_call(
        _compress_kernel,
        grid=(bsz,),
        in_specs=[blk, blk] + [full(a) for a in args[2:]],
        out_specs=[out(HEAD_DIM), out(LANES)],
        out_shape=[shape(HEAD_DIM), shape(LANES)],
        compiler_params=_params(1),
        name="compress",
    )(*args)


def _flash_step(s, v, m_ref, acc_ref, first=False):
    row_max = jnp.max(s, axis=1, keepdims=True)
    if first:
        m_new = jnp.broadcast_to(row_max, m_ref.shape)
    else:
        m_prev = m_ref[...]
        m_new = jnp.maximum(m_prev, row_max)
    p = jnp.exp2(s - jnp.concatenate([m_new] * (s.shape[1] // LANES), axis=1))
    pv = _dot(p.astype(bf16), v)
    acc_ref[...] = pv if first else jnp.exp2(m_prev - m_new) * acc_ref[...] + pv
    m_ref[...] = m_new


def _flash_init(m_ref, acc_ref):
    m_ref[...] = jnp.full(m_ref.shape, MASKED, f32)
    acc_ref[...] = jnp.zeros(acc_ref.shape, f32)


def _flash_out_t(acc_ref):
    acc_t = acc_ref[...].T
    return acc_t[:HEAD_DIM] / acc_t[HEAD_DIM:HEAD_DIM + 1]


def _nsa_kernel(n_top, n_tiles, q_ref, g_ref, kc_ref, vc_ref, ks_ref, vs_ref, kw_ref, vw_ref,
                bc_ref, bn_ref, ovt_ref, o_ref, qs_ref, qw_ref, oc_ref,
                m_ref, acc_ref, mw_ref, accw_ref):
    i = pl.program_id(1)
    kv_heads = range(NSA_KV_HEADS)
    rows = NSA_GROUP * TQ
    start = i * TQ
    assert WINDOW % TQ == 0 and TQ % KU == 0
    first_interior = WINDOW // TQ
    window_extra = (WINDOW - TQ) // KU - 1

    def attend(hk, q_rows_ref, k_ref, v_ref, off, width, bias, state, first=False):
        off = pl.multiple_of(off, KU)
        lanes = slice(hk * LANES, (hk + 1) * LANES)
        s = _dot_nt(q_rows_ref[hk], k_ref[0, pl.ds(off, width), lanes])
        if bias is not None:
            s = s + bias
        _flash_step(s, v_ref[0, pl.ds(off, width), lanes], state[0].at[hk], state[1].at[hk],
                    first)

    def when(cond, guarded):
        return pl.when(cond) if guarded else (lambda fn: fn())

    def near_steps(q_rows_ref, k_ref, v_ref, state, guarded, extra=0):
        group = lambda hk: slice(hk * NSA_GROUP, (hk + 1) * NSA_GROUP)

        @when(i >= 1, guarded)
        def _():
            for hk in kv_heads:
                bias = bn_ref[group(hk)].reshape(rows, KU + TQ)
                if extra:
                    bias = jnp.concatenate([jnp.zeros((rows, extra * KU), f32), bias], axis=1)
                attend(hk, q_rows_ref, k_ref, v_ref, start - (1 + extra) * KU,
                       TQ + (1 + extra) * KU, bias, state)

        if guarded:
            @pl.when(i == 0)
            def _():
                for hk in kv_heads:
                    attend(hk, q_rows_ref, k_ref, v_ref, start, TQ,
                           bn_ref[group(hk), :, KU:].reshape(rows, TQ), state)

    def select(hk, live):
        heads = range(hk * NSA_GROUP, (hk + 1) * NSA_GROUP)
        qpad = jnp.concatenate([q_ref[0, :, g * LANES:(g + 1) * LANES] for g in heads], axis=0)
        qw_ref[hk] = qpad

        n_cmp = min(bc_ref.shape[2], -(-live * SEL_BLOCK // (CMP_STRIDE * LANES)) * LANES)
        bc = bc_ref[hk * NSA_GROUP:(hk + 1) * NSA_GROUP, :, :n_cmp].reshape(rows, n_cmp)
        s_c = _dot_nt(qpad[:, :HEAD_DIM], kc_ref[0, hk, :n_cmp]) + bc
        m_c = jnp.maximum(jnp.max(s_c, axis=1, keepdims=True), 0.1 * MASKED)
        e_c = jnp.exp2(s_c - m_c).astype(bf16)
        oc_ref[hk] = _dot(e_c, vc_ref[0, hk, :n_cmp])

        nblk = ovt_ref.shape[0] - SUBLANES
        imp = jnp.zeros((nblk, TQ), f32)
        for g in range(NSA_GROUP):
            t = _dot_nt(ovt_ref[:, :n_cmp], e_c[g * TQ:(g + 1) * TQ])
            imp = imp + t[:nblk] / jnp.maximum(t[nblk:nblk + 1], 1e-30)
        blk = lax.broadcasted_iota(jnp.int32, (nblk, TQ), 0)
        cur = (start + lax.broadcasted_iota(jnp.int32, (nblk, TQ), 1)) // SEL_BLOCK
        forced = (blk == 0) | (blk == cur) | (blk == cur - 1)
        imp = jnp.where(blk > cur, NEG_BLOCK, imp + jnp.where(forced, FORCED_BONUS, 0.0))
        sub = SUBLANES
        groups = [imp[lo:lo + sub] for lo in range(0, live, sub)]
        ranks = [jnp.zeros((sub, TQ), f32) for _ in groups]
        row = lax.broadcasted_iota(jnp.int32, (sub, TQ), 0)
        for b2 in range(live):
            other = imp[b2:b2 + 1, :]
            for gi, grp in enumerate(groups):
                lo = gi * sub
                if lo > b2:
                    ranks[gi] = jnp.where(other >= grp, ranks[gi] + 1.0, ranks[gi])
                elif lo + sub - 1 < b2:
                    ranks[gi] = jnp.where(other > grp, ranks[gi] + 1.0, ranks[gi])
                else:
                    ranks[gi] = ranks[gi] + jnp.where(row + lo > b2,
                                                      jnp.where(other >= grp, 1.0, 0.0),
                                                      jnp.where(other > grp, 1.0, 0.0))
        rank = jnp.concatenate(ranks + [jnp.zeros((nblk - live, TQ), f32)] * (live < nblk), axis=0)
        usable = (rank < n_top) & (blk <= cur)
        sel_t = jnp.where(usable, 0.0, UNSELECTED)
        sel_pad = jnp.concatenate([jnp.zeros((LANES - nblk, TQ), f32), sel_t], axis=0).T
        sel_rows = jnp.concatenate([sel_pad.astype(bf16)] * NSA_GROUP, axis=0)
        qs_ref[hk] = qpad + sel_rows

    def head(guarded, live):
        for hk in kv_heads:
            select(hk, live)

        win = (mw_ref, accw_ref)
        if guarded:
            _flash_init(*win)

        @when(i >= first_interior, guarded)
        def _():
            r = lax.broadcasted_iota(jnp.int32, (rows, TQ), 0) & (TQ - 1)
            c = lax.broadcasted_iota(jnp.int32, (rows, TQ), 1)
            for hk in kv_heads:
                attend(hk, qw_ref, kw_ref, vw_ref, start - WINDOW, TQ,
                       jnp.where(c > r, 0.0, MASKED), win, first=not guarded)

        near_steps(qw_ref, kw_ref, vw_ref, win, guarded, window_extra)
        _flash_init(m_ref, acc_ref)

    def tail(guarded, extra=0):
        near_steps(qs_ref, ks_ref, vs_ref, (m_ref, acc_ref), guarded, extra)
        outs = []
        for hk in kv_heads:
            oc_t = oc_ref[hk].T
            o_c = oc_t[:HEAD_DIM] / jnp.maximum(oc_t[HEAD_DIM:HEAD_DIM + 1], 1e-30)
            o_s = _flash_out_t(acc_ref.at[hk])
            o_w = _flash_out_t(accw_ref.at[hk])
            gates = g_ref[0]
            for g in range(NSA_GROUP):
                sl = slice(g * TQ, (g + 1) * TQ)
                at = hk * _GATE_ROWS + 3 * g
                outs.append(gates[at:at + 1] * o_c[:, sl]
                            + gates[at + 1:at + 2] * o_s[:, sl]
                            + gates[at + 2:at + 3] * o_w[:, sl])
        o_ref[0] = jnp.concatenate(outs, axis=0).T.astype(bf16)

    interior = i >= first_interior
    blocks_upto = lambda hi: min(-(-hi * TQ // (SEL_BLOCK * SUBLANES)) * SUBLANES, LANES - HEAD_DIM)
    pl.when(jnp.logical_not(interior))(lambda: head(True, blocks_upto(first_interior)))
    lo = first_interior
    while lo < n_tiles:
        hi = min(2 * lo, n_tiles)
        pl.when((i >= lo) & (i < hi))(functools.partial(head, False, blocks_upto(hi)))
        lo = hi

    sel = (m_ref, acc_ref)
    n_far = jnp.maximum(i * (TQ // KU) - 1, 0)

    def far_steps(off, n_steps):
        for step in range(n_steps):
            for hk in kv_heads:
                attend(hk, qs_ref, ks_ref, vs_ref, off + step * (2 * KU), 2 * KU, None, sel)

    def far_trip(c, carry):
        far_steps(c * (8 * KU), 4)
        return carry

    lax.fori_loop(0, n_far // 8, far_trip, 0)
    rem_off = (n_far // 8) * (8 * KU)

    @pl.when((n_far & 4) != 0)
    def _():
        far_steps(rem_off, 2)

    edge = jnp.logical_not(interior)

    @pl.when(edge & ((n_far & 2) != 0))
    def _():
        far_steps(rem_off + (n_far & 4) * KU, 1)

    @pl.when(edge & ((n_far & 1) != 0))
    def _():
        for hk in kv_heads:
            attend(hk, qs_ref, ks_ref, vs_ref, rem_off + (n_far & 6) * KU, KU, None, sel)

    for extra in range(4):
        pl.when(interior & ((n_far & 3) == extra))(functools.partial(tail, False, extra))
    pl.when(edge)(lambda: tail(True))


def _nsa(qa, gates, kcmp, vcmp, ks, vs, kw, vw, cmp_bias, near_bias, ovt):
    bsz, seq, _ = qa.shape
    nc = kcmp.shape[2]
    n_top = min(SEL_TOPK, seq // SEL_BLOCK)
    rows = NSA_GROUP * TQ
    tok = lambda w: pl.BlockSpec((1, TQ, w), lambda b, i: (b, i, 0))
    kv = pl.BlockSpec((1, seq, NSA_KV_HEADS * LANES), lambda b, i: (b, 0, 0))
    cmp = lambda a: pl.BlockSpec((1,) + a.shape[1:], lambda b, i: (b, 0, 0, 0))
    per_kv = lambda width, dt: pltpu.VMEM((NSA_KV_HEADS, rows, width), dt)
    return pl.pallas_call(
        functools.partial(_nsa_kernel, n_top, seq // TQ),
        grid=(bsz, seq // TQ),
        in_specs=[tok(NSA_HEADS * LANES),
                  pl.BlockSpec((1, LANES, TQ), lambda b, i: (b, 0, i)),
                  cmp(kcmp), cmp(vcmp), kv, kv, kv, kv,
                  pl.BlockSpec((NSA_HEADS, TQ, nc), lambda b, i: (0, i, 0)),
                  pl.BlockSpec((NSA_HEADS, TQ, KU + TQ), lambda b, i: (0, 0, 0)),
                  pl.BlockSpec(ovt.shape, lambda b, i: (0, 0))],
        out_specs=tok(NSA_HEADS * HEAD_DIM),
        out_shape=jax.ShapeDtypeStruct((bsz, seq, NSA_HEADS * HEAD_DIM), bf16),
        scratch_shapes=[per_kv(LANES, bf16), per_kv(LANES, bf16),
                        per_kv(LANES, f32), per_kv(LANES, f32), per_kv(LANES, f32),
                        per_kv(LANES, f32), per_kv(LANES, f32)],
        compiler_params=_params(2),
        name="nsa",
    )(qa, gates, kcmp, vcmp, ks, vs, kw, vw, cmp_bias, near_bias, ovt)


def _sb_kernel(q_ref, k_ref, v_ref, tri_ref, o_ref, carry_ref, acc_ref):
    i = pl.program_id(2)
    n_heads = carry_ref.shape[0]
    lane = lax.broadcasted_iota(jnp.int32, (TS, LANES), 1)
    q_heads = []
    for pair in range(n_heads // 2):
        q = q_ref[0, :, pair * LANES:(pair + 1) * LANES]
        zero = jnp.zeros_like(q)
        q_heads += [jnp.where(lane < HEAD_DIM, q, zero), jnp.where(lane >= HEAD_DIM, q, zero)]
    pair_lanes = lambda hh: slice((hh // 2) * LANES, (hh // 2 + 1) * LANES)
    r = lax.broadcasted_iota(jnp.int32, (TS, TS), 0)
    c = lax.broadcasted_iota(jnp.int32, (TS, TS), 1)
    before = c < r

    def chunks(jobs, first):
        offs = [pl.multiple_of(off, TS) for off, _ in jobs]
        stage = []
        for (_, diagonal), off in zip(jobs, offs):
            for hh in range(n_heads):
                z = _dot_nt(q_heads[hh], k_ref[0, pl.ds(off, TS), pair_lanes(hh)])
                neg = -z
                soft = jnp.log2(1.0 + jnp.exp2(jnp.minimum(z, neg)))
                log_keep = jnp.minimum(neg, 0.0) - soft
                log_sig = log_keep + z
                if diagonal:
                    log_keep = jnp.where(before, log_keep, 0.0)
                later = _dot(log_keep.astype(bf16), tri_ref[...])
                stage.append((log_sig + later, jnp.sum(log_keep, axis=1, keepdims=True)))
        for hh in range(n_heads):
            carry = None if first else carry_ref[hh]
            acc = None if first else acc_ref[hh]
            for j, ((_, diagonal), off) in enumerate(zip(jobs, offs)):
                base, total = stage[n_heads * j + hh]
                if carry is not None:
                    base = base + jnp.concatenate([carry] * (TS // LANES), axis=1)
                a = jnp.exp2(base)
                if diagonal:
                    a = jnp.where(before, a, 0.0)
                out = _dot(a.astype(bf16), v_ref[0, pl.ds(off, TS), pair_lanes(hh)])
                acc = out if acc is None else acc + out
                total = jnp.broadcast_to(total, (TS, LANES))
                carry = total if carry is None else carry + total
            carry_ref[hh] = carry
            acc_ref[hh] = acc

    @pl.when(i == 0)
    def _():
        chunks([(0, True)], True)

    @pl.when(i >= 1)
    def _():
        chunks([(i * TS, True), ((i - 1) * TS, False)], True)

    def any_live():
        return jnp.max(carry_ref[...]) > EXP2_UNDERFLOW

    def more(state):
        n, live = state
        return jnp.logical_and(n < i, live)

    def older(state):
        n, _ = state
        chunks([((i - 1 - n) * TS, False)], False)
        return n + 1, any_live()

    lax.while_loop(more, older, (jnp.int32(1), any_live()))
    o_ref[0] = jnp.concatenate(
        [jnp.where(lane < HEAD_DIM, acc_ref[2 * pair], acc_ref[2 * pair + 1])
         for pair in range(n_heads // 2)], axis=1).astype(bf16)


SB_STEP_HEADS = 8


def _sb(qb, kb, vb, tri):
    bsz, seq, width = qb.shape
    step_w = SB_STEP_HEADS * HEAD_DIM
    q_spec = pl.BlockSpec((1, TS, step_w), lambda b, p, i: (b, i, p))
    kv_spec = pl.BlockSpec((1, seq, step_w), lambda b, p, i: (b, 0, p))
    state = pltpu.VMEM((SB_STEP_HEADS, TS, LANES), f32)
    return pl.pallas_call(
        _sb_kernel,
        grid=(bsz, width // step_w, seq // TS),
        in_specs=[q_spec, kv_spec, kv_spec, pl.BlockSpec(tri.shape, lambda b, p, i: (0, 0))],
        out_specs=q_spec,
        out_shape=jax.ShapeDtypeStruct((bsz, seq, width), bf16),
        scratch_shapes=[state, state],
        compiler_params=_params(3),
        name="sb",
    )(qb, kb, vb, tri)


def _post_kernel(ya_ref, yb_ref, ma_ref, mb_ref, x_ref, gate1_ref, g2_ref, sc_ref, sh_ref,
                 gate_ref, wa_ref, wb_ref, wo_ref, w1_ref, w2_ref, o_ref):
    y_a = _dot(ya_ref[0], wa_ref[...])
    y_b = _dot(yb_ref[0], wb_ref[...])
    mixed = ma_ref[0].astype(f32) * y_a + mb_ref[0].astype(f32) * y_b
    hres = x_ref[0] + gate1_ref[0] * _dot(mixed.astype(bf16), wo_ref[...])
    d = hres.shape[1]
    ms = jnp.mean(hres * hres, axis=-1, keepdims=True)
    u = (hres * lax.rsqrt(ms + EPS) * g2_ref[...]) * (1.0 + sc_ref[0]) + sh_ref[0]
    ub = u.astype(bf16)
    ff = jnp.zeros(hres.shape, f32)
    for c in range(w1_ref.shape[1] // d):
        hid = jnp.maximum(_dot(ub, w1_ref[:, c * d:(c + 1) * d]), 0.0)
        ff = ff + _dot((hid * hid).astype(bf16), w2_ref[c * d:(c + 1) * d, :])
    o_ref[0] = hres + gate_ref[0] * ff


def _post(ya, yb, ma, mb, x, gate1, g2, scale2, shift2, gate2, wa, wb, wo, w1, w2):
    bsz, seq, d = x.shape
    tok = lambda w: pl.BlockSpec((1, TM, w), lambda b, i: (b, i, 0))
    mod = pl.BlockSpec((1, 1, d), lambda b, i: (b, 0, 0))
    const = lambda a: pl.BlockSpec(a.shape, lambda b, i: (0,) * a.ndim,
                                   pipeline_mode=pl.Buffered(1))
    return pl.pallas_call(
        _post_kernel,
        grid=(bsz, seq // TM),
        in_specs=[tok(ya.shape[2]), tok(yb.shape[2]), tok(d), tok(d), tok(d), mod,
                  const(g2), mod, mod, mod, const(wa), const(wb), const(wo), const(w1), const(w2)],
        out_specs=tok(d),
        out_shape=jax.ShapeDtypeStruct((bsz, seq, d), f32),
        compiler_params=_params(2),
        name="post",
    )(ya, yb, ma, mb, x, gate1, g2, scale2, shift2, gate2, wa, wb, wo, w1, w2)


def _overlap_t(nc_pad, nsel_pad, nc, nsel):
    c_start = np.arange(nc_pad) * CMP_STRIDE
    s_start = np.arange(nsel_pad) * SEL_BLOCK
    ov = (np.minimum(c_start[None, :] + CMP_BLOCK, s_start[:, None] + SEL_BLOCK)
          - np.maximum(c_start[None, :], s_start[:, None]))
    ov = np.clip(ov, 0, CMP_BLOCK).astype(np.float32) / CMP_BLOCK
    ov[nsel:, :] = 0.0
    ov[:, nc:] = 0.0
    return ov


def _layer(h, mod, rel_tiles, p):
    bsz, seq, d = h.shape
    shift1, scale1, gate1, shift2, scale2, gate2 = [
        mod[:, k * d:(k + 1) * d].reshape(bsz, 1, d) for k in range(6)]
    cmp_bias, near_bias = rel_tiles

    gq = jnp.tile(p["q_norm_g"], 2).reshape(1, LANES)
    gk = jnp.tile(p["k_norm_g"], (1, 2))

    (qa, kc, vc, ks, vs, kw, vw, gates, qb, kb, vb, ma, mb) = _inproj(
        h, p["norm1_g"].reshape(1, d), scale1, shift1, _pack_w_in(p["w_in"], d), gq, gk)

    nch = seq // CMP_STRIDE
    kcmp, vcmp = _compress(kc, vc, p["cmp_pos"], p["cmp_k_w1"], p["cmp_k_w2"],
                           p["cmp_v_w1"], p["cmp_v_w2"], p["k_norm_g"][0].reshape(1, HEAD_DIM))

    nc = (seq - CMP_BLOCK) // CMP_STRIDE + 1
    ovt = jnp.asarray(np.concatenate([_overlap_t(nch, HEAD_DIM, nc, seq // SEL_BLOCK),
                                      np.ones((SUBLANES, nch), np.float32)], axis=0), bf16)
    y_nsa = _nsa(qa, gates, kcmp, vcmp, ks, vs, kw, vw, cmp_bias, near_bias, ovt)

    y_sb = _sb(qb, kb, vb, jnp.asarray(np.tril(np.ones((TS, TS)), -1), bf16))

    return _post(y_nsa, y_sb, ma, mb, h, gate1, p["norm2_g"].reshape(1, d), scale2, shift2, gate2,
                 p["w_up_nsa"].astype(bf16), p["w_up_sb"].astype(bf16), p["w_out"].astype(bf16),
                 p["mlp_w1"].astype(bf16), p["mlp_w2"].astype(bf16))


def kernel(x, c, rel_bias, ada_w, ada_b, norm1_g, norm2_g, w_in, cmp_pos, cmp_k_w1, cmp_k_w2,
           cmp_v_w1, cmp_v_w2, q_norm_g, k_norm_g, w_up_nsa, w_up_sb, w_out, mlp_w1, mlp_w2):
    bsz, seq, d = x.shape
    assert seq % TM == 0 and seq // SEL_BLOCK <= HEAD_DIM and seq >= WINDOW + TQ
    assert CMP_BLOCK == 2 * CMP_STRIDE and KU % SEL_BLOCK == 0
    tbl = rel_bias.astype(f32)
    rel_tiles = _bias_tiles(tbl, seq, seq // CMP_STRIDE)
    stacked = dict(norm1_g=norm1_g, norm2_g=norm2_g, w_in=w_in, cmp_pos=cmp_pos,
                   cmp_k_w1=cmp_k_w1, cmp_k_w2=cmp_k_w2, cmp_v_w1=cmp_v_w1, cmp_v_w2=cmp_v_w2,
                   q_norm_g=q_norm_g, k_norm_g=k_norm_g, w_up_nsa=w_up_nsa, w_up_sb=w_up_sb,
                   w_out=w_out, mlp_w1=mlp_w1, mlp_w2=mlp_w2)
    h = x
    for layer in range(ada_w.shape[0]):
        mod = _adaln(c, ada_w[layer], ada_b[layer])
        h = _layer(h, mod, rel_tiles, {k: v[layer] for k, v in stacked.items()})
    return h
```
